```python
import math
import jax, jax.numpy as jnp
from jax import lax
import numpy as np

D_MODEL = 1024
BATCH = 4
SEQ = 8192
DEPTH = 2

GRID_W = 64
CTX_LEN = 256
N_BRANCH = 3
SGU_WIDTH = 512
SGU_GROUPS = 4
SGU_CHUNK = 128
S5_WIDTH = 384
S5_GROUP = 16
S5_GROUPS = S5_WIDTH // S5_GROUP
S5_STATE = 64
S5_DT_MIN = 1e-3
S5_DT_MAX = 1e-1
NA_HEADS = 8
NA_HEAD_DIM = 64
NA_WIDTH = NA_HEADS * NA_HEAD_DIM
NA_KR = 8
NA_KC = 16
ROPE_BASE = 10000.0
N_GROUPS = 4
EXPERTS_PER_GROUP = 8
N_EXPERTS = N_GROUPS * EXPERTS_PER_GROUP
TOP_K = 2
D_EXPERT = 256
ADA_CHUNKS = 6
IN_WIDTH = 2 * SGU_WIDTH + S5_WIDTH + 3 * NA_WIDTH + N_BRANCH * D_MODEL
EPS = 1e-6
NEG_INF = -1e30

kernel_name = 'hybrid_parallel_mixer_hmoe_prefix_dit'


def rmsnorm(x, g):
    xf = x.astype(jnp.float32)
    y = xf * lax.rsqrt(jnp.mean(xf * xf, axis=-1, keepdims=True) + EPS)
    return (y * g.astype(jnp.float32)).astype(x.dtype)


def layernorm(x, g):
    xf = x.astype(jnp.float32)
    xc = xf - jnp.mean(xf, axis=-1, keepdims=True)
    y = xc * lax.rsqrt(jnp.mean(xc * xc, axis=-1, keepdims=True) + EPS)
    return (y * g.astype(jnp.float32)).astype(x.dtype)


def modulate(h, shift, scale):
    return h * (1 + scale) + shift


def split_projection(p):
    o1 = 2 * SGU_WIDTH
    o2 = o1 + S5_WIDTH
    o3 = o2 + 3 * NA_WIDTH
    return p[..., :o1], p[..., o1:o2], p[..., o2:o3], p[..., o3:]


def chunk_sgu(uv, norm_g, w_s, b_s):
    bsz, s, _ = uv.shape
    u, v = jnp.split(jax.nn.gelu(uv), 2, axis=-1)
    v = layernorm(v, norm_g).reshape(bsz, s // SGU_CHUNK, SGU_CHUNK, SGU_GROUPS, SGU_WIDTH // SGU_GROUPS)
    sp = jnp.einsum('gqp,bnpgc->bnqgc', w_s, v) + jnp.transpose(b_s)[None, None, :, :, None]
    return u * sp.reshape(bsz, s, SGU_WIDTH)


def _linear_recurrence(e1, e2):
    a1, b1 = e1
    a2, b2 = e2
    return a1 * a2, a2 * b1 + b2


def s5_scan(xt, lam_bar, b_bar, reverse):
    bu = jnp.einsum('sbgc,gpc->sbgp', xt.astype(jnp.complex64), b_bar)
    a = jnp.broadcast_to(lam_bar, (xt.shape[0], 1) + lam_bar.shape)
    return lax.associative_scan(_linear_recurrence, (a, bu), reverse=reverse, axis=0)


def s5_mix(x_ctx, x_lat, a_re, a_im, log_dt, b_re, b_im, c_re, c_im, d, glu_w, with_ctx_out):
    def to_groups(x):
        t = jnp.moveaxis(x.astype(jnp.float32), 1, 0)
        return t.reshape(t.shape[0], t.shape[1], S5_GROUPS, S5_GROUP)

    xt_c, xt_l = to_groups(x_ctx), to_groups(x_lat)
    dg = d.astype(jnp.float32).reshape(S5_GROUPS, S5_GROUP)
    y_l = dg * xt_l
    y_c = dg * xt_c if with_ctx_out else None
    for direction, reverse in ((0, False), (1, True)):
        lam = lax.complex(a_re[direction].astype(jnp.float32), a_im[direction].astype(jnp.float32))
        dt = jnp.exp(log_dt[direction].astype(jnp.float32))[:, None]
        lam_bar = jnp.exp(lam * dt)
        b_c = lax.complex(b_re[direction].astype(jnp.float32), b_im[direction].astype(jnp.float32))
        b_bar = ((lam_bar - 1) / lam)[:, :, None] * b_c
        c_mat = lax.complex(c_re[direction].astype(jnp.float32), c_im[direction].astype(jnp.float32))
        _, s_c = s5_scan(xt_c, lam_bar, b_bar, reverse)
        s_end = s_c[0] if reverse else s_c[-1]
        a_l, s_l = s5_scan(xt_l, lam_bar, b_bar, reverse)
        s_l = s_l + a_l * s_end[None]
        y_l = y_l + jnp.einsum('sbgp,gcp->sbgc', s_l, c_mat).real
        if with_ctx_out:
            y_c = y_c + jnp.einsum('sbgp,gcp->sbgc', s_c, c_mat).real

    def finish(y, like):
        y = jnp.moveaxis(y.reshape(y.shape[0], y.shape[1], S5_WIDTH), 0, 1)
        y = jax.nn.gelu(y)
        y = y * jax.nn.sigmoid(y @ glu_w.astype(jnp.float32))
        return y.astype(like.dtype)

    y_ctx = finish(y_c, x_ctx) if with_ctx_out else None
    return y_ctx, finish(y_l, x_lat)


def axial_rope(x, rows, cols):
    seg = NA_HEAD_DIM // 2
    inv_freq = ROPE_BASE ** (-jnp.arange(seg // 2, dtype=jnp.float32) / (seg // 2))

    def rotate(xs, pos):
        ang = pos.astype(jnp.float32)[:, None] * inv_freq
        cos = jnp.cos(ang)[None, :, None, :]
        sin = jnp.sin(ang)[None, :, None, :]
        x1, x2 = jnp.split(xs, 2, axis=-1)
        return jnp.concatenate([x1 * cos - x2 * sin, x2 * cos + x1 * sin], axis=-1)

    xf = x.astype(jnp.float32)
    out = jnp.concatenate([rotate(xf[..., :seg], rows), rotate(xf[..., seg:], cols)], axis=-1)
    return out.astype(x.dtype)


def neighbourhood_attention(qkv_ctx, qkv_lat, rpb, with_ctx_out):
    bsz, n_lat, _ = qkv_lat.shape
    n_rows = n_lat // GRID_W
    kr = min(NA_KR, n_rows)
    scale = NA_HEAD_DIM ** -0.5

    def heads(t):
        return t.reshape(t.shape[0], t.shape[1], NA_HEADS, NA_HEAD_DIM)

    q, k, v = (heads(t) for t in jnp.split(qkv_lat, 3, axis=-1))
    qc, kc, vc = (heads(t) for t in jnp.split(qkv_ctx, 3, axis=-1))
    pos = jnp.arange(n_lat)
    q = axial_rope(q, pos // GRID_W, pos % GRID_W)
    k = axial_rope(k, pos // GRID_W, pos % GRID_W)
    grid = (bsz, n_rows, GRID_W, NA_HEADS, NA_HEAD_DIM)
    qg, kg, vg = q.reshape(grid), k.reshape(grid), v.reshape(grid)
    r = jnp.arange(n_rows)
    key_rows = jnp.clip(r - kr // 2, 0, n_rows - kr)[:, None] + jnp.arange(kr)[None, :]
    k_win = kg[:, key_rows]
    v_win = vg[:, key_rows]
    w = jnp.arange(GRID_W)
    col_start = jnp.clip(w - NA_KC // 2, 0, GRID_W - NA_KC)
    col_mask = (w[None, :] >= col_start[:, None]) & (w[None, :] < col_start[:, None] + NA_KC)
    d_row = key_rows - r[:, None] + (NA_KR - 1)
    d_col = jnp.clip(w[None, :] - w[:, None], -(NA_KC - 1), NA_KC - 1) + (NA_KC - 1)
    bias = rpb[:, d_row[:, None, :, None], d_col[None, :, None, :]]
    bias = jnp.moveaxis(bias, 0, 1).astype(jnp.float32)
    s_win = jnp.einsum('brwhd,brjuhd->brhwju', qg, k_win).astype(jnp.float32) * scale + bias[None]
    s_win = jnp.where(col_mask[:, None, :], s_win, NEG_INF)
    s_ctx = jnp.einsum('brwhd,bchd->brhwc', qg, kc).astype(jnp.float32) * scale
    m = jnp.maximum(jnp.max(s_win, axis=(-2, -1)), jnp.max(s_ctx, axis=-1))[..., None]
    e_win = jnp.exp(s_win - m[..., None])
    e_ctx = jnp.exp(s_ctx - m)
    denom = jnp.sum(e_win, axis=(-2, -1)) + jnp.sum(e_ctx, axis=-1)
    p_win = (e_win / denom[..., None, None]).astype(v.dtype)
    p_ctx = (e_ctx / denom[..., None]).astype(v.dtype)
    o = jnp.einsum('brhwju,brjuhd->brwhd', p_win, v_win) + jnp.einsum('brhwc,bchd->brwhd', p_ctx, vc)
    o_lat = o.reshape(bsz, n_lat, NA_WIDTH)
    o_ctx = None
    if with_ctx_out:
        p = jax.nn.softmax(jnp.einsum('bqhd,bkhd->bhqk', qc, kc).astype(jnp.float32) * scale, axis=-1)
        o_ctx = jnp.einsum('bhqk,bkhd->bqhd', p.astype(vc.dtype), vc).reshape(bsz, -1, NA_WIDTH)
    return o_ctx, o_lat


def hier_moe(h, rg_w, rg_b, re_w, re_b, e_gate, e_up, e_down):
    bsz, s, _ = h.shape
    lg = (h @ rg_w + rg_b).astype(jnp.float32)
    pg = jax.nn.softmax(lg, axis=-1)
    grp = jnp.argmax(lg, axis=-1)
    g_weight = jnp.max(pg, axis=-1, keepdims=True)
    g_onehot = jax.nn.one_hot(grp, N_GROUPS, dtype=jnp.float32)
    le = (h @ re_w + re_b).astype(jnp.float32).reshape(bsz, s, N_GROUPS, EXPERTS_PER_GROUP)
    le_sel = jnp.einsum('bsge,bsg->bse', le, g_onehot)
    top_v, top_i = lax.top_k(le_sel, TOP_K)
    top_w = jax.nn.softmax(top_v, axis=-1)
    w_in_group = jnp.sum(jax.nn.one_hot(top_i, EXPERTS_PER_GROUP, dtype=jnp.float32) * top_w[..., None], axis=-2)
    combine = g_weight[..., None] * g_onehot[..., None] * w_in_group[..., None, :]
    combine = combine.reshape(bsz, s, N_EXPERTS).astype(h.dtype)
    out = jnp.zeros_like(h)
    for g in range(N_GROUPS):
        sl = slice(g * EXPERTS_PER_GROUP, (g + 1) * EXPERTS_PER_GROUP)
        hid = jax.nn.silu(jnp.einsum('bsd,edf->bsef', h, e_gate[sl])) * jnp.einsum('bsd,edf->bsef', h, e_up[sl])
        out = out + jnp.einsum('bsef,efd->bsd', hid * combine[..., sl, None], e_down[sl])
    return out


def hybrid_layer(x, xc, mod, mod_c, norm_mix_g, norm_ffn_g, w_in, sgu_norm_g, sgu_w, sgu_b,
                 s5_a_re, s5_a_im, s5_log_dt, s5_b_re, s5_b_im, s5_c_re, s5_c_im, s5_d, s5_glu_w,
                 na_rpb, w_br_a, w_br_b, w_br_c, w_out, rg_w, rg_b, re_w, re_b, e_gate, e_up, e_down,
                 with_ctx_out):
    sh1, sc1, g1, sh2, sc2, g2 = jnp.split(mod[:, None, :], ADA_CHUNKS, axis=-1)
    csh1, csc1, cg1, csh2, csc2, cg2 = jnp.split(mod_c, ADA_CHUNKS, axis=-1)
    h = modulate(rmsnorm(x, norm_mix_g), sh1, sc1)
    hc = modulate(rmsnorm(xc, norm_mix_g), csh1, csc1)
    a_l, b_l, c_l, gate_l = split_projection(h @ w_in)
    a_c, b_c, c_c, gate_c = split_projection(hc @ w_in)
    yb_c, yb_l = s5_mix(b_c, b_l, s5_a_re, s5_a_im, s5_log_dt, s5_b_re, s5_b_im, s5_c_re, s5_c_im,
                        s5_d, s5_glu_w, with_ctx_out)
    yc_c, yc_l = neighbourhood_attention(c_c, c_l, na_rpb, with_ctx_out)

    def merge(ya, yb, yc, gates):
        gt = jax.nn.sigmoid(gates.reshape(gates.shape[:-1] + (N_BRANCH, D_MODEL)))
        m = gt[..., 0, :] * (ya @ w_br_a) + gt[..., 1, :] * (yb @ w_br_b) + gt[..., 2, :] * (yc @ w_br_c)
        return m @ w_out

    x = x + g1 * merge(chunk_sgu(a_l, sgu_norm_g, sgu_w, sgu_b), yb_l, yc_l, gate_l)
    x = x + g2 * hier_moe(modulate(rmsnorm(x, norm_ffn_g), sh2, sc2), rg_w, rg_b, re_w, re_b, e_gate, e_up, e_down)
    if with_ctx_out:
        xc = xc + cg1 * merge(chunk_sgu(a_c, sgu_norm_g, sgu_w, sgu_b), yb_c, yc_c, gate_c)
        xc = xc + cg2 * hier_moe(modulate(rmsnorm(xc, norm_ffn_g), csh2, csc2), rg_w, rg_b, re_w, re_b,
                                 e_gate, e_up, e_down)
    return x, xc


def setup_inputs(seed: int = 0) -> dict:
    key = jax.random.key(seed)
    ks = iter(jax.random.split(key, 48))

    def nrm(shape, scale):
        return scale * jax.random.normal(next(ks), shape, jnp.float32)

    def gain(shape):
        return 1.0 + nrm(shape, 0.02)

    L = DEPTH
    s5_state_shape = (L, 2, S5_GROUPS, S5_STATE)
    a_im_base = jnp.pi * jnp.arange(S5_STATE, dtype=jnp.float32)
    return {
        'x': nrm((BATCH, SEQ, D_MODEL), 1.0),
        'c': nrm((BATCH, D_MODEL), 1.0),
        'ctx': nrm((BATCH, CTX_LEN, D_MODEL), 1.0),
        'c_ctx': nrm((D_MODEL,), 1.0),
        'ada_w': nrm((L, D_MODEL, ADA_CHUNKS * D_MODEL), D_MODEL ** -0.5),
        'ada_b': nrm((L, ADA_CHUNKS * D_MODEL), 0.02),
        'norm_mix_g': gain((L, D_MODEL)),
        'norm_ffn_g': gain((L, D_MODEL)),
        'w_in': nrm((L, D_MODEL, IN_WIDTH), D_MODEL ** -0.5),
        'sgu_norm_g': gain((L, SGU_WIDTH)),
        'sgu_w': nrm((L, SGU_GROUPS, SGU_CHUNK, SGU_CHUNK), SGU_CHUNK ** -0.5),
        'sgu_b': gain((L, SGU_GROUPS, SGU_CHUNK)),
        's5_a_re': -0.5 + nrm(s5_state_shape, 0.01),
        's5_a_im': a_im_base + nrm(s5_state_shape, 0.01),
        's5_log_dt': jax.random.uniform(next(ks), (L, 2, S5_GROUPS), jnp.float32,
                                        math.log(S5_DT_MIN), math.log(S5_DT_MAX)),
        's5_b_re': nrm((L, 2, S5_GROUPS, S5_STATE, S5_GROUP), (2 * S5_GROUP) ** -0.5),
        's5_b_im': nrm((L, 2, S5_GROUPS, S5_STATE, S5_GROUP), (2 * S5_GROUP) ** -0.5),
        's5_c_re': nrm((L, 2, S5_GROUPS, S5_GROUP, S5_STATE), (2 * S5_STATE) ** -0.5),
        's5_c_im': nrm((L, 2, S5_GROUPS, S5_GROUP, S5_STATE), (2 * S5_STATE) ** -0.5),
        's5_d': nrm((L, S5_WIDTH), 1.0),
        's5_glu_w': nrm((L, S5_WIDTH, S5_WIDTH), S5_WIDTH ** -0.5),
        'na_rpb': nrm((L, NA_HEADS, 2 * NA_KR - 1, 2 * NA_KC - 1), 0.02),
        'w_br_a': nrm((L, SGU_WIDTH, D_MODEL), SGU_WIDTH ** -0.5),
        'w_br_b': nrm((L, S5_WIDTH, D_MODEL), S5_WIDTH ** -0.5),
        'w_br_c': nrm((L, NA_WIDTH, D_MODEL), NA_WIDTH ** -0.5),
        'w_out': nrm((L, D_MODEL, D_MODEL), D_MODEL ** -0.5),
        'router_group_w': nrm((L, D_MODEL, N_GROUPS), D_MODEL ** -0.5),
        'router_group_b': nrm((L, N_GROUPS), 0.01),
        'router_expert_w': nrm((L, D_MODEL, N_EXPERTS), D_MODEL ** -0.5),
        'router_expert_b': nrm((L, N_EXPERTS), 0.01),
        'exp_w_gate': nrm((L, N_EXPERTS, D_MODEL, D_EXPERT), D_MODEL ** -0.5),
        'exp_w_up': nrm((L, N_EXPERTS, D_MODEL, D_EXPERT), D_MODEL ** -0.5),
        'exp_w_down': nrm((L, N_EXPERTS, D_EXPERT, D_MODEL), D_EXPERT ** -0.5),
        'final_norm_g': gain((D_MODEL,)),
    }


def reference(x, c, ctx, c_ctx, ada_w, ada_b, norm_mix_g, norm_ffn_g, w_in, sgu_norm_g, sgu_w, sgu_b,
              s5_a_re, s5_a_im, s5_log_dt, s5_b_re, s5_b_im, s5_c_re, s5_c_im, s5_d, s5_glu_w, na_rpb,
              w_br_a, w_br_b, w_br_c, w_out, router_group_w, router_group_b, router_expert_w, router_expert_b,
              exp_w_gate, exp_w_up, exp_w_down, final_norm_g):
    xc = ctx
    silu_c = jax.nn.silu(c)
    silu_cc = jax.nn.silu(c_ctx)
    for l in range(DEPTH):
        mod = silu_c @ ada_w[l] + ada_b[l]
        mod_c = silu_cc @ ada_w[l] + ada_b[l]
        x, xc = hybrid_layer(
            x, xc, mod, mod_c, norm_mix_g[l], norm_ffn_g[l], w_in[l], sgu_norm_g[l], sgu_w[l], sgu_b[l],
            s5_a_re[l], s5_a_im[l], s5_log_dt[l], s5_b_re[l], s5_b_im[l], s5_c_re[l], s5_c_im[l], s5_d[l],
            s5_glu_w[l], na_rpb[l], w_br_a[l], w_br_b[l], w_br_c[l], w_out[l], router_group_w[l],
            router_group_b[l], router_expert_w[l], router_expert_b[l], exp_w_gate[l], exp_w_up[l],
            exp_w_down[l], with_ctx_out=(l < DEPTH - 1))
    return rmsnorm(x, final_norm_g)
```

```python
import functools
import math

import jax
import jax.numpy as jnp
import numpy as np
from jax import lax
from jax.experimental import pallas as pl
from jax.experimental.pallas import tpu as pltpu

F32 = jnp.float32
BF16 = jnp.bfloat16

GRID_W = 64
N_BRANCH = 3
SGU_WIDTH = 512
SGU_GROUPS = 4
SGU_CHUNK = 128
S5_WIDTH = 384
S5_GROUP = 16
S5_GROUPS = S5_WIDTH // S5_GROUP
S5_STATE = 64
NA_HEADS = 8
NA_HEAD_DIM = 64
NA_WIDTH = NA_HEADS * NA_HEAD_DIM
NA_KR = 8
NA_KC = 16
ROPE_BASE = 10000.0
N_GROUPS = 4
EXPERTS_PER_GROUP = 8
N_EXPERTS = N_GROUPS * EXPERTS_PER_GROUP
TOP_K = 2
D_EXPERT = 256
ADA_CHUNKS = 6
EPS = 1e-6
NEG_INF = -1e30

LANES = 128
SUBLANES = 8
VMEM_LIMIT_BYTES = 56 * 1024 * 1024

S5_LANES = 2 * S5_GROUPS * S5_STATE
S5_SLABS = S5_LANES // LANES
S5_CHUNK = 128
S5_PITCH = S5_CHUNK + 4


def _cparams(*sem):
    return pltpu.CompilerParams(dimension_semantics=sem, vmem_limit_bytes=VMEM_LIMIT_BYTES)


def _full(shape):
    n = len(shape)
    return pl.BlockSpec(shape, lambda *_: (0,) * n)


def _ada_kernel(c_ref, w_ref, b_ref, o_ref):
    c = c_ref[...]
    s = c * jax.nn.sigmoid(c)
    o_ref[...] = jnp.dot(s, w_ref[...], preferred_element_type=F32) + b_ref[...]


def ada_modulation(cc, ada_w, ada_b):
    n_layers, d, n = ada_w.shape
    tn = 1536
    return pl.pallas_call(
        _ada_kernel,
        grid=(n_layers, n // tn),
        in_specs=[
            pl.BlockSpec((SUBLANES, d), lambda l, j: (0, 0)),
            pl.BlockSpec((None, d, tn), lambda l, j: (l, 0, j)),
            pl.BlockSpec((None, 1, tn), lambda l, j: (l, 0, j)),
        ],
        out_specs=pl.BlockSpec((None, SUBLANES, tn), lambda l, j: (l, 0, j)),
        out_shape=jax.ShapeDtypeStruct((n_layers, SUBLANES, n), F32),
        compiler_params=_cparams("parallel", "parallel"),
        name="ada_modulation",
    )(cc, ada_w, ada_b.reshape(n_layers, 1, n))


def _gelu(x):
    return jax.nn.gelu(x)


def _mixer_in_kernel(x_ref, mod_ref, g_ref, w_ref, lng_ref, sw_ref, sb_ref, *rest, rope):
    if rope:
        cos_ref, sin_ref, swap_ref, ya_ref, b_ref, q_ref, k_ref, v_ref, gate_ref = rest
    else:
        ya_ref, b_ref, q_ref, k_ref, v_ref, gate_ref = rest
    tm = x_ref.shape[0]
    xf = x_ref[...]
    ms = jnp.mean(xf * xf, axis=-1, keepdims=True)
    y = xf * lax.rsqrt(ms + EPS) * g_ref[...]
    h = y * (1.0 + mod_ref[1:2, :]) + mod_ref[0:1, :]
    hb = h.astype(BF16)

    def proj(lo, hi):
        return jnp.dot(hb, w_ref[:, lo:hi], preferred_element_type=F32)

    o1 = 2 * SGU_WIDTH
    o2 = o1 + S5_WIDTH
    oq, ok, ov = o2, o2 + NA_WIDTH, o2 + 2 * NA_WIDTH
    o3 = o2 + 3 * NA_WIDTH

    u = _gelu(proj(0, SGU_WIDTH))
    v = _gelu(proj(SGU_WIDTH, o1))
    vc = v - jnp.mean(v, axis=-1, keepdims=True)
    vn = vc * lax.rsqrt(jnp.mean(vc * vc, axis=-1, keepdims=True) + EPS) * lng_ref[...]
    vb = vn.astype(BF16)
    cw = SGU_WIDTH // SGU_GROUPS
    for c in range(tm // SGU_CHUNK):
        r0 = c * SGU_CHUNK
        for g in range(SGU_GROUPS):
            sp = jnp.dot(sw_ref[g], vb[r0:r0 + SGU_CHUNK, g * cw:(g + 1) * cw],
                         preferred_element_type=F32) + sb_ref[g]
            ya_ref[r0:r0 + SGU_CHUNK, g * cw:(g + 1) * cw] = (
                u[r0:r0 + SGU_CHUNK, g * cw:(g + 1) * cw] * sp).astype(ya_ref.dtype)

    b_ref[...] = proj(o1, o2).astype(b_ref.dtype)

    q = proj(oq, ok)
    k = proj(ok, ov)
    if rope:
        cos = cos_ref[...]
        sin = sin_ref[...]
        qs = jnp.dot(q.astype(BF16), swap_ref[...], preferred_element_type=F32)
        ks = jnp.dot(k.astype(BF16), swap_ref[...], preferred_element_type=F32)
        q = q * cos + qs * sin
        k = k * cos + ks * sin
    q_ref[...] = (q * (NA_HEAD_DIM ** -0.5)).astype(q_ref.dtype)
    k_ref[...] = k.astype(k_ref.dtype)
    v_ref[...] = proj(ov, o3).astype(v_ref.dtype)
    gate_ref[...] = proj(o3, o3 + N_BRANCH * x_ref.shape[1]).astype(gate_ref.dtype)


def mixer_in(x, mod, norm_g, w_in, sgu_norm_g, sgu_w, sgu_bias, rope_tabs, tm):
    bsz, n_tok, d = x.shape
    rope = rope_tabs is not None
    in_specs = [
        pl.BlockSpec((None, tm, d), lambda b, i: (b, i, 0)),
        pl.BlockSpec((None, ADA_CHUNKS, d), lambda b, i: (b, 0, 0)),
        _full((1, d)),
        _full(w_in.shape),
        _full((1, SGU_WIDTH)),
        _full(sgu_w.shape),
        _full(sgu_bias.shape),
    ]
    args = [x, mod, norm_g.reshape(1, d), w_in, sgu_norm_g.reshape(1, SGU_WIDTH), sgu_w, sgu_bias]
    if rope:
        cos_t, sin_t, swap = rope_tabs
        in_specs += [
            pl.BlockSpec((tm, NA_WIDTH), lambda b, i: (i, 0)),
            pl.BlockSpec((tm, NA_WIDTH), lambda b, i: (i, 0)),
            _full(swap.shape),
        ]
        args += [cos_t, sin_t, swap]

    def tok(width):
        return pl.BlockSpec((None, tm, width), lambda b, i: (b, i, 0))

    out_shapes = [
        jax.ShapeDtypeStruct((bsz, n_tok, SGU_WIDTH), BF16),
        jax.ShapeDtypeStruct((bsz, n_tok, S5_WIDTH), BF16),
        jax.ShapeDtypeStruct((bsz, n_tok, NA_WIDTH), BF16),
        jax.ShapeDtypeStruct((bsz, n_tok, NA_WIDTH), BF16),
        jax.ShapeDtypeStruct((bsz, n_tok, NA_WIDTH), BF16),
        jax.ShapeDtypeStruct((bsz, n_tok, N_BRANCH * d), BF16),
    ]
    out_specs = [tok(SGU_WIDTH), tok(S5_WIDTH), tok(NA_WIDTH), tok(NA_WIDTH), tok(NA_WIDTH), tok(N_BRANCH * d)]
    return pl.pallas_call(
        functools.partial(_mixer_in_kernel, rope=rope),
        grid=(bsz, n_tok // tm),
        in_specs=in_specs,
        out_specs=out_specs,
        out_shape=out_shapes,
        compiler_params=_cparams("parallel", "parallel"),
        name="mixer_in_rope" if rope else "mixer_in",
    )(*args)


def rope_tables(n_tok):
    pos = jnp.arange(n_tok)
    rows = (pos // GRID_W).astype(F32)
    cols = (pos % GRID_W).astype(F32)
    seg = NA_HEAD_DIM // 2
    half = seg // 2
    inv_freq = ROPE_BASE ** (-jnp.arange(half, dtype=F32) / half)
    ang_r = rows[:, None] * inv_freq
    ang_c = cols[:, None] * inv_freq
    cos = jnp.concatenate([jnp.cos(ang_r)] * 2 + [jnp.cos(ang_c)] * 2, axis=-1)
    sin = jnp.concatenate([-jnp.sin(ang_r), jnp.sin(ang_r), -jnp.sin(ang_c), jnp.sin(ang_c)], axis=-1)
    d = np.arange(NA_WIDTH)
    partner = np.where((d % seg) < half, d + half, d - half)
    swap = np.zeros((NA_WIDTH, NA_WIDTH), np.float32)
    swap[partner, d] = 1.0
    return jnp.tile(cos, (1, NA_HEADS)), jnp.tile(sin, (1, NA_HEADS)), jnp.asarray(swap, BF16)


def _s5_kernel(xf_ref, xr_ref, s0_ref, win_ref, lre_ref, lim_ref, wout_ref, rev_ref,
               yf_ref, yr_ref, send_ref, lhs_scr, st_scr, state_scr):
    bsz, tc, _ = xf_ref.shape
    pitch = S5_PITCH
    rows = bsz * pitch
    half = S5_SLABS // 2
    i = pl.program_id(0)

    @pl.when(i == 0)
    def _():
        state_scr[...] = s0_ref[...]
        lhs_scr[...] = jnp.zeros_like(lhs_scr)

    rev = rev_ref[...]
    for b in range(bsz):
        lhs_scr[0, b * pitch:b * pitch + tc, :] = xf_ref[b].astype(F32)
        lhs_scr[1, b * pitch:b * pitch + tc, :] = jnp.dot(rev, xr_ref[b], preferred_element_type=F32)

    for d in range(2):
        lhs = lhs_scr[d].astype(BF16)
        for k in range(half):
            res = jnp.dot(lhs, win_ref[d, :, 2 * k * LANES:(2 * k + 2) * LANES], preferred_element_type=F32)
            st_scr[2 * k, d * rows:(d + 1) * rows, :] = res[:, :LANES]
            st_scr[2 * k + 1, d * rows:(d + 1) * rows, :] = res[:, LANES:]

    def step(t, carry):
        out = [None] * S5_SLABS
        for c in range(half):
            sre, sim = carry[c], carry[c + half]
            idx = pl.ds(t, 2 * bsz, stride=pitch)
            lr = lre_ref[c]
            li = lim_ref[c]
            nre = lr * sre - li * sim + st_scr[c, idx, :]
            nim = lr * sim + li * sre + st_scr[c + half, idx, :]
            st_scr[c, idx, :] = nre
            st_scr[c + half, idx, :] = nim
            out[c], out[c + half] = nre, nim
        return tuple(out)

    init = tuple(state_scr[:, c * LANES:(c + 1) * LANES] for c in range(S5_SLABS))
    fin = lax.fori_loop(0, tc, step, init)
    for c in range(S5_SLABS):
        state_scr[:, c * LANES:(c + 1) * LANES] = fin[c]
    send_ref[...] = state_scr[...]

    for d in range(2):
        s_all = jnp.concatenate([st_scr[c, d * rows:(d + 1) * rows, :] for c in range(S5_SLABS)], axis=-1)
        y = jnp.dot(s_all.astype(BF16), wout_ref[d], preferred_element_type=F32)
        for b in range(bsz):
            yb = y[b * pitch:b * pitch + tc, :]
            if d == 0:
                yf_ref[b] = yb
            else:
                yr_ref[b] = jnp.dot(rev, yb.astype(BF16), preferred_element_type=F32)


def s5_scan(xb, s0, w_in, lam_re, lam_im, w_out):
    bsz, n_tok, width = xb.shape
    assert 2 * bsz == SUBLANES and n_tok % S5_CHUNK == 0
    n = n_tok // S5_CHUNK
    tc = S5_CHUNK
    rev = jnp.asarray(np.eye(tc, dtype=np.float32)[::-1], BF16)
    blk = (bsz, tc, width)
    return pl.pallas_call(
        _s5_kernel,
        grid=(n,),
        in_specs=[
            pl.BlockSpec(blk, lambda i: (0, i, 0)),
            pl.BlockSpec(blk, lambda i: (0, n - 1 - i, 0)),
            _full(s0.shape),
            _full(w_in.shape),
            _full(lam_re.shape),
            _full(lam_im.shape),
            _full(w_out.shape),
            _full(rev.shape),
        ],
        out_specs=[
            pl.BlockSpec(blk, lambda i: (0, i, 0)),
            pl.BlockSpec(blk, lambda i: (0, n - 1 - i, 0)),
            _full(s0.shape),
        ],
        out_shape=[
            jax.ShapeDtypeStruct(xb.shape, F32),
            jax.ShapeDtypeStruct(xb.shape, F32),
            jax.ShapeDtypeStruct(s0.shape, F32),
        ],
        scratch_shapes=[
            pltpu.VMEM((2, bsz * S5_PITCH, width), F32),
            pltpu.VMEM((S5_SLABS, 2 * bsz * S5_PITCH, LANES), F32),
            pltpu.VMEM(s0.shape, F32),
        ],
        compiler_params=_cparams("arbitrary"),
        name="s5_scan",
    )(xb, xb, s0, w_in, lam_re, lam_im, w_out, rev)


NA_ROWS_PER_STEP = 8
_NT = (((1,), (1,)), ((), ()))


def _na_head_pair(qp, kp, vp, kcp, vcp, bias_fn):
    lane_head = lax.broadcasted_iota(jnp.int32, qp.shape, 1) // NA_HEAD_DIM
    o_pair = None
    for hh in range(2):
        qm = jnp.where(lane_head == hh, qp, jnp.zeros_like(qp))
        s_ctx = lax.dot_general(qm, kcp, _NT, preferred_element_type=F32)
        m = jnp.max(s_ctx, axis=-1, keepdims=True)
        if kp is not None:
            s_win = lax.dot_general(qm, kp, _NT, preferred_element_type=F32) + bias_fn(hh)
            m = jnp.maximum(m, jnp.max(s_win, axis=-1, keepdims=True))
            e_win = jnp.exp(s_win - m)
        e_ctx = jnp.exp(s_ctx - m)
        den = jnp.sum(e_ctx, axis=-1, keepdims=True)
        o = jnp.dot(e_ctx.astype(BF16), vcp, preferred_element_type=F32)
        if kp is not None:
            den = den + jnp.sum(e_win, axis=-1, keepdims=True)
            o = o + jnp.dot(e_win.astype(BF16), vp, preferred_element_type=F32)
        o = o * (1.0 / den)
        o_pair = o if hh == 0 else jnp.where(lane_head == 0, o_pair, o)
    return o_pair


def _na_kernel(q_ref, k_ref, v_ref, kc_ref, vc_ref, tab_ref, o_ref, *, n_rows):
    i = pl.program_id(1)
    win = NA_KR * GRID_W

    def row_body(rr, carry):
        r = i * NA_ROWS_PER_STEP + rr
        kstart = jnp.clip(r - NA_KR // 2, 0, n_rows - NA_KR)
        d0 = kstart - r + (NA_KR - 1)
        q0 = pl.multiple_of(rr * GRID_W, GRID_W)
        k0 = pl.multiple_of(kstart * GRID_W, GRID_W)
        for p in range(NA_HEADS // 2):
            ls = slice(p * LANES, (p + 1) * LANES)

            def bias_fn(hh, p=p):
                return jnp.concatenate(
                    [tab_ref[2 * p + hh, d0 + 2 * jj] for jj in range(NA_KR // 2)], axis=-1)

            o_pair = _na_head_pair(q_ref[pl.ds(q0, GRID_W), ls], k_ref[pl.ds(k0, win), ls],
                                   v_ref[pl.ds(k0, win), ls], kc_ref[:, ls], vc_ref[:, ls], bias_fn)
            o_ref[pl.ds(q0, GRID_W), ls] = o_pair.astype(o_ref.dtype)
        return carry

    lax.fori_loop(0, NA_ROWS_PER_STEP, row_body, 0)


def na_bias_table(rpb):
    w = np.arange(GRID_W)
    col_start = np.clip(w - NA_KC // 2, 0, GRID_W - NA_KC)
    col_mask = (w[None, :] >= col_start[:, None]) & (w[None, :] < col_start[:, None] + NA_KC)
    d_col = np.clip(w[None, :] - w[:, None], -(NA_KC - 1), NA_KC - 1) + (NA_KC - 1)
    full = jnp.where(col_mask[None, None], rpb.astype(F32)[:, :, d_col], NEG_INF)
    return jnp.concatenate([full[:, :-1], full[:, 1:]], axis=-1)


def neighbourhood_attention(q, k, v, kc, vc, table):
    bsz, n_tok, width = q.shape
    n_ctx = kc.shape[1]
    n_rows = n_tok // GRID_W
    tq = NA_ROWS_PER_STEP * GRID_W
    return pl.pallas_call(
        functools.partial(_na_kernel, n_rows=n_rows),
        grid=(bsz, n_rows // NA_ROWS_PER_STEP),
        in_specs=[
            pl.BlockSpec((None, tq, width), lambda b, i: (b, i, 0)),
            pl.BlockSpec((None, n_tok, width), lambda b, i: (b, 0, 0)),
            pl.BlockSpec((None, n_tok, width), lambda b, i: (b, 0, 0)),
            pl.BlockSpec((None, n_ctx, width), lambda b, i: (b, 0, 0)),
            pl.BlockSpec((None, n_ctx, width), lambda b, i: (b, 0, 0)),
            _full(table.shape),
        ],
        out_specs=pl.BlockSpec((None, tq, width), lambda b, i: (b, i, 0)),
        out_shape=jax.ShapeDtypeStruct(q.shape, BF16),
        compiler_params=_cparams("parallel", "arbitrary"),
        name="neighbourhood_attention",
    )(q, k, v, kc, vc, table)


def _ctx_attn_kernel(q_ref, k_ref, v_ref, o_ref):
    for p in range(NA_HEADS // 2):
        ls = slice(p * LANES, (p + 1) * LANES)
        o_pair = _na_head_pair(q_ref[:, ls], None, None, k_ref[:, ls], v_ref[:, ls], None)
        o_ref[:, ls] = o_pair.astype(o_ref.dtype)


def context_attention(qc, kc, vc):
    bsz, n_ctx, width = qc.shape
    spec = pl.BlockSpec((None, n_ctx, width), lambda b: (b, 0, 0))
    return pl.pallas_call(
        _ctx_attn_kernel,
        grid=(bsz,),
        in_specs=[spec, spec, spec],
        out_specs=spec,
        out_shape=jax.ShapeDtypeStruct(qc.shape, BF16),
        compiler_params=_cparams("parallel"),
        name="context_attention",
    )(qc, kc, vc)


ROUTER_LANES = LANES
EXPERT_LANE0 = N_GROUPS


def _route(logits):
    lane = lax.broadcasted_iota(jnp.int32, logits.shape, 1)
    big = jnp.int32(ROUTER_LANES)
    is_g = lane < N_GROUPS
    lg = jnp.where(is_g, logits, -jnp.inf)
    mg = jnp.max(lg, axis=-1, keepdims=True)
    grp = jnp.min(jnp.where(lg == mg, lane, big), axis=-1, keepdims=True)
    g_weight = 1.0 / jnp.sum(jnp.where(is_g, jnp.exp(logits - mg), 0.0), axis=-1, keepdims=True)
    e_idx = lane - EXPERT_LANE0
    sel = (e_idx >= 0) & (e_idx < N_EXPERTS) & ((e_idx // EXPERTS_PER_GROUP) == grp)
    ls1 = jnp.where(sel, logits, -jnp.inf)
    v1 = jnp.max(ls1, axis=-1, keepdims=True)
    i1 = jnp.min(jnp.where(ls1 == v1, lane, big), axis=-1, keepdims=True)
    ls2 = jnp.where(lane == i1, -jnp.inf, ls1)
    v2 = jnp.max(ls2, axis=-1, keepdims=True)
    i2 = jnp.min(jnp.where(ls2 == v2, lane, big), axis=-1, keepdims=True)
    e2 = jnp.exp(v2 - v1)
    w1 = 1.0 / (1.0 + e2)
    w2 = e2 * w1
    return g_weight * (jnp.where(lane == i1, w1, 0.0) + jnp.where(lane == i2, w2, 0.0))


def _merge_kernel(x_ref, ya_ref, xb_ref, yf_ref, yr_ref, yc_ref, gate_ref, mod_ref, d_ref, glu_ref,
                  wa_ref, wb_ref, wc_ref, wo_ref, g_ref, rw_ref, rb_ref, xn_ref, h_ref, comb_ref):
    d = x_ref.shape[1]
    yb = d_ref[...] * xb_ref[...].astype(F32) + yf_ref[...] + yr_ref[...]
    yb = _gelu(yb)
    yb = yb * jax.nn.sigmoid(jnp.dot(yb.astype(BF16), glu_ref[...], preferred_element_type=F32))

    def gate(j):
        return jax.nn.sigmoid(gate_ref[:, j * d:(j + 1) * d].astype(F32))

    m = gate(0) * jnp.dot(ya_ref[...], wa_ref[...], preferred_element_type=F32)
    m = m + gate(1) * jnp.dot(yb.astype(BF16), wb_ref[...], preferred_element_type=F32)
    m = m + gate(2) * jnp.dot(yc_ref[...], wc_ref[...], preferred_element_type=F32)
    xn = x_ref[...] + mod_ref[2:3, :] * jnp.dot(m.astype(BF16), wo_ref[...], preferred_element_type=F32)
    xn_ref[...] = xn
    ms = jnp.mean(xn * xn, axis=-1, keepdims=True)
    h = xn * lax.rsqrt(ms + EPS) * g_ref[...]
    h = h * (1.0 + mod_ref[4:5, :]) + mod_ref[3:4, :]
    h_ref[...] = h.astype(h_ref.dtype)
    logits = jnp.dot(h, rw_ref[...], preferred_element_type=F32) + rb_ref[...]
    comb_ref[...] = _route(logits)


def merge_and_route(x, ya, xb, yf, yr, yc, gates, mod, s5_d, glu_w, w_br_a, w_br_b, w_br_c, w_out,
                    norm_ffn_g, router_w, router_b, tm):
    bsz, n_tok, d = x.shape

    def tok(width):
        return pl.BlockSpec((None, tm, width), lambda b, i: (b, i, 0))

    weights = [s5_d.reshape(1, S5_WIDTH), glu_w, w_br_a, w_br_b, w_br_c, w_out,
               norm_ffn_g.reshape(1, d), router_w, router_b]
    return pl.pallas_call(
        _merge_kernel,
        grid=(bsz, n_tok // tm),
        in_specs=[tok(d), tok(SGU_WIDTH), tok(S5_WIDTH), tok(S5_WIDTH), tok(S5_WIDTH), tok(NA_WIDTH),
                  tok(N_BRANCH * d), pl.BlockSpec((None, ADA_CHUNKS, d), lambda b, i: (b, 0, 0))]
        + [_full(w.shape) for w in weights],
        out_specs=[tok(d), tok(d), tok(ROUTER_LANES)],
        out_shape=[
            jax.ShapeDtypeStruct(x.shape, F32),
            jax.ShapeDtypeStruct(x.shape, BF16),
            jax.ShapeDtypeStruct((bsz, n_tok, ROUTER_LANES), F32),
        ],
        compiler_params=_cparams("parallel", "parallel"),
        name="merge_and_route",
    )(x, ya, xb, yf, yr, yc, gates, mod, *weights)


def router_params(rg_w, rg_b, re_w, re_b):
    d = rg_w.shape[0]
    pad = ROUTER_LANES - N_GROUPS - N_EXPERTS
    w = jnp.concatenate([rg_w, re_w, jnp.zeros((d, pad), F32)], axis=1).astype(F32)
    b = jnp.concatenate([rg_b, re_b, jnp.zeros((pad,), F32)]).astype(F32).reshape(1, ROUTER_LANES)
    return w, b


def _moe_kernel(xn_ref, h_ref, comb_ref, mod_ref, wg_ref, wu_ref, wd_ref, exp_ref, fg_ref, o_ref, hid_scr,
                *, final_norm):
    g = pl.program_id(2)

    @pl.when(g == 0)
    def _():
        o_ref[...] = xn_ref[...]

    h = h_ref[...]
    comb = comb_ref[...]
    comb_hi = comb.astype(BF16)
    comb_lo = (comb - comb_hi.astype(F32)).astype(BF16)
    for e in range(EXPERTS_PER_GROUP):
        cs = slice(e * D_EXPERT, (e + 1) * D_EXPERT)
        a = jnp.dot(h, wg_ref[:, cs], preferred_element_type=F32)
        u = jnp.dot(h, wu_ref[:, cs], preferred_element_type=F32)
        ce = (jnp.dot(comb_hi, exp_ref[:, cs], preferred_element_type=F32)
              + jnp.dot(comb_lo, exp_ref[:, cs], preferred_element_type=F32))
        hid_scr[:, cs] = (a * jax.nn.sigmoid(a) * u * ce).astype(BF16)
    o_ref[...] += mod_ref[5:6, :] * jnp.dot(hid_scr[...], wd_ref[...], preferred_element_type=F32)

    if final_norm:
        @pl.when(g == pl.num_programs(2) - 1)
        def _():
            xo = o_ref[...]
            ms = jnp.mean(xo * xo, axis=-1, keepdims=True)
            o_ref[...] = xo * lax.rsqrt(ms + EPS) * fg_ref[...]


def moe_dense(xn, h, comb, mod, wg, wu, wd, expand, final_g, tm, final_norm):
    bsz, n_tok, d = xn.shape
    gw = EXPERTS_PER_GROUP * D_EXPERT

    def tok(width):
        return pl.BlockSpec((None, tm, width), lambda b, i, g: (b, i, 0))

    return pl.pallas_call(
        functools.partial(_moe_kernel, final_norm=final_norm),
        grid=(bsz, n_tok // tm, N_GROUPS),
        in_specs=[
            tok(d), tok(d), tok(ROUTER_LANES),
            pl.BlockSpec((None, ADA_CHUNKS, d), lambda b, i, g: (b, 0, 0)),
            pl.BlockSpec((None, d, gw), lambda b, i, g: (g, 0, 0)),
            pl.BlockSpec((None, d, gw), lambda b, i, g: (g, 0, 0)),
            pl.BlockSpec((None, gw, d), lambda b, i, g: (g, 0, 0)),
            pl.BlockSpec((None, ROUTER_LANES, gw), lambda b, i, g: (g, 0, 0)),
            pl.BlockSpec((1, d), lambda b, i, g: (0, 0)),
        ],
        out_specs=tok(d),
        out_shape=jax.ShapeDtypeStruct(xn.shape, F32),
        scratch_shapes=[pltpu.VMEM((tm, gw), BF16)],
        compiler_params=_cparams("parallel", "parallel", "arbitrary"),
        name="moe_dense",
    )(xn, h, comb, mod, wg, wu, wd, expand, final_g.reshape(1, d))


def moe_params(e_gate, e_up, e_down):
    _, d, f = e_gate.shape

    def side_by_side(w):
        w = w.reshape(N_GROUPS, EXPERTS_PER_GROUP, d, f).transpose(0, 2, 1, 3)
        return w.reshape(N_GROUPS, d, EXPERTS_PER_GROUP * f).astype(BF16)

    wd = e_down.reshape(N_GROUPS, EXPERTS_PER_GROUP * f, d).astype(BF16)
    expand = np.zeros((N_GROUPS, ROUTER_LANES, EXPERTS_PER_GROUP * f), np.float32)
    for g in range(N_GROUPS):
        for e in range(EXPERTS_PER_GROUP):
            expand[g, EXPERT_LANE0 + g * EXPERTS_PER_GROUP + e, e * f:(e + 1) * f] = 1.0
    return side_by_side(e_gate), side_by_side(e_up), wd, jnp.asarray(expand, BF16)


def s5_params(a_re, a_im, log_dt, b_re, b_im, c_re, c_im, bsz):
    lam = lax.complex(a_re.astype(F32), a_im.astype(F32))
    dt = jnp.exp(log_dt.astype(F32))[..., None]
    lam_bar = jnp.exp(lam * dt)
    b_bar = ((lam_bar - 1) / lam)[..., None] * lax.complex(b_re.astype(F32), b_im.astype(F32))
    eye = jnp.eye(S5_GROUPS, dtype=F32)
    gc, gp = S5_GROUPS * S5_GROUP, S5_GROUPS * S5_STATE

    def in_mat(m):
        return jnp.einsum('dgpc,gh->dgchp', m, eye).reshape(2, gc, gp)

    def out_mat(m):
        return jnp.einsum('dgcp,gh->dgphc', m, eye).reshape(2, gp, gc)

    w_in = jnp.concatenate([in_mat(b_bar.real), in_mat(b_bar.imag)], axis=-1).astype(BF16)
    w_out = jnp.concatenate([out_mat(c_re.astype(F32)), -out_mat(c_im.astype(F32))], axis=1).astype(BF16)

    def tiles(v):
        t = v.reshape(2, gp // LANES, 1, LANES)
        t = jnp.broadcast_to(t, (2, gp // LANES, bsz, LANES))
        return jnp.concatenate([t[0], t[1]], axis=1)

    lam_flat = lam_bar.reshape(2, gp)
    return w_in, tiles(lam_flat.real), tiles(lam_flat.imag), w_out


TOKEN_TILE = 256
MOE_TOKEN_TILE = 512


def kernel(x, c, ctx, c_ctx, ada_w, ada_b, norm_mix_g, norm_ffn_g, w_in, sgu_norm_g, sgu_w, sgu_b, s5_a_re, s5_a_im, s5_log_dt, s5_b_re, s5_b_im, s5_c_re, s5_c_im, s5_d, s5_glu_w, na_rpb, w_br_a, w_br_b, w_br_c, w_out, router_group_w, router_group_b, router_expert_w, router_expert_b, exp_w_gate, exp_w_up, exp_w_down, final_norm_g):
    bsz, n_tok, d = x.shape
    n_ctx = ctx.shape[1]
    depth = ada_w.shape[0]
    assert bsz + 1 <= SUBLANES

    cc = jnp.concatenate([c, c_ctx[None], jnp.zeros((SUBLANES - bsz - 1, d), F32)], axis=0)
    mod_all = ada_modulation(cc, ada_w, ada_b)
    rope_tabs = rope_tables(n_tok)
    s_zero = jnp.zeros((2 * bsz, S5_LANES), F32)
    tm_c = min(TOKEN_TILE, n_ctx)
    tm_moe = min(MOE_TOKEN_TILE, n_tok)

    xc = ctx
    for l in range(depth):
        with_ctx_out = l < depth - 1
        mod = mod_all[l, :bsz].reshape(bsz, ADA_CHUNKS, d)
        mod_c = jnp.broadcast_to(mod_all[l, bsz].reshape(1, ADA_CHUNKS, d), (bsz, ADA_CHUNKS, d))
        w_in_l = w_in[l].astype(BF16)
        sgu_w_l = sgu_w[l].astype(BF16)
        sgu_bias = jnp.broadcast_to(sgu_b[l].astype(F32)[:, :, None], (SGU_GROUPS, SGU_CHUNK, SGU_CHUNK))
        s5_w_in, s5_lre, s5_lim, s5_w_out = s5_params(
            s5_a_re[l], s5_a_im[l], s5_log_dt[l], s5_b_re[l], s5_b_im[l], s5_c_re[l], s5_c_im[l], bsz)
        table = na_bias_table(na_rpb[l])
        r_w, r_b = router_params(router_group_w[l], router_group_b[l], router_expert_w[l], router_expert_b[l])
        wg, wu, wd, expand = moe_params(exp_w_gate[l], exp_w_up[l], exp_w_down[l])
        merge_w = (s5_d[l].astype(F32), s5_glu_w[l].astype(BF16), w_br_a[l].astype(BF16),
                   w_br_b[l].astype(BF16), w_br_c[l].astype(BF16), w_out[l].astype(BF16),
                   norm_ffn_g[l].astype(F32), r_w, r_b)

        ya_c, xb_c, q_c, k_c, v_c, gate_c = mixer_in(
            xc, mod_c, norm_mix_g[l], w_in_l, sgu_norm_g[l], sgu_w_l, sgu_bias, None, tm_c)
        ya_l, xb_l, q_l, k_l, v_l, gate_l = mixer_in(
            x, mod, norm_mix_g[l], w_in_l, sgu_norm_g[l], sgu_w_l, sgu_bias, rope_tabs, TOKEN_TILE)
        ycf, ycr, s_ctx = s5_scan(xb_c, s_zero, s5_w_in, s5_lre, s5_lim, s5_w_out)
        ylf, ylr, _ = s5_scan(xb_l, s_ctx, s5_w_in, s5_lre, s5_lim, s5_w_out)
        yc_l = neighbourhood_attention(q_l, k_l, v_l, k_c, v_c, table)
        xn, h2, comb = merge_and_route(x, ya_l, xb_l, ylf, ylr, yc_l, gate_l, mod, *merge_w, TOKEN_TILE)
        x = moe_dense(xn, h2, comb, mod, wg, wu, wd, expand, final_norm_g, tm_moe, not with_ctx_out)
        if with_ctx_out:
            yc_c = context_attention(q_c, k_c, v_c)
            xcn, hc2, comb_c = merge_and_route(xc, ya_c, xb_c, ycf, ycr, yc_c, gate_c, mod_c, *merge_w, tm_c)
            xc = moe_dense(xcn, hc2, comb_c, mod_c, wg, wu, wd, expand, final_norm_g, tm_c, False)
    return x
```

```python
import functools
import math

import jax
import jax.numpy as jnp
import numpy as np
from jax import lax
from jax.experimental import pallas as pl
from jax.experimental.pallas import tpu as pltpu

F32 = jnp.float32
BF16 = jnp.bfloat16

GRID_W = 64
N_BRANCH = 3
SGU_WIDTH = 512
SGU_GROUPS = 4
SGU_CHUNK = 128
S5_WIDTH = 384
S5_GROUP = 16
S5_GROUPS = S5_WIDTH // S5_GROUP
S5_STATE = 64
NA_HEADS = 8
NA_HEAD_DIM = 64
NA_WIDTH = NA_HEADS * NA_HEAD_DIM
NA_KR = 8
NA_KC = 16
ROPE_BASE = 10000.0
N_GROUPS = 4
EXPERTS_PER_GROUP = 8
N_EXPERTS = N_GROUPS * EXPERTS_PER_GROUP
TOP_K = 2
D_EXPERT = 256
ADA_CHUNKS = 6
EPS = 1e-6
NEG_INF = -1e30

LANES = 128
SUBLANES = 8
VMEM_LIMIT_BYTES = 56 * 1024 * 1024

S5_LANES = 2 * S5_GROUPS * S5_STATE
S5_SLABS = S5_LANES // LANES
S5_CHUNK = 128
S5_PITCH = S5_CHUNK + 4


def _cparams(*sem):
    return pltpu.CompilerParams(dimension_semantics=sem, vmem_limit_bytes=VMEM_LIMIT_BYTES)


def _full(shape):
    n = len(shape)
    return pl.BlockSpec(shape, lambda *_: (0,) * n)


def _ada_kernel(c_ref, w_ref, b_ref, o_ref):
    c = c_ref[...]
    s = c * jax.nn.sigmoid(c)
    o_ref[...] = jnp.dot(s, w_ref[...], preferred_element_type=F32) + b_ref[...]


def ada_modulation(cc, ada_w, ada_b):
    n_layers, d, n = ada_w.shape
    tn = 1536
    return pl.pallas_call(
        _ada_kernel,
        grid=(n_layers, n // tn),
        in_specs=[
            pl.BlockSpec((SUBLANES, d), lambda l, j: (0, 0)),
            pl.BlockSpec((None, d, tn), lambda l, j: (l, 0, j)),
            pl.BlockSpec((None, 1, tn), lambda l, j: (l, 0, j)),
        ],
        out_specs=pl.BlockSpec((None, SUBLANES, tn), lambda l, j: (l, 0, j)),
        out_shape=jax.ShapeDtypeStruct((n_layers, SUBLANES, n), F32),
        compiler_params=_cparams("parallel", "parallel"),
        name="ada_modulation",
    )(cc, ada_w, ada_b.reshape(n_layers, 1, n))


def _gelu(x):
    return jax.nn.gelu(x)


def _mixer_in_kernel(x_ref, mod_ref, g_ref, w_ref, lng_ref, sw_ref, sb_ref, *rest, rope):
    if rope:
        cos_ref, sin_ref, swap_ref, ya_ref, b_ref, q_ref, k_ref, v_ref, gate_ref = rest
    else:
        ya_ref, b_ref, q_ref, k_ref, v_ref, gate_ref = rest
    tm = x_ref.shape[0]
    xf = x_ref[...]
    ms = jnp.mean(xf * xf, axis=-1, keepdims=True)
    y = xf * lax.rsqrt(ms + EPS) * g_ref[...]
    h = y * (1.0 + mod_ref[1:2, :]) + mod_ref[0:1, :]
    hb = h.astype(BF16)

    def proj(lo, hi):
        return jnp.dot(hb, w_ref[:, lo:hi], preferred_element_type=F32)

    o1 = 2 * SGU_WIDTH
    o2 = o1 + S5_WIDTH
    oq, ok, ov = o2, o2 + NA_WIDTH, o2 + 2 * NA_WIDTH
    o3 = o2 + 3 * NA_WIDTH

    u = _gelu(proj(0, SGU_WIDTH))
    v = _gelu(proj(SGU_WIDTH, o1))
    vc = v - jnp.mean(v, axis=-1, keepdims=True)
    vn = vc * lax.rsqrt(jnp.mean(vc * vc, axis=-1, keepdims=True) + EPS) * lng_ref[...]
    vb = vn.astype(BF16)
    cw = SGU_WIDTH // SGU_GROUPS
    for c in range(tm // SGU_CHUNK):
        r0 = c * SGU_CHUNK
        for g in range(SGU_GROUPS):
            sp = jnp.dot(sw_ref[g], vb[r0:r0 + SGU_CHUNK, g * cw:(g + 1) * cw],
                         preferred_element_type=F32) + sb_ref[g]
            ya_ref[r0:r0 + SGU_CHUNK, g * cw:(g + 1) * cw] = (
                u[r0:r0 + SGU_CHUNK, g * cw:(g + 1) * cw] * sp).astype(ya_ref.dtype)

    b_ref[...] = proj(o1, o2).astype(b_ref.dtype)

    q = proj(oq, ok)
    k = proj(ok, ov)
    if rope:
        cos = cos_ref[...]
        sin = sin_ref[...]
        qs = jnp.dot(q.astype(BF16), swap_ref[...], preferred_element_type=F32)
        ks = jnp.dot(k.astype(BF16), swap_ref[...], preferred_element_type=F32)
        q = q * cos + qs * sin
        k = k * cos + ks * sin
    q_ref[...] = (q * (NA_HEAD_DIM ** -0.5)).astype(q_ref.dtype)
    k_ref[...] = k.astype(k_ref.dtype)
    v_ref[...] = proj(ov, o3).astype(v_ref.dtype)
    gate_ref[...] = proj(o3, o3 + N_BRANCH * x_ref.shape[1]).astype(gate_ref.dtype)


def mixer_in(x, mod, norm_g, w_in, sgu_norm_g, sgu_w, sgu_bias, rope_tabs, tm):
    bsz, n_tok, d = x.shape
    rope = rope_tabs is not None
    in_specs = [
        pl.BlockSpec((None, tm, d), lambda b, i: (b, i, 0)),
        pl.BlockSpec((None, ADA_CHUNKS, d), lambda b, i: (b, 0, 0)),
        _full((1, d)),
        _full(w_in.shape),
        _full((1, SGU_WIDTH)),
        _full(sgu_w.shape),
        _full(sgu_bias.shape),
    ]
    args = [x, mod, norm_g.reshape(1, d), w_in, sgu_norm_g.reshape(1, SGU_WIDTH), sgu_w, sgu_bias]
    if rope:
        cos_t, sin_t, swap = rope_tabs
        in_specs += [
            pl.BlockSpec((tm, NA_WIDTH), lambda b, i: (i, 0)),
            pl.BlockSpec((tm, NA_WIDTH), lambda b, i: (i, 0)),
            _full(swap.shape),
        ]
        args += [cos_t, sin_t, swap]

    def tok(width):
        return pl.BlockSpec((None, tm, width), lambda b, i: (b, i, 0))

    out_shapes = [
        jax.ShapeDtypeStruct((bsz, n_tok, SGU_WIDTH), BF16),
        jax.ShapeDtypeStruct((bsz, n_tok, S5_WIDTH), BF16),
        jax.ShapeDtypeStruct((bsz, n_tok, NA_WIDTH), BF16),
        jax.ShapeDtypeStruct((bsz, n_tok, NA_WIDTH), BF16),
        jax.ShapeDtypeStruct((bsz, n_tok, NA_WIDTH), BF16),
        jax.ShapeDtypeStruct((bsz, n_tok, N_BRANCH * d), BF16),
    ]
    out_specs = [tok(SGU_WIDTH), tok(S5_WIDTH), tok(NA_WIDTH), tok(NA_WIDTH), tok(NA_WIDTH), tok(N_BRANCH * d)]
    return pl.pallas_call(
        functools.partial(_mixer_in_kernel, rope=rope),
        grid=(bsz, n_tok // tm),
        in_specs=in_specs,
        out_specs=out_specs,
        out_shape=out_shapes,
        compiler_params=_cparams("parallel", "parallel"),
        name="mixer_in_rope" if rope else "mixer_in",
    )(*args)


def rope_tables(n_tok):
    pos = jnp.arange(n_tok)
    rows = (pos // GRID_W).astype(F32)
    cols = (pos % GRID_W).astype(F32)
    seg = NA_HEAD_DIM // 2
    half = seg // 2
    inv_freq = ROPE_BASE ** (-jnp.arange(half, dtype=F32) / half)
    ang_r = rows[:, None] * inv_freq
    ang_c = cols[:, None] * inv_freq
    cos = jnp.concatenate([jnp.cos(ang_r)] * 2 + [jnp.cos(ang_c)] * 2, axis=-1)
    sin = jnp.concatenate([-jnp.sin(ang_r), jnp.sin(ang_r), -jnp.sin(ang_c), jnp.sin(ang_c)], axis=-1)
    d = np.arange(NA_WIDTH)
    partner = np.where((d % seg) < half, d + half, d - half)
    swap = np.zeros((NA_WIDTH, NA_WIDTH), np.float32)
    swap[partner, d] = 1.0
    return jnp.tile(cos, (1, NA_HEADS)), jnp.tile(sin, (1, NA_HEADS)), jnp.asarray(swap, BF16)


def _s5_kernel(xf_ref, xr_ref, s0_ref, win_ref, lre_ref, lim_ref, wout_ref, rev_ref,
               yf_ref, yr_ref, send_ref, lhs_scr, st_scr, state_scr):
    bsz, tc, _ = xf_ref.shape
    pitch = S5_PITCH
    rows = bsz * pitch
    half = S5_SLABS // 2
    i = pl.program_id(0)

    @pl.when(i == 0)
    def _():
        state_scr[...] = s0_ref[...]
        lhs_scr[...] = jnp.zeros_like(lhs_scr)

    rev = rev_ref[...]
    for b in range(bsz):
        lhs_scr[0, b * pitch:b * pitch + tc, :] = xf_ref[b].astype(F32)
        lhs_scr[1, b * pitch:b * pitch + tc, :] = jnp.dot(rev, xr_ref[b], preferred_element_type=F32)

    for d in range(2):
        lhs = lhs_scr[d].astype(BF16)
        for k in range(half):
            res = jnp.dot(lhs, win_ref[d, :, 2 * k * LANES:(2 * k + 2) * LANES], preferred_element_type=F32)
            st_scr[2 * k, d * rows:(d + 1) * rows, :] = res[:, :LANES]
            st_scr[2 * k + 1, d * rows:(d + 1) * rows, :] = res[:, LANES:]

    def step(t, carry):
        out = [None] * S5_SLABS
        for c in range(half):
            sre, sim = carry[c], carry[c + half]
            idx = pl.ds(t, 2 * bsz, stride=pitch)
            lr = lre_ref[c]
            li = lim_ref[c]
            nre = lr * sre - li * sim + st_scr[c, idx, :]
            nim = lr * sim + li * sre + st_scr[c + half, idx, :]
            st_scr[c, idx, :] = nre
            st_scr[c + half, idx, :] = nim
            out[c], out[c + half] = nre, nim
        return tuple(out)

    init = tuple(state_scr[:, c * LANES:(c + 1) * LANES] for c in range(S5_SLABS))
    fin = lax.fori_loop(0, tc, step, init)
    for c in range(S5_SLABS):
        state_scr[:, c * LANES:(c + 1) * LANES] = fin[c]
    send_ref[...] = state_scr[...]

    for d in range(2):
        s_all = jnp.concatenate([st_scr[c, d * rows:(d + 1) * rows, :] for c in range(S5_SLABS)], axis=-1)
        y = jnp.dot(s_all.astype(BF16), wout_ref[d], preferred_element_type=F32)
        for b in range(bsz):
            yb = y[b * pitch:b * pitch + tc, :]
            if d == 0:
                yf_ref[b] = yb
            else:
                yr_ref[b] = jnp.dot(rev, yb.astype(BF16), preferred_element_type=F32)


def s5_scan(xb, s0, w_in, lam_re, lam_im, w_out):
    bsz, n_tok, width = xb.shape
    assert 2 * bsz == SUBLANES and n_tok % S5_CHUNK == 0
    n = n_tok // S5_CHUNK
    tc = S5_CHUNK
    rev = jnp.asarray(np.eye(tc, dtype=np.float32)[::-1], BF16)
    blk = (bsz, tc, width)
    return pl.pallas_call(
        _s5_kernel,
        grid=(n,),
        in_specs=[
            pl.BlockSpec(blk, lambda i: (0, i, 0)),
            pl.BlockSpec(blk, lambda i: (0, n - 1 - i, 0)),
            _full(s0.shape),
            _full(w_in.shape),
            _full(lam_re.shape),
            _full(lam_im.shape),
            _full(w_out.shape),
            _full(rev.shape),
        ],
        out_specs=[
            pl.BlockSpec(blk, lambda i: (0, i, 0)),
            pl.BlockSpec(blk, lambda i: (0, n - 1 - i, 0)),
            _full(s0.shape),
        ],
        out_shape=[
            jax.ShapeDtypeStruct(xb.shape, F32),
            jax.ShapeDtypeStruct(xb.shape, F32),
            jax.ShapeDtypeStruct(s0.shape, F32),
        ],
        scratch_shapes=[
            pltpu.VMEM((2, bsz * S5_PITCH, width), F32),
            pltpu.VMEM((S5_SLABS, 2 * bsz * S5_PITCH, LANES), F32),
            pltpu.VMEM(s0.shape, F32),
        ],
        compiler_params=_cparams("arbitrary"),
        name="s5_scan",
    )(xb, xb, s0, w_in, lam_re, lam_im, w_out, rev)


NA_ROWS_PER_STEP = 8
_NT = (((1,), (1,)), ((), ()))


def _na_head_pair(qp, kp, vp, kcp, vcp, bias_fn):
    lane_head = lax.broadcasted_iota(jnp.int32, qp.shape, 1) // NA_HEAD_DIM
    o_pair = None
    for hh in range(2):
        qm = jnp.where(lane_head == hh, qp, jnp.zeros_like(qp))
        s_ctx = lax.dot_general(qm, kcp, _NT, preferred_element_type=F32)
        m = jnp.max(s_ctx, axis=-1, keepdims=True)
        if kp is not None:
            s_win = lax.dot_general(qm, kp, _NT, preferred_element_type=F32) + bias_fn(hh)
            m = jnp.maximum(m, jnp.max(s_win, axis=-1, keepdims=True))
            e_win = jnp.exp(s_win - m)
        e_ctx = jnp.exp(s_ctx - m)
        den = jnp.sum(e_ctx, axis=-1, keepdims=True)
        o = jnp.dot(e_ctx.astype(BF16), vcp, preferred_element_type=F32)
        if kp is not None:
            den = den + jnp.sum(e_win, axis=-1, keepdims=True)
            o = o + jnp.dot(e_win.astype(BF16), vp, preferred_element_type=F32)
        o = o * (1.0 / den)
        o_pair = o if hh == 0 else jnp.where(lane_head == 0, o_pair, o)
    return o_pair


def _na_kernel(q_ref, k_ref, v_ref, kc_ref, vc_ref, tab_ref, o_ref, *, n_rows):
    i = pl.program_id(1)
    win = NA_KR * GRID_W

    def row_body(rr, carry):
        r = i * NA_ROWS_PER_STEP + rr
        kstart = jnp.clip(r - NA_KR // 2, 0, n_rows - NA_KR)
        d0 = kstart - r + (NA_KR - 1)
        q0 = pl.multiple_of(rr * GRID_W, GRID_W)
        k0 = pl.multiple_of(kstart * GRID_W, GRID_W)
        for p in range(NA_HEADS // 2):
            ls = slice(p * LANES, (p + 1) * LANES)

            def bias_fn(hh, p=p):
                return jnp.concatenate(
                    [tab_ref[2 * p + hh, d0 + 2 * jj] for jj in range(NA_KR // 2)], axis=-1)

            o_pair = _na_head_pair(q_ref[pl.ds(q0, GRID_W), ls], k_ref[pl.ds(k0, win), ls],
                                   v_ref[pl.ds(k0, win), ls], kc_ref[:, ls], vc_ref[:, ls], bias_fn)
            o_ref[pl.ds(q0, GRID_W), ls] = o_pair.astype(o_ref.dtype)
        return carry

    lax.fori_loop(0, NA_ROWS_PER_STEP, row_body, 0)


def na_bias_table(rpb):
    w = np.arange(GRID_W)
    col_start = np.clip(w - NA_KC // 2, 0, GRID_W - NA_KC)
    col_mask = (w[None, :] >= col_start[:, None]) & (w[None, :] < col_start[:, None] + NA_KC)
    d_col = np.clip(w[None, :] - w[:, None], -(NA_KC - 1), NA_KC - 1) + (NA_KC - 1)
    full = jnp.where(col_mask[None, None], rpb.astype(F32)[:, :, d_col], NEG_INF)
    return jnp.concatenate([full[:, :-1], full[:, 1:]], axis=-1)


def neighbourhood_attention(q, k, v, kc, vc, table):
    bsz, n_tok, width = q.shape
    n_ctx = kc.shape[1]
    n_rows = n_tok // GRID_W
    tq = NA_ROWS_PER_STEP * GRID_W
    return pl.pallas_call(
        functools.partial(_na_kernel, n_rows=n_rows),
        grid=(bsz, n_rows // NA_ROWS_PER_STEP),
        in_specs=[
            pl.BlockSpec((None, tq, width), lambda b, i: (b, i, 0)),
            pl.BlockSpec((None, n_tok, width), lambda b, i: (b, 0, 0)),
            pl.BlockSpec((None, n_tok, width), lambda b, i: (b, 0, 0)),
            pl.BlockSpec((None, n_ctx, width), lambda b, i: (b, 0, 0)),
            pl.BlockSpec((None, n_ctx, width), lambda b, i: (b, 0, 0)),
            _full(table.shape),
        ],
        out_specs=pl.BlockSpec((None, tq, width), lambda b, i: (b, i, 0)),
        out_shape=jax.ShapeDtypeStruct(q.shape, BF16),
        compiler_params=_cparams("parallel", "arbitrary"),
        name="neighbourhood_attention",
    )(q, k, v, kc, vc, table)


def _ctx_attn_kernel(q_ref, k_ref, v_ref, o_ref):
    for p in range(NA_HEADS // 2):
        ls = slice(p * LANES, (p + 1) * LANES)
        o_pair = _na_head_pair(q_ref[:, ls], None, None, k_ref[:, ls], v_ref[:, ls], None)
        o_ref[:, ls] = o_pair.astype(o_ref.dtype)


def context_attention(qc, kc, vc):
    bsz, n_ctx, width = qc.shape
    spec = pl.BlockSpec((None, n_ctx, width), lambda b: (b, 0, 0))
    return pl.pallas_call(
        _ctx_attn_kernel,
        grid=(bsz,),
        in_specs=[spec, spec, spec],
        out_specs=spec,
        out_shape=jax.ShapeDtypeStruct(qc.shape, BF16),
        compiler_params=_cparams("parallel"),
        name="context_attention",
    )(qc, kc, vc)


ROUTER_LANES = LANES
EXPERT_LANE0 = N_GROUPS
GROUP_ID_LANE = 0
RANK_LANE = 1


def _route(logits):
    lane = lax.broadcasted_iota(jnp.int32, logits.shape, 1)
    big = jnp.int32(ROUTER_LANES)
    is_g = lane < N_GROUPS
    lg = jnp.where(is_g, logits, -jnp.inf)
    mg = jnp.max(lg, axis=-1, keepdims=True)
    grp = jnp.min(jnp.where(lg == mg, lane, big), axis=-1, keepdims=True)
    g_weight = 1.0 / jnp.sum(jnp.where(is_g, jnp.exp(logits - mg), 0.0), axis=-1, keepdims=True)
    e_idx = lane - EXPERT_LANE0
    sel = (e_idx >= 0) & (e_idx < N_EXPERTS) & ((e_idx // EXPERTS_PER_GROUP) == grp)
    ls1 = jnp.where(sel, logits, -jnp.inf)
    v1 = jnp.max(ls1, axis=-1, keepdims=True)
    i1 = jnp.min(jnp.where(ls1 == v1, lane, big), axis=-1, keepdims=True)
    ls2 = jnp.where(lane == i1, -jnp.inf, ls1)
    v2 = jnp.max(ls2, axis=-1, keepdims=True)
    i2 = jnp.min(jnp.where(ls2 == v2, lane, big), axis=-1, keepdims=True)
    e2 = jnp.exp(v2 - v1)
    w1 = 1.0 / (1.0 + e2)
    w2 = e2 * w1
    comb = g_weight * (jnp.where(lane == i1, w1, 0.0) + jnp.where(lane == i2, w2, 0.0))
    return jnp.where(lane == GROUP_ID_LANE, grp.astype(F32), comb)


def _merge_kernel(x_ref, ya_ref, xb_ref, yf_ref, yr_ref, yc_ref, gate_ref, mod_ref, d_ref, glu_ref,
                  wa_ref, wb_ref, wc_ref, wo_ref, g_ref, rw_ref, rb_ref, xn_ref, h_ref, comb_ref):
    d = x_ref.shape[1]
    yb = d_ref[...] * xb_ref[...].astype(F32) + yf_ref[...] + yr_ref[...]
    yb = _gelu(yb)
    yb = yb * jax.nn.sigmoid(jnp.dot(yb.astype(BF16), glu_ref[...], preferred_element_type=F32))

    def gate(j):
        return jax.nn.sigmoid(gate_ref[:, j * d:(j + 1) * d].astype(F32))

    m = gate(0) * jnp.dot(ya_ref[...], wa_ref[...], preferred_element_type=F32)
    m = m + gate(1) * jnp.dot(yb.astype(BF16), wb_ref[...], preferred_element_type=F32)
    m = m + gate(2) * jnp.dot(yc_ref[...], wc_ref[...], preferred_element_type=F32)
    xn = x_ref[...] + mod_ref[2:3, :] * jnp.dot(m.astype(BF16), wo_ref[...], preferred_element_type=F32)
    xn_ref[...] = xn
    ms = jnp.mean(xn * xn, axis=-1, keepdims=True)
    h = xn * lax.rsqrt(ms + EPS) * g_ref[...]
    h = h * (1.0 + mod_ref[4:5, :]) + mod_ref[3:4, :]
    h_ref[...] = h.astype(h_ref.dtype)
    logits = jnp.dot(h, rw_ref[...], preferred_element_type=F32) + rb_ref[...]
    comb_ref[...] = _route(logits)


def merge_and_route(x, ya, xb, yf, yr, yc, gates, mod, s5_d, glu_w, w_br_a, w_br_b, w_br_c, w_out,
                    norm_ffn_g, router_w, router_b, tm):
    bsz, n_tok, d = x.shape

    def tok(width):
        return pl.BlockSpec((None, tm, width), lambda b, i: (b, i, 0))

    weights = [s5_d.reshape(1, S5_WIDTH), glu_w, w_br_a, w_br_b, w_br_c, w_out,
               norm_ffn_g.reshape(1, d), router_w, router_b]
    return pl.pallas_call(
        _merge_kernel,
        grid=(bsz, n_tok // tm),
        in_specs=[tok(d), tok(SGU_WIDTH), tok(S5_WIDTH), tok(S5_WIDTH), tok(S5_WIDTH), tok(NA_WIDTH),
                  tok(N_BRANCH * d), pl.BlockSpec((None, ADA_CHUNKS, d), lambda b, i: (b, 0, 0))]
        + [_full(w.shape) for w in weights],
        out_specs=[tok(d), tok(d), tok(ROUTER_LANES)],
        out_shape=[
            jax.ShapeDtypeStruct(x.shape, F32),
            jax.ShapeDtypeStruct(x.shape, BF16),
            jax.ShapeDtypeStruct((bsz, n_tok, ROUTER_LANES), F32),
        ],
        compiler_params=_cparams("parallel", "parallel"),
        name="merge_and_route",
    )(x, ya, xb, yf, yr, yc, gates, mod, *weights)


def router_params(rg_w, rg_b, re_w, re_b):
    d = rg_w.shape[0]
    pad = ROUTER_LANES - N_GROUPS - N_EXPERTS
    w = jnp.concatenate([rg_w, re_w, jnp.zeros((d, pad), F32)], axis=1).astype(F32)
    b = jnp.concatenate([rg_b, re_b, jnp.zeros((pad,), F32)]).astype(F32).reshape(1, ROUTER_LANES)
    return w, b


MOE_BLOCK = 128


def _split_bf16(x):
    hi = x.astype(BF16)
    return hi, (x - hi.astype(F32)).astype(BF16)


def _moe_kernel(xn_ref, h_ref, comb_ref, mod_ref, wg_ref, wu_ref, wd_ref, exp_ref, fg_ref, o_ref,
                aux_col, aux_row, cnt_ref, *, final_norm):
    g = pl.program_id(2)
    tm = h_ref.shape[0]

    @pl.when(g == 0)
    def _():
        o_ref[...] = xn_ref[...]
        comb = comb_ref[...]
        lane = lax.broadcasted_iota(jnp.int32, comb.shape, 1)
        grp = comb[:, GROUP_ID_LANE:GROUP_ID_LANE + 1]
        onehot = jnp.where(lane < N_GROUPS, jnp.where(lane.astype(F32) == grp, 1.0, 0.0), 0.0)
        row_i = lax.broadcasted_iota(jnp.int32, (tm, tm), 0)
        col_i = lax.broadcasted_iota(jnp.int32, (tm, tm), 1)
        tri = jnp.where(col_i < row_i, 1.0, 0.0).astype(BF16)
        ranks = jnp.dot(tri, onehot.astype(BF16), preferred_element_type=F32)
        own = jnp.sum(onehot * ranks, axis=-1, keepdims=True)
        aux = jnp.where(lane == GROUP_ID_LANE, grp, jnp.where(lane == RANK_LANE, own, 0.0))
        aux_col[...] = aux
        aux_row[...] = aux.T
        for gg in range(N_GROUPS):
            cnt_ref[gg] = jnp.sum(onehot[:, gg:gg + 1]).astype(jnp.int32)

    gf = g.astype(F32)
    rank_row = jnp.where(aux_row[GROUP_ID_LANE:GROUP_ID_LANE + 1, :] == gf,
                         aux_row[RANK_LANE:RANK_LANE + 1, :], -1.0)
    rank_col = jnp.where(aux_col[:, GROUP_ID_LANE:GROUP_ID_LANE + 1] == gf,
                         aux_col[:, RANK_LANE:RANK_LANE + 1], -1.0)
    n_blocks = (cnt_ref[g] + MOE_BLOCK - 1) // MOE_BLOCK
    slot_r = lax.broadcasted_iota(jnp.int32, (MOE_BLOCK, tm), 0).astype(F32)
    slot_c = lax.broadcasted_iota(jnp.int32, (tm, MOE_BLOCK), 1).astype(F32)
    scale = mod_ref[5:6, :]

    def block(j, carry):
        base = (j * MOE_BLOCK).astype(F32)
        gather = jnp.where(rank_row - base == slot_r, 1.0, 0.0).astype(BF16)
        hc = jnp.dot(gather, h_ref[...], preferred_element_type=F32).astype(BF16)
        comb_hi, comb_lo = _split_bf16(comb_ref[...])
        cc = (jnp.dot(gather, comb_hi, preferred_element_type=F32)
              + jnp.dot(gather, comb_lo, preferred_element_type=F32))
        cc_hi, cc_lo = _split_bf16(cc)
        a = jnp.dot(hc, wg_ref[...], preferred_element_type=F32)
        u = jnp.dot(hc, wu_ref[...], preferred_element_type=F32)
        ce = (jnp.dot(cc_hi, exp_ref[...], preferred_element_type=F32)
              + jnp.dot(cc_lo, exp_ref[...], preferred_element_type=F32))
        hid = (a * jax.nn.sigmoid(a) * u * ce).astype(BF16)
        oc = jnp.dot(hid, wd_ref[...], preferred_element_type=F32).astype(BF16)
        scatter = jnp.where(rank_col - base == slot_c, 1.0, 0.0).astype(BF16)
        o_ref[...] += scale * jnp.dot(scatter, oc, preferred_element_type=F32)
        return carry

    lax.fori_loop(0, n_blocks, block, 0)

    if final_norm:
        @pl.when(g == pl.num_programs(2) - 1)
        def _():
            xo = o_ref[...]
            ms = jnp.mean(xo * xo, axis=-1, keepdims=True)
            o_ref[...] = xo * lax.rsqrt(ms + EPS) * fg_ref[...]


def moe_grouped(xn, h, comb, mod, wg, wu, wd, expand, final_g, tm, final_norm):
    bsz, n_tok, d = xn.shape
    gw = EXPERTS_PER_GROUP * D_EXPERT

    def tok(width):
        return pl.BlockSpec((None, tm, width), lambda b, i, g: (b, i, 0))

    return pl.pallas_call(
        functools.partial(_moe_kernel, final_norm=final_norm),
        grid=(bsz, n_tok // tm, N_GROUPS),
        in_specs=[
            tok(d), tok(d), tok(ROUTER_LANES),
            pl.BlockSpec((None, ADA_CHUNKS, d), lambda b, i, g: (b, 0, 0)),
            pl.BlockSpec((None, d, gw), lambda b, i, g: (g, 0, 0)),
            pl.BlockSpec((None, d, gw), lambda b, i, g: (g, 0, 0)),
            pl.BlockSpec((None, gw, d), lambda b, i, g: (g, 0, 0)),
            pl.BlockSpec((None, ROUTER_LANES, gw), lambda b, i, g: (g, 0, 0)),
            pl.BlockSpec((1, d), lambda b, i, g: (0, 0)),
        ],
        out_specs=tok(d),
        out_shape=jax.ShapeDtypeStruct(xn.shape, F32),
        scratch_shapes=[
            pltpu.VMEM((tm, ROUTER_LANES), F32),
            pltpu.VMEM((ROUTER_LANES, tm), F32),
            pltpu.SMEM((N_GROUPS,), jnp.int32),
        ],
        compiler_params=_cparams("parallel", "parallel", "arbitrary"),
        name="moe_grouped",
    )(xn, h, comb, mod, wg, wu, wd, expand, final_g.reshape(1, d))


def moe_params(e_gate, e_up, e_down):
    _, d, f = e_gate.shape

    def side_by_side(w):
        w = w.reshape(N_GROUPS, EXPERTS_PER_GROUP, d, f).transpose(0, 2, 1, 3)
        return w.reshape(N_GROUPS, d, EXPERTS_PER_GROUP * f).astype(BF16)

    wd = e_down.reshape(N_GROUPS, EXPERTS_PER_GROUP * f, d).astype(BF16)
    expand = np.zeros((N_GROUPS, ROUTER_LANES, EXPERTS_PER_GROUP * f), np.float32)
    for g in range(N_GROUPS):
        for e in range(EXPERTS_PER_GROUP):
            expand[g, EXPERT_LANE0 + g * EXPERTS_PER_GROUP + e, e * f:(e + 1) * f] = 1.0
    return side_by_side(e_gate), side_by_side(e_up), wd, jnp.asarray(expand, BF16)


def s5_params(a_re, a_im, log_dt, b_re, b_im, c_re, c_im, bsz):
    lam = lax.complex(a_re.astype(F32), a_im.astype(F32))
    dt = jnp.exp(log_dt.astype(F32))[..., None]
    lam_bar = jnp.exp(lam * dt)
    b_bar = ((lam_bar - 1) / lam)[..., None] * lax.complex(b_re.astype(F32), b_im.astype(F32))
    eye = jnp.eye(S5_GROUPS, dtype=F32)
    gc, gp = S5_GROUPS * S5_GROUP, S5_GROUPS * S5_STATE

    def in_mat(m):
        return jnp.einsum('dgpc,gh->dgchp', m, eye).reshape(2, gc, gp)

    def out_mat(m):
        return jnp.einsum('dgcp,gh->dgphc', m, eye).reshape(2, gp, gc)

    w_in = jnp.concatenate([in_mat(b_bar.real), in_mat(b_bar.imag)], axis=-1).astype(BF16)
    w_out = jnp.concatenate([out_mat(c_re.astype(F32)), -out_mat(c_im.astype(F32))], axis=1).astype(BF16)

    def tiles(v):
        t = v.reshape(2, gp // LANES, 1, LANES)
        t = jnp.broadcast_to(t, (2, gp // LANES, bsz, LANES))
        return jnp.concatenate([t[0], t[1]], axis=1)

    lam_flat = lam_bar.reshape(2, gp)
    return w_in, tiles(lam_flat.real), tiles(lam_flat.imag), w_out


TOKEN_TILE = 256
MOE_TOKEN_TILE = 1024


def kernel(x, c, ctx, c_ctx, ada_w, ada_b, norm_mix_g, norm_ffn_g, w_in, sgu_norm_g, sgu_w, sgu_b, s5_a_re, s5_a_im, s5_log_dt, s5_b_re, s5_b_im, s5_c_re, s5_c_im, s5_d, s5_glu_w, na_rpb, w_br_a, w_br_b, w_br_c, w_out, router_group_w, router_group_b, router_expert_w, router_expert_b, exp_w_gate, exp_w_up, exp_w_down, final_norm_g):
    bsz, n_tok, d = x.shape
    n_ctx = ctx.shape[1]
    depth = ada_w.shape[0]
    assert bsz + 1 <= SUBLANES

    cc = jnp.concatenate([c, c_ctx[None], jnp.zeros((SUBLANES - bsz - 1, d), F32)], axis=0)
    mod_all = ada_modulation(cc, ada_w, ada_b)
    rope_tabs = rope_tables(n_tok)
    s_zero = jnp.zeros((2 * bsz, S5_LANES), F32)
    tm_c = min(TOKEN_TILE, n_ctx)
    tm_moe = min(MOE_TOKEN_TILE, n_tok)

    xc = ctx
    for l in range(depth):
        with_ctx_out = l < depth - 1
        mod = mod_all[l, :bsz].reshape(bsz, ADA_CHUNKS, d)
        mod_c = jnp.broadcast_to(mod_all[l, bsz].reshape(1, ADA_CHUNKS, d), (bsz, ADA_CHUNKS, d))
        w_in_l = w_in[l].astype(BF16)
        sgu_w_l = sgu_w[l].astype(BF16)
        sgu_bias = jnp.broadcast_to(sgu_b[l].astype(F32)[:, :, None], (SGU_GROUPS, SGU_CHUNK, SGU_CHUNK))
        s5_w_in, s5_lre, s5_lim, s5_w_out = s5_params(
            s5_a_re[l], s5_a_im[l], s5_log_dt[l], s5_b_re[l], s5_b_im[l], s5_c_re[l], s5_c_im[l], bsz)
        table = na_bias_table(na_rpb[l])
        r_w, r_b = router_params(router_group_w[l], router_group_b[l], router_expert_w[l], router_expert_b[l])
        wg, wu, wd, expand = moe_params(exp_w_gate[l], exp_w_up[l], exp_w_down[l])
        merge_w = (s5_d[l].astype(F32), s5_glu_w[l].astype(BF16), w_br_a[l].astype(BF16),
                   w_br_b[l].astype(BF16), w_br_c[l].astype(BF16), w_out[l].astype(BF16),
                   norm_ffn_g[l].astype(F32), r_w, r_b)

        ya_c, xb_c, q_c, k_c, v_c, gate_c = mixer_in(
            xc, mod_c, norm_mix_g[l], w_in_l, sgu_norm_g[l], sgu_w_l, sgu_bias, None, tm_c)
        ya_l, xb_l, q_l, k_l, v_l, gate_l = mixer_in(
            x, mod, norm_mix_g[l], w_in_l, sgu_norm_g[l], sgu_w_l, sgu_bias, rope_tabs, TOKEN_TILE)
        ycf, ycr, s_ctx = s5_scan(xb_c, s_zero, s5_w_in, s5_lre, s5_lim, s5_w_out)
        ylf, ylr, _ = s5_scan(xb_l, s_ctx, s5_w_in, s5_lre, s5_lim, s5_w_out)
        yc_l = neighbourhood_attention(q_l, k_l, v_l, k_c, v_c, table)
        xn, h2, comb = merge_and_route(x, ya_l, xb_l, ylf, ylr, yc_l, gate_l, mod, *merge_w, TOKEN_TILE)
        x = moe_grouped(xn, h2, comb, mod, wg, wu, wd, expand, final_norm_g, tm_moe, not with_ctx_out)
        if with_ctx_out:
            yc_c = context_attention(q_c, k_c, v_c)
            xcn, hc2, comb_c = merge_and_route(xc, ya_c, xb_c, ycf, ycr, yc_c, gate_c, mod_c, *merge_w, tm_c)
            xc = moe_grouped(xcn, hc2, comb_c, mod_c, wg, wu, wd, expand, final_norm_g, tm_c, False)
    return x
```

```python
import functools
import math

import jax
import jax.numpy as jnp
import numpy as np
from jax import lax
from jax.experimental import pallas as pl
from jax.experimental.pallas import tpu as pltpu

F32 = jnp.float32
BF16 = jnp.bfloat16

GRID_W = 64
N_BRANCH = 3
SGU_WIDTH = 512
SGU_GROUPS = 4
SGU_CHUNK = 128
S5_WIDTH = 384
S5_GROUP = 16
S5_GROUPS = S5_WIDTH // S5_GROUP
S5_STATE = 64
NA_HEADS = 8
NA_HEAD_DIM = 64
NA_WIDTH = NA_HEADS * NA_HEAD_DIM
NA_KR = 8
NA_KC = 16
ROPE_BASE = 10000.0
N_GROUPS = 4
EXPERTS_PER_GROUP = 8
N_EXPERTS = N_GROUPS * EXPERTS_PER_GROUP
TOP_K = 2
D_EXPERT = 256
ADA_CHUNKS = 6
EPS = 1e-6
NEG_INF = -1e30

LANES = 128
SUBLANES = 8
VMEM_LIMIT_BYTES = 56 * 1024 * 1024

S5_LANES = 2 * S5_GROUPS * S5_STATE
S5_SLABS = S5_LANES // LANES
S5_CHUNK = 128
S5_PITCH = S5_CHUNK + 4


def _cparams(*sem):
    return pltpu.CompilerParams(dimension_semantics=sem, vmem_limit_bytes=VMEM_LIMIT_BYTES)


def _full(shape):
    n = len(shape)
    return pl.BlockSpec(shape, lambda *_: (0,) * n)


def _ada_kernel(c_ref, w_ref, b_ref, o_ref):
    c = c_ref[...]
    s = c * jax.nn.sigmoid(c)
    o_ref[...] = jnp.dot(s, w_ref[...], preferred_element_type=F32) + b_ref[...]


def ada_modulation(cc, ada_w, ada_b):
    n_layers, d, n = ada_w.shape
    tn = 1536
    return pl.pallas_call(
        _ada_kernel,
        grid=(n_layers, n // tn),
        in_specs=[
            pl.BlockSpec((SUBLANES, d), lambda l, j: (0, 0)),
            pl.BlockSpec((None, d, tn), lambda l, j: (l, 0, j)),
            pl.BlockSpec((None, 1, tn), lambda l, j: (l, 0, j)),
        ],
        out_specs=pl.BlockSpec((None, SUBLANES, tn), lambda l, j: (l, 0, j)),
        out_shape=jax.ShapeDtypeStruct((n_layers, SUBLANES, n), F32),
        compiler_params=_cparams("parallel", "parallel"),
        name="ada_modulation",
    )(cc, ada_w, ada_b.reshape(n_layers, 1, n))


def _gelu(x):
    return jax.nn.gelu(x)


def _mixer_in_kernel(x_ref, mod_ref, g_ref, w_ref, lng_ref, sw_ref, sb_ref, *rest, rope):
    if rope:
        cos_ref, sin_ref, swap_ref, ya_ref, b_ref, q_ref, k_ref, v_ref, gate_ref = rest
    else:
        ya_ref, b_ref, q_ref, k_ref, v_ref, gate_ref = rest
    tm = x_ref.shape[0]
    xf = x_ref[...]
    ms = jnp.mean(xf * xf, axis=-1, keepdims=True)
    y = xf * lax.rsqrt(ms + EPS) * g_ref[...]
    h = y * (1.0 + mod_ref[1:2, :]) + mod_ref[0:1, :]
    hb = h.astype(BF16)

    def proj(lo, hi):
        return jnp.dot(hb, w_ref[:, lo:hi], preferred_element_type=F32)

    o1 = 2 * SGU_WIDTH
    o2 = o1 + S5_WIDTH
    oq, ok, ov = o2, o2 + NA_WIDTH, o2 + 2 * NA_WIDTH
    o3 = o2 + 3 * NA_WIDTH

    u = _gelu(proj(0, SGU_WIDTH))
    v = _gelu(proj(SGU_WIDTH, o1))
    vc = v - jnp.mean(v, axis=-1, keepdims=True)
    vn = vc * lax.rsqrt(jnp.mean(vc * vc, axis=-1, keepdims=True) + EPS) * lng_ref[...]
    vb = vn.astype(BF16)
    cw = SGU_WIDTH // SGU_GROUPS
    for c in range(tm // SGU_CHUNK):
        r0 = c * SGU_CHUNK
        for g in range(SGU_GROUPS):
            sp = jnp.dot(sw_ref[g], vb[r0:r0 + SGU_CHUNK, g * cw:(g + 1) * cw],
                         preferred_element_type=F32) + sb_ref[g]
            ya_ref[r0:r0 + SGU_CHUNK, g * cw:(g + 1) * cw] = (
                u[r0:r0 + SGU_CHUNK, g * cw:(g + 1) * cw] * sp).astype(ya_ref.dtype)

    b_ref[...] = proj(o1, o2).astype(b_ref.dtype)

    q = proj(oq, ok)
    k = proj(ok, ov)
    if rope:
        cos = cos_ref[...]
        sin = sin_ref[...]
        qs = jnp.dot(q.astype(BF16), swap_ref[...], preferred_element_type=F32)
        ks = jnp.dot(k.astype(BF16), swap_ref[...], preferred_element_type=F32)
        q = q * cos + qs * sin
        k = k * cos + ks * sin
    q_ref[...] = (q * (NA_HEAD_DIM ** -0.5)).astype(q_ref.dtype)
    k_ref[...] = k.astype(k_ref.dtype)
    v_ref[...] = proj(ov, o3).astype(v_ref.dtype)
    gate_ref[...] = proj(o3, o3 + N_BRANCH * x_ref.shape[1]).astype(gate_ref.dtype)


def mixer_in(x, mod, norm_g, w_in, sgu_norm_g, sgu_w, sgu_bias, rope_tabs, tm):
    bsz, n_tok, d = x.shape
    rope = rope_tabs is not None
    in_specs = [
        pl.BlockSpec((None, tm, d), lambda b, i: (b, i, 0)),
        pl.BlockSpec((None, ADA_CHUNKS, d), lambda b, i: (b, 0, 0)),
        _full((1, d)),
        _full(w_in.shape),
        _full((1, SGU_WIDTH)),
        _full(sgu_w.shape),
        _full(sgu_bias.shape),
    ]
    args = [x, mod, norm_g.reshape(1, d), w_in, sgu_norm_g.reshape(1, SGU_WIDTH), sgu_w, sgu_bias]
    if rope:
        cos_t, sin_t, swap = rope_tabs
        in_specs += [
            pl.BlockSpec((tm, NA_WIDTH), lambda b, i: (i, 0)),
            pl.BlockSpec((tm, NA_WIDTH), lambda b, i: (i, 0)),
            _full(swap.shape),
        ]
        args += [cos_t, sin_t, swap]

    def tok(width):
        return pl.BlockSpec((None, tm, width), lambda b, i: (b, i, 0))

    out_shapes = [
        jax.ShapeDtypeStruct((bsz, n_tok, SGU_WIDTH), BF16),
        jax.ShapeDtypeStruct((bsz, n_tok, S5_WIDTH), BF16),
        jax.ShapeDtypeStruct((bsz, n_tok, NA_WIDTH), BF16),
        jax.ShapeDtypeStruct((bsz, n_tok, NA_WIDTH), BF16),
        jax.ShapeDtypeStruct((bsz, n_tok, NA_WIDTH), BF16),
        jax.ShapeDtypeStruct((bsz, n_tok, N_BRANCH * d), BF16),
    ]
    out_specs = [tok(SGU_WIDTH), tok(S5_WIDTH), tok(NA_WIDTH), tok(NA_WIDTH), tok(NA_WIDTH), tok(N_BRANCH * d)]
    return pl.pallas_call(
        functools.partial(_mixer_in_kernel, rope=rope),
        grid=(bsz, n_tok // tm),
        in_specs=in_specs,
        out_specs=out_specs,
        out_shape=out_shapes,
        compiler_params=_cparams("parallel", "parallel"),
        name="mixer_in_rope" if rope else "mixer_in",
    )(*args)


def rope_tables(n_tok):
    pos = jnp.arange(n_tok)
    rows = (pos // GRID_W).astype(F32)
    cols = (pos % GRID_W).astype(F32)
    seg = NA_HEAD_DIM // 2
    half = seg // 2
    inv_freq = ROPE_BASE ** (-jnp.arange(half, dtype=F32) / half)
    ang_r = rows[:, None] * inv_freq
    ang_c = cols[:, None] * inv_freq
    cos = jnp.concatenate([jnp.cos(ang_r)] * 2 + [jnp.cos(ang_c)] * 2, axis=-1)
    sin = jnp.concatenate([-jnp.sin(ang_r), jnp.sin(ang_r), -jnp.sin(ang_c), jnp.sin(ang_c)], axis=-1)
    d = np.arange(NA_WIDTH)
    partner = np.where((d % seg) < half, d + half, d - half)
    swap = np.zeros((NA_WIDTH, NA_WIDTH), np.float32)
    swap[partner, d] = 1.0
    return jnp.tile(cos, (1, NA_HEADS)), jnp.tile(sin, (1, NA_HEADS)), jnp.asarray(swap, BF16)


def _s5_kernel(xf_ref, xr_ref, s0_ref, win_ref, lre_ref, lim_ref, wout_ref, rev_ref,
               yf_ref, yr_ref, send_ref, lhs_scr, st_scr, state_scr):
    bsz, tc, _ = xf_ref.shape
    pitch = S5_PITCH
    rows = bsz * pitch
    half = S5_SLABS // 2
    i = pl.program_id(0)

    @pl.when(i == 0)
    def _():
        state_scr[...] = s0_ref[...]
        lhs_scr[...] = jnp.zeros_like(lhs_scr)

    rev = rev_ref[...]
    for b in range(bsz):
        lhs_scr[0, b * pitch:b * pitch + tc, :] = xf_ref[b].astype(F32)
        lhs_scr[1, b * pitch:b * pitch + tc, :] = jnp.dot(rev, xr_ref[b], preferred_element_type=F32)

    for d in range(2):
        lhs = lhs_scr[d].astype(BF16)
        for k in range(half):
            res = jnp.dot(lhs, win_ref[d, :, 2 * k * LANES:(2 * k + 2) * LANES], preferred_element_type=F32)
            st_scr[2 * k, d * rows:(d + 1) * rows, :] = res[:, :LANES]
            st_scr[2 * k + 1, d * rows:(d + 1) * rows, :] = res[:, LANES:]

    def step(t, carry):
        out = [None] * S5_SLABS
        for c in range(half):
            sre, sim = carry[c], carry[c + half]
            idx = pl.ds(t, 2 * bsz, stride=pitch)
            lr = lre_ref[c]
            li = lim_ref[c]
            nre = lr * sre - li * sim + st_scr[c, idx, :]
            nim = lr * sim + li * sre + st_scr[c + half, idx, :]
            st_scr[c, idx, :] = nre
            st_scr[c + half, idx, :] = nim
            out[c], out[c + half] = nre, nim
        return tuple(out)

    init = tuple(state_scr[:, c * LANES:(c + 1) * LANES] for c in range(S5_SLABS))
    fin = lax.fori_loop(0, tc, step, init)
    for c in range(S5_SLABS):
        state_scr[:, c * LANES:(c + 1) * LANES] = fin[c]
    send_ref[...] = state_scr[...]

    for d in range(2):
        s_all = jnp.concatenate([st_scr[c, d * rows:(d + 1) * rows, :] for c in range(S5_SLABS)], axis=-1)
        y = jnp.dot(s_all.astype(BF16), wout_ref[d], preferred_element_type=F32)
        for b in range(bsz):
            yb = y[b * pitch:b * pitch + tc, :]
            if d == 0:
                yf_ref[b] = yb
            else:
                yr_ref[b] = jnp.dot(rev, yb.astype(BF16), preferred_element_type=F32)


def s5_scan(xb, s0, w_in, lam_re, lam_im, w_out):
    bsz, n_tok, width = xb.shape
    assert 2 * bsz == SUBLANES and n_tok % S5_CHUNK == 0
    n = n_tok // S5_CHUNK
    tc = S5_CHUNK
    rev = jnp.asarray(np.eye(tc, dtype=np.float32)[::-1], BF16)
    blk = (bsz, tc, width)
    return pl.pallas_call(
        _s5_kernel,
        grid=(n,),
        in_specs=[
            pl.BlockSpec(blk, lambda i: (0, i, 0)),
            pl.BlockSpec(blk, lambda i: (0, n - 1 - i, 0)),
            _full(s0.shape),
            _full(w_in.shape),
            _full(lam_re.shape),
            _full(lam_im.shape),
            _full(w_out.shape),
            _full(rev.shape),
        ],
        out_specs=[
            pl.BlockSpec(blk, lambda i: (0, i, 0)),
            pl.BlockSpec(blk, lambda i: (0, n - 1 - i, 0)),
            _full(s0.shape),
        ],
        out_shape=[
            jax.ShapeDtypeStruct(xb.shape, F32),
            jax.ShapeDtypeStruct(xb.shape, F32),
            jax.ShapeDtypeStruct(s0.shape, F32),
        ],
        scratch_shapes=[
            pltpu.VMEM((2, bsz * S5_PITCH, width), F32),
            pltpu.VMEM((S5_SLABS, 2 * bsz * S5_PITCH, LANES), F32),
            pltpu.VMEM(s0.shape, F32),
        ],
        compiler_params=_cparams("arbitrary"),
        name="s5_scan",
    )(xb, xb, s0, w_in, lam_re, lam_im, w_out, rev)


NA_ROWS_PER_STEP = 4
_NT = (((1,), (1,)), ((), ()))


def _na_head_pair(qp, kp, vp, kcp, vcp, bias_fn):
    lane_head = lax.broadcasted_iota(jnp.int32, qp.shape, 1) // NA_HEAD_DIM
    o_pair = None
    for hh in range(2):
        qm = jnp.where(lane_head == hh, qp, jnp.zeros_like(qp))
        s_ctx = lax.dot_general(qm, kcp, _NT, preferred_element_type=F32)
        m = jnp.max(s_ctx, axis=-1, keepdims=True)
        if kp is not None:
            s_win = lax.dot_general(qm, kp, _NT, preferred_element_type=F32) + bias_fn(hh)
            m = jnp.maximum(m, jnp.max(s_win, axis=-1, keepdims=True))
            e_win = jnp.exp(s_win - m)
        e_ctx = jnp.exp(s_ctx - m)
        den = jnp.sum(e_ctx, axis=-1, keepdims=True)
        o = jnp.dot(e_ctx.astype(BF16), vcp, preferred_element_type=F32)
        if kp is not None:
            den = den + jnp.sum(e_win, axis=-1, keepdims=True)
            o = o + jnp.dot(e_win.astype(BF16), vp, preferred_element_type=F32)
        o = o * (1.0 / den)
        o_pair = o if hh == 0 else jnp.where(lane_head == 0, o_pair, o)
    return o_pair


NA_UNION_ROWS = 12


def _na_kernel(q_ref, k_ref, v_ref, kc_ref, vc_ref, tab_ref, mask_ref, o_ref, *, n_rows):
    i = pl.program_id(1)
    r0 = i * NA_ROWS_PER_STEP
    ks = jnp.clip(r0 - NA_KR // 2, 0, n_rows - NA_UNION_ROWS)
    k0 = pl.multiple_of(ks * GRID_W, GRID_W)
    n_keys = NA_UNION_ROWS * GRID_W
    n_q = NA_ROWS_PER_STEP * GRID_W
    n_pairs = NA_UNION_ROWS // 2

    tab_idx, mask_idx = [], []
    for rr in range(NA_ROWS_PER_STEP):
        r = r0 + rr
        kst = jnp.clip(r - NA_KR // 2, 0, n_rows - NA_KR)
        for jj in range(n_pairs):
            key0 = ks + 2 * jj
            out0 = jnp.logical_or(key0 < kst, key0 >= kst + NA_KR)
            out1 = jnp.logical_or(key0 + 1 < kst, key0 + 1 >= kst + NA_KR)
            tab_idx.append(jnp.clip(key0 - r + NA_KR, 0, 2 * NA_KR - 1))
            mask_idx.append(out0.astype(jnp.int32) + 2 * out1.astype(jnp.int32))

    for p in range(NA_HEADS // 2):
        ls = slice(p * LANES, (p + 1) * LANES)
        qp = q_ref[:, ls]
        lane_head = lax.broadcasted_iota(jnp.int32, qp.shape, 1) // NA_HEAD_DIM
        zero = jnp.zeros_like(qp)
        q_stack = jnp.concatenate([jnp.where(lane_head == 0, qp, zero), jnp.where(lane_head == 1, qp, zero)], axis=0)
        k_all = jnp.concatenate([k_ref[pl.ds(k0, n_keys), ls], kc_ref[:, ls]], axis=0)
        v_all = jnp.concatenate([v_ref[pl.ds(k0, n_keys), ls], vc_ref[:, ls]], axis=0)
        s = lax.dot_general(q_stack, k_all, _NT, preferred_element_type=F32)
        bias_rows = []
        for hh in range(2):
            for rr in range(NA_ROWS_PER_STEP):
                tiles = [tab_ref[2 * p + hh, tab_idx[rr * n_pairs + jj]] + mask_ref[mask_idx[rr * n_pairs + jj]]
                         for jj in range(n_pairs)]
                bias_rows.append(jnp.concatenate(tiles, axis=-1))
        bias = jnp.concatenate(bias_rows, axis=0)
        s_win = s[:, :n_keys] + bias
        s_ctx = s[:, n_keys:]
        m = jnp.maximum(jnp.max(s_win, axis=-1, keepdims=True), jnp.max(s_ctx, axis=-1, keepdims=True))
        e_win = jnp.exp(s_win - m)
        e_ctx = jnp.exp(s_ctx - m)
        den = jnp.sum(e_win, axis=-1, keepdims=True) + jnp.sum(e_ctx, axis=-1, keepdims=True)
        e_all = jnp.concatenate([e_win.astype(BF16), e_ctx.astype(BF16)], axis=-1)
        o = jnp.dot(e_all, v_all, preferred_element_type=F32) * (1.0 / den)
        o_ref[:, ls] = jnp.where(lane_head == 0, o[:n_q], o[n_q:]).astype(o_ref.dtype)


def na_bias_table(rpb):
    w = np.arange(GRID_W)
    col_start = np.clip(w - NA_KC // 2, 0, GRID_W - NA_KC)
    col_mask = (w[None, :] >= col_start[:, None]) & (w[None, :] < col_start[:, None] + NA_KC)
    d_col = np.clip(w[None, :] - w[:, None], -(NA_KC - 1), NA_KC - 1) + (NA_KC - 1)
    full = jnp.where(col_mask[None, None], rpb.astype(F32)[:, :, d_col], NEG_INF)
    pad = jnp.zeros_like(full[:, :1])
    table = jnp.concatenate([jnp.concatenate([pad, full], axis=1), jnp.concatenate([full, pad], axis=1)], axis=-1)
    half = np.zeros((4, GRID_W, 2 * GRID_W), np.float32)
    half[1, :, :GRID_W] = NEG_INF
    half[2, :, GRID_W:] = NEG_INF
    half[3] = NEG_INF
    return table, jnp.asarray(half)


def neighbourhood_attention(q, k, v, kc, vc, tables):
    bsz, n_tok, width = q.shape
    n_ctx = kc.shape[1]
    n_rows = n_tok // GRID_W
    assert n_rows >= NA_UNION_ROWS and n_rows % NA_ROWS_PER_STEP == 0
    table, half_masks = tables
    tq = NA_ROWS_PER_STEP * GRID_W
    return pl.pallas_call(
        functools.partial(_na_kernel, n_rows=n_rows),
        grid=(bsz, n_rows // NA_ROWS_PER_STEP),
        in_specs=[
            pl.BlockSpec((None, tq, width), lambda b, i: (b, i, 0)),
            pl.BlockSpec((None, n_tok, width), lambda b, i: (b, 0, 0)),
            pl.BlockSpec((None, n_tok, width), lambda b, i: (b, 0, 0)),
            pl.BlockSpec((None, n_ctx, width), lambda b, i: (b, 0, 0)),
            pl.BlockSpec((None, n_ctx, width), lambda b, i: (b, 0, 0)),
            _full(table.shape),
            _full(half_masks.shape),
        ],
        out_specs=pl.BlockSpec((None, tq, width), lambda b, i: (b, i, 0)),
        out_shape=jax.ShapeDtypeStruct(q.shape, BF16),
        compiler_params=_cparams("parallel", "arbitrary"),
        name="neighbourhood_attention",
    )(q, k, v, kc, vc, table, half_masks)


def _ctx_attn_kernel(q_ref, k_ref, v_ref, o_ref):
    for p in range(NA_HEADS // 2):
        ls = slice(p * LANES, (p + 1) * LANES)
        o_pair = _na_head_pair(q_ref[:, ls], None, None, k_ref[:, ls], v_ref[:, ls], None)
        o_ref[:, ls] = o_pair.astype(o_ref.dtype)


def context_attention(qc, kc, vc):
    bsz, n_ctx, width = qc.shape
    spec = pl.BlockSpec((None, n_ctx, width), lambda b: (b, 0, 0))
    return pl.pallas_call(
        _ctx_attn_kernel,
        grid=(bsz,),
        in_specs=[spec, spec, spec],
        out_specs=spec,
        out_shape=jax.ShapeDtypeStruct(qc.shape, BF16),
        compiler_params=_cparams("parallel"),
        name="context_attention",
    )(qc, kc, vc)


ROUTER_LANES = LANES
EXPERT_LANE0 = N_GROUPS
GROUP_ID_LANE = 0
RANK_LANE = 1


def _route(logits):
    lane = lax.broadcasted_iota(jnp.int32, logits.shape, 1)
    big = jnp.int32(ROUTER_LANES)
    is_g = lane < N_GROUPS
    lg = jnp.where(is_g, logits, -jnp.inf)
    mg = jnp.max(lg, axis=-1, keepdims=True)
    grp = jnp.min(jnp.where(lg == mg, lane, big), axis=-1, keepdims=True)
    g_weight = 1.0 / jnp.sum(jnp.where(is_g, jnp.exp(logits - mg), 0.0), axis=-1, keepdims=True)
    e_idx = lane - EXPERT_LANE0
    sel = (e_idx >= 0) & (e_idx < N_EXPERTS) & ((e_idx // EXPERTS_PER_GROUP) == grp)
    ls1 = jnp.where(sel, logits, -jnp.inf)
    v1 = jnp.max(ls1, axis=-1, keepdims=True)
    i1 = jnp.min(jnp.where(ls1 == v1, lane, big), axis=-1, keepdims=True)
    ls2 = jnp.where(lane == i1, -jnp.inf, ls1)
    v2 = jnp.max(ls2, axis=-1, keepdims=True)
    i2 = jnp.min(jnp.where(ls2 == v2, lane, big), axis=-1, keepdims=True)
    e2 = jnp.exp(v2 - v1)
    w1 = 1.0 / (1.0 + e2)
    w2 = e2 * w1
    comb = g_weight * (jnp.where(lane == i1, w1, 0.0) + jnp.where(lane == i2, w2, 0.0))
    return jnp.where(lane == GROUP_ID_LANE, grp.astype(F32), comb)


def _merge_kernel(x_ref, ya_ref, xb_ref, yf_ref, yr_ref, yc_ref, gate_ref, mod_ref, d_ref, glu_ref,
                  wa_ref, wb_ref, wc_ref, wo_ref, g_ref, rw_ref, rb_ref, xn_ref, h_ref, comb_ref):
    d = x_ref.shape[1]
    yb = d_ref[...] * xb_ref[...].astype(F32) + yf_ref[...] + yr_ref[...]
    yb = _gelu(yb)
    yb = yb * jax.nn.sigmoid(jnp.dot(yb.astype(BF16), glu_ref[...], preferred_element_type=F32))

    def gate(j):
        return jax.nn.sigmoid(gate_ref[:, j * d:(j + 1) * d].astype(F32))

    m = gate(0) * jnp.dot(ya_ref[...], wa_ref[...], preferred_element_type=F32)
    m = m + gate(1) * jnp.dot(yb.astype(BF16), wb_ref[...], preferred_element_type=F32)
    m = m + gate(2) * jnp.dot(yc_ref[...], wc_ref[...], preferred_element_type=F32)
    xn = x_ref[...] + mod_ref[2:3, :] * jnp.dot(m.astype(BF16), wo_ref[...], preferred_element_type=F32)
    xn_ref[...] = xn
    ms = jnp.mean(xn * xn, axis=-1, keepdims=True)
    h = xn * lax.rsqrt(ms + EPS) * g_ref[...]
    h = h * (1.0 + mod_ref[4:5, :]) + mod_ref[3:4, :]
    h_ref[...] = h.astype(h_ref.dtype)
    logits = jnp.dot(h, rw_ref[...], preferred_element_type=F32) + rb_ref[...]
    comb_ref[...] = _route(logits)


def merge_and_route(x, ya, xb, yf, yr, yc, gates, mod, s5_d, glu_w, w_br_a, w_br_b, w_br_c, w_out,
                    norm_ffn_g, router_w, router_b, tm):
    bsz, n_tok, d = x.shape

    def tok(width):
        return pl.BlockSpec((None, tm, width), lambda b, i: (b, i, 0))

    weights = [s5_d.reshape(1, S5_WIDTH), glu_w, w_br_a, w_br_b, w_br_c, w_out,
               norm_ffn_g.reshape(1, d), router_w, router_b]
    return pl.pallas_call(
        _merge_kernel,
        grid=(bsz, n_tok // tm),
        in_specs=[tok(d), tok(SGU_WIDTH), tok(S5_WIDTH), tok(S5_WIDTH), tok(S5_WIDTH), tok(NA_WIDTH),
                  tok(N_BRANCH * d), pl.BlockSpec((None, ADA_CHUNKS, d), lambda b, i: (b, 0, 0))]
        + [_full(w.shape) for w in weights],
        out_specs=[tok(d), tok(d), tok(ROUTER_LANES)],
        out_shape=[
            jax.ShapeDtypeStruct(x.shape, F32),
            jax.ShapeDtypeStruct(x.shape, BF16),
            jax.ShapeDtypeStruct((bsz, n_tok, ROUTER_LANES), F32),
        ],
        compiler_params=_cparams("parallel", "parallel"),
        name="merge_and_route",
    )(x, ya, xb, yf, yr, yc, gates, mod, *weights)


def router_params(rg_w, rg_b, re_w, re_b):
    d = rg_w.shape[0]
    pad = ROUTER_LANES - N_GROUPS - N_EXPERTS
    w = jnp.concatenate([rg_w, re_w, jnp.zeros((d, pad), F32)], axis=1).astype(F32)
    b = jnp.concatenate([rg_b, re_b, jnp.zeros((pad,), F32)]).astype(F32).reshape(1, ROUTER_LANES)
    return w, b


MOE_BLOCK = 128


def _split_bf16(x):
    hi = x.astype(BF16)
    return hi, (x - hi.astype(F32)).astype(BF16)


def _moe_kernel(xn_ref, h_ref, comb_ref, mod_ref, wg_ref, wu_ref, wd_ref, exp_ref, fg_ref, o_ref,
                aux_col, aux_row, cnt_ref, *, final_norm):
    g = pl.program_id(2)
    tm = h_ref.shape[0]

    @pl.when(g == 0)
    def _():
        o_ref[...] = xn_ref[...]
        comb = comb_ref[...]
        lane = lax.broadcasted_iota(jnp.int32, comb.shape, 1)
        grp = comb[:, GROUP_ID_LANE:GROUP_ID_LANE + 1]
        onehot = jnp.where(lane < N_GROUPS, jnp.where(lane.astype(F32) == grp, 1.0, 0.0), 0.0)
        row_i = lax.broadcasted_iota(jnp.int32, (tm, tm), 0)
        col_i = lax.broadcasted_iota(jnp.int32, (tm, tm), 1)
        tri = jnp.where(col_i < row_i, 1.0, 0.0).astype(BF16)
        ranks = jnp.dot(tri, onehot.astype(BF16), preferred_element_type=F32)
        own = jnp.sum(onehot * ranks, axis=-1, keepdims=True)
        aux = jnp.where(lane == GROUP_ID_LANE, grp, jnp.where(lane == RANK_LANE, own, 0.0))
        aux_col[...] = aux
        aux_row[...] = aux.T
        for gg in range(N_GROUPS):
            cnt_ref[gg] = jnp.sum(onehot[:, gg:gg + 1]).astype(jnp.int32)

    gf = g.astype(F32)
    rank_row = jnp.where(aux_row[GROUP_ID_LANE:GROUP_ID_LANE + 1, :] == gf,
                         aux_row[RANK_LANE:RANK_LANE + 1, :], -1.0)
    rank_col = jnp.where(aux_col[:, GROUP_ID_LANE:GROUP_ID_LANE + 1] == gf,
                         aux_col[:, RANK_LANE:RANK_LANE + 1], -1.0)
    n_blocks = (cnt_ref[g] + MOE_BLOCK - 1) // MOE_BLOCK
    slot_r = lax.broadcasted_iota(jnp.int32, (MOE_BLOCK, tm), 0).astype(F32)
    slot_c = lax.broadcasted_iota(jnp.int32, (tm, MOE_BLOCK), 1).astype(F32)
    scale = mod_ref[5:6, :]

    def block(j, carry):
        base = (j * MOE_BLOCK).astype(F32)
        gather = jnp.where(rank_row - base == slot_r, 1.0, 0.0).astype(BF16)
        hc = jnp.dot(gather, h_ref[...], preferred_element_type=F32).astype(BF16)
        comb_hi, comb_lo = _split_bf16(comb_ref[...])
        cc = (jnp.dot(gather, comb_hi, preferred_element_type=F32)
              + jnp.dot(gather, comb_lo, preferred_element_type=F32))
        cc_hi, cc_lo = _split_bf16(cc)
        a = jnp.dot(hc, wg_ref[...], preferred_element_type=F32)
        u = jnp.dot(hc, wu_ref[...], preferred_element_type=F32)
        ce = (jnp.dot(cc_hi, exp_ref[...], preferred_element_type=F32)
              + jnp.dot(cc_lo, exp_ref[...], preferred_element_type=F32))
        hid = (a * jax.nn.sigmoid(a) * u * ce).astype(BF16)
        oc = jnp.dot(hid, wd_ref[...], preferred_element_type=F32).astype(BF16)
        scatter = jnp.where(rank_col - base == slot_c, 1.0, 0.0).astype(BF16)
        o_ref[...] += scale * jnp.dot(scatter, oc, preferred_element_type=F32)
        return carry

    lax.fori_loop(0, n_blocks, block, 0)

    if final_norm:
        @pl.when(g == pl.num_programs(2) - 1)
        def _():
            xo = o_ref[...]
            ms = jnp.mean(xo * xo, axis=-1, keepdims=True)
            o_ref[...] = xo * lax.rsqrt(ms + EPS) * fg_ref[...]


def moe_grouped(xn, h, comb, mod, wg, wu, wd, expand, final_g, tm, final_norm):
    bsz, n_tok, d = xn.shape
    gw = EXPERTS_PER_GROUP * D_EXPERT

    def tok(width):
        return pl.BlockSpec((None, tm, width), lambda b, i, g: (b, i, 0))

    return pl.pallas_call(
        functools.partial(_moe_kernel, final_norm=final_norm),
        grid=(bsz, n_tok // tm, N_GROUPS),
        in_specs=[
            tok(d), tok(d), tok(ROUTER_LANES),
            pl.BlockSpec((None, ADA_CHUNKS, d), lambda b, i, g: (b, 0, 0)),
            pl.BlockSpec((None, d, gw), lambda b, i, g: (g, 0, 0)),
            pl.BlockSpec((None, d, gw), lambda b, i, g: (g, 0, 0)),
            pl.BlockSpec((None, gw, d), lambda b, i, g: (g, 0, 0)),
            pl.BlockSpec((None, ROUTER_LANES, gw), lambda b, i, g: (g, 0, 0)),
            pl.BlockSpec((1, d), lambda b, i, g: (0, 0)),
        ],
        out_specs=tok(d),
        out_shape=jax.ShapeDtypeStruct(xn.shape, F32),
        scratch_shapes=[
            pltpu.VMEM((tm, ROUTER_LANES), F32),
            pltpu.VMEM((ROUTER_LANES, tm), F32),
            pltpu.SMEM((N_GROUPS,), jnp.int32),
        ],
        compiler_params=_cparams("parallel", "parallel", "arbitrary"),
        name="moe_grouped",
    )(xn, h, comb, mod, wg, wu, wd, expand, final_g.reshape(1, d))


def moe_params(e_gate, e_up, e_down):
    _, d, f = e_gate.shape

    def side_by_side(w):
        w = w.reshape(N_GROUPS, EXPERTS_PER_GROUP, d, f).transpose(0, 2, 1, 3)
        return w.reshape(N_GROUPS, d, EXPERTS_PER_GROUP * f).astype(BF16)

    wd = e_down.reshape(N_GROUPS, EXPERTS_PER_GROUP * f, d).astype(BF16)
    expand = np.zeros((N_GROUPS, ROUTER_LANES, EXPERTS_PER_GROUP * f), np.float32)
    for g in range(N_GROUPS):
        for e in range(EXPERTS_PER_GROUP):
            expand[g, EXPERT_LANE0 + g * EXPERTS_PER_GROUP + e, e * f:(e + 1) * f] = 1.0
    return side_by_side(e_gate), side_by_side(e_up), wd, jnp.asarray(expand, BF16)


def s5_params(a_re, a_im, log_dt, b_re, b_im, c_re, c_im, bsz):
    lam = lax.complex(a_re.astype(F32), a_im.astype(F32))
    dt = jnp.exp(log_dt.astype(F32))[..., None]
    lam_bar = jnp.exp(lam * dt)
    b_bar = ((lam_bar - 1) / lam)[..., None] * lax.complex(b_re.astype(F32), b_im.astype(F32))
    eye = jnp.eye(S5_GROUPS, dtype=F32)
    gc, gp = S5_GROUPS * S5_GROUP, S5_GROUPS * S5_STATE

    def in_mat(m):
        return jnp.einsum('dgpc,gh->dgchp', m, eye).reshape(2, gc, gp)

    def out_mat(m):
        return jnp.einsum('dgcp,gh->dgphc', m, eye).reshape(2, gp, gc)

    w_in = jnp.concatenate([in_mat(b_bar.real), in_mat(b_bar.imag)], axis=-1).astype(BF16)
    w_out = jnp.concatenate([out_mat(c_re.astype(F32)), -out_mat(c_im.astype(F32))], axis=1).astype(BF16)

    def tiles(v):
        t = v.reshape(2, gp // LANES, 1, LANES)
        t = jnp.broadcast_to(t, (2, gp // LANES, bsz, LANES))
        return jnp.concatenate([t[0], t[1]], axis=1)

    lam_flat = lam_bar.reshape(2, gp)
    return w_in, tiles(lam_flat.real), tiles(lam_flat.imag), w_out


TOKEN_TILE = 256
MOE_TOKEN_TILE = 1024


def kernel(x, c, ctx, c_ctx, ada_w, ada_b, norm_mix_g, norm_ffn_g, w_in, sgu_norm_g, sgu_w, sgu_b, s5_a_re, s5_a_im, s5_log_dt, s5_b_re, s5_b_im, s5_c_re, s5_c_im, s5_d, s5_glu_w, na_rpb, w_br_a, w_br_b, w_br_c, w_out, router_group_w, router_group_b, router_expert_w, router_expert_b, exp_w_gate, exp_w_up, exp_w_down, final_norm_g):
    bsz, n_tok, d = x.shape
    n_ctx = ctx.shape[1]
    depth = ada_w.shape[0]
    assert bsz + 1 <= SUBLANES

    cc = jnp.concatenate([c, c_ctx[None], jnp.zeros((SUBLANES - bsz - 1, d), F32)], axis=0)
    mod_all = ada_modulation(cc, ada_w, ada_b)
    rope_tabs = rope_tables(n_tok)
    s_zero = jnp.zeros((2 * bsz, S5_LANES), F32)
    tm_c = min(TOKEN_TILE, n_ctx)
    tm_moe = min(MOE_TOKEN_TILE, n_tok)

    xc = ctx
    for l in range(depth):
        with_ctx_out = l < depth - 1
        mod = mod_all[l, :bsz].reshape(bsz, ADA_CHUNKS, d)
        mod_c = jnp.broadcast_to(mod_all[l, bsz].reshape(1, ADA_CHUNKS, d), (bsz, ADA_CHUNKS, d))
        w_in_l = w_in[l].astype(BF16)
        sgu_w_l = sgu_w[l].astype(BF16)
        sgu_bias = jnp.broadcast_to(sgu_b[l].astype(F32)[:, :, None], (SGU_GROUPS, SGU_CHUNK, SGU_CHUNK))
        s5_w_in, s5_lre, s5_lim, s5_w_out = s5_params(
            s5_a_re[l], s5_a_im[l], s5_log_dt[l], s5_b_re[l], s5_b_im[l], s5_c_re[l], s5_c_im[l], bsz)
        table = na_bias_table(na_rpb[l])
        r_w, r_b = router_params(router_group_w[l], router_group_b[l], router_expert_w[l], router_expert_b[l])
        wg, wu, wd, expand = moe_params(exp_w_gate[l], exp_w_up[l], exp_w_down[l])
        merge_w = (s5_d[l].astype(F32), s5_glu_w[l].astype(BF16), w_br_a[l].astype(BF16),
                   w_br_b[l].astype(BF16), w_br_c[l].astype(BF16), w_out[l].astype(BF16),
                   norm_ffn_g[l].astype(F32), r_w, r_b)

        ya_c, xb_c, q_c, k_c, v_c, gate_c = mixer_in(
            xc, mod_c, norm_mix_g[l], w_in_l, sgu_norm_g[l], sgu_w_l, sgu_bias, None, tm_c)
        ya_l, xb_l, q_l, k_l, v_l, gate_l = mixer_in(
            x, mod, norm_mix_g[l], w_in_l, sgu_norm_g[l], sgu_w_l, sgu_bias, rope_tabs, TOKEN_TILE)
        ycf, ycr, s_ctx = s5_scan(xb_c, s_zero, s5_w_in, s5_lre, s5_lim, s5_w_out)
        ylf, ylr, _ = s5_scan(xb_l, s_ctx, s5_w_in, s5_lre, s5_lim, s5_w_out)
        yc_l = neighbourhood_attention(q_l, k_l, v_l, k_c, v_c, table)
        xn, h2, comb = merge_and_route(x, ya_l, xb_l, ylf, ylr, yc_l, gate_l, mod, *merge_w, TOKEN_TILE)
        x = moe_grouped(xn, h2, comb, mod, wg, wu, wd, expand, final_norm_g, tm_moe, not with_ctx_out)
        if with_ctx_out:
            yc_c = context_attention(q_c, k_c, v_c)
            xcn, hc2, comb_c = merge_and_route(xc, ya_c, xb_c, ycf, ycr, yc_c, gate_c, mod_c, *merge_w, tm_c)
            xc = moe_grouped(xcn, hc2, comb_c, mod_c, wg, wu, wd, expand, final_norm_g, tm_c, False)
    return x
```

```python
import functools
import math

import jax
import jax.numpy as jnp
import numpy as np
from jax import lax
from jax.experimental import pallas as pl
from jax.experimental.pallas import tpu as pltpu

F32 = jnp.float32
BF16 = jnp.bfloat16

GRID_W = 64
N_BRANCH = 3
SGU_WIDTH = 512
SGU_GROUPS = 4
SGU_CHUNK = 128
S5_WIDTH = 384
S5_GROUP = 16
S5_GROUPS = S5_WIDTH // S5_GROUP
S5_STATE = 64
NA_HEADS = 8
NA_HEAD_DIM = 64
NA_WIDTH = NA_HEADS * NA_HEAD_DIM
NA_KR = 8
NA_KC = 16
ROPE_BASE = 10000.0
N_GROUPS = 4
EXPERTS_PER_GROUP = 8
N_EXPERTS = N_GROUPS * EXPERTS_PER_GROUP
TOP_K = 2
D_EXPERT = 256
ADA_CHUNKS = 6
EPS = 1e-6
NEG_INF = -1e30

LANES = 128
SUBLANES = 8
VMEM_LIMIT_BYTES = 56 * 1024 * 1024

S5_LANES = 2 * S5_GROUPS * S5_STATE
S5_SLABS = S5_LANES // LANES
S5_BLOCKS = S5_WIDTH // LANES
S5_CHUNK = 128
S5_PITCH = S5_CHUNK + 4


def _cparams(*sem):
    return pltpu.CompilerParams(dimension_semantics=sem, vmem_limit_bytes=VMEM_LIMIT_BYTES)


def _full(shape):
    n = len(shape)
    return pl.BlockSpec(shape, lambda *_: (0,) * n)


def _ada_kernel(c_ref, w_ref, b_ref, o_ref):
    c = c_ref[...]
    s = c * jax.nn.sigmoid(c)
    o_ref[...] = jnp.dot(s, w_ref[...], preferred_element_type=F32) + b_ref[...]


def ada_modulation(cc, ada_w, ada_b):
    n_layers, d, n = ada_w.shape
    tn = 1536
    return pl.pallas_call(
        _ada_kernel,
        grid=(n_layers, n // tn),
        in_specs=[
            pl.BlockSpec((SUBLANES, d), lambda l, j: (0, 0)),
            pl.BlockSpec((None, d, tn), lambda l, j: (l, 0, j)),
            pl.BlockSpec((None, 1, tn), lambda l, j: (l, 0, j)),
        ],
        out_specs=pl.BlockSpec((None, SUBLANES, tn), lambda l, j: (l, 0, j)),
        out_shape=jax.ShapeDtypeStruct((n_layers, SUBLANES, n), F32),
        compiler_params=_cparams("parallel", "parallel"),
        name="ada_modulation",
    )(cc, ada_w, ada_b.reshape(n_layers, 1, n))


def _gelu(x):
    return jax.nn.gelu(x)


def _mixer_in_kernel(x_ref, mod_ref, g_ref, w_ref, lng_ref, sw_ref, sb_ref, *rest, rope):
    if rope:
        cos_ref, sin_ref, swap_ref, ya_ref, b_ref, q_ref, k_ref, v_ref, gate_ref = rest
    else:
        ya_ref, b_ref, q_ref, k_ref, v_ref, gate_ref = rest
    tm = x_ref.shape[0]
    xf = x_ref[...]
    ms = jnp.mean(xf * xf, axis=-1, keepdims=True)
    y = xf * lax.rsqrt(ms + EPS) * g_ref[...]
    h = y * (1.0 + mod_ref[1:2, :]) + mod_ref[0:1, :]
    hb = h.astype(BF16)

    def proj(lo, hi):
        return jnp.dot(hb, w_ref[:, lo:hi], preferred_element_type=F32)

    o1 = 2 * SGU_WIDTH
    o2 = o1 + S5_WIDTH
    oq, ok, ov = o2, o2 + NA_WIDTH, o2 + 2 * NA_WIDTH
    o3 = o2 + 3 * NA_WIDTH

    u = _gelu(proj(0, SGU_WIDTH))
    v = _gelu(proj(SGU_WIDTH, o1))
    vc = v - jnp.mean(v, axis=-1, keepdims=True)
    vn = vc * lax.rsqrt(jnp.mean(vc * vc, axis=-1, keepdims=True) + EPS) * lng_ref[...]
    vb = vn.astype(BF16)
    cw = SGU_WIDTH // SGU_GROUPS
    for c in range(tm // SGU_CHUNK):
        r0 = c * SGU_CHUNK
        for g in range(SGU_GROUPS):
            sp = jnp.dot(sw_ref[g], vb[r0:r0 + SGU_CHUNK, g * cw:(g + 1) * cw],
                         preferred_element_type=F32) + sb_ref[g]
            ya_ref[r0:r0 + SGU_CHUNK, g * cw:(g + 1) * cw] = (
                u[r0:r0 + SGU_CHUNK, g * cw:(g + 1) * cw] * sp).astype(ya_ref.dtype)

    b_ref[...] = proj(o1, o2).astype(b_ref.dtype)

    q = proj(oq, ok)
    k = proj(ok, ov)
    if rope:
        cos = cos_ref[...]
        sin = sin_ref[...]

        def rotate(t):
            ts = jnp.dot(t.astype(BF16), swap_ref[...], preferred_element_type=F32)
            return jnp.concatenate(
                [t[:, j * LANES:(j + 1) * LANES] * cos + ts[:, j * LANES:(j + 1) * LANES] * sin
                 for j in range(NA_WIDTH // LANES)], axis=-1)

        q = rotate(q)
        k = rotate(k)
    q_ref[...] = (q * (NA_HEAD_DIM ** -0.5)).astype(q_ref.dtype)
    k_ref[...] = k.astype(k_ref.dtype)
    v_ref[...] = proj(ov, o3).astype(v_ref.dtype)
    gate_ref[...] = proj(o3, o3 + N_BRANCH * x_ref.shape[1]).astype(gate_ref.dtype)


def mixer_in(x, mod, norm_g, w_in, sgu_norm_g, sgu_w, sgu_bias, rope_tabs, tm):
    bsz, n_tok, d = x.shape
    rope = rope_tabs is not None
    in_specs = [
        pl.BlockSpec((None, tm, d), lambda b, i: (b, i, 0)),
        pl.BlockSpec((None, ADA_CHUNKS, d), lambda b, i: (b, 0, 0)),
        _full((1, d)),
        _full(w_in.shape),
        _full((1, SGU_WIDTH)),
        _full(sgu_w.shape),
        _full(sgu_bias.shape),
    ]
    args = [x, mod, norm_g.reshape(1, d), w_in, sgu_norm_g.reshape(1, SGU_WIDTH), sgu_w, sgu_bias]
    if rope:
        cos_t, sin_t, swap = rope_tabs
        in_specs += [
            pl.BlockSpec((tm, LANES), lambda b, i: (i, 0)),
            pl.BlockSpec((tm, LANES), lambda b, i: (i, 0)),
            _full(swap.shape),
        ]
        args += [cos_t, sin_t, swap]

    def tok(width):
        return pl.BlockSpec((None, tm, width), lambda b, i: (b, i, 0))

    out_shapes = [
        jax.ShapeDtypeStruct((bsz, n_tok, SGU_WIDTH), BF16),
        jax.ShapeDtypeStruct((bsz, n_tok, S5_WIDTH), BF16),
        jax.ShapeDtypeStruct((bsz, n_tok, NA_WIDTH), BF16),
        jax.ShapeDtypeStruct((bsz, n_tok, NA_WIDTH), BF16),
        jax.ShapeDtypeStruct((bsz, n_tok, NA_WIDTH), BF16),
        jax.ShapeDtypeStruct((bsz, n_tok, N_BRANCH * d), BF16),
    ]
    out_specs = [tok(SGU_WIDTH), tok(S5_WIDTH), tok(NA_WIDTH), tok(NA_WIDTH), tok(NA_WIDTH), tok(N_BRANCH * d)]
    return pl.pallas_call(
        functools.partial(_mixer_in_kernel, rope=rope),
        grid=(bsz, n_tok // tm),
        in_specs=in_specs,
        out_specs=out_specs,
        out_shape=out_shapes,
        compiler_params=_cparams("parallel", "parallel"),
        name="mixer_in_rope" if rope else "mixer_in",
    )(*args)


def rope_tables(n_tok):
    pos = np.arange(n_tok)
    rows = (pos // GRID_W).astype(np.float32)
    cols = (pos % GRID_W).astype(np.float32)
    seg = NA_HEAD_DIM // 2
    half = seg // 2
    inv_freq = (ROPE_BASE ** (-np.arange(half, dtype=np.float32) / half)).astype(np.float32)
    ang_r = rows[:, None] * inv_freq
    ang_c = cols[:, None] * inv_freq
    cos = np.concatenate([np.cos(ang_r)] * 2 + [np.cos(ang_c)] * 2, axis=-1)
    sin = np.concatenate([-np.sin(ang_r), np.sin(ang_r), -np.sin(ang_c), np.sin(ang_c)], axis=-1)
    d = np.arange(NA_WIDTH)
    partner = np.where((d % seg) < half, d + half, d - half)
    swap = np.zeros((NA_WIDTH, NA_WIDTH), np.float32)
    swap[partner, d] = 1.0
    reps = LANES // NA_HEAD_DIM
    return (jnp.asarray(np.tile(cos, (1, reps)), F32), jnp.asarray(np.tile(sin, (1, reps)), F32),
            jnp.asarray(swap, BF16))


def _s5_kernel(xf_ref, xr_ref, s0_ref, win_ref, lre_ref, lim_ref, wout_ref, rev_ref,
               yf_ref, yr_ref, send_ref, lhs_scr, st_scr, state_scr):
    bsz, tc, _ = xf_ref.shape
    pitch = S5_PITCH
    rows = bsz * pitch
    half = S5_SLABS // 2
    i = pl.program_id(0)

    @pl.when(i == 0)
    def _():
        state_scr[...] = s0_ref[...]
        lhs_scr[...] = jnp.zeros_like(lhs_scr)

    rev = rev_ref[...]
    for b in range(bsz):
        lhs_scr[0, b * pitch:b * pitch + tc, :] = xf_ref[b].astype(F32)
        lhs_scr[1, b * pitch:b * pitch + tc, :] = jnp.dot(rev, xr_ref[b], preferred_element_type=F32)

    bw = half // S5_BLOCKS
    for d in range(2):
        lhs = lhs_scr[d].astype(BF16)
        for cb in range(S5_BLOCKS):
            res = jnp.dot(lhs[:, cb * LANES:(cb + 1) * LANES], win_ref[d, cb], preferred_element_type=F32)
            for k in range(bw):
                st_scr[cb * bw + k, d * rows:(d + 1) * rows, :] = res[:, k * LANES:(k + 1) * LANES]
                st_scr[half + cb * bw + k, d * rows:(d + 1) * rows, :] = res[:, (bw + k) * LANES:(bw + k + 1) * LANES]

    def step(t, carry):
        out = [None] * S5_SLABS
        for c in range(half):
            sre, sim = carry[c], carry[c + half]
            idx = pl.ds(t, 2 * bsz, stride=pitch)
            lr = lre_ref[c]
            li = lim_ref[c]
            nre = lr * sre - li * sim + st_scr[c, idx, :]
            nim = lr * sim + li * sre + st_scr[c + half, idx, :]
            st_scr[c, idx, :] = nre
            st_scr[c + half, idx, :] = nim
            out[c], out[c + half] = nre, nim
        return tuple(out)

    init = tuple(state_scr[:, c * LANES:(c + 1) * LANES] for c in range(S5_SLABS))
    fin = lax.fori_loop(0, tc, step, init)
    for c in range(S5_SLABS):
        state_scr[:, c * LANES:(c + 1) * LANES] = fin[c]
    send_ref[...] = state_scr[...]

    for d in range(2):
        y_blocks = []
        for cb in range(S5_BLOCKS):
            slabs = [cb * bw + k for k in range(bw)] + [half + cb * bw + k for k in range(bw)]
            s_blk = jnp.concatenate([st_scr[c, d * rows:(d + 1) * rows, :] for c in slabs], axis=-1)
            y_blocks.append(jnp.dot(s_blk.astype(BF16), wout_ref[d, cb], preferred_element_type=F32))
        y = jnp.concatenate(y_blocks, axis=-1)
        for b in range(bsz):
            yb = y[b * pitch:b * pitch + tc, :]
            if d == 0:
                yf_ref[b] = yb
            else:
                yr_ref[b] = jnp.dot(rev, yb.astype(BF16), preferred_element_type=F32)


def s5_scan(xb, s0, w_in, lam_re, lam_im, w_out):
    bsz, n_tok, width = xb.shape
    assert 2 * bsz == SUBLANES and n_tok % S5_CHUNK == 0
    n = n_tok // S5_CHUNK
    tc = S5_CHUNK
    rev = jnp.asarray(np.eye(tc, dtype=np.float32)[::-1], BF16)
    blk = (bsz, tc, width)
    return pl.pallas_call(
        _s5_kernel,
        grid=(n,),
        in_specs=[
            pl.BlockSpec(blk, lambda i: (0, i, 0)),
            pl.BlockSpec(blk, lambda i: (0, n - 1 - i, 0)),
            _full(s0.shape),
            _full(w_in.shape),
            _full(lam_re.shape),
            _full(lam_im.shape),
            _full(w_out.shape),
            _full(rev.shape),
        ],
        out_specs=[
            pl.BlockSpec(blk, lambda i: (0, i, 0)),
            pl.BlockSpec(blk, lambda i: (0, n - 1 - i, 0)),
            _full(s0.shape),
        ],
        out_shape=[
            jax.ShapeDtypeStruct(xb.shape, F32),
            jax.ShapeDtypeStruct(xb.shape, F32),
            jax.ShapeDtypeStruct(s0.shape, F32),
        ],
        scratch_shapes=[
            pltpu.VMEM((2, bsz * S5_PITCH, width), F32),
            pltpu.VMEM((S5_SLABS, 2 * bsz * S5_PITCH, LANES), F32),
            pltpu.VMEM(s0.shape, F32),
        ],
        compiler_params=_cparams("arbitrary"),
        name="s5_scan",
    )(xb, xb, s0, w_in, lam_re, lam_im, w_out, rev)


NA_ROWS_PER_STEP = 4
_NT = (((1,), (1,)), ((), ()))


def _na_head_pair(qp, kp, vp, kcp, vcp, bias_fn):
    lane_head = lax.broadcasted_iota(jnp.int32, qp.shape, 1) // NA_HEAD_DIM
    o_pair = None
    for hh in range(2):
        qm = jnp.where(lane_head == hh, qp, jnp.zeros_like(qp))
        s_ctx = lax.dot_general(qm, kcp, _NT, preferred_element_type=F32)
        m = jnp.max(s_ctx, axis=-1, keepdims=True)
        if kp is not None:
            s_win = lax.dot_general(qm, kp, _NT, preferred_element_type=F32) + bias_fn(hh)
            m = jnp.maximum(m, jnp.max(s_win, axis=-1, keepdims=True))
            e_win = jnp.exp(s_win - m)
        e_ctx = jnp.exp(s_ctx - m)
        den = jnp.sum(e_ctx, axis=-1, keepdims=True)
        o = jnp.dot(e_ctx.astype(BF16), vcp, preferred_element_type=F32)
        if kp is not None:
            den = den + jnp.sum(e_win, axis=-1, keepdims=True)
            o = o + jnp.dot(e_win.astype(BF16), vp, preferred_element_type=F32)
        o = o * (1.0 / den)
        o_pair = o if hh == 0 else jnp.where(lane_head == 0, o_pair, o)
    return o_pair


NA_UNION_ROWS = 12


def _na_kernel(q_ref, k_ref, v_ref, kc_ref, vc_ref, tab_ref, mask_ref, o_ref, *, n_rows):
    i = pl.program_id(1)
    r0 = i * NA_ROWS_PER_STEP
    ks = jnp.clip(r0 - NA_KR // 2, 0, n_rows - NA_UNION_ROWS)
    k0 = pl.multiple_of(ks * GRID_W, GRID_W)
    n_keys = NA_UNION_ROWS * GRID_W
    n_q = NA_ROWS_PER_STEP * GRID_W
    n_pairs = NA_UNION_ROWS // 2

    tab_idx, mask_idx = [], []
    for rr in range(NA_ROWS_PER_STEP):
        r = r0 + rr
        kst = jnp.clip(r - NA_KR // 2, 0, n_rows - NA_KR)
        for jj in range(n_pairs):
            key0 = ks + 2 * jj
            out0 = jnp.logical_or(key0 < kst, key0 >= kst + NA_KR)
            out1 = jnp.logical_or(key0 + 1 < kst, key0 + 1 >= kst + NA_KR)
            tab_idx.append(jnp.clip(key0 - r + NA_KR, 0, 2 * NA_KR - 1))
            mask_idx.append(out0.astype(jnp.int32) + 2 * out1.astype(jnp.int32))

    for p in range(NA_HEADS // 2):
        ls = slice(p * LANES, (p + 1) * LANES)
        qp = q_ref[:, ls]
        lane_head = lax.broadcasted_iota(jnp.int32, qp.shape, 1) // NA_HEAD_DIM
        zero = jnp.zeros_like(qp)
        q_stack = jnp.concatenate([jnp.where(lane_head == 0, qp, zero), jnp.where(lane_head == 1, qp, zero)], axis=0)
        k_all = jnp.concatenate([k_ref[pl.ds(k0, n_keys), ls], kc_ref[:, ls]], axis=0)
        v_all = jnp.concatenate([v_ref[pl.ds(k0, n_keys), ls], vc_ref[:, ls]], axis=0)
        s = lax.dot_general(q_stack, k_all, _NT, preferred_element_type=F32)
        bias_rows = []
        for hh in range(2):
            for rr in range(NA_ROWS_PER_STEP):
                tiles = [tab_ref[2 * p + hh, tab_idx[rr * n_pairs + jj]] + mask_ref[mask_idx[rr * n_pairs + jj]]
                         for jj in range(n_pairs)]
                bias_rows.append(jnp.concatenate(tiles, axis=-1))
        bias = jnp.concatenate(bias_rows, axis=0)
        s_win = s[:, :n_keys] + bias
        s_ctx = s[:, n_keys:]
        m = jnp.maximum(jnp.max(s_win, axis=-1, keepdims=True), jnp.max(s_ctx, axis=-1, keepdims=True))
        e_win = jnp.exp(s_win - m)
        e_ctx = jnp.exp(s_ctx - m)
        den = jnp.sum(e_win, axis=-1, keepdims=True) + jnp.sum(e_ctx, axis=-1, keepdims=True)
        e_all = jnp.concatenate([e_win.astype(BF16), e_ctx.astype(BF16)], axis=-1)
        o = jnp.dot(e_all, v_all, preferred_element_type=F32) * (1.0 / den)
        o_ref[:, ls] = jnp.where(lane_head == 0, o[:n_q], o[n_q:]).astype(o_ref.dtype)


def na_bias_table(rpb):
    w = np.arange(GRID_W)
    col_start = np.clip(w - NA_KC // 2, 0, GRID_W - NA_KC)
    col_mask = (w[None, :] >= col_start[:, None]) & (w[None, :] < col_start[:, None] + NA_KC)
    d_col = np.clip(w[None, :] - w[:, None], -(NA_KC - 1), NA_KC - 1) + (NA_KC - 1)
    pick = (d_col[None] == np.arange(2 * NA_KC - 1)[:, None, None]).astype(np.float32)
    full = jnp.einsum('hdj,jwu->hdwu', rpb.astype(F32), jnp.asarray(pick), precision=lax.Precision.HIGHEST)
    full = jnp.where(col_mask[None, None], full, NEG_INF)
    pad = jnp.zeros_like(full[:, :1])
    table = jnp.concatenate([jnp.concatenate([pad, full], axis=1), jnp.concatenate([full, pad], axis=1)], axis=-1)
    half = np.zeros((4, GRID_W, 2 * GRID_W), np.float32)
    half[1, :, :GRID_W] = NEG_INF
    half[2, :, GRID_W:] = NEG_INF
    half[3] = NEG_INF
    return table, jnp.asarray(half)


def neighbourhood_attention(q, k, v, kc, vc, tables):
    bsz, n_tok, width = q.shape
    n_ctx = kc.shape[1]
    n_rows = n_tok // GRID_W
    assert n_rows >= NA_UNION_ROWS and n_rows % NA_ROWS_PER_STEP == 0
    table, half_masks = tables
    tq = NA_ROWS_PER_STEP * GRID_W
    return pl.pallas_call(
        functools.partial(_na_kernel, n_rows=n_rows),
        grid=(bsz, n_rows // NA_ROWS_PER_STEP),
        in_specs=[
            pl.BlockSpec((None, tq, width), lambda b, i: (b, i, 0)),
            pl.BlockSpec((None, n_tok, width), lambda b, i: (b, 0, 0)),
            pl.BlockSpec((None, n_tok, width), lambda b, i: (b, 0, 0)),
            pl.BlockSpec((None, n_ctx, width), lambda b, i: (b, 0, 0)),
            pl.BlockSpec((None, n_ctx, width), lambda b, i: (b, 0, 0)),
            _full(table.shape),
            _full(half_masks.shape),
        ],
        out_specs=pl.BlockSpec((None, tq, width), lambda b, i: (b, i, 0)),
        out_shape=jax.ShapeDtypeStruct(q.shape, BF16),
        compiler_params=_cparams("parallel", "arbitrary"),
        name="neighbourhood_attention",
    )(q, k, v, kc, vc, table, half_masks)


def _ctx_attn_kernel(q_ref, k_ref, v_ref, o_ref):
    for p in range(NA_HEADS // 2):
        ls = slice(p * LANES, (p + 1) * LANES)
        o_pair = _na_head_pair(q_ref[:, ls], None, None, k_ref[:, ls], v_ref[:, ls], None)
        o_ref[:, ls] = o_pair.astype(o_ref.dtype)


def context_attention(qc, kc, vc):
    bsz, n_ctx, width = qc.shape
    spec = pl.BlockSpec((None, n_ctx, width), lambda b: (b, 0, 0))
    return pl.pallas_call(
        _ctx_attn_kernel,
        grid=(bsz,),
        in_specs=[spec, spec, spec],
        out_specs=spec,
        out_shape=jax.ShapeDtypeStruct(qc.shape, BF16),
        compiler_params=_cparams("parallel"),
        name="context_attention",
    )(qc, kc, vc)


ROUTER_LANES = LANES
EXPERT_LANE0 = N_GROUPS
GROUP_ID_LANE = 0
RANK_LANE = 1


def _route(logits):
    lane = lax.broadcasted_iota(jnp.int32, logits.shape, 1)
    big = jnp.int32(ROUTER_LANES)
    is_g = lane < N_GROUPS
    lg = jnp.where(is_g, logits, -jnp.inf)
    mg = jnp.max(lg, axis=-1, keepdims=True)
    grp = jnp.min(jnp.where(lg == mg, lane, big), axis=-1, keepdims=True)
    g_weight = 1.0 / jnp.sum(jnp.where(is_g, jnp.exp(logits - mg), 0.0), axis=-1, keepdims=True)
    e_idx = lane - EXPERT_LANE0
    sel = (e_idx >= 0) & (e_idx < N_EXPERTS) & ((e_idx // EXPERTS_PER_GROUP) == grp)
    ls1 = jnp.where(sel, logits, -jnp.inf)
    v1 = jnp.max(ls1, axis=-1, keepdims=True)
    i1 = jnp.min(jnp.where(ls1 == v1, lane, big), axis=-1, keepdims=True)
    ls2 = jnp.where(lane == i1, -jnp.inf, ls1)
    v2 = jnp.max(ls2, axis=-1, keepdims=True)
    i2 = jnp.min(jnp.where(ls2 == v2, lane, big), axis=-1, keepdims=True)
    e2 = jnp.exp(v2 - v1)
    w1 = 1.0 / (1.0 + e2)
    w2 = e2 * w1
    comb = g_weight * (jnp.where(lane == i1, w1, 0.0) + jnp.where(lane == i2, w2, 0.0))
    return jnp.where(lane == GROUP_ID_LANE, grp.astype(F32), comb)


def _merge_kernel(x_ref, ya_ref, xb_ref, yf_ref, yr_ref, yc_ref, gate_ref, mod_ref, d_ref, glu_ref,
                  wa_ref, wb_ref, wc_ref, wo_ref, g_ref, rw_ref, rb_ref, xn_ref, h_ref, comb_ref):
    d = x_ref.shape[1]
    yb = d_ref[...] * xb_ref[...].astype(F32) + yf_ref[...] + yr_ref[...]
    yb = _gelu(yb)
    yb = yb * jax.nn.sigmoid(jnp.dot(yb.astype(BF16), glu_ref[...], preferred_element_type=F32))

    def gate(j):
        return jax.nn.sigmoid(gate_ref[:, j * d:(j + 1) * d].astype(F32))

    m = gate(0) * jnp.dot(ya_ref[...], wa_ref[...], preferred_element_type=F32)
    m = m + gate(1) * jnp.dot(yb.astype(BF16), wb_ref[...], preferred_element_type=F32)
    m = m + gate(2) * jnp.dot(yc_ref[...], wc_ref[...], preferred_element_type=F32)
    xn = x_ref[...] + mod_ref[2:3, :] * jnp.dot(m.astype(BF16), wo_ref[...], preferred_element_type=F32)
    xn_ref[...] = xn
    ms = jnp.mean(xn * xn, axis=-1, keepdims=True)
    h = xn * lax.rsqrt(ms + EPS) * g_ref[...]
    h = h * (1.0 + mod_ref[4:5, :]) + mod_ref[3:4, :]
    h_ref[...] = h.astype(h_ref.dtype)
    logits = jnp.dot(h, rw_ref[...], preferred_element_type=F32) + rb_ref[...]
    comb_ref[...] = _route(logits)


def merge_and_route(x, ya, xb, yf, yr, yc, gates, mod, s5_d, glu_w, w_br_a, w_br_b, w_br_c, w_out,
                    norm_ffn_g, router_w, router_b, tm):
    bsz, n_tok, d = x.shape

    def tok(width):
        return pl.BlockSpec((None, tm, width), lambda b, i: (b, i, 0))

    weights = [s5_d.reshape(1, S5_WIDTH), glu_w, w_br_a, w_br_b, w_br_c, w_out,
               norm_ffn_g.reshape(1, d), router_w, router_b]
    return pl.pallas_call(
        _merge_kernel,
        grid=(bsz, n_tok // tm),
        in_specs=[tok(d), tok(SGU_WIDTH), tok(S5_WIDTH), tok(S5_WIDTH), tok(S5_WIDTH), tok(NA_WIDTH),
                  tok(N_BRANCH * d), pl.BlockSpec((None, ADA_CHUNKS, d), lambda b, i: (b, 0, 0))]
        + [_full(w.shape) for w in weights],
        out_specs=[tok(d), tok(d), tok(ROUTER_LANES)],
        out_shape=[
            jax.ShapeDtypeStruct(x.shape, F32),
            jax.ShapeDtypeStruct(x.shape, BF16),
            jax.ShapeDtypeStruct((bsz, n_tok, ROUTER_LANES), F32),
        ],
        compiler_params=_cparams("parallel", "parallel"),
        name="merge_and_route",
    )(x, ya, xb, yf, yr, yc, gates, mod, *weights)


def router_params(rg_w, rg_b, re_w, re_b):
    d = rg_w.shape[0]
    pad = ROUTER_LANES - N_GROUPS - N_EXPERTS
    w = jnp.concatenate([rg_w, re_w, jnp.zeros((d, pad), F32)], axis=1).astype(F32)
    b = jnp.concatenate([rg_b, re_b, jnp.zeros((pad,), F32)]).astype(F32).reshape(1, ROUTER_LANES)
    return w, b


MOE_BLOCK = 144


def _split_bf16(x):
    hi = x.astype(BF16)
    return hi, (x - hi.astype(F32)).astype(BF16)


def _moe_kernel(xn_ref, h_ref, comb_ref, mod_ref, wg_ref, wu_ref, wd_ref, sel_ref, fg_ref, o_ref,
                aux_col, aux_row, hid_scr, cnt_ref, *, final_norm):
    g = pl.program_id(2)
    tm = h_ref.shape[0]

    @pl.when(g == 0)
    def _():
        o_ref[...] = xn_ref[...]
        comb = comb_ref[...]
        lane = lax.broadcasted_iota(jnp.int32, comb.shape, 1)
        grp = comb[:, GROUP_ID_LANE:GROUP_ID_LANE + 1]
        onehot = jnp.where(lane < N_GROUPS, jnp.where(lane.astype(F32) == grp, 1.0, 0.0), 0.0)
        row_i = lax.broadcasted_iota(jnp.int32, (tm, tm), 0)
        col_i = lax.broadcasted_iota(jnp.int32, (tm, tm), 1)
        tri = jnp.where(col_i < row_i, 1.0, 0.0).astype(BF16)
        ranks = jnp.dot(tri, onehot.astype(BF16), preferred_element_type=F32)
        own = jnp.sum(onehot * ranks, axis=-1, keepdims=True)
        aux = jnp.where(lane == GROUP_ID_LANE, grp, jnp.where(lane == RANK_LANE, own, 0.0))
        aux_col[...] = aux
        aux_row[...] = aux.T
        for gg in range(N_GROUPS):
            cnt_ref[gg] = jnp.sum(onehot[:, gg:gg + 1]).astype(jnp.int32)

    gf = g.astype(F32)
    rank_row = jnp.where(aux_row[GROUP_ID_LANE:GROUP_ID_LANE + 1, :] == gf,
                         aux_row[RANK_LANE:RANK_LANE + 1, :], -1.0)
    rank_col = jnp.where(aux_col[:, GROUP_ID_LANE:GROUP_ID_LANE + 1] == gf,
                         aux_col[:, RANK_LANE:RANK_LANE + 1], -1.0)
    n_blocks = (cnt_ref[g] + MOE_BLOCK - 1) // MOE_BLOCK
    slot_r = lax.broadcasted_iota(jnp.int32, (MOE_BLOCK, tm), 0).astype(F32)
    slot_c = lax.broadcasted_iota(jnp.int32, (tm, MOE_BLOCK), 1).astype(F32)
    scale = mod_ref[5:6, :]

    comb_hi, comb_lo = _split_bf16(comb_ref[...])

    def block(j, carry):
        base = (j * MOE_BLOCK).astype(F32)
        gather = jnp.where(rank_row - base == slot_r, 1.0, 0.0).astype(BF16)
        hc = jnp.dot(gather, h_ref[...], preferred_element_type=F32).astype(BF16)
        cc = (jnp.dot(gather, comb_hi, preferred_element_type=F32)
              + jnp.dot(gather, comb_lo, preferred_element_type=F32))
        cc_hi, cc_lo = _split_bf16(cc)
        cw = (jnp.dot(cc_hi, sel_ref[...], preferred_element_type=F32)
              + jnp.dot(cc_lo, sel_ref[...], preferred_element_type=F32))
        for e in range(EXPERTS_PER_GROUP):
            a = jnp.dot(hc, wg_ref[e], preferred_element_type=F32)
            u = jnp.dot(hc, wu_ref[e], preferred_element_type=F32)
            hid_scr[:, e * D_EXPERT:(e + 1) * D_EXPERT] = (
                a * jax.nn.sigmoid(a) * u * cw[:, e:e + 1]).astype(BF16)
        oc = jnp.dot(hid_scr[...], wd_ref[...], preferred_element_type=F32).astype(BF16)
        scatter = jnp.where(rank_col - base == slot_c, 1.0, 0.0).astype(BF16)
        o_ref[...] += scale * jnp.dot(scatter, oc, preferred_element_type=F32)
        return carry

    lax.fori_loop(0, n_blocks, block, 0)

    if final_norm:
        @pl.when(g == pl.num_programs(2) - 1)
        def _():
            xo = o_ref[...]
            ms = jnp.mean(xo * xo, axis=-1, keepdims=True)
            o_ref[...] = xo * lax.rsqrt(ms + EPS) * fg_ref[...]


def moe_grouped(xn, h, comb, mod, wg, wu, wd, sel, final_g, tm, final_norm):
    bsz, n_tok, d = xn.shape
    gw = EXPERTS_PER_GROUP * D_EXPERT

    def tok(width):
        return pl.BlockSpec((None, tm, width), lambda b, i, g: (b, i, 0))

    return pl.pallas_call(
        functools.partial(_moe_kernel, final_norm=final_norm),
        grid=(bsz, n_tok // tm, N_GROUPS),
        in_specs=[
            tok(d), tok(d), tok(ROUTER_LANES),
            pl.BlockSpec((None, ADA_CHUNKS, d), lambda b, i, g: (b, 0, 0)),
            pl.BlockSpec((None, EXPERTS_PER_GROUP, d, D_EXPERT), lambda b, i, g: (g, 0, 0, 0)),
            pl.BlockSpec((None, EXPERTS_PER_GROUP, d, D_EXPERT), lambda b, i, g: (g, 0, 0, 0)),
            pl.BlockSpec((None, gw, d), lambda b, i, g: (g, 0, 0)),
            pl.BlockSpec((None, ROUTER_LANES, ROUTER_LANES), lambda b, i, g: (g, 0, 0)),
            pl.BlockSpec((1, d), lambda b, i, g: (0, 0)),
        ],
        out_specs=tok(d),
        out_shape=jax.ShapeDtypeStruct(xn.shape, F32),
        scratch_shapes=[
            pltpu.VMEM((tm, ROUTER_LANES), F32),
            pltpu.VMEM((ROUTER_LANES, tm), F32),
            pltpu.VMEM((MOE_BLOCK, gw), BF16),
            pltpu.SMEM((N_GROUPS,), jnp.int32),
        ],
        compiler_params=_cparams("parallel", "parallel", "arbitrary"),
        name="moe_grouped",
    )(xn, h, comb, mod, wg, wu, wd, sel, final_g.reshape(1, d))


def moe_params(e_gate, e_up, e_down):
    _, d, f = e_gate.shape
    wg = e_gate.astype(BF16).reshape(N_GROUPS, EXPERTS_PER_GROUP, d, f)
    wu = e_up.astype(BF16).reshape(N_GROUPS, EXPERTS_PER_GROUP, d, f)
    wd = e_down.astype(BF16).reshape(N_GROUPS, EXPERTS_PER_GROUP * f, d)
    sel = np.zeros((N_GROUPS, ROUTER_LANES, ROUTER_LANES), np.float32)
    for g in range(N_GROUPS):
        for e in range(EXPERTS_PER_GROUP):
            sel[g, EXPERT_LANE0 + g * EXPERTS_PER_GROUP + e, e] = 1.0
    return wg, wu, wd, jnp.asarray(sel, BF16)


def s5_params(a_re, a_im, log_dt, b_re, b_im, c_re, c_im, bsz):
    lam = lax.complex(a_re.astype(F32), a_im.astype(F32))
    dt = jnp.exp(log_dt.astype(F32))[..., None]
    lam_bar = jnp.exp(lam * dt)
    b_bar = ((lam_bar - 1) / lam)[..., None] * lax.complex(b_re.astype(F32), b_im.astype(F32))
    gpb = S5_GROUPS // S5_BLOCKS
    eye = jnp.eye(gpb, dtype=F32)
    gp = S5_GROUPS * S5_STATE

    def in_mat(m):
        m = m.reshape(2, S5_BLOCKS, gpb, S5_STATE, S5_GROUP)
        return jnp.einsum('dkgpc,gh->dkgchp', m, eye).reshape(2, S5_BLOCKS, gpb * S5_GROUP, gpb * S5_STATE)

    def out_mat(m):
        m = m.reshape(2, S5_BLOCKS, gpb, S5_GROUP, S5_STATE)
        return jnp.einsum('dkgcp,gh->dkgphc', m, eye).reshape(2, S5_BLOCKS, gpb * S5_STATE, gpb * S5_GROUP)

    w_in = jnp.concatenate([in_mat(b_bar.real), in_mat(b_bar.imag)], axis=-1).astype(BF16)
    w_out = jnp.concatenate([out_mat(c_re.astype(F32)), -out_mat(c_im.astype(F32))], axis=2).astype(BF16)

    def tiles(v):
        t = v.reshape(2, gp // LANES, 1, LANES)
        t = jnp.broadcast_to(t, (2, gp // LANES, bsz, LANES))
        return jnp.concatenate([t[0], t[1]], axis=1)

    lam_flat = lam_bar.reshape(2, gp)
    return w_in, tiles(lam_flat.real), tiles(lam_flat.imag), w_out


TOKEN_TILE = 256
MOE_TOKEN_TILE = 1024


def kernel(x, c, ctx, c_ctx, ada_w, ada_b, norm_mix_g, norm_ffn_g, w_in, sgu_norm_g, sgu_w, sgu_b, s5_a_re, s5_a_im, s5_log_dt, s5_b_re, s5_b_im, s5_c_re, s5_c_im, s5_d, s5_glu_w, na_rpb, w_br_a, w_br_b, w_br_c, w_out, router_group_w, router_group_b, router_expert_w, router_expert_b, exp_w_gate, exp_w_up, exp_w_down, final_norm_g):
    bsz, n_tok, d = x.shape
    n_ctx = ctx.shape[1]
    depth = ada_w.shape[0]
    assert bsz + 1 <= SUBLANES

    cc = jnp.concatenate([c, c_ctx[None], jnp.zeros((SUBLANES - bsz - 1, d), F32)], axis=0)
    mod_all = ada_modulation(cc, ada_w, ada_b)
    rope_tabs = rope_tables(n_tok)
    s_zero = jnp.zeros((2 * bsz, S5_LANES), F32)
    tm_c = min(TOKEN_TILE, n_ctx)
    tm_moe = min(MOE_TOKEN_TILE, n_tok)

    xc = ctx
    for l in range(depth):
        with_ctx_out = l < depth - 1
        mod = mod_all[l, :bsz].reshape(bsz, ADA_CHUNKS, d)
        mod_c = jnp.broadcast_to(mod_all[l, bsz].reshape(1, ADA_CHUNKS, d), (bsz, ADA_CHUNKS, d))
        w_in_l = w_in[l].astype(BF16)
        sgu_w_l = sgu_w[l].astype(BF16)
        sgu_bias = jnp.broadcast_to(sgu_b[l].astype(F32)[:, :, None], (SGU_GROUPS, SGU_CHUNK, SGU_CHUNK))
        s5_w_in, s5_lre, s5_lim, s5_w_out = s5_params(
            s5_a_re[l], s5_a_im[l], s5_log_dt[l], s5_b_re[l], s5_b_im[l], s5_c_re[l], s5_c_im[l], bsz)
        table = na_bias_table(na_rpb[l])
        r_w, r_b = router_params(router_group_w[l], router_group_b[l], router_expert_w[l], router_expert_b[l])
        wg, wu, wd, lane_sel = moe_params(exp_w_gate[l], exp_w_up[l], exp_w_down[l])
        merge_w = (s5_d[l].astype(F32), s5_glu_w[l].astype(BF16), w_br_a[l].astype(BF16),
                   w_br_b[l].astype(BF16), w_br_c[l].astype(BF16), w_out[l].astype(BF16),
                   norm_ffn_g[l].astype(F32), r_w, r_b)

        ya_c, xb_c, q_c, k_c, v_c, gate_c = mixer_in(
            xc, mod_c, norm_mix_g[l], w_in_l, sgu_norm_g[l], sgu_w_l, sgu_bias, None, tm_c)
        ya_l, xb_l, q_l, k_l, v_l, gate_l = mixer_in(
            x, mod, norm_mix_g[l], w_in_l, sgu_norm_g[l], sgu_w_l, sgu_bias, rope_tabs, TOKEN_TILE)
        ycf, ycr, s_ctx = s5_scan(xb_c, s_zero, s5_w_in, s5_lre, s5_lim, s5_w_out)
        ylf, ylr, _ = s5_scan(xb_l, s_ctx, s5_w_in, s5_lre, s5_lim, s5_w_out)
        yc_l = neighbourhood_attention(q_l, k_l, v_l, k_c, v_c, table)
        xn, h2, comb = merge_and_route(x, ya_l, xb_l, ylf, ylr, yc_l, gate_l, mod, *merge_w, TOKEN_TILE)
        x = moe_grouped(xn, h2, comb, mod, wg, wu, wd, lane_sel, final_norm_g, tm_moe, not with_ctx_out)
        if with_ctx_out:
            yc_c = context_attention(q_c, k_c, v_c)
            xcn, hc2, comb_c = merge_and_route(xc, ya_c, xb_c, ycf, ycr, yc_c, gate_c, mod_c, *merge_w, tm_c)
            xc = moe_grouped(xcn, hc2, comb_c, mod_c, wg, wu, wd, lane_sel, final_norm_g, tm_c, False)
    return x
```

```python
import functools
import math

import jax
import jax.numpy as jnp
import numpy as np
from jax import lax
from jax.experimental import pallas as pl
from jax.experimental.pallas import tpu as pltpu

F32 = jnp.float32
BF16 = jnp.bfloat16

GRID_W = 64
N_BRANCH = 3
SGU_WIDTH = 512
SGU_GROUPS = 4
SGU_CHUNK = 128
S5_WIDTH = 384
S5_GROUP = 16
S5_GROUPS = S5_WIDTH // S5_GROUP
S5_STATE = 64
NA_HEADS = 8
NA_HEAD_DIM = 64
NA_WIDTH = NA_HEADS * NA_HEAD_DIM
NA_KR = 8
NA_KC = 16
ROPE_BASE = 10000.0
N_GROUPS = 4
EXPERTS_PER_GROUP = 8
N_EXPERTS = N_GROUPS * EXPERTS_PER_GROUP
TOP_K = 2
D_EXPERT = 256
ADA_CHUNKS = 6
EPS = 1e-6
NEG_INF = -1e30

LANES = 128
SUBLANES = 8
VMEM_LIMIT_BYTES = 56 * 1024 * 1024

S5_LANES = 2 * S5_GROUPS * S5_STATE
S5_SLABS = S5_LANES // LANES
S5_BLOCKS = S5_WIDTH // LANES
S5_CHUNK = 128
S5_PITCH = S5_CHUNK + 4


def _cparams(*sem):
    return pltpu.CompilerParams(dimension_semantics=sem, vmem_limit_bytes=VMEM_LIMIT_BYTES)


def _full(shape):
    n = len(shape)
    return pl.BlockSpec(shape, lambda *_: (0,) * n)


def _ada_kernel(c_ref, w_ref, b_ref, o_ref):
    c = c_ref[...]
    s = c * jax.nn.sigmoid(c)
    o_ref[...] = jnp.dot(s, w_ref[...], preferred_element_type=F32) + b_ref[...]


def ada_modulation(cc, ada_w, ada_b):
    n_layers, d, n = ada_w.shape
    tn = 1536
    return pl.pallas_call(
        _ada_kernel,
        grid=(n_layers, n // tn),
        in_specs=[
            pl.BlockSpec((SUBLANES, d), lambda l, j: (0, 0)),
            pl.BlockSpec((None, d, tn), lambda l, j: (l, 0, j)),
            pl.BlockSpec((None, 1, tn), lambda l, j: (l, 0, j)),
        ],
        out_specs=pl.BlockSpec((None, SUBLANES, tn), lambda l, j: (l, 0, j)),
        out_shape=jax.ShapeDtypeStruct((n_layers, SUBLANES, n), F32),
        compiler_params=_cparams("parallel", "parallel"),
        name="ada_modulation",
    )(cc, ada_w, ada_b.reshape(n_layers, 1, n))


def _gelu(x):
    return jax.nn.gelu(x)


def _sigmoid(x):
    return 0.5 * jnp.tanh(0.5 * x) + 0.5


def _mixer_in_kernel(x_ref, mod_ref, g_ref, w_ref, lng_ref, sw_ref, sb_ref, *rest, rope):
    if rope:
        cos_ref, sin_ref, swap_ref, ya_ref, b_ref, q_ref, k_ref, v_ref, gate_ref = rest
    else:
        ya_ref, b_ref, q_ref, k_ref, v_ref, gate_ref = rest
    tm = x_ref.shape[0]
    xf = x_ref[...]
    ms = jnp.mean(xf * xf, axis=-1, keepdims=True)
    y = xf * lax.rsqrt(ms + EPS) * g_ref[...]
    h = y * (1.0 + mod_ref[1:2, :]) + mod_ref[0:1, :]
    hb = h.astype(BF16)

    def proj(lo, hi):
        return jnp.dot(hb, w_ref[:, lo:hi], preferred_element_type=F32)

    o1 = 2 * SGU_WIDTH
    o2 = o1 + S5_WIDTH
    oq, ok, ov = o2, o2 + NA_WIDTH, o2 + 2 * NA_WIDTH
    o3 = o2 + 3 * NA_WIDTH

    u = _gelu(proj(0, SGU_WIDTH))
    v = _gelu(proj(SGU_WIDTH, o1))
    vc = v - jnp.mean(v, axis=-1, keepdims=True)
    vn = vc * lax.rsqrt(jnp.mean(vc * vc, axis=-1, keepdims=True) + EPS) * lng_ref[...]
    vb = vn.astype(BF16)
    cw = SGU_WIDTH // SGU_GROUPS
    for c in range(tm // SGU_CHUNK):
        r0 = c * SGU_CHUNK
        for g in range(SGU_GROUPS):
            sp = jnp.dot(sw_ref[g], vb[r0:r0 + SGU_CHUNK, g * cw:(g + 1) * cw],
                         preferred_element_type=F32) + sb_ref[g]
            ya_ref[r0:r0 + SGU_CHUNK, g * cw:(g + 1) * cw] = (
                u[r0:r0 + SGU_CHUNK, g * cw:(g + 1) * cw] * sp).astype(ya_ref.dtype)

    b_ref[...] = proj(o1, o2).astype(b_ref.dtype)

    q = proj(oq, ok)
    k = proj(ok, ov)
    if rope:
        cos = cos_ref[...]
        sin = sin_ref[...]

        def rotate(t):
            ts = jnp.dot(t.astype(BF16), swap_ref[...], preferred_element_type=F32)
            return jnp.concatenate(
                [t[:, j * LANES:(j + 1) * LANES] * cos + ts[:, j * LANES:(j + 1) * LANES] * sin
                 for j in range(NA_WIDTH // LANES)], axis=-1)

        q = rotate(q)
        k = rotate(k)
    q_scale = NA_HEAD_DIM ** -0.5 * (LOG2E if rope else 1.0)
    q_ref[...] = (q * q_scale).astype(q_ref.dtype)
    k_ref[...] = k.astype(k_ref.dtype)
    v_ref[...] = proj(ov, o3).astype(v_ref.dtype)
    gate_ref[...] = _sigmoid(proj(o3, o3 + N_BRANCH * x_ref.shape[1])).astype(gate_ref.dtype)


def mixer_in(x, mod, norm_g, w_in, sgu_norm_g, sgu_w, sgu_bias, rope_tabs, tm):
    bsz, n_tok, d = x.shape
    rope = rope_tabs is not None
    in_specs = [
        pl.BlockSpec((None, tm, d), lambda b, i: (b, i, 0)),
        pl.BlockSpec((None, ADA_CHUNKS, d), lambda b, i: (b, 0, 0)),
        _full((1, d)),
        _full(w_in.shape),
        _full((1, SGU_WIDTH)),
        _full(sgu_w.shape),
        _full(sgu_bias.shape),
    ]
    args = [x, mod, norm_g.reshape(1, d), w_in, sgu_norm_g.reshape(1, SGU_WIDTH), sgu_w, sgu_bias]
    if rope:
        cos_t, sin_t, swap = rope_tabs
        in_specs += [
            pl.BlockSpec((tm, LANES), lambda b, i: (i, 0)),
            pl.BlockSpec((tm, LANES), lambda b, i: (i, 0)),
            _full(swap.shape),
        ]
        args += [cos_t, sin_t, swap]

    def tok(width):
        return pl.BlockSpec((None, tm, width), lambda b, i: (b, i, 0))

    out_shapes = [
        jax.ShapeDtypeStruct((bsz, n_tok, SGU_WIDTH), BF16),
        jax.ShapeDtypeStruct((bsz, n_tok, S5_WIDTH), BF16),
        jax.ShapeDtypeStruct((bsz, n_tok, NA_WIDTH), BF16),
        jax.ShapeDtypeStruct((bsz, n_tok, NA_WIDTH), BF16),
        jax.ShapeDtypeStruct((bsz, n_tok, NA_WIDTH), BF16),
        jax.ShapeDtypeStruct((bsz, n_tok, N_BRANCH * d), BF16),
    ]
    out_specs = [tok(SGU_WIDTH), tok(S5_WIDTH), tok(NA_WIDTH), tok(NA_WIDTH), tok(NA_WIDTH), tok(N_BRANCH * d)]
    return pl.pallas_call(
        functools.partial(_mixer_in_kernel, rope=rope),
        grid=(bsz, n_tok // tm),
        in_specs=in_specs,
        out_specs=out_specs,
        out_shape=out_shapes,
        compiler_params=_cparams("parallel", "parallel"),
        name="mixer_in_rope" if rope else "mixer_in",
    )(*args)


def rope_tables(n_tok):
    pos = np.arange(n_tok)
    rows = (pos // GRID_W).astype(np.float32)
    cols = (pos % GRID_W).astype(np.float32)
    seg = NA_HEAD_DIM // 2
    half = seg // 2
    inv_freq = (ROPE_BASE ** (-np.arange(half, dtype=np.float32) / half)).astype(np.float32)
    ang_r = rows[:, None] * inv_freq
    ang_c = cols[:, None] * inv_freq
    cos = np.concatenate([np.cos(ang_r)] * 2 + [np.cos(ang_c)] * 2, axis=-1)
    sin = np.concatenate([-np.sin(ang_r), np.sin(ang_r), -np.sin(ang_c), np.sin(ang_c)], axis=-1)
    d = np.arange(NA_WIDTH)
    partner = np.where((d % seg) < half, d + half, d - half)
    swap = np.zeros((NA_WIDTH, NA_WIDTH), np.float32)
    swap[partner, d] = 1.0
    reps = LANES // NA_HEAD_DIM
    return (jnp.asarray(np.tile(cos, (1, reps)), F32), jnp.asarray(np.tile(sin, (1, reps)), F32),
            jnp.asarray(swap, BF16))


def _s5_kernel(xf_ref, xr_ref, s0_ref, win_ref, lre_ref, lim_ref, wout_ref, rev_ref,
               yf_ref, yr_ref, send_ref, lhs_scr, st_scr, state_scr):
    bsz, tc, _ = xf_ref.shape
    pitch = S5_PITCH
    rows = bsz * pitch
    half = S5_SLABS // 2
    i = pl.program_id(0)

    @pl.when(i == 0)
    def _():
        state_scr[...] = s0_ref[...]
        lhs_scr[...] = jnp.zeros_like(lhs_scr)

    rev = rev_ref[...]
    for b in range(bsz):
        lhs_scr[0, b * pitch:b * pitch + tc, :] = xf_ref[b].astype(F32)
        lhs_scr[1, b * pitch:b * pitch + tc, :] = jnp.dot(rev, xr_ref[b], preferred_element_type=F32)

    bw = half // S5_BLOCKS
    for d in range(2):
        lhs = lhs_scr[d].astype(BF16)
        for cb in range(S5_BLOCKS):
            res = jnp.dot(lhs[:, cb * LANES:(cb + 1) * LANES], win_ref[d, cb], preferred_element_type=F32)
            for k in range(bw):
                st_scr[cb * bw + k, d * rows:(d + 1) * rows, :] = res[:, k * LANES:(k + 1) * LANES]
                st_scr[half + cb * bw + k, d * rows:(d + 1) * rows, :] = res[:, (bw + k) * LANES:(bw + k + 1) * LANES]

    def step(t, carry):
        out = [None] * S5_SLABS
        for c in range(half):
            sre, sim = carry[c], carry[c + half]
            idx = pl.ds(t, 2 * bsz, stride=pitch)
            lr = lre_ref[c]
            li = lim_ref[c]
            nre = lr * sre - li * sim + st_scr[c, idx, :]
            nim = lr * sim + li * sre + st_scr[c + half, idx, :]
            st_scr[c, idx, :] = nre
            st_scr[c + half, idx, :] = nim
            out[c], out[c + half] = nre, nim
        return tuple(out)

    init = tuple(state_scr[:, c * LANES:(c + 1) * LANES] for c in range(S5_SLABS))
    fin = lax.fori_loop(0, tc, step, init)
    for c in range(S5_SLABS):
        state_scr[:, c * LANES:(c + 1) * LANES] = fin[c]
    send_ref[...] = state_scr[...]

    for d in range(2):
        y_blocks = []
        for cb in range(S5_BLOCKS):
            slabs = [cb * bw + k for k in range(bw)] + [half + cb * bw + k for k in range(bw)]
            s_blk = jnp.concatenate([st_scr[c, d * rows:(d + 1) * rows, :] for c in slabs], axis=-1)
            y_blocks.append(jnp.dot(s_blk.astype(BF16), wout_ref[d, cb], preferred_element_type=F32))
        y = jnp.concatenate(y_blocks, axis=-1)
        for b in range(bsz):
            yb = y[b * pitch:b * pitch + tc, :]
            if d == 0:
                yf_ref[b] = yb
            else:
                yr_ref[b] = jnp.dot(rev, yb.astype(BF16), preferred_element_type=F32)


def s5_scan(xb, s0, w_in, lam_re, lam_im, w_out):
    bsz, n_tok, width = xb.shape
    assert 2 * bsz == SUBLANES and n_tok % S5_CHUNK == 0
    n = n_tok // S5_CHUNK
    tc = S5_CHUNK
    rev = jnp.asarray(np.eye(tc, dtype=np.float32)[::-1], BF16)
    blk = (bsz, tc, width)
    return pl.pallas_call(
        _s5_kernel,
        grid=(n,),
        in_specs=[
            pl.BlockSpec(blk, lambda i: (0, i, 0)),
            pl.BlockSpec(blk, lambda i: (0, n - 1 - i, 0)),
            _full(s0.shape),
            _full(w_in.shape),
            _full(lam_re.shape),
            _full(lam_im.shape),
            _full(w_out.shape),
            _full(rev.shape),
        ],
        out_specs=[
            pl.BlockSpec(blk, lambda i: (0, i, 0)),
            pl.BlockSpec(blk, lambda i: (0, n - 1 - i, 0)),
            _full(s0.shape),
        ],
        out_shape=[
            jax.ShapeDtypeStruct(xb.shape, F32),
            jax.ShapeDtypeStruct(xb.shape, F32),
            jax.ShapeDtypeStruct(s0.shape, F32),
        ],
        scratch_shapes=[
            pltpu.VMEM((2, bsz * S5_PITCH, width), F32),
            pltpu.VMEM((S5_SLABS, 2 * bsz * S5_PITCH, LANES), F32),
            pltpu.VMEM(s0.shape, F32),
        ],
        compiler_params=_cparams("arbitrary"),
        name="s5_scan",
    )(xb, xb, s0, w_in, lam_re, lam_im, w_out, rev)


NA_ROWS_PER_STEP = 4
_NT = (((1,), (1,)), ((), ()))


def _na_head_pair(qp, kp, vp, kcp, vcp, bias_fn):
    lane_head = lax.broadcasted_iota(jnp.int32, qp.shape, 1) // NA_HEAD_DIM
    o_pair = None
    for hh in range(2):
        qm = jnp.where(lane_head == hh, qp, jnp.zeros_like(qp))
        s_ctx = lax.dot_general(qm, kcp, _NT, preferred_element_type=F32)
        m = jnp.max(s_ctx, axis=-1, keepdims=True)
        if kp is not None:
            s_win = lax.dot_general(qm, kp, _NT, preferred_element_type=F32) + bias_fn(hh)
            m = jnp.maximum(m, jnp.max(s_win, axis=-1, keepdims=True))
            e_win = jnp.exp(s_win - m)
        e_ctx = jnp.exp(s_ctx - m)
        den = jnp.sum(e_ctx, axis=-1, keepdims=True)
        o = jnp.dot(e_ctx.astype(BF16), vcp, preferred_element_type=F32)
        if kp is not None:
            den = den + jnp.sum(e_win, axis=-1, keepdims=True)
            o = o + jnp.dot(e_win.astype(BF16), vp, preferred_element_type=F32)
        o = o * (1.0 / den)
        o_pair = o if hh == 0 else jnp.where(lane_head == 0, o_pair, o)
    return o_pair


NA_UNION_ROWS = 12
NA_SLAB = 32
LOG2E = math.log2(math.e)
NA_TAB_LEFT_OUT = 2 * NA_KR
NA_TAB_RIGHT_OUT = NA_TAB_LEFT_OUT + 2 * NA_KR - 1
NA_TAB_BOTH_OUT = NA_TAB_RIGHT_OUT + 2 * NA_KR - 1


def _na_union_start(i, n_rows):
    return jnp.clip(i * NA_ROWS_PER_STEP - NA_KR // 2, 0, n_rows - NA_UNION_ROWS)


def _na_kernel(q_ref, k_ref, v_ref, kc_ref, vc_ref, tab_ref, o_ref, s_scr, p_scr, rden_scr, *, n_rows):
    i = pl.program_id(1)
    r0 = i * NA_ROWS_PER_STEP
    ks = _na_union_start(i, n_rows)
    n_keys = NA_UNION_ROWS * GRID_W
    n_q = NA_ROWS_PER_STEP * GRID_W
    n_pairs = NA_UNION_ROWS // 2
    slabs_per_row = GRID_W // NA_SLAB

    def table_entry(rr, jj):
        r = r0 + rr
        kst = jnp.clip(r - NA_KR // 2, 0, n_rows - NA_KR)
        key0 = ks + 2 * jj
        out0 = jnp.logical_or(key0 < kst, key0 >= kst + NA_KR)
        out1 = jnp.logical_or(key0 + 1 < kst, key0 + 1 >= kst + NA_KR)
        e = key0 - r + NA_KR
        both_in = jnp.clip(e, 0, 2 * NA_KR - 1)
        left_out = NA_TAB_LEFT_OUT + jnp.clip(e, 0, 2 * NA_KR - 2)
        right_out = NA_TAB_RIGHT_OUT + jnp.clip(e - 1, 0, 2 * NA_KR - 2)
        return jnp.where(out0, jnp.where(out1, NA_TAB_BOTH_OUT, left_out), jnp.where(out1, right_out, both_in))

    entries = [[table_entry(rr, jj) for jj in range(n_pairs)] for rr in range(NA_ROWS_PER_STEP)]

    lane_head = lax.broadcasted_iota(jnp.int32, (n_q, LANES), 1) // NA_HEAD_DIM

    def scores(p):
        ls = slice(p * LANES, (p + 1) * LANES)
        qp = q_ref[:, ls]
        zero = jnp.zeros_like(qp)
        q_stack = jnp.concatenate([jnp.where(lane_head == 0, qp, zero), jnp.where(lane_head == 1, qp, zero)], axis=0)
        k_all = jnp.concatenate([k_ref[0, :, ls], kc_ref[:, ls]], axis=0)
        s_scr[p] = lax.dot_general(q_stack, k_all, _NT, preferred_element_type=F32)

    scores(0)
    for p in range(NA_HEADS // 2):
        if p + 1 < NA_HEADS // 2:
            scores(p + 1)
        ls = slice(p * LANES, (p + 1) * LANES)
        v_all = jnp.concatenate([v_ref[0, :, ls], vc_ref[:, ls]], axis=0)
        s_p, p_p, rden_p = s_scr.at[p], p_scr.at[p], rden_scr.at[p]
        for sl in range(2 * n_q // NA_SLAB):
            hh = sl // (NA_ROWS_PER_STEP * slabs_per_row)
            rr = (sl // slabs_per_row) % NA_ROWS_PER_STEP
            q0 = (sl % slabs_per_row) * NA_SLAB
            rows = slice(sl * NA_SLAB, (sl + 1) * NA_SLAB)
            bias = jnp.concatenate(
                [tab_ref[2 * p + hh, entries[rr][jj], q0:q0 + NA_SLAB, :] for jj in range(n_pairs)], axis=-1)
            s_win = s_p[rows, :n_keys] + bias
            s_ctx = s_p[rows, n_keys:]
            m = jnp.maximum(jnp.max(s_win, axis=-1, keepdims=True), jnp.max(s_ctx, axis=-1, keepdims=True))
            e_win = jnp.exp2(s_win - m)
            e_ctx = jnp.exp2(s_ctx - m)
            den = jnp.sum(e_win, axis=-1, keepdims=True) + jnp.sum(e_ctx, axis=-1, keepdims=True)
            p_p[rows, :n_keys] = e_win.astype(BF16)
            p_p[rows, n_keys:] = e_ctx.astype(BF16)
            rden_p[rows, :] = jnp.broadcast_to(1.0 / den, (NA_SLAB, LANES))
        o = jnp.dot(p_p[...], v_all, preferred_element_type=F32) * rden_p[...]
        o_ref[:, ls] = jnp.where(lane_head == 0, o[:n_q], o[n_q:]).astype(o_ref.dtype)


def na_bias_table(rpb):
    w = np.arange(GRID_W)
    col_start = np.clip(w - NA_KC // 2, 0, GRID_W - NA_KC)
    col_mask = (w[None, :] >= col_start[:, None]) & (w[None, :] < col_start[:, None] + NA_KC)
    d_col = np.clip(w[None, :] - w[:, None], -(NA_KC - 1), NA_KC - 1) + (NA_KC - 1)
    pick = (d_col[None] == np.arange(2 * NA_KC - 1)[:, None, None]).astype(np.float32)
    full = jnp.einsum('hdj,jwu->hdwu', rpb.astype(F32), jnp.asarray(pick), precision=lax.Precision.HIGHEST)
    full = jnp.where(col_mask[None, None], full * LOG2E, NEG_INF)
    pad = jnp.zeros_like(full[:, :1])
    neg = jnp.full_like(full, NEG_INF)
    both_in = jnp.concatenate([jnp.concatenate([pad, full], axis=1), jnp.concatenate([full, pad], axis=1)], axis=-1)
    left_out = jnp.concatenate([neg, full], axis=-1)
    right_out = jnp.concatenate([full, neg], axis=-1)
    both_out = jnp.concatenate([neg[:, :1], neg[:, :1]], axis=-1)
    return jnp.concatenate([both_in, left_out, right_out, both_out], axis=1)


def neighbourhood_attention(q, k, v, kc, vc, table):
    bsz, n_tok, width = q.shape
    n_ctx = kc.shape[1]
    n_rows = n_tok // GRID_W
    assert n_rows >= NA_UNION_ROWS and n_rows % NA_ROWS_PER_STEP == 0
    tq = NA_ROWS_PER_STEP * GRID_W
    n_keys = NA_UNION_ROWS * GRID_W
    window = pl.BlockSpec((pl.Element(1), pl.Element(n_keys), pl.Element(width)),
                          lambda b, i: (b, _na_union_start(i, n_rows) * GRID_W, 0))
    return pl.pallas_call(
        functools.partial(_na_kernel, n_rows=n_rows),
        grid=(bsz, n_rows // NA_ROWS_PER_STEP),
        in_specs=[
            pl.BlockSpec((None, tq, width), lambda b, i: (b, i, 0)),
            window,
            window,
            pl.BlockSpec((None, n_ctx, width), lambda b, i: (b, 0, 0)),
            pl.BlockSpec((None, n_ctx, width), lambda b, i: (b, 0, 0)),
            _full(table.shape),
        ],
        out_specs=pl.BlockSpec((None, tq, width), lambda b, i: (b, i, 0)),
        out_shape=jax.ShapeDtypeStruct(q.shape, BF16),
        scratch_shapes=[
            pltpu.VMEM((NA_HEADS // 2, 2 * tq, n_keys + n_ctx), F32),
            pltpu.VMEM((NA_HEADS // 2, 2 * tq, n_keys + n_ctx), BF16),
            pltpu.VMEM((NA_HEADS // 2, 2 * tq, LANES), F32),
        ],
        compiler_params=_cparams("parallel", "arbitrary"),
        name="neighbourhood_attention",
    )(q, k, v, kc, vc, table)


def _ctx_attn_kernel(q_ref, k_ref, v_ref, o_ref):
    for p in range(NA_HEADS // 2):
        ls = slice(p * LANES, (p + 1) * LANES)
        o_pair = _na_head_pair(q_ref[:, ls], None, None, k_ref[:, ls], v_ref[:, ls], None)
        o_ref[:, ls] = o_pair.astype(o_ref.dtype)


def context_attention(qc, kc, vc):
    bsz, n_ctx, width = qc.shape
    spec = pl.BlockSpec((None, n_ctx, width), lambda b: (b, 0, 0))
    return pl.pallas_call(
        _ctx_attn_kernel,
        grid=(bsz,),
        in_specs=[spec, spec, spec],
        out_specs=spec,
        out_shape=jax.ShapeDtypeStruct(qc.shape, BF16),
        compiler_params=_cparams("parallel"),
        name="context_attention",
    )(qc, kc, vc)


ROUTER_LANES = LANES
EXPERT_LANE0 = N_GROUPS
GROUP_ID_LANE = 0
RANK_LANE = 1


def _route(logits):
    lane = lax.broadcasted_iota(jnp.int32, logits.shape, 1)
    big = jnp.int32(ROUTER_LANES)
    is_g = lane < N_GROUPS
    lg = jnp.where(is_g, logits, -jnp.inf)
    mg = jnp.max(lg, axis=-1, keepdims=True)
    grp = jnp.min(jnp.where(lg == mg, lane, big), axis=-1, keepdims=True)
    g_weight = 1.0 / jnp.sum(jnp.where(is_g, jnp.exp(logits - mg), 0.0), axis=-1, keepdims=True)
    e_idx = lane - EXPERT_LANE0
    sel = (e_idx >= 0) & (e_idx < N_EXPERTS) & ((e_idx // EXPERTS_PER_GROUP) == grp)
    ls1 = jnp.where(sel, logits, -jnp.inf)
    v1 = jnp.max(ls1, axis=-1, keepdims=True)
    i1 = jnp.min(jnp.where(ls1 == v1, lane, big), axis=-1, keepdims=True)
    ls2 = jnp.where(lane == i1, -jnp.inf, ls1)
    v2 = jnp.max(ls2, axis=-1, keepdims=True)
    i2 = jnp.min(jnp.where(ls2 == v2, lane, big), axis=-1, keepdims=True)
    e2 = jnp.exp(v2 - v1)
    w1 = 1.0 / (1.0 + e2)
    w2 = e2 * w1
    comb = g_weight * (jnp.where(lane == i1, w1, 0.0) + jnp.where(lane == i2, w2, 0.0))
    return jnp.where(lane == GROUP_ID_LANE, grp.astype(F32), comb)


def _merge_kernel(x_ref, ya_ref, xb_ref, yf_ref, yr_ref, yc_ref, gate_ref, mod_ref, d_ref, glu_ref,
                  wa_ref, wb_ref, wc_ref, wo_ref, g_ref, rw_ref, rb_ref, xn_ref, h_ref, comb_ref):
    d = x_ref.shape[1]
    yb = d_ref[...] * xb_ref[...].astype(F32) + yf_ref[...] + yr_ref[...]
    yb = _gelu(yb)
    yb = yb * _sigmoid(jnp.dot(yb.astype(BF16), glu_ref[...], preferred_element_type=F32))

    def gate(j):
        return gate_ref[:, j * d:(j + 1) * d].astype(F32)

    m = gate(0) * jnp.dot(ya_ref[...], wa_ref[...], preferred_element_type=F32)
    m = m + gate(1) * jnp.dot(yb.astype(BF16), wb_ref[...], preferred_element_type=F32)
    m = m + gate(2) * jnp.dot(yc_ref[...], wc_ref[...], preferred_element_type=F32)
    xn = x_ref[...] + mod_ref[2:3, :] * jnp.dot(m.astype(BF16), wo_ref[...], preferred_element_type=F32)
    xn_ref[...] = xn
    ms = jnp.mean(xn * xn, axis=-1, keepdims=True)
    h = xn * lax.rsqrt(ms + EPS) * g_ref[...]
    h = h * (1.0 + mod_ref[4:5, :]) + mod_ref[3:4, :]
    h_ref[...] = h.astype(h_ref.dtype)
    logits = jnp.dot(h, rw_ref[...], preferred_element_type=F32) + rb_ref[...]
    comb_ref[...] = _route(logits)


def merge_and_route(x, ya, xb, yf, yr, yc, gates, mod, s5_d, glu_w, w_br_a, w_br_b, w_br_c, w_out,
                    norm_ffn_g, router_w, router_b, tm):
    bsz, n_tok, d = x.shape

    def tok(width):
        return pl.BlockSpec((None, tm, width), lambda b, i: (b, i, 0))

    weights = [s5_d.reshape(1, S5_WIDTH), glu_w, w_br_a, w_br_b, w_br_c, w_out,
               norm_ffn_g.reshape(1, d), router_w, router_b]
    return pl.pallas_call(
        _merge_kernel,
        grid=(bsz, n_tok // tm),
        in_specs=[tok(d), tok(SGU_WIDTH), tok(S5_WIDTH), tok(S5_WIDTH), tok(S5_WIDTH), tok(NA_WIDTH),
                  tok(N_BRANCH * d), pl.BlockSpec((None, ADA_CHUNKS, d), lambda b, i: (b, 0, 0))]
        + [_full(w.shape) for w in weights],
        out_specs=[tok(d), tok(d), tok(ROUTER_LANES)],
        out_shape=[
            jax.ShapeDtypeStruct(x.shape, F32),
            jax.ShapeDtypeStruct(x.shape, BF16),
            jax.ShapeDtypeStruct((bsz, n_tok, ROUTER_LANES), F32),
        ],
        compiler_params=_cparams("parallel", "parallel"),
        name="merge_and_route",
    )(x, ya, xb, yf, yr, yc, gates, mod, *weights)


def router_params(rg_w, rg_b, re_w, re_b):
    d = rg_w.shape[0]
    pad = ROUTER_LANES - N_GROUPS - N_EXPERTS
    w = jnp.concatenate([rg_w, re_w, jnp.zeros((d, pad), F32)], axis=1).astype(F32)
    b = jnp.concatenate([rg_b, re_b, jnp.zeros((pad,), F32)]).astype(F32).reshape(1, ROUTER_LANES)
    return w, b


MOE_BLOCK = 144


def _split_bf16(x):
    hi = x.astype(BF16)
    return hi, (x - hi.astype(F32)).astype(BF16)


def _moe_kernel(xn_ref, h_ref, comb_ref, mod_ref, wg_ref, wu_ref, wd_ref, sel_ref, fg_ref, o_ref,
                aux_col, aux_row, hid_scr, cnt_ref, *, final_norm):
    g = pl.program_id(2)
    tm = h_ref.shape[0]

    @pl.when(g == 0)
    def _():
        o_ref[...] = xn_ref[...]
        comb = comb_ref[...]
        lane = lax.broadcasted_iota(jnp.int32, comb.shape, 1)
        grp = comb[:, GROUP_ID_LANE:GROUP_ID_LANE + 1]
        onehot = jnp.where(lane < N_GROUPS, jnp.where(lane.astype(F32) == grp, 1.0, 0.0), 0.0)
        row_i = lax.broadcasted_iota(jnp.int32, (tm, tm), 0)
        col_i = lax.broadcasted_iota(jnp.int32, (tm, tm), 1)
        tri = jnp.where(col_i < row_i, 1.0, 0.0).astype(BF16)
        ranks = jnp.dot(tri, onehot.astype(BF16), preferred_element_type=F32)
        own = jnp.sum(onehot * ranks, axis=-1, keepdims=True)
        aux = jnp.where(lane == GROUP_ID_LANE, grp, jnp.where(lane == RANK_LANE, own, 0.0))
        aux_col[...] = aux
        aux_row[...] = aux.T
        for gg in range(N_GROUPS):
            cnt_ref[gg] = jnp.sum(onehot[:, gg:gg + 1]).astype(jnp.int32)

    gf = g.astype(F32)
    rank_row = jnp.where(aux_row[GROUP_ID_LANE:GROUP_ID_LANE + 1, :] == gf,
                         aux_row[RANK_LANE:RANK_LANE + 1, :], -1.0)
    rank_col = jnp.where(aux_col[:, GROUP_ID_LANE:GROUP_ID_LANE + 1] == gf,
                         aux_col[:, RANK_LANE:RANK_LANE + 1], -1.0)
    n_blocks = (cnt_ref[g] + MOE_BLOCK - 1) // MOE_BLOCK
    slot_r = lax.broadcasted_iota(jnp.int32, (MOE_BLOCK, tm), 0).astype(F32)
    slot_c = lax.broadcasted_iota(jnp.int32, (tm, MOE_BLOCK), 1).astype(F32)
    scale = mod_ref[5:6, :]

    comb_hi, comb_lo = _split_bf16(comb_ref[...])

    def block(j, carry):
        base = (j * MOE_BLOCK).astype(F32)
        gather = jnp.where(rank_row - base == slot_r, 1.0, 0.0).astype(BF16)
        hc = jnp.dot(gather, h_ref[...], preferred_element_type=F32).astype(BF16)
        cc = (jnp.dot(gather, comb_hi, preferred_element_type=F32)
              + jnp.dot(gather, comb_lo, preferred_element_type=F32))
        cc_hi, cc_lo = _split_bf16(cc)
        cw = (jnp.dot(cc_hi, sel_ref[...], preferred_element_type=F32)
              + jnp.dot(cc_lo, sel_ref[...], preferred_element_type=F32))
        for e in range(EXPERTS_PER_GROUP):
            a = jnp.dot(hc, wg_ref[e], preferred_element_type=F32)
            u = jnp.dot(hc, wu_ref[e], preferred_element_type=F32)
            hid_scr[:, e * D_EXPERT:(e + 1) * D_EXPERT] = (
                a * jax.nn.sigmoid(a) * u * cw[:, e:e + 1]).astype(BF16)
        oc = jnp.dot(hid_scr[...], wd_ref[...], preferred_element_type=F32).astype(BF16)
        scatter = jnp.where(rank_col - base == slot_c, 1.0, 0.0).astype(BF16)
        o_ref[...] += scale * jnp.dot(scatter, oc, preferred_element_type=F32)
        return carry

    lax.fori_loop(0, n_blocks, block, 0)

    if final_norm:
        @pl.when(g == pl.num_programs(2) - 1)
        def _():
            xo = o_ref[...]
            ms = jnp.mean(xo * xo, axis=-1, keepdims=True)
            o_ref[...] = xo * lax.rsqrt(ms + EPS) * fg_ref[...]


def moe_grouped(xn, h, comb, mod, wg, wu, wd, sel, final_g, tm, final_norm):
    bsz, n_tok, d = xn.shape
    gw = EXPERTS_PER_GROUP * D_EXPERT

    def tok(width):
        return pl.BlockSpec((None, tm, width), lambda b, i, g: (b, i, 0))

    return pl.pallas_call(
        functools.partial(_moe_kernel, final_norm=final_norm),
        grid=(bsz, n_tok // tm, N_GROUPS),
        in_specs=[
            tok(d), tok(d), tok(ROUTER_LANES),
            pl.BlockSpec((None, ADA_CHUNKS, d), lambda b, i, g: (b, 0, 0)),
            pl.BlockSpec((None, EXPERTS_PER_GROUP, d, D_EXPERT), lambda b, i, g: (g, 0, 0, 0)),
            pl.BlockSpec((None, EXPERTS_PER_GROUP, d, D_EXPERT), lambda b, i, g: (g, 0, 0, 0)),
            pl.BlockSpec((None, gw, d), lambda b, i, g: (g, 0, 0)),
            pl.BlockSpec((None, ROUTER_LANES, ROUTER_LANES), lambda b, i, g: (g, 0, 0)),
            pl.BlockSpec((1, d), lambda b, i, g: (0, 0)),
        ],
        out_specs=tok(d),
        out_shape=jax.ShapeDtypeStruct(xn.shape, F32),
        scratch_shapes=[
            pltpu.VMEM((tm, ROUTER_LANES), F32),
            pltpu.VMEM((ROUTER_LANES, tm), F32),
            pltpu.VMEM((MOE_BLOCK, gw), BF16),
            pltpu.SMEM((N_GROUPS,), jnp.int32),
        ],
        compiler_params=_cparams("parallel", "parallel", "arbitrary"),
        name="moe_grouped",
    )(xn, h, comb, mod, wg, wu, wd, sel, final_g.reshape(1, d))


def moe_params(e_gate, e_up, e_down):
    _, d, f = e_gate.shape
    wg = e_gate.astype(BF16).reshape(N_GROUPS, EXPERTS_PER_GROUP, d, f)
    wu = e_up.astype(BF16).reshape(N_GROUPS, EXPERTS_PER_GROUP, d, f)
    wd = e_down.astype(BF16).reshape(N_GROUPS, EXPERTS_PER_GROUP * f, d)
    sel = np.zeros((N_GROUPS, ROUTER_LANES, ROUTER_LANES), np.float32)
    for g in range(N_GROUPS):
        for e in range(EXPERTS_PER_GROUP):
            sel[g, EXPERT_LANE0 + g * EXPERTS_PER_GROUP + e, e] = 1.0
    return wg, wu, wd, jnp.asarray(sel, BF16)


def s5_params(a_re, a_im, log_dt, b_re, b_im, c_re, c_im, bsz):
    lam = lax.complex(a_re.astype(F32), a_im.astype(F32))
    dt = jnp.exp(log_dt.astype(F32))[..., None]
    lam_bar = jnp.exp(lam * dt)
    b_bar = ((lam_bar - 1) / lam)[..., None] * lax.complex(b_re.astype(F32), b_im.astype(F32))
    gpb = S5_GROUPS // S5_BLOCKS
    eye = jnp.eye(gpb, dtype=F32)
    gp = S5_GROUPS * S5_STATE

    def in_mat(m):
        m = m.reshape(2, S5_BLOCKS, gpb, S5_STATE, S5_GROUP)
        return jnp.einsum('dkgpc,gh->dkgchp', m, eye).reshape(2, S5_BLOCKS, gpb * S5_GROUP, gpb * S5_STATE)

    def out_mat(m):
        m = m.reshape(2, S5_BLOCKS, gpb, S5_GROUP, S5_STATE)
        return jnp.einsum('dkgcp,gh->dkgphc', m, eye).reshape(2, S5_BLOCKS, gpb * S5_STATE, gpb * S5_GROUP)

    w_in = jnp.concatenate([in_mat(b_bar.real), in_mat(b_bar.imag)], axis=-1).astype(BF16)
    w_out = jnp.concatenate([out_mat(c_re.astype(F32)), -out_mat(c_im.astype(F32))], axis=2).astype(BF16)

    def tiles(v):
        t = v.reshape(2, gp // LANES, 1, LANES)
        t = jnp.broadcast_to(t, (2, gp // LANES, bsz, LANES))
        return jnp.concatenate([t[0], t[1]], axis=1)

    lam_flat = lam_bar.reshape(2, gp)
    return w_in, tiles(lam_flat.real), tiles(lam_flat.imag), w_out


TOKEN_TILE = 256
MOE_TOKEN_TILE = 1024


def kernel(x, c, ctx, c_ctx, ada_w, ada_b, norm_mix_g, norm_ffn_g, w_in, sgu_norm_g, sgu_w, sgu_b, s5_a_re, s5_a_im, s5_log_dt, s5_b_re, s5_b_im, s5_c_re, s5_c_im, s5_d, s5_glu_w, na_rpb, w_br_a, w_br_b, w_br_c, w_out, router_group_w, router_group_b, router_expert_w, router_expert_b, exp_w_gate, exp_w_up, exp_w_down, final_norm_g):
    bsz, n_tok, d = x.shape
    n_ctx = ctx.shape[1]
    depth = ada_w.shape[0]
    assert bsz + 1 <= SUBLANES

    cc = jnp.concatenate([c, c_ctx[None], jnp.zeros((SUBLANES - bsz - 1, d), F32)], axis=0)
    mod_all = ada_modulation(cc, ada_w, ada_b)
    rope_tabs = rope_tables(n_tok)
    s_zero = jnp.zeros((2 * bsz, S5_LANES), F32)
    tm_c = min(TOKEN_TILE, n_ctx)
    tm_moe = min(MOE_TOKEN_TILE, n_tok)

    xc = ctx
    for l in range(depth):
        with_ctx_out = l < depth - 1
        mod = mod_all[l, :bsz].reshape(bsz, ADA_CHUNKS, d)
        mod_c = jnp.broadcast_to(mod_all[l, bsz].reshape(1, ADA_CHUNKS, d), (bsz, ADA_CHUNKS, d))
        w_in_l = w_in[l].astype(BF16)
        sgu_w_l = sgu_w[l].astype(BF16)
        sgu_bias = jnp.broadcast_to(sgu_b[l].astype(F32)[:, :, None], (SGU_GROUPS, SGU_CHUNK, SGU_CHUNK))
        s5_w_in, s5_lre, s5_lim, s5_w_out = s5_params(
            s5_a_re[l], s5_a_im[l], s5_log_dt[l], s5_b_re[l], s5_b_im[l], s5_c_re[l], s5_c_im[l], bsz)
        table = na_bias_table(na_rpb[l])
        r_w, r_b = router_params(router_group_w[l], router_group_b[l], router_expert_w[l], router_expert_b[l])
        wg, wu, wd, lane_sel = moe_params(exp_w_gate[l], exp_w_up[l], exp_w_down[l])
        merge_w = (s5_d[l].astype(F32), s5_glu_w[l].astype(BF16), w_br_a[l].astype(BF16),
                   w_br_b[l].astype(BF16), w_br_c[l].astype(BF16), w_out[l].astype(BF16),
                   norm_ffn_g[l].astype(F32), r_w, r_b)

        ya_c, xb_c, q_c, k_c, v_c, gate_c = mixer_in(
            xc, mod_c, norm_mix_g[l], w_in_l, sgu_norm_g[l], sgu_w_l, sgu_bias, None, tm_c)
        ya_l, xb_l, q_l, k_l, v_l, gate_l = mixer_in(
            x, mod, norm_mix_g[l], w_in_l, sgu_norm_g[l], sgu_w_l, sgu_bias, rope_tabs, TOKEN_TILE)
        ycf, ycr, s_ctx = s5_scan(xb_c, s_zero, s5_w_in, s5_lre, s5_lim, s5_w_out)
        ylf, ylr, _ = s5_scan(xb_l, s_ctx, s5_w_in, s5_lre, s5_lim, s5_w_out)
        yc_l = neighbourhood_attention(q_l, k_l, v_l, k_c, v_c, table)
        xn, h2, comb = merge_and_route(x, ya_l, xb_l, ylf, ylr, yc_l, gate_l, mod, *merge_w, TOKEN_TILE)
        x = moe_grouped(xn, h2, comb, mod, wg, wu, wd, lane_sel, final_norm_g, tm_moe, not with_ctx_out)
        if with_ctx_out:
            yc_c = context_attention(q_c, k_c, v_c)
            xcn, hc2, comb_c = merge_and_route(xc, ya_c, xb_c, ycf, ycr, yc_c, gate_c, mod_c, *merge_w, tm_c)

            def flat(t):
                return t.reshape(1, bsz * n_ctx, t.shape[-1])

            xc = moe_grouped(flat(xcn), flat(hc2), flat(comb_c), mod_c[:1], wg, wu, wd, lane_sel, final_norm_g,
                             min(MOE_TOKEN_TILE, bsz * n_ctx), False).reshape(bsz, n_ctx, d)
    return x
```

```python
import functools
import math

import jax
import jax.numpy as jnp
import numpy as np
from jax import lax
from jax.experimental import pallas as pl
from jax.experimental.pallas import tpu as pltpu

F32 = jnp.float32
BF16 = jnp.bfloat16

GRID_W = 64
N_BRANCH = 3
SGU_WIDTH = 512
SGU_GROUPS = 4
SGU_CHUNK = 128
S5_WIDTH = 384
S5_GROUP = 16
S5_GROUPS = S5_WIDTH // S5_GROUP
S5_STATE = 64
NA_HEADS = 8
NA_HEAD_DIM = 64
NA_WIDTH = NA_HEADS * NA_HEAD_DIM
NA_KR = 8
NA_KC = 16
ROPE_BASE = 10000.0
N_GROUPS = 4
EXPERTS_PER_GROUP = 8
N_EXPERTS = N_GROUPS * EXPERTS_PER_GROUP
TOP_K = 2
D_EXPERT = 256
ADA_CHUNKS = 6
EPS = 1e-6
NEG_INF = -1e30

LANES = 128
SUBLANES = 8
VMEM_LIMIT_BYTES = 56 * 1024 * 1024

S5_LANES = 2 * S5_GROUPS * S5_STATE
S5_SLABS = S5_LANES // LANES
S5_BLOCKS = S5_WIDTH // LANES
S5_CHUNK = 128
S5_PITCH = S5_CHUNK + 4


def _cparams(*sem):
    return pltpu.CompilerParams(dimension_semantics=sem, vmem_limit_bytes=VMEM_LIMIT_BYTES)


def _full(shape):
    n = len(shape)
    return pl.BlockSpec(shape, lambda *_: (0,) * n, pipeline_mode=pl.Buffered(1))


def _ada_kernel(c_ref, w_ref, b_ref, o_ref):
    c = c_ref[...]
    s = c * jax.nn.sigmoid(c)
    o_ref[...] = jnp.dot(s, w_ref[...], preferred_element_type=F32) + b_ref[...]


def ada_modulation(cc, ada_w, ada_b):
    n_layers, d, n = ada_w.shape
    tn = 1536
    return pl.pallas_call(
        _ada_kernel,
        grid=(n_layers, n // tn),
        in_specs=[
            pl.BlockSpec((SUBLANES, d), lambda l, j: (0, 0)),
            pl.BlockSpec((None, d, tn), lambda l, j: (l, 0, j)),
            pl.BlockSpec((None, 1, tn), lambda l, j: (l, 0, j)),
        ],
        out_specs=pl.BlockSpec((None, SUBLANES, tn), lambda l, j: (l, 0, j)),
        out_shape=jax.ShapeDtypeStruct((n_layers, SUBLANES, n), F32),
        compiler_params=_cparams("parallel", "parallel"),
        name="ada_modulation",
    )(cc, ada_w, ada_b.reshape(n_layers, 1, n))


def _gelu(x):
    return jax.nn.gelu(x)


def _sigmoid(x):
    return 0.5 * jnp.tanh(0.5 * x) + 0.5


def _mixer_in_kernel(x_ref, mod_ref, g_ref, w_ref, lng_ref, sw_ref, sb_ref, *rest, rope):
    if rope:
        cos_ref, sin_ref, swap_ref, ya_ref, b_ref, q_ref, k_ref, v_ref, gate_ref = rest
    else:
        ya_ref, b_ref, q_ref, k_ref, v_ref, gate_ref = rest
    tm = x_ref.shape[0]
    xf = x_ref[...]
    ms = jnp.mean(xf * xf, axis=-1, keepdims=True)
    y = xf * lax.rsqrt(ms + EPS) * g_ref[...]
    h = y * (1.0 + mod_ref[1:2, :]) + mod_ref[0:1, :]
    hb = h.astype(BF16)

    def proj(lo, hi):
        return jnp.dot(hb, w_ref[:, lo:hi], preferred_element_type=F32)

    o1 = 2 * SGU_WIDTH
    o2 = o1 + S5_WIDTH
    oq, ok, ov = o2, o2 + NA_WIDTH, o2 + 2 * NA_WIDTH
    o3 = o2 + 3 * NA_WIDTH

    u = _gelu(proj(0, SGU_WIDTH))
    v = _gelu(proj(SGU_WIDTH, o1))
    vc = v - jnp.mean(v, axis=-1, keepdims=True)
    vn = vc * lax.rsqrt(jnp.mean(vc * vc, axis=-1, keepdims=True) + EPS) * lng_ref[...]
    vb = vn.astype(BF16)
    cw = SGU_WIDTH // SGU_GROUPS
    for c in range(tm // SGU_CHUNK):
        r0 = c * SGU_CHUNK
        for g in range(SGU_GROUPS):
            sp = jnp.dot(sw_ref[g], vb[r0:r0 + SGU_CHUNK, g * cw:(g + 1) * cw],
                         preferred_element_type=F32) + sb_ref[g]
            ya_ref[r0:r0 + SGU_CHUNK, g * cw:(g + 1) * cw] = (
                u[r0:r0 + SGU_CHUNK, g * cw:(g + 1) * cw] * sp).astype(ya_ref.dtype)

    b_ref[...] = proj(o1, o2).astype(b_ref.dtype)

    q = proj(oq, ok)
    k = proj(ok, ov)
    if rope:
        cos = cos_ref[...]
        sin = sin_ref[...]

        def rotate(t):
            ts = jnp.dot(t.astype(BF16), swap_ref[...], preferred_element_type=F32)
            return jnp.concatenate(
                [t[:, j * LANES:(j + 1) * LANES] * cos + ts[:, j * LANES:(j + 1) * LANES] * sin
                 for j in range(NA_WIDTH // LANES)], axis=-1)

        q = rotate(q)
        k = rotate(k)
    q_scale = NA_HEAD_DIM ** -0.5 * (LOG2E if rope else 1.0)
    q_ref[...] = (q * q_scale).astype(q_ref.dtype)
    k_ref[...] = k.astype(k_ref.dtype)
    v_ref[...] = proj(ov, o3).astype(v_ref.dtype)
    gate_ref[...] = _sigmoid(proj(o3, o3 + N_BRANCH * x_ref.shape[1])).astype(gate_ref.dtype)


def mixer_in(x, mod, norm_g, w_in, sgu_norm_g, sgu_w, sgu_bias, rope_tabs, tm):
    bsz, n_tok, d = x.shape
    rope = rope_tabs is not None
    in_specs = [
        pl.BlockSpec((None, tm, d), lambda b, i: (b, i, 0)),
        pl.BlockSpec((None, ADA_CHUNKS, d), lambda b, i: (b, 0, 0)),
        _full((1, d)),
        _full(w_in.shape),
        _full((1, SGU_WIDTH)),
        _full(sgu_w.shape),
        _full(sgu_bias.shape),
    ]
    args = [x, mod, norm_g.reshape(1, d), w_in, sgu_norm_g.reshape(1, SGU_WIDTH), sgu_w, sgu_bias]
    if rope:
        cos_t, sin_t, swap = rope_tabs
        in_specs += [
            pl.BlockSpec((tm, LANES), lambda b, i: (i, 0)),
            pl.BlockSpec((tm, LANES), lambda b, i: (i, 0)),
            _full(swap.shape),
        ]
        args += [cos_t, sin_t, swap]

    def tok(width):
        return pl.BlockSpec((None, tm, width), lambda b, i: (b, i, 0))

    out_shapes = [
        jax.ShapeDtypeStruct((bsz, n_tok, SGU_WIDTH), BF16),
        jax.ShapeDtypeStruct((bsz, n_tok, S5_WIDTH), BF16),
        jax.ShapeDtypeStruct((bsz, n_tok, NA_WIDTH), BF16),
        jax.ShapeDtypeStruct((bsz, n_tok, NA_WIDTH), BF16),
        jax.ShapeDtypeStruct((bsz, n_tok, NA_WIDTH), BF16),
        jax.ShapeDtypeStruct((bsz, n_tok, N_BRANCH * d), BF16),
    ]
    out_specs = [tok(SGU_WIDTH), tok(S5_WIDTH), tok(NA_WIDTH), tok(NA_WIDTH), tok(NA_WIDTH), tok(N_BRANCH * d)]
    return pl.pallas_call(
        functools.partial(_mixer_in_kernel, rope=rope),
        grid=(bsz, n_tok // tm),
        in_specs=in_specs,
        out_specs=out_specs,
        out_shape=out_shapes,
        compiler_params=_cparams("parallel", "parallel"),
        name="mixer_in_rope" if rope else "mixer_in",
    )(*args)


def rope_tables(n_tok):
    pos = np.arange(n_tok)
    rows = (pos // GRID_W).astype(np.float32)
    cols = (pos % GRID_W).astype(np.float32)
    seg = NA_HEAD_DIM // 2
    half = seg // 2
    inv_freq = (ROPE_BASE ** (-np.arange(half, dtype=np.float32) / half)).astype(np.float32)
    ang_r = rows[:, None] * inv_freq
    ang_c = cols[:, None] * inv_freq
    cos = np.concatenate([np.cos(ang_r)] * 2 + [np.cos(ang_c)] * 2, axis=-1)
    sin = np.concatenate([-np.sin(ang_r), np.sin(ang_r), -np.sin(ang_c), np.sin(ang_c)], axis=-1)
    d = np.arange(NA_WIDTH)
    partner = np.where((d % seg) < half, d + half, d - half)
    swap = np.zeros((NA_WIDTH, NA_WIDTH), np.float32)
    swap[partner, d] = 1.0
    reps = LANES // NA_HEAD_DIM
    return (jnp.asarray(np.tile(cos, (1, reps)), F32), jnp.asarray(np.tile(sin, (1, reps)), F32),
            jnp.asarray(swap, BF16))


def _s5_kernel(xf_ref, xr_ref, s0_ref, win_ref, lre_ref, lim_ref, wout_ref, rev_ref,
               yf_ref, yr_ref, send_ref, lhs_scr, st_scr, state_scr):
    bsz, tc, _ = xf_ref.shape
    pitch = S5_PITCH
    rows = bsz * pitch
    half = S5_SLABS // 2
    i = pl.program_id(0)

    @pl.when(i == 0)
    def _():
        state_scr[...] = s0_ref[...]
        lhs_scr[...] = jnp.zeros_like(lhs_scr)

    rev = rev_ref[...]
    for b in range(bsz):
        lhs_scr[0, b * pitch:b * pitch + tc, :] = xf_ref[b].astype(F32)
        lhs_scr[1, b * pitch:b * pitch + tc, :] = jnp.dot(rev, xr_ref[b], preferred_element_type=F32)

    bw = half // S5_BLOCKS
    for d in range(2):
        lhs = lhs_scr[d].astype(BF16)
        for cb in range(S5_BLOCKS):
            res = jnp.dot(lhs[:, cb * LANES:(cb + 1) * LANES], win_ref[d, cb], preferred_element_type=F32)
            for k in range(bw):
                st_scr[cb * bw + k, d * rows:(d + 1) * rows, :] = res[:, k * LANES:(k + 1) * LANES]
                st_scr[half + cb * bw + k, d * rows:(d + 1) * rows, :] = res[:, (bw + k) * LANES:(bw + k + 1) * LANES]

    def step(t, carry):
        out = [None] * S5_SLABS
        for c in range(half):
            sre, sim = carry[c], carry[c + half]
            idx = pl.ds(t, 2 * bsz, stride=pitch)
            lr = lre_ref[c]
            li = lim_ref[c]
            nre = lr * sre - li * sim + st_scr[c, idx, :]
            nim = lr * sim + li * sre + st_scr[c + half, idx, :]
            st_scr[c, idx, :] = nre
            st_scr[c + half, idx, :] = nim
            out[c], out[c + half] = nre, nim
        return tuple(out)

    init = tuple(state_scr[:, c * LANES:(c + 1) * LANES] for c in range(S5_SLABS))
    fin = lax.fori_loop(0, tc, step, init)
    for c in range(S5_SLABS):
        state_scr[:, c * LANES:(c + 1) * LANES] = fin[c]
    send_ref[...] = state_scr[...]

    for d in range(2):
        y_blocks = []
        for cb in range(S5_BLOCKS):
            slabs = [cb * bw + k for k in range(bw)] + [half + cb * bw + k for k in range(bw)]
            s_blk = jnp.concatenate([st_scr[c, d * rows:(d + 1) * rows, :] for c in slabs], axis=-1)
            y_blocks.append(jnp.dot(s_blk.astype(BF16), wout_ref[d, cb], preferred_element_type=F32))
        y = jnp.concatenate(y_blocks, axis=-1)
        for b in range(bsz):
            yb = y[b * pitch:b * pitch + tc, :]
            if d == 0:
                yf_ref[b] = yb
            else:
                yr_ref[b] = jnp.dot(rev, yb.astype(BF16), preferred_element_type=F32)


def s5_scan(xb, s0, w_in, lam_re, lam_im, w_out):
    bsz, n_tok, width = xb.shape
    assert 2 * bsz == SUBLANES and n_tok % S5_CHUNK == 0
    n = n_tok // S5_CHUNK
    tc = S5_CHUNK
    rev = jnp.asarray(np.eye(tc, dtype=np.float32)[::-1], BF16)
    blk = (bsz, tc, width)
    return pl.pallas_call(
        _s5_kernel,
        grid=(n,),
        in_specs=[
            pl.BlockSpec(blk, lambda i: (0, i, 0)),
            pl.BlockSpec(blk, lambda i: (0, n - 1 - i, 0)),
            _full(s0.shape),
            _full(w_in.shape),
            _full(lam_re.shape),
            _full(lam_im.shape),
            _full(w_out.shape),
            _full(rev.shape),
        ],
        out_specs=[
            pl.BlockSpec(blk, lambda i: (0, i, 0)),
            pl.BlockSpec(blk, lambda i: (0, n - 1 - i, 0)),
            _full(s0.shape),
        ],
        out_shape=[
            jax.ShapeDtypeStruct(xb.shape, F32),
            jax.ShapeDtypeStruct(xb.shape, F32),
            jax.ShapeDtypeStruct(s0.shape, F32),
        ],
        scratch_shapes=[
            pltpu.VMEM((2, bsz * S5_PITCH, width), F32),
            pltpu.VMEM((S5_SLABS, 2 * bsz * S5_PITCH, LANES), F32),
            pltpu.VMEM(s0.shape, F32),
        ],
        compiler_params=_cparams("arbitrary"),
        name="s5_scan",
    )(xb, xb, s0, w_in, lam_re, lam_im, w_out, rev)


NA_ROWS_PER_STEP = 4
_NT = (((1,), (1,)), ((), ()))


def _na_head_pair(qp, kp, vp, kcp, vcp, bias_fn):
    lane_head = lax.broadcasted_iota(jnp.int32, qp.shape, 1) // NA_HEAD_DIM
    o_pair = None
    for hh in range(2):
        qm = jnp.where(lane_head == hh, qp, jnp.zeros_like(qp))
        s_ctx = lax.dot_general(qm, kcp, _NT, preferred_element_type=F32)
        m = jnp.max(s_ctx, axis=-1, keepdims=True)
        if kp is not None:
            s_win = lax.dot_general(qm, kp, _NT, preferred_element_type=F32) + bias_fn(hh)
            m = jnp.maximum(m, jnp.max(s_win, axis=-1, keepdims=True))
            e_win = jnp.exp(s_win - m)
        e_ctx = jnp.exp(s_ctx - m)
        den = jnp.sum(e_ctx, axis=-1, keepdims=True)
        o = jnp.dot(e_ctx.astype(BF16), vcp, preferred_element_type=F32)
        if kp is not None:
            den = den + jnp.sum(e_win, axis=-1, keepdims=True)
            o = o + jnp.dot(e_win.astype(BF16), vp, preferred_element_type=F32)
        o = o * (1.0 / den)
        o_pair = o if hh == 0 else jnp.where(lane_head == 0, o_pair, o)
    return o_pair


NA_UNION_ROWS = 12
NA_SLAB = 32
LOG2E = math.log2(math.e)
NA_TAB_LEFT_OUT = 2 * NA_KR
NA_TAB_RIGHT_OUT = NA_TAB_LEFT_OUT + 2 * NA_KR - 1
NA_TAB_BOTH_OUT = NA_TAB_RIGHT_OUT + 2 * NA_KR - 1


def _na_union_start(i, n_rows):
    return jnp.clip(i * NA_ROWS_PER_STEP - NA_KR // 2, 0, n_rows - NA_UNION_ROWS)


def _na_kernel(q_ref, k_ref, v_ref, kc_ref, vc_ref, tab_ref, o_ref, s_scr, p_scr, rden_scr, *, n_rows):
    i = pl.program_id(1)
    r0 = i * NA_ROWS_PER_STEP
    ks = _na_union_start(i, n_rows)
    n_keys = NA_UNION_ROWS * GRID_W
    n_q = NA_ROWS_PER_STEP * GRID_W
    n_pairs = NA_UNION_ROWS // 2
    slabs_per_row = GRID_W // NA_SLAB

    def table_entry(rr, jj):
        r = r0 + rr
        kst = jnp.clip(r - NA_KR // 2, 0, n_rows - NA_KR)
        key0 = ks + 2 * jj
        out0 = jnp.logical_or(key0 < kst, key0 >= kst + NA_KR)
        out1 = jnp.logical_or(key0 + 1 < kst, key0 + 1 >= kst + NA_KR)
        e = key0 - r + NA_KR
        both_in = jnp.clip(e, 0, 2 * NA_KR - 1)
        left_out = NA_TAB_LEFT_OUT + jnp.clip(e, 0, 2 * NA_KR - 2)
        right_out = NA_TAB_RIGHT_OUT + jnp.clip(e - 1, 0, 2 * NA_KR - 2)
        return jnp.where(out0, jnp.where(out1, NA_TAB_BOTH_OUT, left_out), jnp.where(out1, right_out, both_in))

    entries = [[table_entry(rr, jj) for jj in range(n_pairs)] for rr in range(NA_ROWS_PER_STEP)]

    lane_head = lax.broadcasted_iota(jnp.int32, (n_q, LANES), 1) // NA_HEAD_DIM

    def scores(p):
        ls = slice(p * LANES, (p + 1) * LANES)
        qp = q_ref[:, ls]
        zero = jnp.zeros_like(qp)
        q_stack = jnp.concatenate([jnp.where(lane_head == 0, qp, zero), jnp.where(lane_head == 1, qp, zero)], axis=0)
        k_all = jnp.concatenate([k_ref[0, :, ls], kc_ref[:, ls]], axis=0)
        s_scr[p] = lax.dot_general(q_stack, k_all, _NT, preferred_element_type=F32)

    scores(0)
    for p in range(NA_HEADS // 2):
        if p + 1 < NA_HEADS // 2:
            scores(p + 1)
        ls = slice(p * LANES, (p + 1) * LANES)
        v_all = jnp.concatenate([v_ref[0, :, ls], vc_ref[:, ls]], axis=0)
        s_p, p_p, rden_p = s_scr.at[p], p_scr.at[p], rden_scr.at[p]
        for sl in range(2 * n_q // NA_SLAB):
            hh = sl // (NA_ROWS_PER_STEP * slabs_per_row)
            rr = (sl // slabs_per_row) % NA_ROWS_PER_STEP
            q0 = (sl % slabs_per_row) * NA_SLAB
            rows = slice(sl * NA_SLAB, (sl + 1) * NA_SLAB)
            bias = jnp.concatenate(
                [tab_ref[2 * p + hh, entries[rr][jj], q0:q0 + NA_SLAB, :] for jj in range(n_pairs)], axis=-1)
            s_win = s_p[rows, :n_keys] + bias
            s_ctx = s_p[rows, n_keys:]
            m = jnp.maximum(jnp.max(s_win, axis=-1, keepdims=True), jnp.max(s_ctx, axis=-1, keepdims=True))
            e_win = jnp.exp2(s_win - m)
            e_ctx = jnp.exp2(s_ctx - m)
            den = jnp.sum(e_win, axis=-1, keepdims=True) + jnp.sum(e_ctx, axis=-1, keepdims=True)
            p_p[rows, :n_keys] = e_win.astype(BF16)
            p_p[rows, n_keys:] = e_ctx.astype(BF16)
            rden_p[rows, :] = jnp.broadcast_to(1.0 / den, (NA_SLAB, LANES))
        o = jnp.dot(p_p[...], v_all, preferred_element_type=F32) * rden_p[...]
        o_ref[:, ls] = jnp.where(lane_head == 0, o[:n_q], o[n_q:]).astype(o_ref.dtype)


def na_bias_table(rpb):
    w = np.arange(GRID_W)
    col_start = np.clip(w - NA_KC // 2, 0, GRID_W - NA_KC)
    col_mask = (w[None, :] >= col_start[:, None]) & (w[None, :] < col_start[:, None] + NA_KC)
    d_col = np.clip(w[None, :] - w[:, None], -(NA_KC - 1), NA_KC - 1) + (NA_KC - 1)
    pick = (d_col[None] == np.arange(2 * NA_KC - 1)[:, None, None]).astype(np.float32)
    full = jnp.einsum('hdj,jwu->hdwu', rpb.astype(F32), jnp.asarray(pick), precision=lax.Precision.HIGHEST)
    full = jnp.where(col_mask[None, None], full * LOG2E, NEG_INF)
    pad = jnp.zeros_like(full[:, :1])
    neg = jnp.full_like(full, NEG_INF)
    both_in = jnp.concatenate([jnp.concatenate([pad, full], axis=1), jnp.concatenate([full, pad], axis=1)], axis=-1)
    left_out = jnp.concatenate([neg, full], axis=-1)
    right_out = jnp.concatenate([full, neg], axis=-1)
    both_out = jnp.concatenate([neg[:, :1], neg[:, :1]], axis=-1)
    return jnp.concatenate([both_in, left_out, right_out, both_out], axis=1)


def neighbourhood_attention(q, k, v, kc, vc, table):
    bsz, n_tok, width = q.shape
    n_ctx = kc.shape[1]
    n_rows = n_tok // GRID_W
    assert n_rows >= NA_UNION_ROWS and n_rows % NA_ROWS_PER_STEP == 0
    tq = NA_ROWS_PER_STEP * GRID_W
    n_keys = NA_UNION_ROWS * GRID_W
    window = pl.BlockSpec((pl.Element(1), pl.Element(n_keys), pl.Element(width)),
                          lambda b, i: (b, _na_union_start(i, n_rows) * GRID_W, 0))
    return pl.pallas_call(
        functools.partial(_na_kernel, n_rows=n_rows),
        grid=(bsz, n_rows // NA_ROWS_PER_STEP),
        in_specs=[
            pl.BlockSpec((None, tq, width), lambda b, i: (b, i, 0)),
            window,
            window,
            pl.BlockSpec((None, n_ctx, width), lambda b, i: (b, 0, 0)),
            pl.BlockSpec((None, n_ctx, width), lambda b, i: (b, 0, 0)),
            _full(table.shape),
        ],
        out_specs=pl.BlockSpec((None, tq, width), lambda b, i: (b, i, 0)),
        out_shape=jax.ShapeDtypeStruct(q.shape, BF16),
        scratch_shapes=[
            pltpu.VMEM((NA_HEADS // 2, 2 * tq, n_keys + n_ctx), F32),
            pltpu.VMEM((NA_HEADS // 2, 2 * tq, n_keys + n_ctx), BF16),
            pltpu.VMEM((NA_HEADS // 2, 2 * tq, LANES), F32),
        ],
        compiler_params=_cparams("parallel", "arbitrary"),
        name="neighbourhood_attention",
    )(q, k, v, kc, vc, table)


def _ctx_attn_kernel(q_ref, k_ref, v_ref, o_ref):
    for p in range(NA_HEADS // 2):
        ls = slice(p * LANES, (p + 1) * LANES)
        o_pair = _na_head_pair(q_ref[:, ls], None, None, k_ref[:, ls], v_ref[:, ls], None)
        o_ref[:, ls] = o_pair.astype(o_ref.dtype)


def context_attention(qc, kc, vc):
    bsz, n_ctx, width = qc.shape
    spec = pl.BlockSpec((None, n_ctx, width), lambda b: (b, 0, 0))
    return pl.pallas_call(
        _ctx_attn_kernel,
        grid=(bsz,),
        in_specs=[spec, spec, spec],
        out_specs=spec,
        out_shape=jax.ShapeDtypeStruct(qc.shape, BF16),
        compiler_params=_cparams("parallel"),
        name="context_attention",
    )(qc, kc, vc)


ROUTER_LANES = LANES
EXPERT_LANE0 = N_GROUPS
GROUP_ID_LANE = 0
RANK_LANE = 1


def _route(logits):
    lane = lax.broadcasted_iota(jnp.int32, logits.shape, 1)
    big = jnp.int32(ROUTER_LANES)
    is_g = lane < N_GROUPS
    lg = jnp.where(is_g, logits, -jnp.inf)
    mg = jnp.max(lg, axis=-1, keepdims=True)
    grp = jnp.min(jnp.where(lg == mg, lane, big), axis=-1, keepdims=True)
    g_weight = 1.0 / jnp.sum(jnp.where(is_g, jnp.exp(logits - mg), 0.0), axis=-1, keepdims=True)
    e_idx = lane - EXPERT_LANE0
    sel = (e_idx >= 0) & (e_idx < N_EXPERTS) & ((e_idx // EXPERTS_PER_GROUP) == grp)
    ls1 = jnp.where(sel, logits, -jnp.inf)
    v1 = jnp.max(ls1, axis=-1, keepdims=True)
    i1 = jnp.min(jnp.where(ls1 == v1, lane, big), axis=-1, keepdims=True)
    ls2 = jnp.where(lane == i1, -jnp.inf, ls1)
    v2 = jnp.max(ls2, axis=-1, keepdims=True)
    i2 = jnp.min(jnp.where(ls2 == v2, lane, big), axis=-1, keepdims=True)
    e2 = jnp.exp(v2 - v1)
    w1 = 1.0 / (1.0 + e2)
    w2 = e2 * w1
    comb = g_weight * (jnp.where(lane == i1, w1, 0.0) + jnp.where(lane == i2, w2, 0.0))
    return jnp.where(lane == GROUP_ID_LANE, grp.astype(F32), comb)


def _merge_kernel(x_ref, ya_ref, xb_ref, yf_ref, yr_ref, yc_ref, gate_ref, mod_ref, d_ref, glu_ref,
                  wa_ref, wb_ref, wc_ref, wo_ref, g_ref, rw_ref, rb_ref, own_ref, xn_ref, h_ref, comb_ref):
    d = x_ref.shape[1]
    yb = d_ref[...] * xb_ref[...].astype(F32) + yf_ref[...] + yr_ref[...]
    yb = _gelu(yb)
    yb = yb * _sigmoid(jnp.dot(yb.astype(BF16), glu_ref[...], preferred_element_type=F32))

    def gate(j):
        return gate_ref[:, j * d:(j + 1) * d].astype(F32)

    m = gate(0) * jnp.dot(ya_ref[...], wa_ref[...], preferred_element_type=F32)
    m = m + gate(1) * jnp.dot(yb.astype(BF16), wb_ref[...], preferred_element_type=F32)
    m = m + gate(2) * jnp.dot(yc_ref[...], wc_ref[...], preferred_element_type=F32)
    xn = x_ref[...] + mod_ref[2:3, :] * jnp.dot(m.astype(BF16), wo_ref[...], preferred_element_type=F32)
    xn_ref[...] = xn
    ms = jnp.mean(xn * xn, axis=-1, keepdims=True)
    h = xn * lax.rsqrt(ms + EPS) * g_ref[...]
    h = h * (1.0 + mod_ref[4:5, :]) + mod_ref[3:4, :]
    h_ref[:, :d] = h.astype(h_ref.dtype)
    logits = jnp.dot(h, rw_ref[...], preferred_element_type=F32) + rb_ref[...]
    comb = _route(logits)
    comb_ref[...] = comb
    own = jnp.dot(comb, own_ref[...], preferred_element_type=F32)
    own_hi, own_lo = _split_bf16(own)
    lane = lax.broadcasted_iota(jnp.int32, own.shape, 1)
    h_ref[:, d:] = jnp.where(lane < EXPERTS_PER_GROUP, own_hi, own_lo)


def merge_and_route(x, ya, xb, yf, yr, yc, gates, mod, s5_d, glu_w, w_br_a, w_br_b, w_br_c, w_out,
                    norm_ffn_g, router_w, router_b, tm):
    bsz, n_tok, d = x.shape

    def tok(width):
        return pl.BlockSpec((None, tm, width), lambda b, i: (b, i, 0))

    own = np.zeros((ROUTER_LANES, ROUTER_LANES), np.float32)
    for e in range(N_EXPERTS):
        own[EXPERT_LANE0 + e, e % EXPERTS_PER_GROUP] = 1.0
        own[EXPERT_LANE0 + e, EXPERTS_PER_GROUP + e % EXPERTS_PER_GROUP] = 1.0
    weights = [s5_d.reshape(1, S5_WIDTH), glu_w, w_br_a, w_br_b, w_br_c, w_out,
               norm_ffn_g.reshape(1, d), router_w, router_b, jnp.asarray(own)]
    return pl.pallas_call(
        _merge_kernel,
        grid=(bsz, n_tok // tm),
        in_specs=[tok(d), tok(SGU_WIDTH), tok(S5_WIDTH), tok(S5_WIDTH), tok(S5_WIDTH), tok(NA_WIDTH),
                  tok(N_BRANCH * d), pl.BlockSpec((None, ADA_CHUNKS, d), lambda b, i: (b, 0, 0))]
        + [_full(w.shape) for w in weights],
        out_specs=[tok(d), tok(d + ROUTER_LANES), tok(ROUTER_LANES)],
        out_shape=[
            jax.ShapeDtypeStruct(x.shape, F32),
            jax.ShapeDtypeStruct((bsz, n_tok, d + ROUTER_LANES), BF16),
            jax.ShapeDtypeStruct((bsz, n_tok, ROUTER_LANES), F32),
        ],
        compiler_params=_cparams("parallel", "parallel"),
        name="merge_and_route",
    )(x, ya, xb, yf, yr, yc, gates, mod, *weights)


def router_params(rg_w, rg_b, re_w, re_b):
    d = rg_w.shape[0]
    pad = ROUTER_LANES - N_GROUPS - N_EXPERTS
    w = jnp.concatenate([rg_w, re_w, jnp.zeros((d, pad), F32)], axis=1).astype(F32)
    b = jnp.concatenate([rg_b, re_b, jnp.zeros((pad,), F32)]).astype(F32).reshape(1, ROUTER_LANES)
    return w, b


MOE_BLOCK = 144


def _split_bf16(x):
    hi = x.astype(BF16)
    return hi, (x - hi.astype(F32)).astype(BF16)


MOE_SUBTILE = 512


def _moe_kernel(xn_ref, hx_ref, comb_ref, mod_ref, wg_ref, wu_ref, wd_ref, fg_ref, o_ref,
                aux_col, aux_row, hid_scr, cnt_ref, *, final_norm):
    g = pl.program_id(2)
    tm, d = o_ref.shape
    st = min(MOE_SUBTILE, tm)
    n_sub = tm // st

    @pl.when(g == 0)
    def _():
        o_ref[...] = xn_ref[...]
        row_i = lax.broadcasted_iota(jnp.int32, (st, st), 0)
        col_i = lax.broadcasted_iota(jnp.int32, (st, st), 1)
        tri = jnp.where(col_i < row_i, 1.0, 0.0).astype(BF16)
        for s in range(n_sub):
            comb = comb_ref[s * st:(s + 1) * st, :]
            lane = lax.broadcasted_iota(jnp.int32, comb.shape, 1)
            grp = comb[:, GROUP_ID_LANE:GROUP_ID_LANE + 1]
            onehot = jnp.where(lane < N_GROUPS, jnp.where(lane.astype(F32) == grp, 1.0, 0.0), 0.0)
            ranks = jnp.dot(tri, onehot.astype(BF16), preferred_element_type=F32)
            own = jnp.sum(onehot * ranks, axis=-1, keepdims=True)
            aux = jnp.where(lane == GROUP_ID_LANE, grp, jnp.where(lane == RANK_LANE, own, 0.0))
            aux_col[s * st:(s + 1) * st, :] = aux
            aux_row[:, s * st:(s + 1) * st] = aux.T
            for gg in range(N_GROUPS):
                cnt_ref[s * N_GROUPS + gg] = jnp.sum(onehot[:, gg:gg + 1]).astype(jnp.int32)

    gf = g.astype(F32)
    slot_r = lax.broadcasted_iota(jnp.int32, (MOE_BLOCK, st), 0).astype(F32)
    slot_c = lax.broadcasted_iota(jnp.int32, (st, MOE_BLOCK), 1).astype(F32)
    scale = mod_ref[5:6, :]

    for s in range(n_sub):
        rows = slice(s * st, (s + 1) * st)
        rank_row = jnp.where(aux_row[GROUP_ID_LANE:GROUP_ID_LANE + 1, rows] == gf,
                             aux_row[RANK_LANE:RANK_LANE + 1, rows], -1.0)
        rank_col = jnp.where(aux_col[rows, GROUP_ID_LANE:GROUP_ID_LANE + 1] == gf,
                             aux_col[rows, RANK_LANE:RANK_LANE + 1], -1.0)
        n_blocks = (cnt_ref[s * N_GROUPS + g] + MOE_BLOCK - 1) // MOE_BLOCK

        def block(j, carry, rows=rows, rank_row=rank_row, rank_col=rank_col):
            base = (j * MOE_BLOCK).astype(F32)
            gather = jnp.where(rank_row - base == slot_r, 1.0, 0.0).astype(BF16)
            hcx = jnp.dot(gather, hx_ref[rows, :], preferred_element_type=F32)
            hc = hcx[:, :d].astype(BF16)
            wt = hcx[:, d:]
            for e in range(EXPERTS_PER_GROUP):
                a = jnp.dot(hc, wg_ref[e], preferred_element_type=F32)
                u = jnp.dot(hc, wu_ref[e], preferred_element_type=F32)
                cw = wt[:, e:e + 1] + wt[:, EXPERTS_PER_GROUP + e:EXPERTS_PER_GROUP + e + 1]
                hid_scr[:, e * D_EXPERT:(e + 1) * D_EXPERT] = (a * jax.nn.sigmoid(a) * u * cw).astype(BF16)
            oc = jnp.dot(hid_scr[...], wd_ref[...], preferred_element_type=F32).astype(BF16)
            scatter = jnp.where(rank_col - base == slot_c, 1.0, 0.0).astype(BF16)
            o_ref[rows, :] += scale * jnp.dot(scatter, oc, preferred_element_type=F32)
            return carry

        lax.fori_loop(0, n_blocks, block, 0)

    if final_norm:
        @pl.when(g == pl.num_programs(2) - 1)
        def _():
            xo = o_ref[...]
            ms = jnp.mean(xo * xo, axis=-1, keepdims=True)
            o_ref[...] = xo * lax.rsqrt(ms + EPS) * fg_ref[...]


def moe_grouped(xn, hx, comb, mod, wg, wu, wd, final_g, tm, final_norm):
    bsz, n_tok, d = xn.shape
    gw = EXPERTS_PER_GROUP * D_EXPERT
    n_sub = tm // min(MOE_SUBTILE, tm)

    def tok(width):
        return pl.BlockSpec((None, tm, width), lambda b, i, g: (b, i, 0))

    return pl.pallas_call(
        functools.partial(_moe_kernel, final_norm=final_norm),
        grid=(bsz, n_tok // tm, N_GROUPS),
        in_specs=[
            tok(d), tok(d + ROUTER_LANES), tok(ROUTER_LANES),
            pl.BlockSpec((None, ADA_CHUNKS, d), lambda b, i, g: (b, 0, 0)),
            pl.BlockSpec((None, EXPERTS_PER_GROUP, d, D_EXPERT), lambda b, i, g: (g, 0, 0, 0)),
            pl.BlockSpec((None, EXPERTS_PER_GROUP, d, D_EXPERT), lambda b, i, g: (g, 0, 0, 0)),
            pl.BlockSpec((None, gw, d), lambda b, i, g: (g, 0, 0)),
            pl.BlockSpec((1, d), lambda b, i, g: (0, 0)),
        ],
        out_specs=tok(d),
        out_shape=jax.ShapeDtypeStruct(xn.shape, F32),
        scratch_shapes=[
            pltpu.VMEM((tm, ROUTER_LANES), F32),
            pltpu.VMEM((ROUTER_LANES, tm), F32),
            pltpu.VMEM((MOE_BLOCK, gw), BF16),
            pltpu.SMEM((n_sub * N_GROUPS,), jnp.int32),
        ],
        compiler_params=_cparams("parallel", "parallel", "arbitrary"),
        name="moe_grouped",
    )(xn, hx, comb, mod, wg, wu, wd, final_g.reshape(1, d))


def moe_params(e_gate, e_up, e_down):
    _, d, f = e_gate.shape
    wg = e_gate.astype(BF16).reshape(N_GROUPS, EXPERTS_PER_GROUP, d, f)
    wu = e_up.astype(BF16).reshape(N_GROUPS, EXPERTS_PER_GROUP, d, f)
    wd = e_down.astype(BF16).reshape(N_GROUPS, EXPERTS_PER_GROUP * f, d)
    return wg, wu, wd


def s5_params(a_re, a_im, log_dt, b_re, b_im, c_re, c_im, bsz):
    lam = lax.complex(a_re.astype(F32), a_im.astype(F32))
    dt = jnp.exp(log_dt.astype(F32))[..., None]
    lam_bar = jnp.exp(lam * dt)
    b_bar = ((lam_bar - 1) / lam)[..., None] * lax.complex(b_re.astype(F32), b_im.astype(F32))
    gpb = S5_GROUPS // S5_BLOCKS
    eye = jnp.eye(gpb, dtype=F32)
    gp = S5_GROUPS * S5_STATE

    def in_mat(m):
        m = m.reshape(2, S5_BLOCKS, gpb, S5_STATE, S5_GROUP)
        return jnp.einsum('dkgpc,gh->dkgchp', m, eye).reshape(2, S5_BLOCKS, gpb * S5_GROUP, gpb * S5_STATE)

    def out_mat(m):
        m = m.reshape(2, S5_BLOCKS, gpb, S5_GROUP, S5_STATE)
        return jnp.einsum('dkgcp,gh->dkgphc', m, eye).reshape(2, S5_BLOCKS, gpb * S5_STATE, gpb * S5_GROUP)

    w_in = jnp.concatenate([in_mat(b_bar.real), in_mat(b_bar.imag)], axis=-1).astype(BF16)
    w_out = jnp.concatenate([out_mat(c_re.astype(F32)), -out_mat(c_im.astype(F32))], axis=2).astype(BF16)

    def tiles(v):
        t = v.reshape(2, gp // LANES, 1, LANES)
        t = jnp.broadcast_to(t, (2, gp // LANES, bsz, LANES))
        return jnp.concatenate([t[0], t[1]], axis=1)

    lam_flat = lam_bar.reshape(2, gp)
    return w_in, tiles(lam_flat.real), tiles(lam_flat.imag), w_out


TOKEN_TILE = 256
MIXER_TOKEN_TILE = 512
MERGE_TOKEN_TILE = 512
MOE_TOKEN_TILE = 1024


def kernel(x, c, ctx, c_ctx, ada_w, ada_b, norm_mix_g, norm_ffn_g, w_in, sgu_norm_g, sgu_w, sgu_b, s5_a_re, s5_a_im, s5_log_dt, s5_b_re, s5_b_im, s5_c_re, s5_c_im, s5_d, s5_glu_w, na_rpb, w_br_a, w_br_b, w_br_c, w_out, router_group_w, router_group_b, router_expert_w, router_expert_b, exp_w_gate, exp_w_up, exp_w_down, final_norm_g):
    bsz, n_tok, d = x.shape
    n_ctx = ctx.shape[1]
    depth = ada_w.shape[0]
    assert bsz + 1 <= SUBLANES

    cc = jnp.concatenate([c, c_ctx[None], jnp.zeros((SUBLANES - bsz - 1, d), F32)], axis=0)
    mod_all = ada_modulation(cc, ada_w, ada_b)
    rope_tabs = rope_tables(n_tok)
    s_zero = jnp.zeros((2 * bsz, S5_LANES), F32)
    tm_c = min(TOKEN_TILE, n_ctx)
    tm_moe = min(MOE_TOKEN_TILE, n_tok)

    xc = ctx
    for l in range(depth):
        with_ctx_out = l < depth - 1
        mod = mod_all[l, :bsz].reshape(bsz, ADA_CHUNKS, d)
        mod_c = jnp.broadcast_to(mod_all[l, bsz].reshape(1, ADA_CHUNKS, d), (bsz, ADA_CHUNKS, d))
        w_in_l = w_in[l].astype(BF16)
        sgu_w_l = sgu_w[l].astype(BF16)
        sgu_bias = jnp.broadcast_to(sgu_b[l].astype(F32)[:, :, None], (SGU_GROUPS, SGU_CHUNK, SGU_CHUNK))
        s5_w_in, s5_lre, s5_lim, s5_w_out = s5_params(
            s5_a_re[l], s5_a_im[l], s5_log_dt[l], s5_b_re[l], s5_b_im[l], s5_c_re[l], s5_c_im[l], bsz)
        table = na_bias_table(na_rpb[l])
        r_w, r_b = router_params(router_group_w[l], router_group_b[l], router_expert_w[l], router_expert_b[l])
        wg, wu, wd = moe_params(exp_w_gate[l], exp_w_up[l], exp_w_down[l])
        merge_w = (s5_d[l].astype(F32), s5_glu_w[l].astype(BF16), w_br_a[l].astype(BF16),
                   w_br_b[l].astype(BF16), w_br_c[l].astype(BF16), w_out[l].astype(BF16),
                   norm_ffn_g[l].astype(F32), r_w, r_b)

        ya_c, xb_c, q_c, k_c, v_c, gate_c = mixer_in(
            xc, mod_c, norm_mix_g[l], w_in_l, sgu_norm_g[l], sgu_w_l, sgu_bias, None, tm_c)
        ya_l, xb_l, q_l, k_l, v_l, gate_l = mixer_in(
            x, mod, norm_mix_g[l], w_in_l, sgu_norm_g[l], sgu_w_l, sgu_bias, rope_tabs, MIXER_TOKEN_TILE)
        ycf, ycr, s_ctx = s5_scan(xb_c, s_zero, s5_w_in, s5_lre, s5_lim, s5_w_out)
        ylf, ylr, _ = s5_scan(xb_l, s_ctx, s5_w_in, s5_lre, s5_lim, s5_w_out)
        yc_l = neighbourhood_attention(q_l, k_l, v_l, k_c, v_c, table)
        xn, h2, comb = merge_and_route(x, ya_l, xb_l, ylf, ylr, yc_l, gate_l, mod, *merge_w, MERGE_TOKEN_TILE)
        x = moe_grouped(xn, h2, comb, mod, wg, wu, wd, final_norm_g, tm_moe, not with_ctx_out)
        if with_ctx_out:
            yc_c = context_attention(q_c, k_c, v_c)
            xcn, hc2, comb_c = merge_and_route(xc, ya_c, xb_c, ycf, ycr, yc_c, gate_c, mod_c, *merge_w, tm_c)

            def flat(t):
                return t.reshape(1, bsz * n_ctx, t.shape[-1])

            xc = moe_grouped(flat(xcn), flat(hc2), flat(comb_c), mod_c[:1], wg, wu, wd, final_norm_g,
                             min(MOE_TOKEN_TILE, bsz * n_ctx), False).reshape(bsz, n_ctx, d)
    return x
```

```python
import functools
import math

import jax
import jax.numpy as jnp
import numpy as np
from jax import lax
from jax.experimental import pallas as pl
from jax.experimental.pallas import tpu as pltpu

F32 = jnp.float32
BF16 = jnp.bfloat16

GRID_W = 64
N_BRANCH = 3
SGU_WIDTH = 512
SGU_GROUPS = 4
SGU_CHUNK = 128
S5_WIDTH = 384
S5_GROUP = 16
S5_GROUPS = S5_WIDTH // S5_GROUP
S5_STATE = 64
NA_HEADS = 8
NA_HEAD_DIM = 64
NA_WIDTH = NA_HEADS * NA_HEAD_DIM
NA_KR = 8
NA_KC = 16
ROPE_BASE = 10000.0
N_GROUPS = 4
EXPERTS_PER_GROUP = 8
N_EXPERTS = N_GROUPS * EXPERTS_PER_GROUP
TOP_K = 2
D_EXPERT = 256
ADA_CHUNKS = 6
EPS = 1e-6
NEG_INF = -1e30

LANES = 128
SUBLANES = 8
VMEM_LIMIT_BYTES = 56 * 1024 * 1024

S5_LANES = 2 * S5_GROUPS * S5_STATE
S5_SLABS = S5_LANES // LANES
S5_BLOCKS = S5_WIDTH // LANES
S5_CHUNK = 128
S5_PITCH = S5_CHUNK + 4


def _cparams(*sem):
    return pltpu.CompilerParams(dimension_semantics=sem, vmem_limit_bytes=VMEM_LIMIT_BYTES)


def _full(shape):
    n = len(shape)
    return pl.BlockSpec(shape, lambda *_: (0,) * n, pipeline_mode=pl.Buffered(1))


def _ada_kernel(c_ref, w_ref, b_ref, o_ref):
    c = c_ref[...]
    s = c * jax.nn.sigmoid(c)
    o_ref[...] = jnp.dot(s, w_ref[...], preferred_element_type=F32) + b_ref[...]


def ada_modulation(cc, ada_w, ada_b):
    n_layers, d, n = ada_w.shape
    tn = 1536
    return pl.pallas_call(
        _ada_kernel,
        grid=(n_layers, n // tn),
        in_specs=[
            pl.BlockSpec((SUBLANES, d), lambda l, j: (0, 0)),
            pl.BlockSpec((None, d, tn), lambda l, j: (l, 0, j)),
            pl.BlockSpec((None, 1, tn), lambda l, j: (l, 0, j)),
        ],
        out_specs=pl.BlockSpec((None, SUBLANES, tn), lambda l, j: (l, 0, j)),
        out_shape=jax.ShapeDtypeStruct((n_layers, SUBLANES, n), F32),
        compiler_params=_cparams("parallel", "parallel"),
        name="ada_modulation",
    )(cc, ada_w, ada_b.reshape(n_layers, 1, n))


def _gelu(x):
    return jax.nn.gelu(x)


def _sigmoid(x):
    return 0.5 * jnp.tanh(0.5 * x) + 0.5


def _mixer_in_kernel(x_ref, mod_ref, g_ref, w_ref, lng_ref, sw_ref, sb_ref, *rest, rope):
    if rope:
        cos_ref, sin_ref, swap_ref, ya_ref, b_ref, q_ref, k_ref, v_ref, gate_ref = rest
    else:
        ya_ref, b_ref, q_ref, k_ref, v_ref, gate_ref = rest
    tm = x_ref.shape[0]
    xf = x_ref[...]
    ms = jnp.mean(xf * xf, axis=-1, keepdims=True)
    y = xf * lax.rsqrt(ms + EPS) * g_ref[...]
    h = y * (1.0 + mod_ref[1:2, :]) + mod_ref[0:1, :]
    hb = h.astype(BF16)

    def proj(lo, hi):
        return jnp.dot(hb, w_ref[:, lo:hi], preferred_element_type=F32)

    o1 = 2 * SGU_WIDTH
    o2 = o1 + S5_WIDTH
    oq, ok, ov = o2, o2 + NA_WIDTH, o2 + 2 * NA_WIDTH
    o3 = o2 + 3 * NA_WIDTH

    u = _gelu(proj(0, SGU_WIDTH))
    v = _gelu(proj(SGU_WIDTH, o1))
    vc = v - jnp.mean(v, axis=-1, keepdims=True)
    vn = vc * lax.rsqrt(jnp.mean(vc * vc, axis=-1, keepdims=True) + EPS) * lng_ref[...]
    vb = vn.astype(BF16)
    cw = SGU_WIDTH // SGU_GROUPS
    for c in range(tm // SGU_CHUNK):
        r0 = c * SGU_CHUNK
        for g in range(SGU_GROUPS):
            sp = jnp.dot(sw_ref[g], vb[r0:r0 + SGU_CHUNK, g * cw:(g + 1) * cw],
                         preferred_element_type=F32) + sb_ref[g]
            ya_ref[r0:r0 + SGU_CHUNK, g * cw:(g + 1) * cw] = (
                u[r0:r0 + SGU_CHUNK, g * cw:(g + 1) * cw] * sp).astype(ya_ref.dtype)

    b_ref[...] = proj(o1, o2).astype(b_ref.dtype)

    q = proj(oq, ok)
    k = proj(ok, ov)
    if rope:
        cos = cos_ref[...]
        sin = sin_ref[...]

        def rotate(t):
            ts = jnp.dot(t.astype(BF16), swap_ref[...], preferred_element_type=F32)
            return jnp.concatenate(
                [t[:, j * LANES:(j + 1) * LANES] * cos + ts[:, j * LANES:(j + 1) * LANES] * sin
                 for j in range(NA_WIDTH // LANES)], axis=-1)

        q = rotate(q)
        k = rotate(k)
    q_scale = NA_HEAD_DIM ** -0.5 * (LOG2E if rope else 1.0)
    q_ref[...] = (q * q_scale).astype(q_ref.dtype)
    k_ref[...] = k.astype(k_ref.dtype)
    v_ref[...] = proj(ov, o3).astype(v_ref.dtype)
    gate_ref[...] = _sigmoid(proj(o3, o3 + N_BRANCH * x_ref.shape[1])).astype(gate_ref.dtype)


def mixer_in(x, mod, norm_g, w_in, sgu_norm_g, sgu_w, sgu_bias, rope_tabs, tm):
    bsz, n_tok, d = x.shape
    rope = rope_tabs is not None
    in_specs = [
        pl.BlockSpec((None, tm, d), lambda b, i: (b, i, 0)),
        pl.BlockSpec((None, ADA_CHUNKS, d), lambda b, i: (b, 0, 0)),
        _full((1, d)),
        _full(w_in.shape),
        _full((1, SGU_WIDTH)),
        _full(sgu_w.shape),
        _full(sgu_bias.shape),
    ]
    args = [x, mod, norm_g.reshape(1, d), w_in, sgu_norm_g.reshape(1, SGU_WIDTH), sgu_w, sgu_bias]
    if rope:
        cos_t, sin_t, swap = rope_tabs
        in_specs += [
            pl.BlockSpec((tm, LANES), lambda b, i: (i, 0)),
            pl.BlockSpec((tm, LANES), lambda b, i: (i, 0)),
            _full(swap.shape),
        ]
        args += [cos_t, sin_t, swap]

    def tok(width):
        return pl.BlockSpec((None, tm, width), lambda b, i: (b, i, 0))

    out_shapes = [
        jax.ShapeDtypeStruct((bsz, n_tok, SGU_WIDTH), BF16),
        jax.ShapeDtypeStruct((bsz, n_tok, S5_WIDTH), BF16),
        jax.ShapeDtypeStruct((bsz, n_tok, NA_WIDTH), BF16),
        jax.ShapeDtypeStruct((bsz, n_tok, NA_WIDTH), BF16),
        jax.ShapeDtypeStruct((bsz, n_tok, NA_WIDTH), BF16),
        jax.ShapeDtypeStruct((bsz, n_tok, N_BRANCH * d), BF16),
    ]
    out_specs = [tok(SGU_WIDTH), tok(S5_WIDTH), tok(NA_WIDTH), tok(NA_WIDTH), tok(NA_WIDTH), tok(N_BRANCH * d)]
    return pl.pallas_call(
        functools.partial(_mixer_in_kernel, rope=rope),
        grid=(bsz, n_tok // tm),
        in_specs=in_specs,
        out_specs=out_specs,
        out_shape=out_shapes,
        compiler_params=_cparams("parallel", "parallel"),
        name="mixer_in_rope" if rope else "mixer_in",
    )(*args)


def rope_tables(n_tok):
    pos = np.arange(n_tok)
    rows = (pos // GRID_W).astype(np.float32)
    cols = (pos % GRID_W).astype(np.float32)
    seg = NA_HEAD_DIM // 2
    half = seg // 2
    inv_freq = (ROPE_BASE ** (-np.arange(half, dtype=np.float32) / half)).astype(np.float32)
    ang_r = rows[:, None] * inv_freq
    ang_c = cols[:, None] * inv_freq
    cos = np.concatenate([np.cos(ang_r)] * 2 + [np.cos(ang_c)] * 2, axis=-1)
    sin = np.concatenate([-np.sin(ang_r), np.sin(ang_r), -np.sin(ang_c), np.sin(ang_c)], axis=-1)
    d = np.arange(NA_WIDTH)
    partner = np.where((d % seg) < half, d + half, d - half)
    swap = np.zeros((NA_WIDTH, NA_WIDTH), np.float32)
    swap[partner, d] = 1.0
    reps = LANES // NA_HEAD_DIM
    return (jnp.asarray(np.tile(cos, (1, reps)), F32), jnp.asarray(np.tile(sin, (1, reps)), F32),
            jnp.asarray(swap, BF16))


def _s5_kernel(xf_ref, xr_ref, s0_ref, win_ref, lre_ref, lim_ref, wout_ref, rev_ref,
               yf_ref, yr_ref, send_ref, lhs_scr, st_scr, state_scr):
    bsz, tc, _ = xf_ref.shape
    pitch = S5_PITCH
    rows = bsz * pitch
    half = S5_SLABS // 2
    i = pl.program_id(0)

    @pl.when(i == 0)
    def _():
        state_scr[...] = s0_ref[...]
        lhs_scr[...] = jnp.zeros_like(lhs_scr)

    rev = rev_ref[...]
    for b in range(bsz):
        lhs_scr[0, b * pitch:b * pitch + tc, :] = xf_ref[b].astype(F32)
        lhs_scr[1, b * pitch:b * pitch + tc, :] = jnp.dot(rev, xr_ref[b], preferred_element_type=F32)

    bw = half // S5_BLOCKS
    for d in range(2):
        lhs = lhs_scr[d].astype(BF16)
        for cb in range(S5_BLOCKS):
            res = jnp.dot(lhs[:, cb * LANES:(cb + 1) * LANES], win_ref[d, cb], preferred_element_type=F32)
            for k in range(bw):
                st_scr[cb * bw + k, d * rows:(d + 1) * rows, :] = res[:, k * LANES:(k + 1) * LANES]
                st_scr[half + cb * bw + k, d * rows:(d + 1) * rows, :] = res[:, (bw + k) * LANES:(bw + k + 1) * LANES]

    def step(t, carry):
        out = [None] * S5_SLABS
        for c in range(half):
            sre, sim = carry[c], carry[c + half]
            idx = pl.ds(t, 2 * bsz, stride=pitch)
            lr = lre_ref[c]
            li = lim_ref[c]
            nre = lr * sre - li * sim + st_scr[c, idx, :]
            nim = lr * sim + li * sre + st_scr[c + half, idx, :]
            st_scr[c, idx, :] = nre
            st_scr[c + half, idx, :] = nim
            out[c], out[c + half] = nre, nim
        return tuple(out)

    init = tuple(state_scr[:, c * LANES:(c + 1) * LANES] for c in range(S5_SLABS))
    fin = lax.fori_loop(0, tc, step, init)
    for c in range(S5_SLABS):
        state_scr[:, c * LANES:(c + 1) * LANES] = fin[c]
    send_ref[...] = state_scr[...]

    for d in range(2):
        y_blocks = []
        for cb in range(S5_BLOCKS):
            slabs = [cb * bw + k for k in range(bw)] + [half + cb * bw + k for k in range(bw)]
            s_blk = jnp.concatenate([st_scr[c, d * rows:(d + 1) * rows, :] for c in slabs], axis=-1)
            y_blocks.append(jnp.dot(s_blk.astype(BF16), wout_ref[d, cb], preferred_element_type=F32))
        y = jnp.concatenate(y_blocks, axis=-1)
        for b in range(bsz):
            yb = y[b * pitch:b * pitch + tc, :]
            if d == 0:
                yf_ref[b] = yb
            else:
                yr_ref[b] = jnp.dot(rev, yb.astype(BF16), preferred_element_type=F32)


def s5_scan(xb, s0, w_in, lam_re, lam_im, w_out):
    bsz, n_tok, width = xb.shape
    assert 2 * bsz == SUBLANES and n_tok % S5_CHUNK == 0
    n = n_tok // S5_CHUNK
    tc = S5_CHUNK
    rev = jnp.asarray(np.eye(tc, dtype=np.float32)[::-1], BF16)
    blk = (bsz, tc, width)
    return pl.pallas_call(
        _s5_kernel,
        grid=(n,),
        in_specs=[
            pl.BlockSpec(blk, lambda i: (0, i, 0)),
            pl.BlockSpec(blk, lambda i: (0, n - 1 - i, 0)),
            _full(s0.shape),
            _full(w_in.shape),
            _full(lam_re.shape),
            _full(lam_im.shape),
            _full(w_out.shape),
            _full(rev.shape),
        ],
        out_specs=[
            pl.BlockSpec(blk, lambda i: (0, i, 0)),
            pl.BlockSpec(blk, lambda i: (0, n - 1 - i, 0)),
            _full(s0.shape),
        ],
        out_shape=[
            jax.ShapeDtypeStruct(xb.shape, F32),
            jax.ShapeDtypeStruct(xb.shape, F32),
            jax.ShapeDtypeStruct(s0.shape, F32),
        ],
        scratch_shapes=[
            pltpu.VMEM((2, bsz * S5_PITCH, width), F32),
            pltpu.VMEM((S5_SLABS, 2 * bsz * S5_PITCH, LANES), F32),
            pltpu.VMEM(s0.shape, F32),
        ],
        compiler_params=_cparams("arbitrary"),
        name="s5_scan",
    )(xb, xb, s0, w_in, lam_re, lam_im, w_out, rev)


NA_ROWS_PER_STEP = 4
_NT = (((1,), (1,)), ((), ()))


def _na_head_pair(qp, kp, vp, kcp, vcp, bias_fn):
    lane_head = lax.broadcasted_iota(jnp.int32, qp.shape, 1) // NA_HEAD_DIM
    o_pair = None
    for hh in range(2):
        qm = jnp.where(lane_head == hh, qp, jnp.zeros_like(qp))
        s_ctx = lax.dot_general(qm, kcp, _NT, preferred_element_type=F32)
        m = jnp.max(s_ctx, axis=-1, keepdims=True)
        if kp is not None:
            s_win = lax.dot_general(qm, kp, _NT, preferred_element_type=F32) + bias_fn(hh)
            m = jnp.maximum(m, jnp.max(s_win, axis=-1, keepdims=True))
            e_win = jnp.exp(s_win - m)
        e_ctx = jnp.exp(s_ctx - m)
        den = jnp.sum(e_ctx, axis=-1, keepdims=True)
        o = jnp.dot(e_ctx.astype(BF16), vcp, preferred_element_type=F32)
        if kp is not None:
            den = den + jnp.sum(e_win, axis=-1, keepdims=True)
            o = o + jnp.dot(e_win.astype(BF16), vp, preferred_element_type=F32)
        o = o * (1.0 / den)
        o_pair = o if hh == 0 else jnp.where(lane_head == 0, o_pair, o)
    return o_pair


NA_UNION_ROWS = 12
NA_SLAB = 32
LOG2E = math.log2(math.e)
NA_TAB_LEFT_OUT = 2 * NA_KR
NA_TAB_RIGHT_OUT = NA_TAB_LEFT_OUT + 2 * NA_KR - 1
NA_TAB_BOTH_OUT = NA_TAB_RIGHT_OUT + 2 * NA_KR - 1


def _na_union_start(i, n_rows):
    return jnp.clip(i * NA_ROWS_PER_STEP - NA_KR // 2, 0, n_rows - NA_UNION_ROWS)


def _na_kernel(q_ref, k_ref, v_ref, kc_ref, vc_ref, tab_ref, o_ref, s_scr, p_scr, rden_scr, *, n_rows):
    i = pl.program_id(1)
    r0 = i * NA_ROWS_PER_STEP
    ks = _na_union_start(i, n_rows)
    n_keys = NA_UNION_ROWS * GRID_W
    n_q = NA_ROWS_PER_STEP * GRID_W
    n_pairs = NA_UNION_ROWS // 2
    slabs_per_row = GRID_W // NA_SLAB

    def table_entry(rr, jj):
        r = r0 + rr
        kst = jnp.clip(r - NA_KR // 2, 0, n_rows - NA_KR)
        key0 = ks + 2 * jj
        out0 = jnp.logical_or(key0 < kst, key0 >= kst + NA_KR)
        out1 = jnp.logical_or(key0 + 1 < kst, key0 + 1 >= kst + NA_KR)
        e = key0 - r + NA_KR
        both_in = jnp.clip(e, 0, 2 * NA_KR - 1)
        left_out = NA_TAB_LEFT_OUT + jnp.clip(e, 0, 2 * NA_KR - 2)
        right_out = NA_TAB_RIGHT_OUT + jnp.clip(e - 1, 0, 2 * NA_KR - 2)
        return jnp.where(out0, jnp.where(out1, NA_TAB_BOTH_OUT, left_out), jnp.where(out1, right_out, both_in))

    entries = [[table_entry(rr, jj) for jj in range(n_pairs)] for rr in range(NA_ROWS_PER_STEP)]

    lane_head = lax.broadcasted_iota(jnp.int32, (n_q, LANES), 1) // NA_HEAD_DIM

    def scores(p):
        ls = slice(p * LANES, (p + 1) * LANES)
        qp = q_ref[:, ls]
        zero = jnp.zeros_like(qp)
        q_stack = jnp.concatenate([jnp.where(lane_head == 0, qp, zero), jnp.where(lane_head == 1, qp, zero)], axis=0)
        k_all = jnp.concatenate([k_ref[0, :, ls], kc_ref[:, ls]], axis=0)
        s_scr[p] = lax.dot_general(q_stack, k_all, _NT, preferred_element_type=F32)

    scores(0)
    for p in range(NA_HEADS // 2):
        if p + 1 < NA_HEADS // 2:
            scores(p + 1)
        ls = slice(p * LANES, (p + 1) * LANES)
        v_all = jnp.concatenate([v_ref[0, :, ls], vc_ref[:, ls]], axis=0)
        s_p, p_p, rden_p = s_scr.at[p], p_scr.at[p], rden_scr.at[p]
        for sl in range(2 * n_q // NA_SLAB):
            hh = sl // (NA_ROWS_PER_STEP * slabs_per_row)
            rr = (sl // slabs_per_row) % NA_ROWS_PER_STEP
            q0 = (sl % slabs_per_row) * NA_SLAB
            rows = slice(sl * NA_SLAB, (sl + 1) * NA_SLAB)
            bias = jnp.concatenate(
                [tab_ref[2 * p + hh, entries[rr][jj], q0:q0 + NA_SLAB, :] for jj in range(n_pairs)], axis=-1)
            s_win = s_p[rows, :n_keys] + bias
            s_ctx = s_p[rows, n_keys:]
            m = jnp.maximum(jnp.max(s_win, axis=-1, keepdims=True), jnp.max(s_ctx, axis=-1, keepdims=True))
            e_win = jnp.exp2(s_win - m)
            e_ctx = jnp.exp2(s_ctx - m)
            den = jnp.sum(e_win, axis=-1, keepdims=True) + jnp.sum(e_ctx, axis=-1, keepdims=True)
            p_p[rows, :n_keys] = e_win.astype(BF16)
            p_p[rows, n_keys:] = e_ctx.astype(BF16)
            rden_p[rows, :] = jnp.broadcast_to(1.0 / den, (NA_SLAB, LANES))
        o = jnp.dot(p_p[...], v_all, preferred_element_type=F32) * rden_p[...]
        o_ref[:, ls] = jnp.where(lane_head == 0, o[:n_q], o[n_q:]).astype(o_ref.dtype)


def na_bias_table(rpb):
    w = np.arange(GRID_W)
    col_start = np.clip(w - NA_KC // 2, 0, GRID_W - NA_KC)
    col_mask = (w[None, :] >= col_start[:, None]) & (w[None, :] < col_start[:, None] + NA_KC)
    d_col = np.clip(w[None, :] - w[:, None], -(NA_KC - 1), NA_KC - 1) + (NA_KC - 1)
    pick = (d_col[None] == np.arange(2 * NA_KC - 1)[:, None, None]).astype(np.float32)
    full = jnp.einsum('hdj,jwu->hdwu', rpb.astype(F32), jnp.asarray(pick), precision=lax.Precision.HIGHEST)
    full = jnp.where(col_mask[None, None], full * LOG2E, NEG_INF)
    pad = jnp.zeros_like(full[:, :1])
    neg = jnp.full_like(full, NEG_INF)
    both_in = jnp.concatenate([jnp.concatenate([pad, full], axis=1), jnp.concatenate([full, pad], axis=1)], axis=-1)
    left_out = jnp.concatenate([neg, full], axis=-1)
    right_out = jnp.concatenate([full, neg], axis=-1)
    both_out = jnp.concatenate([neg[:, :1], neg[:, :1]], axis=-1)
    return jnp.concatenate([both_in, left_out, right_out, both_out], axis=1)


def neighbourhood_attention(q, k, v, kc, vc, table):
    bsz, n_tok, width = q.shape
    n_ctx = kc.shape[1]
    n_rows = n_tok // GRID_W
    assert n_rows >= NA_UNION_ROWS and n_rows % NA_ROWS_PER_STEP == 0
    tq = NA_ROWS_PER_STEP * GRID_W
    n_keys = NA_UNION_ROWS * GRID_W
    window = pl.BlockSpec((pl.Element(1), pl.Element(n_keys), pl.Element(width)),
                          lambda b, i: (b, _na_union_start(i, n_rows) * GRID_W, 0))
    return pl.pallas_call(
        functools.partial(_na_kernel, n_rows=n_rows),
        grid=(bsz, n_rows // NA_ROWS_PER_STEP),
        in_specs=[
            pl.BlockSpec((None, tq, width), lambda b, i: (b, i, 0)),
            window,
            window,
            pl.BlockSpec((None, n_ctx, width), lambda b, i: (b, 0, 0)),
            pl.BlockSpec((None, n_ctx, width), lambda b, i: (b, 0, 0)),
            _full(table.shape),
        ],
        out_specs=pl.BlockSpec((None, tq, width), lambda b, i: (b, i, 0)),
        out_shape=jax.ShapeDtypeStruct(q.shape, BF16),
        scratch_shapes=[
            pltpu.VMEM((NA_HEADS // 2, 2 * tq, n_keys + n_ctx), F32),
            pltpu.VMEM((NA_HEADS // 2, 2 * tq, n_keys + n_ctx), BF16),
            pltpu.VMEM((NA_HEADS // 2, 2 * tq, LANES), F32),
        ],
        compiler_params=_cparams("parallel", "arbitrary"),
        name="neighbourhood_attention",
    )(q, k, v, kc, vc, table)


def _ctx_attn_kernel(q_ref, k_ref, v_ref, o_ref):
    for p in range(NA_HEADS // 2):
        ls = slice(p * LANES, (p + 1) * LANES)
        o_pair = _na_head_pair(q_ref[:, ls], None, None, k_ref[:, ls], v_ref[:, ls], None)
        o_ref[:, ls] = o_pair.astype(o_ref.dtype)


def context_attention(qc, kc, vc):
    bsz, n_ctx, width = qc.shape
    spec = pl.BlockSpec((None, n_ctx, width), lambda b: (b, 0, 0))
    return pl.pallas_call(
        _ctx_attn_kernel,
        grid=(bsz,),
        in_specs=[spec, spec, spec],
        out_specs=spec,
        out_shape=jax.ShapeDtypeStruct(qc.shape, BF16),
        compiler_params=_cparams("parallel"),
        name="context_attention",
    )(qc, kc, vc)


ROUTER_LANES = LANES
EXPERT_LANE0 = N_GROUPS
GROUP_ID_LANE = 0
RANK_LANE = 1


def _route(logits):
    lane = lax.broadcasted_iota(jnp.int32, logits.shape, 1)
    big = jnp.int32(ROUTER_LANES)
    is_g = lane < N_GROUPS
    lg = jnp.where(is_g, logits, -jnp.inf)
    mg = jnp.max(lg, axis=-1, keepdims=True)
    grp = jnp.min(jnp.where(lg == mg, lane, big), axis=-1, keepdims=True)
    g_weight = 1.0 / jnp.sum(jnp.where(is_g, jnp.exp(logits - mg), 0.0), axis=-1, keepdims=True)
    e_idx = lane - EXPERT_LANE0
    sel = (e_idx >= 0) & (e_idx < N_EXPERTS) & ((e_idx // EXPERTS_PER_GROUP) == grp)
    ls1 = jnp.where(sel, logits, -jnp.inf)
    v1 = jnp.max(ls1, axis=-1, keepdims=True)
    i1 = jnp.min(jnp.where(ls1 == v1, lane, big), axis=-1, keepdims=True)
    ls2 = jnp.where(lane == i1, -jnp.inf, ls1)
    v2 = jnp.max(ls2, axis=-1, keepdims=True)
    i2 = jnp.min(jnp.where(ls2 == v2, lane, big), axis=-1, keepdims=True)
    e2 = jnp.exp(v2 - v1)
    w1 = 1.0 / (1.0 + e2)
    w2 = e2 * w1
    comb = g_weight * (jnp.where(lane == i1, w1, 0.0) + jnp.where(lane == i2, w2, 0.0))
    return jnp.where(lane == GROUP_ID_LANE, grp.astype(F32), comb)


def _merge_kernel(x_ref, ya_ref, xb_ref, yf_ref, yr_ref, yc_ref, gate_ref, mod_ref, d_ref, glu_ref,
                  wa_ref, wb_ref, wc_ref, wo_ref, g_ref, rw_ref, rb_ref, own_ref, xn_ref, h_ref, comb_ref):
    d = x_ref.shape[1]
    yb = d_ref[...] * xb_ref[...].astype(F32) + yf_ref[...] + yr_ref[...]
    yb = _gelu(yb)
    yb = yb * _sigmoid(jnp.dot(yb.astype(BF16), glu_ref[...], preferred_element_type=F32))

    def gate(j):
        return gate_ref[:, j * d:(j + 1) * d].astype(F32)

    m = gate(0) * jnp.dot(ya_ref[...], wa_ref[...], preferred_element_type=F32)
    m = m + gate(1) * jnp.dot(yb.astype(BF16), wb_ref[...], preferred_element_type=F32)
    m = m + gate(2) * jnp.dot(yc_ref[...], wc_ref[...], preferred_element_type=F32)
    xn = x_ref[...] + mod_ref[2:3, :] * jnp.dot(m.astype(BF16), wo_ref[...], preferred_element_type=F32)
    xn_ref[...] = xn
    ms = jnp.mean(xn * xn, axis=-1, keepdims=True)
    h = xn * lax.rsqrt(ms + EPS) * g_ref[...]
    h = h * (1.0 + mod_ref[4:5, :]) + mod_ref[3:4, :]
    h_ref[:, :d] = h.astype(h_ref.dtype)
    logits = jnp.dot(h, rw_ref[...], preferred_element_type=F32) + rb_ref[...]
    comb = _route(logits)
    comb_ref[...] = comb
    own = jnp.dot(comb, own_ref[...], preferred_element_type=F32)
    own_hi, own_lo = _split_bf16(own)
    lane = lax.broadcasted_iota(jnp.int32, own.shape, 1)
    h_ref[:, d:] = jnp.where(lane < EXPERTS_PER_GROUP, own_hi, own_lo)


def merge_and_route(x, ya, xb, yf, yr, yc, gates, mod, s5_d, glu_w, w_br_a, w_br_b, w_br_c, w_out,
                    norm_ffn_g, router_w, router_b, tm):
    bsz, n_tok, d = x.shape

    def tok(width):
        return pl.BlockSpec((None, tm, width), lambda b, i: (b, i, 0))

    own = np.zeros((ROUTER_LANES, ROUTER_LANES), np.float32)
    for e in range(N_EXPERTS):
        own[EXPERT_LANE0 + e, e % EXPERTS_PER_GROUP] = 1.0
        own[EXPERT_LANE0 + e, EXPERTS_PER_GROUP + e % EXPERTS_PER_GROUP] = 1.0
    weights = [s5_d.reshape(1, S5_WIDTH), glu_w, w_br_a, w_br_b, w_br_c, w_out,
               norm_ffn_g.reshape(1, d), router_w, router_b, jnp.asarray(own)]
    return pl.pallas_call(
        _merge_kernel,
        grid=(bsz, n_tok // tm),
        in_specs=[tok(d), tok(SGU_WIDTH), tok(S5_WIDTH), tok(S5_WIDTH), tok(S5_WIDTH), tok(NA_WIDTH),
                  tok(N_BRANCH * d), pl.BlockSpec((None, ADA_CHUNKS, d), lambda b, i: (b, 0, 0))]
        + [_full(w.shape) for w in weights],
        out_specs=[tok(d), tok(d + ROUTER_LANES), tok(ROUTER_LANES)],
        out_shape=[
            jax.ShapeDtypeStruct(x.shape, F32),
            jax.ShapeDtypeStruct((bsz, n_tok, d + ROUTER_LANES), BF16),
            jax.ShapeDtypeStruct((bsz, n_tok, ROUTER_LANES), F32),
        ],
        compiler_params=_cparams("parallel", "parallel"),
        name="merge_and_route",
    )(x, ya, xb, yf, yr, yc, gates, mod, *weights)


def router_params(rg_w, rg_b, re_w, re_b):
    d = rg_w.shape[0]
    pad = ROUTER_LANES - N_GROUPS - N_EXPERTS
    w = jnp.concatenate([rg_w, re_w, jnp.zeros((d, pad), F32)], axis=1).astype(F32)
    b = jnp.concatenate([rg_b, re_b, jnp.zeros((pad,), F32)]).astype(F32).reshape(1, ROUTER_LANES)
    return w, b


MOE_BLOCK = 144


def _split_bf16(x):
    hi = x.astype(BF16)
    return hi, (x - hi.astype(F32)).astype(BF16)


MOE_SUBTILE = 512


def _moe_kernel(xn_ref, hx_ref, comb_ref, mod_ref, wg_ref, wu_ref, wd_ref, fg_ref, o_ref,
                aux_col, aux_row, hid_scr, cnt_ref, *, final_norm):
    g = pl.program_id(2)
    tm, d = o_ref.shape
    st = min(MOE_SUBTILE, tm)
    n_sub = tm // st

    @pl.when(g == 0)
    def _():
        o_ref[...] = xn_ref[...]
        row_i = lax.broadcasted_iota(jnp.int32, (st, st), 0)
        col_i = lax.broadcasted_iota(jnp.int32, (st, st), 1)
        tri = jnp.where(col_i < row_i, 1.0, 0.0).astype(BF16)
        for s in range(n_sub):
            comb = comb_ref[s * st:(s + 1) * st, :]
            lane = lax.broadcasted_iota(jnp.int32, comb.shape, 1)
            grp = comb[:, GROUP_ID_LANE:GROUP_ID_LANE + 1]
            onehot = jnp.where(lane < N_GROUPS, jnp.where(lane.astype(F32) == grp, 1.0, 0.0), 0.0)
            ranks = jnp.dot(tri, onehot.astype(BF16), preferred_element_type=F32)
            own = jnp.sum(onehot * ranks, axis=-1, keepdims=True)
            aux = jnp.where(lane == GROUP_ID_LANE, grp, jnp.where(lane == RANK_LANE, own, 0.0))
            aux_col[s * st:(s + 1) * st, :] = aux
            aux_row[:, s * st:(s + 1) * st] = aux.T
            for gg in range(N_GROUPS):
                cnt_ref[s * N_GROUPS + gg] = jnp.sum(onehot[:, gg:gg + 1]).astype(jnp.int32)

    gf = g.astype(F32)
    slot_r = lax.broadcasted_iota(jnp.int32, (MOE_BLOCK, st), 0).astype(F32)
    slot_c = lax.broadcasted_iota(jnp.int32, (st, MOE_BLOCK), 1).astype(F32)
    scale = mod_ref[5:6, :]

    for s in range(n_sub):
        rows = slice(s * st, (s + 1) * st)
        rank_row = jnp.where(aux_row[GROUP_ID_LANE:GROUP_ID_LANE + 1, rows] == gf,
                             aux_row[RANK_LANE:RANK_LANE + 1, rows], -1.0)
        rank_col = jnp.where(aux_col[rows, GROUP_ID_LANE:GROUP_ID_LANE + 1] == gf,
                             aux_col[rows, RANK_LANE:RANK_LANE + 1], -1.0)
        n_blocks = (cnt_ref[s * N_GROUPS + g] + MOE_BLOCK - 1) // MOE_BLOCK

        def block(j, carry, rows=rows, rank_row=rank_row, rank_col=rank_col):
            base = (j * MOE_BLOCK).astype(F32)
            gather = jnp.where(rank_row - base == slot_r, 1.0, 0.0).astype(BF16)
            hcx = jnp.dot(gather, hx_ref[rows, :], preferred_element_type=F32)
            hc = hcx[:, :d].astype(BF16)
            wt = hcx[:, d:]
            for e in range(EXPERTS_PER_GROUP):
                a = jnp.dot(hc, wg_ref[e], preferred_element_type=F32)
                u = jnp.dot(hc, wu_ref[e], preferred_element_type=F32)
                cw = wt[:, e:e + 1] + wt[:, EXPERTS_PER_GROUP + e:EXPERTS_PER_GROUP + e + 1]
                hid_scr[:, e * D_EXPERT:(e + 1) * D_EXPERT] = (a * jax.nn.sigmoid(a) * u * cw).astype(BF16)
            oc = jnp.dot(hid_scr[...], wd_ref[...], preferred_element_type=F32).astype(BF16)
            scatter = jnp.where(rank_col - base == slot_c, 1.0, 0.0).astype(BF16)
            o_ref[rows, :] += scale * jnp.dot(scatter, oc, preferred_element_type=F32)
            return carry

        lax.fori_loop(0, n_blocks, block, 0)

    if final_norm:
        @pl.when(g == pl.num_programs(2) - 1)
        def _():
            xo = o_ref[...]
            ms = jnp.mean(xo * xo, axis=-1, keepdims=True)
            o_ref[...] = xo * lax.rsqrt(ms + EPS) * fg_ref[...]


def moe_grouped(xn, hx, comb, mod, wg, wu, wd, final_g, tm, final_norm):
    bsz, n_tok, d = xn.shape
    gw = EXPERTS_PER_GROUP * D_EXPERT
    n_sub = tm // min(MOE_SUBTILE, tm)

    def tok(width, buffers=2):
        return pl.BlockSpec((None, tm, width), lambda b, i, g: (b, i, 0), pipeline_mode=pl.Buffered(buffers))

    return pl.pallas_call(
        functools.partial(_moe_kernel, final_norm=final_norm),
        grid=(bsz, n_tok // tm, N_GROUPS),
        in_specs=[
            tok(d, 1), tok(d + ROUTER_LANES), tok(ROUTER_LANES),
            pl.BlockSpec((None, ADA_CHUNKS, d), lambda b, i, g: (b, 0, 0)),
            pl.BlockSpec((None, EXPERTS_PER_GROUP, d, D_EXPERT), lambda b, i, g: (g, 0, 0, 0)),
            pl.BlockSpec((None, EXPERTS_PER_GROUP, d, D_EXPERT), lambda b, i, g: (g, 0, 0, 0)),
            pl.BlockSpec((None, gw, d), lambda b, i, g: (g, 0, 0)),
            pl.BlockSpec((1, d), lambda b, i, g: (0, 0)),
        ],
        out_specs=tok(d, 1),
        out_shape=jax.ShapeDtypeStruct(xn.shape, F32),
        scratch_shapes=[
            pltpu.VMEM((tm, ROUTER_LANES), F32),
            pltpu.VMEM((ROUTER_LANES, tm), F32),
            pltpu.VMEM((MOE_BLOCK, gw), BF16),
            pltpu.SMEM((n_sub * N_GROUPS,), jnp.int32),
        ],
        compiler_params=_cparams("parallel", "parallel", "arbitrary"),
        name="moe_grouped",
    )(xn, hx, comb, mod, wg, wu, wd, final_g.reshape(1, d))


def moe_params(e_gate, e_up, e_down):
    _, d, f = e_gate.shape
    wg = e_gate.astype(BF16).reshape(N_GROUPS, EXPERTS_PER_GROUP, d, f)
    wu = e_up.astype(BF16).reshape(N_GROUPS, EXPERTS_PER_GROUP, d, f)
    wd = e_down.astype(BF16).reshape(N_GROUPS, EXPERTS_PER_GROUP * f, d)
    return wg, wu, wd


def s5_params(a_re, a_im, log_dt, b_re, b_im, c_re, c_im, bsz):
    lam = lax.complex(a_re.astype(F32), a_im.astype(F32))
    dt = jnp.exp(log_dt.astype(F32))[..., None]
    lam_bar = jnp.exp(lam * dt)
    b_bar = ((lam_bar - 1) / lam)[..., None] * lax.complex(b_re.astype(F32), b_im.astype(F32))
    gpb = S5_GROUPS // S5_BLOCKS
    eye = jnp.eye(gpb, dtype=F32)
    gp = S5_GROUPS * S5_STATE

    def in_mat(m):
        m = m.reshape(2, S5_BLOCKS, gpb, S5_STATE, S5_GROUP)
        return jnp.einsum('dkgpc,gh->dkgchp', m, eye).reshape(2, S5_BLOCKS, gpb * S5_GROUP, gpb * S5_STATE)

    def out_mat(m):
        m = m.reshape(2, S5_BLOCKS, gpb, S5_GROUP, S5_STATE)
        return jnp.einsum('dkgcp,gh->dkgphc', m, eye).reshape(2, S5_BLOCKS, gpb * S5_STATE, gpb * S5_GROUP)

    w_in = jnp.concatenate([in_mat(b_bar.real), in_mat(b_bar.imag)], axis=-1).astype(BF16)
    w_out = jnp.concatenate([out_mat(c_re.astype(F32)), -out_mat(c_im.astype(F32))], axis=2).astype(BF16)

    def tiles(v):
        t = v.reshape(2, gp // LANES, 1, LANES)
        t = jnp.broadcast_to(t, (2, gp // LANES, bsz, LANES))
        return jnp.concatenate([t[0], t[1]], axis=1)

    lam_flat = lam_bar.reshape(2, gp)
    return w_in, tiles(lam_flat.real), tiles(lam_flat.imag), w_out


TOKEN_TILE = 256
MIXER_TOKEN_TILE = 512
MERGE_TOKEN_TILE = 512
MOE_TOKEN_TILE = 2048


def kernel(x, c, ctx, c_ctx, ada_w, ada_b, norm_mix_g, norm_ffn_g, w_in, sgu_norm_g, sgu_w, sgu_b, s5_a_re, s5_a_im, s5_log_dt, s5_b_re, s5_b_im, s5_c_re, s5_c_im, s5_d, s5_glu_w, na_rpb, w_br_a, w_br_b, w_br_c, w_out, router_group_w, router_group_b, router_expert_w, router_expert_b, exp_w_gate, exp_w_up, exp_w_down, final_norm_g):
    bsz, n_tok, d = x.shape
    n_ctx = ctx.shape[1]
    depth = ada_w.shape[0]
    assert bsz + 1 <= SUBLANES

    cc = jnp.concatenate([c, c_ctx[None], jnp.zeros((SUBLANES - bsz - 1, d), F32)], axis=0)
    mod_all = ada_modulation(cc, ada_w, ada_b)
    rope_tabs = rope_tables(n_tok)
    s_zero = jnp.zeros((2 * bsz, S5_LANES), F32)
    tm_c = min(TOKEN_TILE, n_ctx)
    tm_moe = min(MOE_TOKEN_TILE, n_tok)

    xc = ctx
    for l in range(depth):
        with_ctx_out = l < depth - 1
        mod = mod_all[l, :bsz].reshape(bsz, ADA_CHUNKS, d)
        mod_c = jnp.broadcast_to(mod_all[l, bsz].reshape(1, ADA_CHUNKS, d), (bsz, ADA_CHUNKS, d))
        w_in_l = w_in[l].astype(BF16)
        sgu_w_l = sgu_w[l].astype(BF16)
        sgu_bias = jnp.broadcast_to(sgu_b[l].astype(F32)[:, :, None], (SGU_GROUPS, SGU_CHUNK, SGU_CHUNK))
        s5_w_in, s5_lre, s5_lim, s5_w_out = s5_params(
            s5_a_re[l], s5_a_im[l], s5_log_dt[l], s5_b_re[l], s5_b_im[l], s5_c_re[l], s5_c_im[l], bsz)
        table = na_bias_table(na_rpb[l])
        r_w, r_b = router_params(router_group_w[l], router_group_b[l], router_expert_w[l], router_expert_b[l])
        wg, wu, wd = moe_params(exp_w_gate[l], exp_w_up[l], exp_w_down[l])
        merge_w = (s5_d[l].astype(F32), s5_glu_w[l].astype(BF16), w_br_a[l].astype(BF16),
                   w_br_b[l].astype(BF16), w_br_c[l].astype(BF16), w_out[l].astype(BF16),
                   norm_ffn_g[l].astype(F32), r_w, r_b)

        ya_c, xb_c, q_c, k_c, v_c, gate_c = mixer_in(
            xc, mod_c, norm_mix_g[l], w_in_l, sgu_norm_g[l], sgu_w_l, sgu_bias, None, tm_c)
        ya_l, xb_l, q_l, k_l, v_l, gate_l = mixer_in(
            x, mod, norm_mix_g[l], w_in_l, sgu_norm_g[l], sgu_w_l, sgu_bias, rope_tabs, MIXER_TOKEN_TILE)
        ycf, ycr, s_ctx = s5_scan(xb_c, s_zero, s5_w_in, s5_lre, s5_lim, s5_w_out)
        ylf, ylr, _ = s5_scan(xb_l, s_ctx, s5_w_in, s5_lre, s5_lim, s5_w_out)
        yc_l = neighbourhood_attention(q_l, k_l, v_l, k_c, v_c, table)
        xn, h2, comb = merge_and_route(x, ya_l, xb_l, ylf, ylr, yc_l, gate_l, mod, *merge_w, MERGE_TOKEN_TILE)
        x = moe_grouped(xn, h2, comb, mod, wg, wu, wd, final_norm_g, tm_moe, not with_ctx_out)
        if with_ctx_out:
            yc_c = context_attention(q_c, k_c, v_c)
            xcn, hc2, comb_c = merge_and_route(xc, ya_c, xb_c, ycf, ycr, yc_c, gate_c, mod_c, *merge_w, tm_c)

            def flat(t):
                return t.reshape(1, bsz * n_ctx, t.shape[-1])

            xc = moe_grouped(flat(xcn), flat(hc2), flat(comb_c), mod_c[:1], wg, wu, wd, final_norm_g,
                             min(MOE_TOKEN_TILE, bsz * n_ctx), False).reshape(bsz, n_ctx, d)
    return x
```

```python
import functools
import math

import jax
import jax.numpy as jnp
import numpy as np
from jax import lax
from jax.experimental import pallas as pl
from jax.experimental.pallas import tpu as pltpu

F32 = jnp.float32
BF16 = jnp.bfloat16

GRID_W = 64
N_BRANCH = 3
SGU_WIDTH = 512
SGU_GROUPS = 4
SGU_CHUNK = 128
S5_WIDTH = 384
S5_GROUP = 16
S5_GROUPS = S5_WIDTH // S5_GROUP
S5_STATE = 64
NA_HEADS = 8
NA_HEAD_DIM = 64
NA_WIDTH = NA_HEADS * NA_HEAD_DIM
NA_KR = 8
NA_KC = 16
ROPE_BASE = 10000.0
N_GROUPS = 4
EXPERTS_PER_GROUP = 8
N_EXPERTS = N_GROUPS * EXPERTS_PER_GROUP
TOP_K = 2
D_EXPERT = 256
ADA_CHUNKS = 6
EPS = 1e-6
NEG_INF = -1e30

LANES = 128
SUBLANES = 8
VMEM_LIMIT_BYTES = 56 * 1024 * 1024

S5_LANES = 2 * S5_GROUPS * S5_STATE
S5_SLABS = S5_LANES // LANES
S5_BLOCKS = S5_WIDTH // LANES
S5_CHUNK = 128
S5_PITCH = S5_CHUNK + 4


def _cparams(*sem):
    return pltpu.CompilerParams(dimension_semantics=sem, vmem_limit_bytes=VMEM_LIMIT_BYTES)


def _full(shape):
    n = len(shape)
    return pl.BlockSpec(shape, lambda *_: (0,) * n, pipeline_mode=pl.Buffered(1))


def _ada_kernel(c_ref, w_ref, b_ref, o_ref):
    c = c_ref[...]
    s = c * jax.nn.sigmoid(c)
    o_ref[...] = jnp.dot(s, w_ref[...], preferred_element_type=F32) + b_ref[...]


def ada_modulation(cc, ada_w, ada_b):
    n_layers, d, n = ada_w.shape
    tn = 1536
    return pl.pallas_call(
        _ada_kernel,
        grid=(n_layers, n // tn),
        in_specs=[
            pl.BlockSpec((SUBLANES, d), lambda l, j: (0, 0)),
            pl.BlockSpec((None, d, tn), lambda l, j: (l, 0, j)),
            pl.BlockSpec((None, 1, tn), lambda l, j: (l, 0, j)),
        ],
        out_specs=pl.BlockSpec((None, SUBLANES, tn), lambda l, j: (l, 0, j)),
        out_shape=jax.ShapeDtypeStruct((n_layers, SUBLANES, n), F32),
        compiler_params=_cparams("parallel", "parallel"),
        name="ada_modulation",
    )(cc, ada_w, ada_b.reshape(n_layers, 1, n))


def _gelu(x):
    return jax.nn.gelu(x)


def _sigmoid(x):
    return 0.5 * jnp.tanh(0.5 * x) + 0.5


def _mixer_in_kernel(x_ref, mod_ref, g_ref, w_ref, lng_ref, sw_ref, sb_ref, *rest, rope):
    if rope:
        cos_ref, sin_ref, swap_ref, ya_ref, b_ref, q_ref, k_ref, v_ref, gate_ref = rest
    else:
        ya_ref, b_ref, q_ref, k_ref, v_ref, gate_ref = rest
    tm = x_ref.shape[0]
    xf = x_ref[...]
    ms = jnp.mean(xf * xf, axis=-1, keepdims=True)
    y = xf * lax.rsqrt(ms + EPS) * g_ref[...]
    h = y * (1.0 + mod_ref[1:2, :]) + mod_ref[0:1, :]
    hb = h.astype(BF16)

    def proj(lo, hi):
        return jnp.dot(hb, w_ref[:, lo:hi], preferred_element_type=F32)

    o1 = 2 * SGU_WIDTH
    o2 = o1 + S5_WIDTH
    oq, ok, ov = o2, o2 + NA_WIDTH, o2 + 2 * NA_WIDTH
    o3 = o2 + 3 * NA_WIDTH

    u = _gelu(proj(0, SGU_WIDTH))
    v = _gelu(proj(SGU_WIDTH, o1))
    vc = v - jnp.mean(v, axis=-1, keepdims=True)
    vn = vc * lax.rsqrt(jnp.mean(vc * vc, axis=-1, keepdims=True) + EPS) * lng_ref[...]
    vb = vn.astype(BF16)
    cw = SGU_WIDTH // SGU_GROUPS
    for c in range(tm // SGU_CHUNK):
        r0 = c * SGU_CHUNK
        for g in range(SGU_GROUPS):
            sp = jnp.dot(sw_ref[g], vb[r0:r0 + SGU_CHUNK, g * cw:(g + 1) * cw],
                         preferred_element_type=F32) + sb_ref[g]
            ya_ref[r0:r0 + SGU_CHUNK, g * cw:(g + 1) * cw] = (
                u[r0:r0 + SGU_CHUNK, g * cw:(g + 1) * cw] * sp).astype(ya_ref.dtype)

    b_ref[...] = proj(o1, o2).astype(b_ref.dtype)

    q = proj(oq, ok)
    k = proj(ok, ov)
    if rope:
        cos = cos_ref[...]
        sin = sin_ref[...]

        def rotate(t):
            ts = jnp.dot(t.astype(BF16), swap_ref[...], preferred_element_type=F32)
            return jnp.concatenate(
                [t[:, j * LANES:(j + 1) * LANES] * cos + ts[:, j * LANES:(j + 1) * LANES] * sin
                 for j in range(NA_WIDTH // LANES)], axis=-1)

        q = rotate(q)
        k = rotate(k)
    q_scale = NA_HEAD_DIM ** -0.5 * (LOG2E if rope else 1.0)
    q_ref[...] = (q * q_scale).astype(q_ref.dtype)
    k_ref[...] = k.astype(k_ref.dtype)
    v_ref[...] = proj(ov, o3).astype(v_ref.dtype)
    gate_ref[...] = _sigmoid(proj(o3, o3 + N_BRANCH * x_ref.shape[1])).astype(gate_ref.dtype)


def mixer_in(x, mod, norm_g, w_in, sgu_norm_g, sgu_w, sgu_bias, rope_tabs, tm):
    bsz, n_tok, d = x.shape
    rope = rope_tabs is not None
    in_specs = [
        pl.BlockSpec((None, tm, d), lambda b, i: (b, i, 0)),
        pl.BlockSpec((None, ADA_CHUNKS, d), lambda b, i: (b, 0, 0)),
        _full((1, d)),
        _full(w_in.shape),
        _full((1, SGU_WIDTH)),
        _full(sgu_w.shape),
        _full(sgu_bias.shape),
    ]
    args = [x, mod, norm_g.reshape(1, d), w_in, sgu_norm_g.reshape(1, SGU_WIDTH), sgu_w, sgu_bias]
    if rope:
        cos_t, sin_t, swap = rope_tabs
        in_specs += [
            pl.BlockSpec((tm, LANES), lambda b, i: (i, 0)),
            pl.BlockSpec((tm, LANES), lambda b, i: (i, 0)),
            _full(swap.shape),
        ]
        args += [cos_t, sin_t, swap]

    def tok(width):
        return pl.BlockSpec((None, tm, width), lambda b, i: (b, i, 0))

    out_shapes = [
        jax.ShapeDtypeStruct((bsz, n_tok, SGU_WIDTH), BF16),
        jax.ShapeDtypeStruct((bsz, n_tok, S5_WIDTH), BF16),
        jax.ShapeDtypeStruct((bsz, n_tok, NA_WIDTH), BF16),
        jax.ShapeDtypeStruct((bsz, n_tok, NA_WIDTH), BF16),
        jax.ShapeDtypeStruct((bsz, n_tok, NA_WIDTH), BF16),
        jax.ShapeDtypeStruct((bsz, n_tok, N_BRANCH * d), BF16),
    ]
    out_specs = [tok(SGU_WIDTH), tok(S5_WIDTH), tok(NA_WIDTH), tok(NA_WIDTH), tok(NA_WIDTH), tok(N_BRANCH * d)]
    return pl.pallas_call(
        functools.partial(_mixer_in_kernel, rope=rope),
        grid=(bsz, n_tok // tm),
        in_specs=in_specs,
        out_specs=out_specs,
        out_shape=out_shapes,
        compiler_params=_cparams("parallel", "parallel"),
        name="mixer_in_rope" if rope else "mixer_in",
    )(*args)


def rope_tables(n_tok):
    pos = np.arange(n_tok)
    rows = (pos // GRID_W).astype(np.float32)
    cols = (pos % GRID_W).astype(np.float32)
    seg = NA_HEAD_DIM // 2
    half = seg // 2
    inv_freq = (ROPE_BASE ** (-np.arange(half, dtype=np.float32) / half)).astype(np.float32)
    ang_r = rows[:, None] * inv_freq
    ang_c = cols[:, None] * inv_freq
    cos = np.concatenate([np.cos(ang_r)] * 2 + [np.cos(ang_c)] * 2, axis=-1)
    sin = np.concatenate([-np.sin(ang_r), np.sin(ang_r), -np.sin(ang_c), np.sin(ang_c)], axis=-1)
    d = np.arange(NA_WIDTH)
    partner = np.where((d % seg) < half, d + half, d - half)
    swap = np.zeros((NA_WIDTH, NA_WIDTH), np.float32)
    swap[partner, d] = 1.0
    reps = LANES // NA_HEAD_DIM
    return (jnp.asarray(np.tile(cos, (1, reps)), F32), jnp.asarray(np.tile(sin, (1, reps)), F32),
            jnp.asarray(swap, BF16))


def _s5_kernel(xf_ref, xfn_ref, xr_ref, xrn_ref, s0_ref, win_ref, lre_ref, lim_ref, wout_ref, rev_ref,
               yf_ref, yr_ref, send_ref, lhs_scr, st_a, st_b, state_scr):
    bsz = xf_ref.shape[0]
    tc = S5_CHUNK
    pitch = S5_PITCH
    rows = bsz * pitch
    half = S5_SLABS // 2
    bw = half // S5_BLOCKS
    j = pl.program_id(0)
    rev = rev_ref[...]
    first, second = slice(0, tc), slice(tc, 2 * tc)

    def project_in(f_ref, f_half, r_ref, r_half, st):
        for b in range(bsz):
            lhs_scr[0, b * pitch:b * pitch + tc, :] = f_ref[b, f_half, :].astype(F32)
            lhs_scr[1, b * pitch:b * pitch + tc, :] = jnp.dot(rev, r_ref[b, r_half, :], preferred_element_type=F32)
        for d in range(2):
            lhs = lhs_scr[d].astype(BF16)
            for cb in range(S5_BLOCKS):
                res = jnp.dot(lhs[:, cb * LANES:(cb + 1) * LANES], win_ref[d, cb], preferred_element_type=F32)
                for k in range(bw):
                    st[cb * bw + k, d * rows:(d + 1) * rows, :] = res[:, k * LANES:(k + 1) * LANES]
                    st[half + cb * bw + k, d * rows:(d + 1) * rows, :] = res[:, (bw + k) * LANES:(bw + k + 1) * LANES]

    def scan(st, state):
        cur = list(state)
        for t in range(tc):
            idx = pl.ds(t, 2 * bsz, stride=pitch)
            for c in range(half):
                sre, sim = cur[c], cur[c + half]
                lr = lre_ref[c]
                li = lim_ref[c]
                nre = lr * sre - li * sim + st[c, idx, :]
                nim = lr * sim + li * sre + st[c + half, idx, :]
                st[c, idx, :] = nre
                st[c + half, idx, :] = nim
                cur[c], cur[c + half] = nre, nim
        return cur

    def project_out(st, f_half, r_half):
        for d in range(2):
            y_blocks = []
            for cb in range(S5_BLOCKS):
                slabs = [cb * bw + k for k in range(bw)] + [half + cb * bw + k for k in range(bw)]
                s_blk = jnp.concatenate([st[c, d * rows:(d + 1) * rows, :] for c in slabs], axis=-1)
                y_blocks.append(jnp.dot(s_blk.astype(BF16), wout_ref[d, cb], preferred_element_type=F32))
            y = jnp.concatenate(y_blocks, axis=-1)
            for b in range(bsz):
                yb = y[b * pitch:b * pitch + tc, :]
                if d == 0:
                    yf_ref[b, f_half, :] = yb
                else:
                    yr_ref[b, r_half, :] = jnp.dot(rev, yb.astype(BF16), preferred_element_type=F32)

    @pl.when(j == 0)
    def _():
        state_scr[...] = s0_ref[...]
        lhs_scr[...] = jnp.zeros_like(lhs_scr)
        project_in(xf_ref, first, xr_ref, second, st_a)

    project_in(xf_ref, second, xr_ref, first, st_b)
    state = scan(st_a, [state_scr[:, c * LANES:(c + 1) * LANES] for c in range(S5_SLABS)])
    project_out(st_a, first, second)
    project_in(xfn_ref, first, xrn_ref, second, st_a)
    state = scan(st_b, state)
    project_out(st_b, second, first)
    for c in range(S5_SLABS):
        state_scr[:, c * LANES:(c + 1) * LANES] = state[c]
    send_ref[...] = state_scr[...]


def s5_scan(xb, s0, w_in, lam_re, lam_im, w_out):
    bsz, n_tok, width = xb.shape
    tc = S5_CHUNK
    assert 2 * bsz == SUBLANES and n_tok % (2 * tc) == 0
    n = n_tok // (2 * tc)
    rev = jnp.asarray(np.eye(tc, dtype=np.float32)[::-1], BF16)
    blk = (bsz, 2 * tc, width)
    return pl.pallas_call(
        _s5_kernel,
        grid=(n,),
        in_specs=[
            pl.BlockSpec(blk, lambda j: (0, j, 0)),
            pl.BlockSpec(blk, lambda j: (0, jnp.minimum(j + 1, n - 1), 0)),
            pl.BlockSpec(blk, lambda j: (0, n - 1 - j, 0)),
            pl.BlockSpec(blk, lambda j: (0, jnp.maximum(n - 2 - j, 0), 0)),
            _full(s0.shape),
            _full(w_in.shape),
            _full(lam_re.shape),
            _full(lam_im.shape),
            _full(w_out.shape),
            _full(rev.shape),
        ],
        out_specs=[
            pl.BlockSpec(blk, lambda j: (0, j, 0)),
            pl.BlockSpec(blk, lambda j: (0, n - 1 - j, 0)),
            pl.BlockSpec(s0.shape, lambda j: (0, 0)),
        ],
        out_shape=[
            jax.ShapeDtypeStruct(xb.shape, F32),
            jax.ShapeDtypeStruct(xb.shape, F32),
            jax.ShapeDtypeStruct(s0.shape, F32),
        ],
        scratch_shapes=[
            pltpu.VMEM((2, bsz * S5_PITCH, width), F32),
            pltpu.VMEM((S5_SLABS, 2 * bsz * S5_PITCH, LANES), F32),
            pltpu.VMEM((S5_SLABS, 2 * bsz * S5_PITCH, LANES), F32),
            pltpu.VMEM(s0.shape, F32),
        ],
        compiler_params=_cparams("arbitrary"),
        name="s5_scan",
    )(xb, xb, xb, xb, s0, w_in, lam_re, lam_im, w_out, rev)


NA_ROWS_PER_STEP = 4
_NT = (((1,), (1,)), ((), ()))


def _na_head_pair(qp, kp, vp, kcp, vcp, bias_fn):
    lane_head = lax.broadcasted_iota(jnp.int32, qp.shape, 1) // NA_HEAD_DIM
    o_pair = None
    for hh in range(2):
        qm = jnp.where(lane_head == hh, qp, jnp.zeros_like(qp))
        s_ctx = lax.dot_general(qm, kcp, _NT, preferred_element_type=F32)
        m = jnp.max(s_ctx, axis=-1, keepdims=True)
        if kp is not None:
            s_win = lax.dot_general(qm, kp, _NT, preferred_element_type=F32) + bias_fn(hh)
            m = jnp.maximum(m, jnp.max(s_win, axis=-1, keepdims=True))
            e_win = jnp.exp(s_win - m)
        e_ctx = jnp.exp(s_ctx - m)
        den = jnp.sum(e_ctx, axis=-1, keepdims=True)
        o = jnp.dot(e_ctx.astype(BF16), vcp, preferred_element_type=F32)
        if kp is not None:
            den = den + jnp.sum(e_win, axis=-1, keepdims=True)
            o = o + jnp.dot(e_win.astype(BF16), vp, preferred_element_type=F32)
        o = o * (1.0 / den)
        o_pair = o if hh == 0 else jnp.where(lane_head == 0, o_pair, o)
    return o_pair


NA_UNION_ROWS = 12
NA_SLAB = 32
LOG2E = math.log2(math.e)
NA_TAB_LEFT_OUT = 2 * NA_KR
NA_TAB_RIGHT_OUT = NA_TAB_LEFT_OUT + 2 * NA_KR - 1
NA_TAB_BOTH_OUT = NA_TAB_RIGHT_OUT + 2 * NA_KR - 1


def _na_union_start(i, n_rows):
    return jnp.clip(i * NA_ROWS_PER_STEP - NA_KR // 2, 0, n_rows - NA_UNION_ROWS)


def _na_kernel(q_ref, k_ref, v_ref, kc_ref, vc_ref, tab_ref, o_ref, s_scr, p_scr, rden_scr, *, n_rows):
    i = pl.program_id(1)
    r0 = i * NA_ROWS_PER_STEP
    ks = _na_union_start(i, n_rows)
    n_keys = NA_UNION_ROWS * GRID_W
    n_q = NA_ROWS_PER_STEP * GRID_W
    n_pairs = NA_UNION_ROWS // 2
    slabs_per_row = GRID_W // NA_SLAB

    def table_entry(rr, jj):
        r = r0 + rr
        kst = jnp.clip(r - NA_KR // 2, 0, n_rows - NA_KR)
        key0 = ks + 2 * jj
        out0 = jnp.logical_or(key0 < kst, key0 >= kst + NA_KR)
        out1 = jnp.logical_or(key0 + 1 < kst, key0 + 1 >= kst + NA_KR)
        e = key0 - r + NA_KR
        both_in = jnp.clip(e, 0, 2 * NA_KR - 1)
        left_out = NA_TAB_LEFT_OUT + jnp.clip(e, 0, 2 * NA_KR - 2)
        right_out = NA_TAB_RIGHT_OUT + jnp.clip(e - 1, 0, 2 * NA_KR - 2)
        return jnp.where(out0, jnp.where(out1, NA_TAB_BOTH_OUT, left_out), jnp.where(out1, right_out, both_in))

    entries = [[table_entry(rr, jj) for jj in range(n_pairs)] for rr in range(NA_ROWS_PER_STEP)]

    lane_head = lax.broadcasted_iota(jnp.int32, (n_q, LANES), 1) // NA_HEAD_DIM

    def scores(p):
        ls = slice(p * LANES, (p + 1) * LANES)
        qp = q_ref[:, ls]
        zero = jnp.zeros_like(qp)
        q_stack = jnp.concatenate([jnp.where(lane_head == 0, qp, zero), jnp.where(lane_head == 1, qp, zero)], axis=0)
        k_all = jnp.concatenate([k_ref[0, :, ls], kc_ref[:, ls]], axis=0)
        s_scr[p] = lax.dot_general(q_stack, k_all, _NT, preferred_element_type=F32)

    scores(0)
    for p in range(NA_HEADS // 2):
        if p + 1 < NA_HEADS // 2:
            scores(p + 1)
        ls = slice(p * LANES, (p + 1) * LANES)
        v_all = jnp.concatenate([v_ref[0, :, ls], vc_ref[:, ls]], axis=0)
        s_p, p_p, rden_p = s_scr.at[p], p_scr.at[p], rden_scr.at[p]
        for sl in range(2 * n_q // NA_SLAB):
            hh = sl // (NA_ROWS_PER_STEP * slabs_per_row)
            rr = (sl // slabs_per_row) % NA_ROWS_PER_STEP
            q0 = (sl % slabs_per_row) * NA_SLAB
            rows = slice(sl * NA_SLAB, (sl + 1) * NA_SLAB)
            bias = jnp.concatenate(
                [tab_ref[2 * p + hh, entries[rr][jj], q0:q0 + NA_SLAB, :] for jj in range(n_pairs)], axis=-1)
            s_win = s_p[rows, :n_keys] + bias
            s_ctx = s_p[rows, n_keys:]
            m = jnp.maximum(jnp.max(s_win, axis=-1, keepdims=True), jnp.max(s_ctx, axis=-1, keepdims=True))
            e_win = jnp.exp2(s_win - m)
            e_ctx = jnp.exp2(s_ctx - m)
            den = jnp.sum(e_win, axis=-1, keepdims=True) + jnp.sum(e_ctx, axis=-1, keepdims=True)
            p_p[rows, :n_keys] = e_win.astype(BF16)
            p_p[rows, n_keys:] = e_ctx.astype(BF16)
            rden_p[rows, :] = jnp.broadcast_to(1.0 / den, (NA_SLAB, LANES))
        o = jnp.dot(p_p[...], v_all, preferred_element_type=F32) * rden_p[...]
        o_ref[:, ls] = jnp.where(lane_head == 0, o[:n_q], o[n_q:]).astype(o_ref.dtype)


def na_bias_table(rpb):
    w = np.arange(GRID_W)
    col_start = np.clip(w - NA_KC // 2, 0, GRID_W - NA_KC)
    col_mask = (w[None, :] >= col_start[:, None]) & (w[None, :] < col_start[:, None] + NA_KC)
    d_col = np.clip(w[None, :] - w[:, None], -(NA_KC - 1), NA_KC - 1) + (NA_KC - 1)
    pick = (d_col[None] == np.arange(2 * NA_KC - 1)[:, None, None]).astype(np.float32)
    full = jnp.einsum('hdj,jwu->hdwu', rpb.astype(F32), jnp.asarray(pick), precision=lax.Precision.HIGHEST)
    full = jnp.where(col_mask[None, None], full * LOG2E, NEG_INF)
    pad = jnp.zeros_like(full[:, :1])
    neg = jnp.full_like(full, NEG_INF)
    both_in = jnp.concatenate([jnp.concatenate([pad, full], axis=1), jnp.concatenate([full, pad], axis=1)], axis=-1)
    left_out = jnp.concatenate([neg, full], axis=-1)
    right_out = jnp.concatenate([full, neg], axis=-1)
    both_out = jnp.concatenate([neg[:, :1], neg[:, :1]], axis=-1)
    return jnp.concatenate([both_in, left_out, right_out, both_out], axis=1)


def neighbourhood_attention(q, k, v, kc, vc, table):
    bsz, n_tok, width = q.shape
    n_ctx = kc.shape[1]
    n_rows = n_tok // GRID_W
    assert n_rows >= NA_UNION_ROWS and n_rows % NA_ROWS_PER_STEP == 0
    tq = NA_ROWS_PER_STEP * GRID_W
    n_keys = NA_UNION_ROWS * GRID_W
    window = pl.BlockSpec((pl.Element(1), pl.Element(n_keys), pl.Element(width)),
                          lambda b, i: (b, _na_union_start(i, n_rows) * GRID_W, 0))
    return pl.pallas_call(
        functools.partial(_na_kernel, n_rows=n_rows),
        grid=(bsz, n_rows // NA_ROWS_PER_STEP),
        in_specs=[
            pl.BlockSpec((None, tq, width), lambda b, i: (b, i, 0)),
            window,
            window,
            pl.BlockSpec((None, n_ctx, width), lambda b, i: (b, 0, 0)),
            pl.BlockSpec((None, n_ctx, width), lambda b, i: (b, 0, 0)),
            _full(table.shape),
        ],
        out_specs=pl.BlockSpec((None, tq, width), lambda b, i: (b, i, 0)),
        out_shape=jax.ShapeDtypeStruct(q.shape, BF16),
        scratch_shapes=[
            pltpu.VMEM((NA_HEADS // 2, 2 * tq, n_keys + n_ctx), F32),
            pltpu.VMEM((NA_HEADS // 2, 2 * tq, n_keys + n_ctx), BF16),
            pltpu.VMEM((NA_HEADS // 2, 2 * tq, LANES), F32),
        ],
        compiler_params=_cparams("parallel", "arbitrary"),
        name="neighbourhood_attention",
    )(q, k, v, kc, vc, table)


def _ctx_attn_kernel(q_ref, k_ref, v_ref, o_ref):
    for p in range(NA_HEADS // 2):
        ls = slice(p * LANES, (p + 1) * LANES)
        o_pair = _na_head_pair(q_ref[:, ls], None, None, k_ref[:, ls], v_ref[:, ls], None)
        o_ref[:, ls] = o_pair.astype(o_ref.dtype)


def context_attention(qc, kc, vc):
    bsz, n_ctx, width = qc.shape
    spec = pl.BlockSpec((None, n_ctx, width), lambda b: (b, 0, 0))
    return pl.pallas_call(
        _ctx_attn_kernel,
        grid=(bsz,),
        in_specs=[spec, spec, spec],
        out_specs=spec,
        out_shape=jax.ShapeDtypeStruct(qc.shape, BF16),
        compiler_params=_cparams("parallel"),
        name="context_attention",
    )(qc, kc, vc)


ROUTER_LANES = LANES
EXPERT_LANE0 = N_GROUPS
GROUP_ID_LANE = 0
RANK_LANE = 1


def _route(logits):
    lane = lax.broadcasted_iota(jnp.int32, logits.shape, 1)
    big = jnp.int32(ROUTER_LANES)
    is_g = lane < N_GROUPS
    lg = jnp.where(is_g, logits, -jnp.inf)
    mg = jnp.max(lg, axis=-1, keepdims=True)
    grp = jnp.min(jnp.where(lg == mg, lane, big), axis=-1, keepdims=True)
    g_weight = 1.0 / jnp.sum(jnp.where(is_g, jnp.exp(logits - mg), 0.0), axis=-1, keepdims=True)
    e_idx = lane - EXPERT_LANE0
    sel = (e_idx >= 0) & (e_idx < N_EXPERTS) & ((e_idx // EXPERTS_PER_GROUP) == grp)
    ls1 = jnp.where(sel, logits, -jnp.inf)
    v1 = jnp.max(ls1, axis=-1, keepdims=True)
    i1 = jnp.min(jnp.where(ls1 == v1, lane, big), axis=-1, keepdims=True)
    ls2 = jnp.where(lane == i1, -jnp.inf, ls1)
    v2 = jnp.max(ls2, axis=-1, keepdims=True)
    i2 = jnp.min(jnp.where(ls2 == v2, lane, big), axis=-1, keepdims=True)
    e2 = jnp.exp(v2 - v1)
    w1 = 1.0 / (1.0 + e2)
    w2 = e2 * w1
    comb = g_weight * (jnp.where(lane == i1, w1, 0.0) + jnp.where(lane == i2, w2, 0.0))
    return jnp.where(lane == GROUP_ID_LANE, grp.astype(F32), comb)


def _merge_kernel(x_ref, ya_ref, xb_ref, yf_ref, yr_ref, yc_ref, gate_ref, mod_ref, d_ref, glu_ref,
                  wa_ref, wb_ref, wc_ref, wo_ref, g_ref, rw_ref, rb_ref, own_ref, xn_ref, h_ref, comb_ref):
    d = x_ref.shape[1]
    yb = d_ref[...] * xb_ref[...].astype(F32) + yf_ref[...] + yr_ref[...]
    yb = _gelu(yb)
    yb = yb * _sigmoid(jnp.dot(yb.astype(BF16), glu_ref[...], preferred_element_type=F32))

    def gate(j):
        return gate_ref[:, j * d:(j + 1) * d].astype(F32)

    m = gate(0) * jnp.dot(ya_ref[...], wa_ref[...], preferred_element_type=F32)
    m = m + gate(1) * jnp.dot(yb.astype(BF16), wb_ref[...], preferred_element_type=F32)
    m = m + gate(2) * jnp.dot(yc_ref[...], wc_ref[...], preferred_element_type=F32)
    xn = x_ref[...] + mod_ref[2:3, :] * jnp.dot(m.astype(BF16), wo_ref[...], preferred_element_type=F32)
    xn_ref[...] = xn
    ms = jnp.mean(xn * xn, axis=-1, keepdims=True)
    h = xn * lax.rsqrt(ms + EPS) * g_ref[...]
    h = h * (1.0 + mod_ref[4:5, :]) + mod_ref[3:4, :]
    h_ref[:, :d] = h.astype(h_ref.dtype)
    logits = jnp.dot(h, rw_ref[...], preferred_element_type=F32) + rb_ref[...]
    comb = _route(logits)
    comb_ref[...] = comb
    own = jnp.dot(comb, own_ref[...], preferred_element_type=F32)
    own_hi, own_lo = _split_bf16(own)
    lane = lax.broadcasted_iota(jnp.int32, own.shape, 1)
    h_ref[:, d:] = jnp.where(lane < EXPERTS_PER_GROUP, own_hi, own_lo)


def merge_and_route(x, ya, xb, yf, yr, yc, gates, mod, s5_d, glu_w, w_br_a, w_br_b, w_br_c, w_out,
                    norm_ffn_g, router_w, router_b, tm):
    bsz, n_tok, d = x.shape

    def tok(width):
        return pl.BlockSpec((None, tm, width), lambda b, i: (b, i, 0))

    own = np.zeros((ROUTER_LANES, ROUTER_LANES), np.float32)
    for e in range(N_EXPERTS):
        own[EXPERT_LANE0 + e, e % EXPERTS_PER_GROUP] = 1.0
        own[EXPERT_LANE0 + e, EXPERTS_PER_GROUP + e % EXPERTS_PER_GROUP] = 1.0
    weights = [s5_d.reshape(1, S5_WIDTH), glu_w, w_br_a, w_br_b, w_br_c, w_out,
               norm_ffn_g.reshape(1, d), router_w, router_b, jnp.asarray(own)]
    return pl.pallas_call(
        _merge_kernel,
        grid=(bsz, n_tok // tm),
        in_specs=[tok(d), tok(SGU_WIDTH), tok(S5_WIDTH), tok(S5_WIDTH), tok(S5_WIDTH), tok(NA_WIDTH),
                  tok(N_BRANCH * d), pl.BlockSpec((None, ADA_CHUNKS, d), lambda b, i: (b, 0, 0))]
        + [_full(w.shape) for w in weights],
        out_specs=[tok(d), tok(d + ROUTER_LANES), tok(ROUTER_LANES)],
        out_shape=[
            jax.ShapeDtypeStruct(x.shape, F32),
            jax.ShapeDtypeStruct((bsz, n_tok, d + ROUTER_LANES), BF16),
            jax.ShapeDtypeStruct((bsz, n_tok, ROUTER_LANES), F32),
        ],
        compiler_params=_cparams("parallel", "parallel"),
        name="merge_and_route",
    )(x, ya, xb, yf, yr, yc, gates, mod, *weights)


def router_params(rg_w, rg_b, re_w, re_b):
    d = rg_w.shape[0]
    pad = ROUTER_LANES - N_GROUPS - N_EXPERTS
    w = jnp.concatenate([rg_w, re_w, jnp.zeros((d, pad), F32)], axis=1).astype(F32)
    b = jnp.concatenate([rg_b, re_b, jnp.zeros((pad,), F32)]).astype(F32).reshape(1, ROUTER_LANES)
    return w, b


MOE_BLOCK = 144


def _split_bf16(x):
    hi = x.astype(BF16)
    return hi, (x - hi.astype(F32)).astype(BF16)


MOE_SUBTILE = 512


def _moe_kernel(xn_ref, hx_ref, comb_ref, mod_ref, wg_ref, wu_ref, wd_ref, fg_ref, o_ref,
                aux_col, aux_row, hid_scr, cnt_ref, *, final_norm):
    g = pl.program_id(2)
    tm, d = o_ref.shape
    st = min(MOE_SUBTILE, tm)
    n_sub = tm // st

    @pl.when(g == 0)
    def _():
        o_ref[...] = xn_ref[...]
        row_i = lax.broadcasted_iota(jnp.int32, (st, st), 0)
        col_i = lax.broadcasted_iota(jnp.int32, (st, st), 1)
        tri = jnp.where(col_i < row_i, 1.0, 0.0).astype(BF16)
        for s in range(n_sub):
            comb = comb_ref[s * st:(s + 1) * st, :]
            lane = lax.broadcasted_iota(jnp.int32, comb.shape, 1)
            grp = comb[:, GROUP_ID_LANE:GROUP_ID_LANE + 1]
            onehot = jnp.where(lane < N_GROUPS, jnp.where(lane.astype(F32) == grp, 1.0, 0.0), 0.0)
            ranks = jnp.dot(tri, onehot.astype(BF16), preferred_element_type=F32)
            own = jnp.sum(onehot * ranks, axis=-1, keepdims=True)
            aux = jnp.where(lane == GROUP_ID_LANE, grp, jnp.where(lane == RANK_LANE, own, 0.0))
            aux_col[s * st:(s + 1) * st, :] = aux
            aux_row[:, s * st:(s + 1) * st] = aux.T
            for gg in range(N_GROUPS):
                cnt_ref[s * N_GROUPS + gg] = jnp.sum(onehot[:, gg:gg + 1]).astype(jnp.int32)

    gf = g.astype(F32)
    slot_r = lax.broadcasted_iota(jnp.int32, (MOE_BLOCK, st), 0).astype(F32)
    slot_c = lax.broadcasted_iota(jnp.int32, (st, MOE_BLOCK), 1).astype(F32)
    scale = mod_ref[5:6, :]

    for s in range(n_sub):
        rows = slice(s * st, (s + 1) * st)
        rank_row = jnp.where(aux_row[GROUP_ID_LANE:GROUP_ID_LANE + 1, rows] == gf,
                             aux_row[RANK_LANE:RANK_LANE + 1, rows], -1.0)
        rank_col = jnp.where(aux_col[rows, GROUP_ID_LANE:GROUP_ID_LANE + 1] == gf,
                             aux_col[rows, RANK_LANE:RANK_LANE + 1], -1.0)
        n_blocks = (cnt_ref[s * N_GROUPS + g] + MOE_BLOCK - 1) // MOE_BLOCK

        def block(j, carry, rows=rows, rank_row=rank_row, rank_col=rank_col):
            base = (j * MOE_BLOCK).astype(F32)
            gather = jnp.where(rank_row - base == slot_r, 1.0, 0.0).astype(BF16)
            hcx = jnp.dot(gather, hx_ref[rows, :], preferred_element_type=F32)
            hc = hcx[:, :d].astype(BF16)
            wt = hcx[:, d:]
            for e in range(EXPERTS_PER_GROUP):
                a = jnp.dot(hc, wg_ref[e], preferred_element_type=F32)
                u = jnp.dot(hc, wu_ref[e], preferred_element_type=F32)
                cw = wt[:, e:e + 1] + wt[:, EXPERTS_PER_GROUP + e:EXPERTS_PER_GROUP + e + 1]
                hid_scr[:, e * D_EXPERT:(e + 1) * D_EXPERT] = (a * jax.nn.sigmoid(a) * u * cw).astype(BF16)
            oc = jnp.dot(hid_scr[...], wd_ref[...], preferred_element_type=F32).astype(BF16)
            scatter = jnp.where(rank_col - base == slot_c, 1.0, 0.0).astype(BF16)
            o_ref[rows, :] += scale * jnp.dot(scatter, oc, preferred_element_type=F32)
            return carry

        lax.fori_loop(0, n_blocks, block, 0)

    if final_norm:
        @pl.when(g == pl.num_programs(2) - 1)
        def _():
            xo = o_ref[...]
            ms = jnp.mean(xo * xo, axis=-1, keepdims=True)
            o_ref[...] = xo * lax.rsqrt(ms + EPS) * fg_ref[...]


def moe_grouped(xn, hx, comb, mod, wg, wu, wd, final_g, tm, final_norm):
    bsz, n_tok, d = xn.shape
    gw = EXPERTS_PER_GROUP * D_EXPERT
    n_sub = tm // min(MOE_SUBTILE, tm)

    def tok(width):
        return pl.BlockSpec((None, tm, width), lambda b, i, g: (b, i, 0))

    return pl.pallas_call(
        functools.partial(_moe_kernel, final_norm=final_norm),
        grid=(bsz, n_tok // tm, N_GROUPS),
        in_specs=[
            tok(d), tok(d + ROUTER_LANES), tok(ROUTER_LANES),
            pl.BlockSpec((None, ADA_CHUNKS, d), lambda b, i, g: (b, 0, 0)),
            pl.BlockSpec((None, EXPERTS_PER_GROUP, d, D_EXPERT), lambda b, i, g: (g, 0, 0, 0)),
            pl.BlockSpec((None, EXPERTS_PER_GROUP, d, D_EXPERT), lambda b, i, g: (g, 0, 0, 0)),
            pl.BlockSpec((None, gw, d), lambda b, i, g: (g, 0, 0)),
            pl.BlockSpec((1, d), lambda b, i, g: (0, 0)),
        ],
        out_specs=tok(d),
        out_shape=jax.ShapeDtypeStruct(xn.shape, F32),
        scratch_shapes=[
            pltpu.VMEM((tm, ROUTER_LANES), F32),
            pltpu.VMEM((ROUTER_LANES, tm), F32),
            pltpu.VMEM((MOE_BLOCK, gw), BF16),
            pltpu.SMEM((n_sub * N_GROUPS,), jnp.int32),
        ],
        compiler_params=_cparams("parallel", "parallel", "arbitrary"),
        name="moe_grouped",
    )(xn, hx, comb, mod, wg, wu, wd, final_g.reshape(1, d))


def moe_params(e_gate, e_up, e_down):
    _, d, f = e_gate.shape
    wg = e_gate.astype(BF16).reshape(N_GROUPS, EXPERTS_PER_GROUP, d, f)
    wu = e_up.astype(BF16).reshape(N_GROUPS, EXPERTS_PER_GROUP, d, f)
    wd = e_down.astype(BF16).reshape(N_GROUPS, EXPERTS_PER_GROUP * f, d)
    return wg, wu, wd


def s5_params(a_re, a_im, log_dt, b_re, b_im, c_re, c_im, bsz):
    lam = lax.complex(a_re.astype(F32), a_im.astype(F32))
    dt = jnp.exp(log_dt.astype(F32))[..., None]
    lam_bar = jnp.exp(lam * dt)
    b_bar = ((lam_bar - 1) / lam)[..., None] * lax.complex(b_re.astype(F32), b_im.astype(F32))
    gpb = S5_GROUPS // S5_BLOCKS
    eye = jnp.eye(gpb, dtype=F32)
    gp = S5_GROUPS * S5_STATE

    def in_mat(m):
        m = m.reshape(2, S5_BLOCKS, gpb, S5_STATE, S5_GROUP)
        return jnp.einsum('dkgpc,gh->dkgchp', m, eye).reshape(2, S5_BLOCKS, gpb * S5_GROUP, gpb * S5_STATE)

    def out_mat(m):
        m = m.reshape(2, S5_BLOCKS, gpb, S5_GROUP, S5_STATE)
        return jnp.einsum('dkgcp,gh->dkgphc', m, eye).reshape(2, S5_BLOCKS, gpb * S5_STATE, gpb * S5_GROUP)

    w_in = jnp.concatenate([in_mat(b_bar.real), in_mat(b_bar.imag)], axis=-1).astype(BF16)
    w_out = jnp.concatenate([out_mat(c_re.astype(F32)), -out_mat(c_im.astype(F32))], axis=2).astype(BF16)

    def tiles(v):
        t = v.reshape(2, gp // LANES, 1, LANES)
        t = jnp.broadcast_to(t, (2, gp // LANES, bsz, LANES))
        return jnp.concatenate([t[0], t[1]], axis=1)

    lam_flat = lam_bar.reshape(2, gp)
    return w_in, tiles(lam_flat.real), tiles(lam_flat.imag), w_out


TOKEN_TILE = 256
MIXER_TOKEN_TILE = 512
MERGE_TOKEN_TILE = 512
MOE_TOKEN_TILE = 1024


def kernel(x, c, ctx, c_ctx, ada_w, ada_b, norm_mix_g, norm_ffn_g, w_in, sgu_norm_g, sgu_w, sgu_b, s5_a_re, s5_a_im, s5_log_dt, s5_b_re, s5_b_im, s5_c_re, s5_c_im, s5_d, s5_glu_w, na_rpb, w_br_a, w_br_b, w_br_c, w_out, router_group_w, router_group_b, router_expert_w, router_expert_b, exp_w_gate, exp_w_up, exp_w_down, final_norm_g):
    bsz, n_tok, d = x.shape
    n_ctx = ctx.shape[1]
    depth = ada_w.shape[0]
    assert bsz + 1 <= SUBLANES

    cc = jnp.concatenate([c, c_ctx[None], jnp.zeros((SUBLANES - bsz - 1, d), F32)], axis=0)
    mod_all = ada_modulation(cc, ada_w, ada_b)
    rope_tabs = rope_tables(n_tok)
    s_zero = jnp.zeros((2 * bsz, S5_LANES), F32)
    tm_c = min(TOKEN_TILE, n_ctx)
    tm_moe = min(MOE_TOKEN_TILE, n_tok)

    xc = ctx
    for l in range(depth):
        with_ctx_out = l < depth - 1
        mod = mod_all[l, :bsz].reshape(bsz, ADA_CHUNKS, d)
        mod_c = jnp.broadcast_to(mod_all[l, bsz].reshape(1, ADA_CHUNKS, d), (bsz, ADA_CHUNKS, d))
        w_in_l = w_in[l].astype(BF16)
        sgu_w_l = sgu_w[l].astype(BF16)
        sgu_bias = jnp.broadcast_to(sgu_b[l].astype(F32)[:, :, None], (SGU_GROUPS, SGU_CHUNK, SGU_CHUNK))
        s5_w_in, s5_lre, s5_lim, s5_w_out = s5_params(
            s5_a_re[l], s5_a_im[l], s5_log_dt[l], s5_b_re[l], s5_b_im[l], s5_c_re[l], s5_c_im[l], bsz)
        table = na_bias_table(na_rpb[l])
        r_w, r_b = router_params(router_group_w[l], router_group_b[l], router_expert_w[l], router_expert_b[l])
        wg, wu, wd = moe_params(exp_w_gate[l], exp_w_up[l], exp_w_down[l])
        merge_w = (s5_d[l].astype(F32), s5_glu_w[l].astype(BF16), w_br_a[l].astype(BF16),
                   w_br_b[l].astype(BF16), w_br_c[l].astype(BF16), w_out[l].astype(BF16),
                   norm_ffn_g[l].astype(F32), r_w, r_b)

        ya_c, xb_c, q_c, k_c, v_c, gate_c = mixer_in(
            xc, mod_c, norm_mix_g[l], w_in_l, sgu_norm_g[l], sgu_w_l, sgu_bias, None, tm_c)
        ya_l, xb_l, q_l, k_l, v_l, gate_l = mixer_in(
            x, mod, norm_mix_g[l], w_in_l, sgu_norm_g[l], sgu_w_l, sgu_bias, rope_tabs, MIXER_TOKEN_TILE)
        ycf, ycr, s_ctx = s5_scan(xb_c, s_zero, s5_w_in, s5_lre, s5_lim, s5_w_out)
        ylf, ylr, _ = s5_scan(xb_l, s_ctx, s5_w_in, s5_lre, s5_lim, s5_w_out)
        yc_l = neighbourhood_attention(q_l, k_l, v_l, k_c, v_c, table)
        xn, h2, comb = merge_and_route(x, ya_l, xb_l, ylf, ylr, yc_l, gate_l, mod, *merge_w, MERGE_TOKEN_TILE)
        x = moe_grouped(xn, h2, comb, mod, wg, wu, wd, final_norm_g, tm_moe, not with_ctx_out)
        if with_ctx_out:
            yc_c = context_attention(q_c, k_c, v_c)
            xcn, hc2, comb_c = merge_and_route(xc, ya_c, xb_c, ycf, ycr, yc_c, gate_c, mod_c, *merge_w, tm_c)

            def flat(t):
                return t.reshape(1, bsz * n_ctx, t.shape[-1])

            xc = moe_grouped(flat(xcn), flat(hc2), flat(comb_c), mod_c[:1], wg, wu, wd, final_norm_g,
                             min(MOE_TOKEN_TILE, bsz * n_ctx), False).reshape(bsz, n_ctx, d)
    return x
```

```python
import functools
import math

import jax
import jax.numpy as jnp
import numpy as np
from jax import lax
from jax.experimental import pallas as pl
from jax.experimental.pallas import tpu as pltpu

F32 = jnp.float32
BF16 = jnp.bfloat16

GRID_W = 64
N_BRANCH = 3
SGU_WIDTH = 512
SGU_GROUPS = 4
SGU_CHUNK = 128
S5_WIDTH = 384
S5_GROUP = 16
S5_GROUPS = S5_WIDTH // S5_GROUP
S5_STATE = 64
NA_HEADS = 8
NA_HEAD_DIM = 64
NA_WIDTH = NA_HEADS * NA_HEAD_DIM
NA_KR = 8
NA_KC = 16
ROPE_BASE = 10000.0
N_GROUPS = 4
EXPERTS_PER_GROUP = 8
N_EXPERTS = N_GROUPS * EXPERTS_PER_GROUP
TOP_K = 2
D_EXPERT = 256
ADA_CHUNKS = 6
EPS = 1e-6
NEG_INF = -1e30

LANES = 128
SUBLANES = 8
VMEM_LIMIT_BYTES = 56 * 1024 * 1024

S5_LANES = 2 * S5_GROUPS * S5_STATE
S5_SLABS = S5_LANES // LANES
S5_BLOCKS = S5_WIDTH // LANES
S5_CHUNK = 128
S5_PITCH = S5_CHUNK + 4


def _cparams(*sem):
    return pltpu.CompilerParams(dimension_semantics=sem, vmem_limit_bytes=VMEM_LIMIT_BYTES)


def _full(shape):
    n = len(shape)
    return pl.BlockSpec(shape, lambda *_: (0,) * n, pipeline_mode=pl.Buffered(1))


def _ada_kernel(c_ref, w_ref, b_ref, o_ref):
    c = c_ref[...]
    s = c * jax.nn.sigmoid(c)
    o_ref[...] = jnp.dot(s, w_ref[...], preferred_element_type=F32) + b_ref[...]


def ada_modulation(cc, ada_w, ada_b):
    n_layers, d, n = ada_w.shape
    tn = 1536
    return pl.pallas_call(
        _ada_kernel,
        grid=(n_layers, n // tn),
        in_specs=[
            pl.BlockSpec((SUBLANES, d), lambda l, j: (0, 0)),
            pl.BlockSpec((None, d, tn), lambda l, j: (l, 0, j)),
            pl.BlockSpec((None, 1, tn), lambda l, j: (l, 0, j)),
        ],
        out_specs=pl.BlockSpec((None, SUBLANES, tn), lambda l, j: (l, 0, j)),
        out_shape=jax.ShapeDtypeStruct((n_layers, SUBLANES, n), F32),
        compiler_params=_cparams("parallel", "parallel"),
        name="ada_modulation",
    )(cc, ada_w, ada_b.reshape(n_layers, 1, n))


def _gelu(x):
    return jax.nn.gelu(x)


def _sigmoid(x):
    return 0.5 * jnp.tanh(0.5 * x) + 0.5


def _mixer_in_kernel(x_ref, mod_ref, g_ref, w_ref, lng_ref, sw_ref, sb_ref, *rest, rope):
    if rope:
        cos_ref, sin_ref, ya_ref, b_ref, q_ref, k_ref, v_ref, gate_ref = rest
    else:
        ya_ref, b_ref, q_ref, k_ref, v_ref, gate_ref = rest
    tm = x_ref.shape[0]
    xf = x_ref[...]
    ms = jnp.mean(xf * xf, axis=-1, keepdims=True)
    y = xf * lax.rsqrt(ms + EPS) * g_ref[...]
    h = y * (1.0 + mod_ref[1:2, :]) + mod_ref[0:1, :]
    hb = h.astype(BF16)

    def proj(lo, hi):
        return jnp.dot(hb, w_ref[:, lo:hi], preferred_element_type=F32)

    o1 = 2 * SGU_WIDTH
    o2 = o1 + S5_WIDTH
    oq, ok, ov = o2, o2 + NA_WIDTH, o2 + 2 * NA_WIDTH
    o3 = o2 + 3 * NA_WIDTH

    u = _gelu(proj(0, SGU_WIDTH))
    v = _gelu(proj(SGU_WIDTH, o1))
    vc = v - jnp.mean(v, axis=-1, keepdims=True)
    vn = vc * lax.rsqrt(jnp.mean(vc * vc, axis=-1, keepdims=True) + EPS) * lng_ref[...]
    vb = vn.astype(BF16)
    cw = SGU_WIDTH // SGU_GROUPS
    for c in range(tm // SGU_CHUNK):
        r0 = c * SGU_CHUNK
        for g in range(SGU_GROUPS):
            sp = jnp.dot(sw_ref[g], vb[r0:r0 + SGU_CHUNK, g * cw:(g + 1) * cw],
                         preferred_element_type=F32) + sb_ref[g]
            ya_ref[r0:r0 + SGU_CHUNK, g * cw:(g + 1) * cw] = (
                u[r0:r0 + SGU_CHUNK, g * cw:(g + 1) * cw] * sp).astype(ya_ref.dtype)

    b_ref[...] = proj(o1, o2).astype(b_ref.dtype)

    q = proj(oq, ok)
    k = proj(ok, ov)
    if rope:
        cos = cos_ref[...]
        sin = sin_ref[...]

        seg_half = NA_HEAD_DIM // 4
        first_half = (lax.broadcasted_iota(jnp.int32, cos.shape, 1) % (2 * seg_half)) < seg_half

        def rotate(t):
            out = []
            for j in range(NA_WIDTH // LANES):
                tj = t[:, j * LANES:(j + 1) * LANES]
                swapped = jnp.where(first_half, pltpu.roll(tj, LANES - seg_half, 1), pltpu.roll(tj, seg_half, 1))
                out.append(tj * cos + swapped * sin)
            return jnp.concatenate(out, axis=-1)

        q = rotate(q)
        k = rotate(k)
    q_scale = NA_HEAD_DIM ** -0.5 * (LOG2E if rope else 1.0)
    q_ref[...] = (q * q_scale).astype(q_ref.dtype)
    k_ref[...] = k.astype(k_ref.dtype)
    v_ref[...] = proj(ov, o3).astype(v_ref.dtype)
    gate_ref[...] = _sigmoid(proj(o3, o3 + N_BRANCH * x_ref.shape[1])).astype(gate_ref.dtype)


def mixer_in(x, mod, norm_g, w_in, sgu_norm_g, sgu_w, sgu_bias, rope_tabs, tm):
    bsz, n_tok, d = x.shape
    rope = rope_tabs is not None
    in_specs = [
        pl.BlockSpec((None, tm, d), lambda b, i: (b, i, 0)),
        pl.BlockSpec((None, ADA_CHUNKS, d), lambda b, i: (b, 0, 0)),
        _full((1, d)),
        _full(w_in.shape),
        _full((1, SGU_WIDTH)),
        _full(sgu_w.shape),
        _full(sgu_bias.shape),
    ]
    args = [x, mod, norm_g.reshape(1, d), w_in, sgu_norm_g.reshape(1, SGU_WIDTH), sgu_w, sgu_bias]
    if rope:
        cos_t, sin_t = rope_tabs
        in_specs += [
            pl.BlockSpec((tm, LANES), lambda b, i: (i, 0)),
            pl.BlockSpec((tm, LANES), lambda b, i: (i, 0)),
        ]
        args += [cos_t, sin_t]

    def tok(width):
        return pl.BlockSpec((None, tm, width), lambda b, i: (b, i, 0))

    out_shapes = [
        jax.ShapeDtypeStruct((bsz, n_tok, SGU_WIDTH), BF16),
        jax.ShapeDtypeStruct((bsz, n_tok, S5_WIDTH), BF16),
        jax.ShapeDtypeStruct((bsz, n_tok, NA_WIDTH), BF16),
        jax.ShapeDtypeStruct((bsz, n_tok, NA_WIDTH), BF16),
        jax.ShapeDtypeStruct((bsz, n_tok, NA_WIDTH), BF16),
        jax.ShapeDtypeStruct((bsz, n_tok, N_BRANCH * d), BF16),
    ]
    out_specs = [tok(SGU_WIDTH), tok(S5_WIDTH), tok(NA_WIDTH), tok(NA_WIDTH), tok(NA_WIDTH), tok(N_BRANCH * d)]
    return pl.pallas_call(
        functools.partial(_mixer_in_kernel, rope=rope),
        grid=(bsz, n_tok // tm),
        in_specs=in_specs,
        out_specs=out_specs,
        out_shape=out_shapes,
        compiler_params=_cparams("parallel", "parallel"),
        name="mixer_in_rope" if rope else "mixer_in",
    )(*args)


def rope_tables(n_tok):
    pos = np.arange(n_tok)
    rows = (pos // GRID_W).astype(np.float32)
    cols = (pos % GRID_W).astype(np.float32)
    seg = NA_HEAD_DIM // 2
    half = seg // 2
    inv_freq = (ROPE_BASE ** (-np.arange(half, dtype=np.float32) / half)).astype(np.float32)
    ang_r = rows[:, None] * inv_freq
    ang_c = cols[:, None] * inv_freq
    cos = np.concatenate([np.cos(ang_r)] * 2 + [np.cos(ang_c)] * 2, axis=-1)
    sin = np.concatenate([-np.sin(ang_r), np.sin(ang_r), -np.sin(ang_c), np.sin(ang_c)], axis=-1)
    reps = LANES // NA_HEAD_DIM
    return jnp.asarray(np.tile(cos, (1, reps)), F32), jnp.asarray(np.tile(sin, (1, reps)), F32)


def _s5_kernel(xf_ref, xfn_ref, xr_ref, xrn_ref, s0_ref, win_ref, lre_ref, lim_ref, wout_ref, rev_ref,
               yf_ref, yr_ref, send_ref, lhs_scr, st_a, st_b, state_scr):
    bsz = xf_ref.shape[0]
    tc = S5_CHUNK
    pitch = S5_PITCH
    rows = bsz * pitch
    half = S5_SLABS // 2
    bw = half // S5_BLOCKS
    j = pl.program_id(0)
    rev = rev_ref[...]
    first, second = slice(0, tc), slice(tc, 2 * tc)

    def project_in(f_ref, f_half, r_ref, r_half, st):
        for b in range(bsz):
            lhs_scr[0, b * pitch:b * pitch + tc, :] = f_ref[b, f_half, :].astype(F32)
            lhs_scr[1, b * pitch:b * pitch + tc, :] = jnp.dot(rev, r_ref[b, r_half, :], preferred_element_type=F32)
        for d in range(2):
            lhs = lhs_scr[d].astype(BF16)
            for cb in range(S5_BLOCKS):
                res = jnp.dot(lhs[:, cb * LANES:(cb + 1) * LANES], win_ref[d, cb], preferred_element_type=F32)
                for k in range(bw):
                    st[cb * bw + k, d * rows:(d + 1) * rows, :] = res[:, k * LANES:(k + 1) * LANES]
                    st[half + cb * bw + k, d * rows:(d + 1) * rows, :] = res[:, (bw + k) * LANES:(bw + k + 1) * LANES]

    def scan(st, state):
        cur = list(state)
        for t in range(tc):
            idx = pl.ds(t, 2 * bsz, stride=pitch)
            for c in range(half):
                sre, sim = cur[c], cur[c + half]
                lr = lre_ref[c]
                li = lim_ref[c]
                nre = lr * sre - li * sim + st[c, idx, :]
                nim = lr * sim + li * sre + st[c + half, idx, :]
                st[c, idx, :] = nre
                st[c + half, idx, :] = nim
                cur[c], cur[c + half] = nre, nim
        return cur

    def project_out(st, f_half, r_half):
        for d in range(2):
            y_blocks = []
            for cb in range(S5_BLOCKS):
                slabs = [cb * bw + k for k in range(bw)] + [half + cb * bw + k for k in range(bw)]
                s_blk = jnp.concatenate([st[c, d * rows:(d + 1) * rows, :] for c in slabs], axis=-1)
                y_blocks.append(jnp.dot(s_blk.astype(BF16), wout_ref[d, cb], preferred_element_type=F32))
            y = jnp.concatenate(y_blocks, axis=-1)
            for b in range(bsz):
                yb = y[b * pitch:b * pitch + tc, :]
                if d == 0:
                    yf_ref[b, f_half, :] = yb
                else:
                    yr_ref[b, r_half, :] = jnp.dot(rev, yb.astype(BF16), preferred_element_type=F32)

    @pl.when(j == 0)
    def _():
        state_scr[...] = s0_ref[...]
        lhs_scr[...] = jnp.zeros_like(lhs_scr)
        project_in(xf_ref, first, xr_ref, second, st_a)

    project_in(xf_ref, second, xr_ref, first, st_b)
    state = scan(st_a, [state_scr[:, c * LANES:(c + 1) * LANES] for c in range(S5_SLABS)])
    project_out(st_a, first, second)
    project_in(xfn_ref, first, xrn_ref, second, st_a)
    state = scan(st_b, state)
    project_out(st_b, second, first)
    for c in range(S5_SLABS):
        state_scr[:, c * LANES:(c + 1) * LANES] = state[c]
    send_ref[...] = state_scr[...]


def s5_scan(xb, s0, w_in, lam_re, lam_im, w_out):
    bsz, n_tok, width = xb.shape
    tc = S5_CHUNK
    assert 2 * bsz == SUBLANES and n_tok % (2 * tc) == 0
    n = n_tok // (2 * tc)
    rev = jnp.asarray(np.eye(tc, dtype=np.float32)[::-1], BF16)
    blk = (bsz, 2 * tc, width)
    return pl.pallas_call(
        _s5_kernel,
        grid=(n,),
        in_specs=[
            pl.BlockSpec(blk, lambda j: (0, j, 0)),
            pl.BlockSpec(blk, lambda j: (0, jnp.minimum(j + 1, n - 1), 0)),
            pl.BlockSpec(blk, lambda j: (0, n - 1 - j, 0)),
            pl.BlockSpec(blk, lambda j: (0, jnp.maximum(n - 2 - j, 0), 0)),
            _full(s0.shape),
            _full(w_in.shape),
            _full(lam_re.shape),
            _full(lam_im.shape),
            _full(w_out.shape),
            _full(rev.shape),
        ],
        out_specs=[
            pl.BlockSpec(blk, lambda j: (0, j, 0)),
            pl.BlockSpec(blk, lambda j: (0, n - 1 - j, 0)),
            pl.BlockSpec(s0.shape, lambda j: (0, 0)),
        ],
        out_shape=[
            jax.ShapeDtypeStruct(xb.shape, F32),
            jax.ShapeDtypeStruct(xb.shape, F32),
            jax.ShapeDtypeStruct(s0.shape, F32),
        ],
        scratch_shapes=[
            pltpu.VMEM((2, bsz * S5_PITCH, width), F32),
            pltpu.VMEM((S5_SLABS, 2 * bsz * S5_PITCH, LANES), F32),
            pltpu.VMEM((S5_SLABS, 2 * bsz * S5_PITCH, LANES), F32),
            pltpu.VMEM(s0.shape, F32),
        ],
        compiler_params=_cparams("arbitrary"),
        name="s5_scan",
    )(xb, xb, xb, xb, s0, w_in, lam_re, lam_im, w_out, rev)


NA_ROWS_PER_STEP = 4
_NT = (((1,), (1,)), ((), ()))


def _na_head_pair(qp, kp, vp, kcp, vcp, bias_fn):
    lane_head = lax.broadcasted_iota(jnp.int32, qp.shape, 1) // NA_HEAD_DIM
    o_pair = None
    for hh in range(2):
        qm = jnp.where(lane_head == hh, qp, jnp.zeros_like(qp))
        s_ctx = lax.dot_general(qm, kcp, _NT, preferred_element_type=F32)
        m = jnp.max(s_ctx, axis=-1, keepdims=True)
        if kp is not None:
            s_win = lax.dot_general(qm, kp, _NT, preferred_element_type=F32) + bias_fn(hh)
            m = jnp.maximum(m, jnp.max(s_win, axis=-1, keepdims=True))
            e_win = jnp.exp(s_win - m)
        e_ctx = jnp.exp(s_ctx - m)
        den = jnp.sum(e_ctx, axis=-1, keepdims=True)
        o = jnp.dot(e_ctx.astype(BF16), vcp, preferred_element_type=F32)
        if kp is not None:
            den = den + jnp.sum(e_win, axis=-1, keepdims=True)
            o = o + jnp.dot(e_win.astype(BF16), vp, preferred_element_type=F32)
        o = o * (1.0 / den)
        o_pair = o if hh == 0 else jnp.where(lane_head == 0, o_pair, o)
    return o_pair


NA_UNION_ROWS = 12
NA_SLAB = 32
LOG2E = math.log2(math.e)
NA_TAB_LEFT_OUT = 2 * NA_KR
NA_TAB_RIGHT_OUT = NA_TAB_LEFT_OUT + 2 * NA_KR - 1
NA_TAB_BOTH_OUT = NA_TAB_RIGHT_OUT + 2 * NA_KR - 1


def _na_union_start(i, n_rows):
    return jnp.clip(i * NA_ROWS_PER_STEP - NA_KR // 2, 0, n_rows - NA_UNION_ROWS)


def _na_kernel(q_ref, k_ref, v_ref, kc_ref, vc_ref, tab_ref, o_ref, s_scr, p_scr, rden_scr, *, n_rows):
    i = pl.program_id(1)
    r0 = i * NA_ROWS_PER_STEP
    ks = _na_union_start(i, n_rows)
    n_keys = NA_UNION_ROWS * GRID_W
    n_q = NA_ROWS_PER_STEP * GRID_W
    n_pairs = NA_UNION_ROWS // 2
    slabs_per_row = GRID_W // NA_SLAB

    def table_entry(rr, jj):
        r = r0 + rr
        kst = jnp.clip(r - NA_KR // 2, 0, n_rows - NA_KR)
        key0 = ks + 2 * jj
        out0 = jnp.logical_or(key0 < kst, key0 >= kst + NA_KR)
        out1 = jnp.logical_or(key0 + 1 < kst, key0 + 1 >= kst + NA_KR)
        e = key0 - r + NA_KR
        both_in = jnp.clip(e, 0, 2 * NA_KR - 1)
        left_out = NA_TAB_LEFT_OUT + jnp.clip(e, 0, 2 * NA_KR - 2)
        right_out = NA_TAB_RIGHT_OUT + jnp.clip(e - 1, 0, 2 * NA_KR - 2)
        return jnp.where(out0, jnp.where(out1, NA_TAB_BOTH_OUT, left_out), jnp.where(out1, right_out, both_in))

    entries = [[table_entry(rr, jj) for jj in range(n_pairs)] for rr in range(NA_ROWS_PER_STEP)]

    lane_head = lax.broadcasted_iota(jnp.int32, (n_q, LANES), 1) // NA_HEAD_DIM

    def scores(p):
        ls = slice(p * LANES, (p + 1) * LANES)
        qp = q_ref[:, ls]
        zero = jnp.zeros_like(qp)
        q_stack = jnp.concatenate([jnp.where(lane_head == 0, qp, zero), jnp.where(lane_head == 1, qp, zero)], axis=0)
        k_all = jnp.concatenate([k_ref[0, :, ls], kc_ref[:, ls]], axis=0)
        s_scr[p] = lax.dot_general(q_stack, k_all, _NT, preferred_element_type=F32)

    scores(0)
    for p in range(NA_HEADS // 2):
        if p + 1 < NA_HEADS // 2:
            scores(p + 1)
        ls = slice(p * LANES, (p + 1) * LANES)
        v_all = jnp.concatenate([v_ref[0, :, ls], vc_ref[:, ls]], axis=0)
        s_p, p_p, rden_p = s_scr.at[p], p_scr.at[p], rden_scr.at[p]
        for sl in range(2 * n_q // NA_SLAB):
            hh = sl // (NA_ROWS_PER_STEP * slabs_per_row)
            rr = (sl // slabs_per_row) % NA_ROWS_PER_STEP
            q0 = (sl % slabs_per_row) * NA_SLAB
            rows = slice(sl * NA_SLAB, (sl + 1) * NA_SLAB)
            bias = jnp.concatenate(
                [tab_ref[2 * p + hh, entries[rr][jj], q0:q0 + NA_SLAB, :] for jj in range(n_pairs)], axis=-1)
            s_win = s_p[rows, :n_keys] + bias
            s_ctx = s_p[rows, n_keys:]
            m = jnp.maximum(jnp.max(s_win, axis=-1, keepdims=True), jnp.max(s_ctx, axis=-1, keepdims=True))
            e_win = jnp.exp2(s_win - m)
            e_ctx = jnp.exp2(s_ctx - m)
            den = jnp.sum(e_win, axis=-1, keepdims=True) + jnp.sum(e_ctx, axis=-1, keepdims=True)
            p_p[rows, :n_keys] = e_win.astype(BF16)
            p_p[rows, n_keys:] = e_ctx.astype(BF16)
            rden_p[rows, :] = jnp.broadcast_to(1.0 / den, (NA_SLAB, LANES))
        o = jnp.dot(p_p[...], v_all, preferred_element_type=F32) * rden_p[...]
        o_ref[:, ls] = jnp.where(lane_head == 0, o[:n_q], o[n_q:]).astype(o_ref.dtype)


def na_bias_table(rpb):
    w = np.arange(GRID_W)
    col_start = np.clip(w - NA_KC // 2, 0, GRID_W - NA_KC)
    col_mask = (w[None, :] >= col_start[:, None]) & (w[None, :] < col_start[:, None] + NA_KC)
    d_col = np.clip(w[None, :] - w[:, None], -(NA_KC - 1), NA_KC - 1) + (NA_KC - 1)
    pick = (d_col[None] == np.arange(2 * NA_KC - 1)[:, None, None]).astype(np.float32)
    full = jnp.einsum('hdj,jwu->hdwu', rpb.astype(F32), jnp.asarray(pick), precision=lax.Precision.HIGHEST)
    full = jnp.where(col_mask[None, None], full * LOG2E, NEG_INF)
    pad = jnp.zeros_like(full[:, :1])
    neg = jnp.full_like(full, NEG_INF)
    both_in = jnp.concatenate([jnp.concatenate([pad, full], axis=1), jnp.concatenate([full, pad], axis=1)], axis=-1)
    left_out = jnp.concatenate([neg, full], axis=-1)
    right_out = jnp.concatenate([full, neg], axis=-1)
    both_out = jnp.concatenate([neg[:, :1], neg[:, :1]], axis=-1)
    return jnp.concatenate([both_in, left_out, right_out, both_out], axis=1)


def neighbourhood_attention(q, k, v, kc, vc, table):
    bsz, n_tok, width = q.shape
    n_ctx = kc.shape[1]
    n_rows = n_tok // GRID_W
    assert n_rows >= NA_UNION_ROWS and n_rows % NA_ROWS_PER_STEP == 0
    tq = NA_ROWS_PER_STEP * GRID_W
    n_keys = NA_UNION_ROWS * GRID_W
    window = pl.BlockSpec((pl.Element(1), pl.Element(n_keys), pl.Element(width)),
                          lambda b, i: (b, _na_union_start(i, n_rows) * GRID_W, 0))
    return pl.pallas_call(
        functools.partial(_na_kernel, n_rows=n_rows),
        grid=(bsz, n_rows // NA_ROWS_PER_STEP),
        in_specs=[
            pl.BlockSpec((None, tq, width), lambda b, i: (b, i, 0)),
            window,
            window,
            pl.BlockSpec((None, n_ctx, width), lambda b, i: (b, 0, 0)),
            pl.BlockSpec((None, n_ctx, width), lambda b, i: (b, 0, 0)),
            _full(table.shape),
        ],
        out_specs=pl.BlockSpec((None, tq, width), lambda b, i: (b, i, 0)),
        out_shape=jax.ShapeDtypeStruct(q.shape, BF16),
        scratch_shapes=[
            pltpu.VMEM((NA_HEADS // 2, 2 * tq, n_keys + n_ctx), F32),
            pltpu.VMEM((NA_HEADS // 2, 2 * tq, n_keys + n_ctx), BF16),
            pltpu.VMEM((NA_HEADS // 2, 2 * tq, LANES), F32),
        ],
        compiler_params=_cparams("parallel", "arbitrary"),
        name="neighbourhood_attention",
    )(q, k, v, kc, vc, table)


def _ctx_attn_kernel(q_ref, k_ref, v_ref, o_ref):
    for p in range(NA_HEADS // 2):
        ls = slice(p * LANES, (p + 1) * LANES)
        o_pair = _na_head_pair(q_ref[:, ls], None, None, k_ref[:, ls], v_ref[:, ls], None)
        o_ref[:, ls] = o_pair.astype(o_ref.dtype)


def context_attention(qc, kc, vc):
    bsz, n_ctx, width = qc.shape
    spec = pl.BlockSpec((None, n_ctx, width), lambda b: (b, 0, 0))
    return pl.pallas_call(
        _ctx_attn_kernel,
        grid=(bsz,),
        in_specs=[spec, spec, spec],
        out_specs=spec,
        out_shape=jax.ShapeDtypeStruct(qc.shape, BF16),
        compiler_params=_cparams("parallel"),
        name="context_attention",
    )(qc, kc, vc)


ROUTER_LANES = LANES
EXPERT_LANE0 = N_GROUPS
GROUP_ID_LANE = 0
RANK_LANE = 1


def _route(logits):
    lane = lax.broadcasted_iota(jnp.int32, logits.shape, 1)
    big = jnp.int32(ROUTER_LANES)
    is_g = lane < N_GROUPS
    lg = jnp.where(is_g, logits, -jnp.inf)
    mg = jnp.max(lg, axis=-1, keepdims=True)
    grp = jnp.min(jnp.where(lg == mg, lane, big), axis=-1, keepdims=True)
    g_weight = 1.0 / jnp.sum(jnp.where(is_g, jnp.exp(logits - mg), 0.0), axis=-1, keepdims=True)
    e_idx = lane - EXPERT_LANE0
    sel = (e_idx >= 0) & (e_idx < N_EXPERTS) & ((e_idx // EXPERTS_PER_GROUP) == grp)
    ls1 = jnp.where(sel, logits, -jnp.inf)
    v1 = jnp.max(ls1, axis=-1, keepdims=True)
    i1 = jnp.min(jnp.where(ls1 == v1, lane, big), axis=-1, keepdims=True)
    ls2 = jnp.where(lane == i1, -jnp.inf, ls1)
    v2 = jnp.max(ls2, axis=-1, keepdims=True)
    i2 = jnp.min(jnp.where(ls2 == v2, lane, big), axis=-1, keepdims=True)
    e2 = jnp.exp(v2 - v1)
    w1 = 1.0 / (1.0 + e2)
    w2 = e2 * w1
    comb = g_weight * (jnp.where(lane == i1, w1, 0.0) + jnp.where(lane == i2, w2, 0.0))
    return jnp.where(lane == GROUP_ID_LANE, grp.astype(F32), comb)


def _merge_kernel(x_ref, ya_ref, xb_ref, yf_ref, yr_ref, yc_ref, gate_ref, mod_ref, d_ref, glu_ref,
                  wa_ref, wb_ref, wc_ref, wo_ref, g_ref, rw_ref, rb_ref, own_ref, xn_ref, h_ref, comb_ref):
    d = x_ref.shape[1]
    yb = d_ref[...] * xb_ref[...].astype(F32) + yf_ref[...] + yr_ref[...]
    yb = _gelu(yb)
    yb = yb * _sigmoid(jnp.dot(yb.astype(BF16), glu_ref[...], preferred_element_type=F32))

    def gate(j):
        return gate_ref[:, j * d:(j + 1) * d].astype(F32)

    m = gate(0) * jnp.dot(ya_ref[...], wa_ref[...], preferred_element_type=F32)
    m = m + gate(1) * jnp.dot(yb.astype(BF16), wb_ref[...], preferred_element_type=F32)
    m = m + gate(2) * jnp.dot(yc_ref[...], wc_ref[...], preferred_element_type=F32)
    xn = x_ref[...] + mod_ref[2:3, :] * jnp.dot(m.astype(BF16), wo_ref[...], preferred_element_type=F32)
    xn_ref[...] = xn
    ms = jnp.mean(xn * xn, axis=-1, keepdims=True)
    h = xn * lax.rsqrt(ms + EPS) * g_ref[...]
    h = h * (1.0 + mod_ref[4:5, :]) + mod_ref[3:4, :]
    h_ref[:, :d] = h.astype(h_ref.dtype)
    logits = jnp.dot(h, rw_ref[...], preferred_element_type=F32) + rb_ref[...]
    comb = _route(logits)
    comb_ref[...] = comb
    own = jnp.dot(comb, own_ref[...], preferred_element_type=F32)
    own_hi, own_lo = _split_bf16(own)
    lane = lax.broadcasted_iota(jnp.int32, own.shape, 1)
    h_ref[:, d:] = jnp.where(lane < EXPERTS_PER_GROUP, own_hi, own_lo)


def merge_and_route(x, ya, xb, yf, yr, yc, gates, mod, s5_d, glu_w, w_br_a, w_br_b, w_br_c, w_out,
                    norm_ffn_g, router_w, router_b, tm):
    bsz, n_tok, d = x.shape

    def tok(width):
        return pl.BlockSpec((None, tm, width), lambda b, i: (b, i, 0))

    own = np.zeros((ROUTER_LANES, ROUTER_LANES), np.float32)
    for e in range(N_EXPERTS):
        own[EXPERT_LANE0 + e, e % EXPERTS_PER_GROUP] = 1.0
        own[EXPERT_LANE0 + e, EXPERTS_PER_GROUP + e % EXPERTS_PER_GROUP] = 1.0
    weights = [s5_d.reshape(1, S5_WIDTH), glu_w, w_br_a, w_br_b, w_br_c, w_out,
               norm_ffn_g.reshape(1, d), router_w, router_b, jnp.asarray(own)]
    return pl.pallas_call(
        _merge_kernel,
        grid=(bsz, n_tok // tm),
        in_specs=[tok(d), tok(SGU_WIDTH), tok(S5_WIDTH), tok(S5_WIDTH), tok(S5_WIDTH), tok(NA_WIDTH),
                  tok(N_BRANCH * d), pl.BlockSpec((None, ADA_CHUNKS, d), lambda b, i: (b, 0, 0))]
        + [_full(w.shape) for w in weights],
        out_specs=[tok(d), tok(d + ROUTER_LANES), tok(ROUTER_LANES)],
        out_shape=[
            jax.ShapeDtypeStruct(x.shape, F32),
            jax.ShapeDtypeStruct((bsz, n_tok, d + ROUTER_LANES), BF16),
            jax.ShapeDtypeStruct((bsz, n_tok, ROUTER_LANES), F32),
        ],
        compiler_params=_cparams("parallel", "parallel"),
        name="merge_and_route",
    )(x, ya, xb, yf, yr, yc, gates, mod, *weights)


def router_params(rg_w, rg_b, re_w, re_b):
    d = rg_w.shape[0]
    pad = ROUTER_LANES - N_GROUPS - N_EXPERTS
    w = jnp.concatenate([rg_w, re_w, jnp.zeros((d, pad), F32)], axis=1).astype(F32)
    b = jnp.concatenate([rg_b, re_b, jnp.zeros((pad,), F32)]).astype(F32).reshape(1, ROUTER_LANES)
    return w, b


MOE_BLOCK = 144


def _split_bf16(x):
    hi = x.astype(BF16)
    return hi, (x - hi.astype(F32)).astype(BF16)


MOE_SUBTILE = 512


def _moe_kernel(xn_ref, hx_ref, comb_ref, mod_ref, wg_ref, wu_ref, wd_ref, fg_ref, o_ref,
                aux_col, aux_row, hid_scr, cnt_ref, *, final_norm):
    g = pl.program_id(2)
    tm, d = o_ref.shape
    st = min(MOE_SUBTILE, tm)
    n_sub = tm // st

    @pl.when(g == 0)
    def _():
        o_ref[...] = xn_ref[...]
        row_i = lax.broadcasted_iota(jnp.int32, (st, st), 0)
        col_i = lax.broadcasted_iota(jnp.int32, (st, st), 1)
        tri = jnp.where(col_i < row_i, 1.0, 0.0).astype(BF16)
        for s in range(n_sub):
            comb = comb_ref[s * st:(s + 1) * st, :]
            lane = lax.broadcasted_iota(jnp.int32, comb.shape, 1)
            grp = comb[:, GROUP_ID_LANE:GROUP_ID_LANE + 1]
            onehot = jnp.where(lane < N_GROUPS, jnp.where(lane.astype(F32) == grp, 1.0, 0.0), 0.0)
            ranks = jnp.dot(tri, onehot.astype(BF16), preferred_element_type=F32)
            own = jnp.sum(onehot * ranks, axis=-1, keepdims=True)
            aux = jnp.where(lane == GROUP_ID_LANE, grp, jnp.where(lane == RANK_LANE, own, 0.0))
            aux_col[s * st:(s + 1) * st, :] = aux
            aux_row[:, s * st:(s + 1) * st] = aux.T
            for gg in range(N_GROUPS):
                cnt_ref[s * N_GROUPS + gg] = jnp.sum(onehot[:, gg:gg + 1]).astype(jnp.int32)

    gf = g.astype(F32)
    slot_r = lax.broadcasted_iota(jnp.int32, (MOE_BLOCK, st), 0).astype(F32)
    slot_c = lax.broadcasted_iota(jnp.int32, (st, MOE_BLOCK), 1).astype(F32)
    scale = mod_ref[5:6, :]

    for s in range(n_sub):
        rows = slice(s * st, (s + 1) * st)
        rank_row = jnp.where(aux_row[GROUP_ID_LANE:GROUP_ID_LANE + 1, rows] == gf,
                             aux_row[RANK_LANE:RANK_LANE + 1, rows], -1.0)
        rank_col = jnp.where(aux_col[rows, GROUP_ID_LANE:GROUP_ID_LANE + 1] == gf,
                             aux_col[rows, RANK_LANE:RANK_LANE + 1], -1.0)
        n_blocks = (cnt_ref[s * N_GROUPS + g] + MOE_BLOCK - 1) // MOE_BLOCK

        def block(j, carry, rows=rows, rank_row=rank_row, rank_col=rank_col):
            base = (j * MOE_BLOCK).astype(F32)
            gather = jnp.where(rank_row - base == slot_r, 1.0, 0.0).astype(BF16)
            hcx = jnp.dot(gather, hx_ref[rows, :], preferred_element_type=F32)
            hc = hcx[:, :d].astype(BF16)
            wt = hcx[:, d:]
            for e in range(EXPERTS_PER_GROUP):
                a = jnp.dot(hc, wg_ref[e], preferred_element_type=F32)
                u = jnp.dot(hc, wu_ref[e], preferred_element_type=F32)
                cw = wt[:, e:e + 1] + wt[:, EXPERTS_PER_GROUP + e:EXPERTS_PER_GROUP + e + 1]
                hid_scr[:, e * D_EXPERT:(e + 1) * D_EXPERT] = (a * jax.nn.sigmoid(a) * u * cw).astype(BF16)
            oc = jnp.dot(hid_scr[...], wd_ref[...], preferred_element_type=F32).astype(BF16)
            scatter = jnp.where(rank_col - base == slot_c, 1.0, 0.0).astype(BF16)
            o_ref[rows, :] += scale * jnp.dot(scatter, oc, preferred_element_type=F32)
            return carry

        lax.fori_loop(0, n_blocks, block, 0)

    if final_norm:
        @pl.when(g == pl.num_programs(2) - 1)
        def _():
            xo = o_ref[...]
            ms = jnp.mean(xo * xo, axis=-1, keepdims=True)
            o_ref[...] = xo * lax.rsqrt(ms + EPS) * fg_ref[...]


def moe_grouped(xn, hx, comb, mod, wg, wu, wd, final_g, tm, final_norm):
    bsz, n_tok, d = xn.shape
    gw = EXPERTS_PER_GROUP * D_EXPERT
    n_sub = tm // min(MOE_SUBTILE, tm)

    def tok(width):
        return pl.BlockSpec((None, tm, width), lambda b, i, g: (b, i, 0))

    return pl.pallas_call(
        functools.partial(_moe_kernel, final_norm=final_norm),
        grid=(bsz, n_tok // tm, N_GROUPS),
        in_specs=[
            tok(d), tok(d + ROUTER_LANES), tok(ROUTER_LANES),
            pl.BlockSpec((None, ADA_CHUNKS, d), lambda b, i, g: (b, 0, 0)),
            pl.BlockSpec((None, EXPERTS_PER_GROUP, d, D_EXPERT), lambda b, i, g: (g, 0, 0, 0)),
            pl.BlockSpec((None, EXPERTS_PER_GROUP, d, D_EXPERT), lambda b, i, g: (g, 0, 0, 0)),
            pl.BlockSpec((None, gw, d), lambda b, i, g: (g, 0, 0)),
            pl.BlockSpec((1, d), lambda b, i, g: (0, 0)),
        ],
        out_specs=tok(d),
        out_shape=jax.ShapeDtypeStruct(xn.shape, F32),
        scratch_shapes=[
            pltpu.VMEM((tm, ROUTER_LANES), F32),
            pltpu.VMEM((ROUTER_LANES, tm), F32),
            pltpu.VMEM((MOE_BLOCK, gw), BF16),
            pltpu.SMEM((n_sub * N_GROUPS,), jnp.int32),
        ],
        compiler_params=_cparams("parallel", "parallel", "arbitrary"),
        name="moe_grouped",
    )(xn, hx, comb, mod, wg, wu, wd, final_g.reshape(1, d))


def moe_params(e_gate, e_up, e_down):
    _, d, f = e_gate.shape
    wg = e_gate.astype(BF16).reshape(N_GROUPS, EXPERTS_PER_GROUP, d, f)
    wu = e_up.astype(BF16).reshape(N_GROUPS, EXPERTS_PER_GROUP, d, f)
    wd = e_down.astype(BF16).reshape(N_GROUPS, EXPERTS_PER_GROUP * f, d)
    return wg, wu, wd


def s5_params(a_re, a_im, log_dt, b_re, b_im, c_re, c_im, bsz):
    lam = lax.complex(a_re.astype(F32), a_im.astype(F32))
    dt = jnp.exp(log_dt.astype(F32))[..., None]
    lam_bar = jnp.exp(lam * dt)
    b_bar = ((lam_bar - 1) / lam)[..., None] * lax.complex(b_re.astype(F32), b_im.astype(F32))
    gpb = S5_GROUPS // S5_BLOCKS
    eye = jnp.eye(gpb, dtype=F32)
    gp = S5_GROUPS * S5_STATE

    def in_mat(m):
        m = m.reshape(2, S5_BLOCKS, gpb, S5_STATE, S5_GROUP)
        return jnp.einsum('dkgpc,gh->dkgchp', m, eye).reshape(2, S5_BLOCKS, gpb * S5_GROUP, gpb * S5_STATE)

    def out_mat(m):
        m = m.reshape(2, S5_BLOCKS, gpb, S5_GROUP, S5_STATE)
        return jnp.einsum('dkgcp,gh->dkgphc', m, eye).reshape(2, S5_BLOCKS, gpb * S5_STATE, gpb * S5_GROUP)

    w_in = jnp.concatenate([in_mat(b_bar.real), in_mat(b_bar.imag)], axis=-1).astype(BF16)
    w_out = jnp.concatenate([out_mat(c_re.astype(F32)), -out_mat(c_im.astype(F32))], axis=2).astype(BF16)

    def tiles(v):
        t = v.reshape(2, gp // LANES, 1, LANES)
        t = jnp.broadcast_to(t, (2, gp // LANES, bsz, LANES))
        return jnp.concatenate([t[0], t[1]], axis=1)

    lam_flat = lam_bar.reshape(2, gp)
    return w_in, tiles(lam_flat.real), tiles(lam_flat.imag), w_out


TOKEN_TILE = 256
MIXER_TOKEN_TILE = 512
MERGE_TOKEN_TILE = 512
MOE_TOKEN_TILE = 1024


def kernel(x, c, ctx, c_ctx, ada_w, ada_b, norm_mix_g, norm_ffn_g, w_in, sgu_norm_g, sgu_w, sgu_b, s5_a_re, s5_a_im, s5_log_dt, s5_b_re, s5_b_im, s5_c_re, s5_c_im, s5_d, s5_glu_w, na_rpb, w_br_a, w_br_b, w_br_c, w_out, router_group_w, router_group_b, router_expert_w, router_expert_b, exp_w_gate, exp_w_up, exp_w_down, final_norm_g):
    bsz, n_tok, d = x.shape
    n_ctx = ctx.shape[1]
    depth = ada_w.shape[0]
    assert bsz + 1 <= SUBLANES

    cc = jnp.concatenate([c, c_ctx[None], jnp.zeros((SUBLANES - bsz - 1, d), F32)], axis=0)
    mod_all = ada_modulation(cc, ada_w, ada_b)
    rope_tabs = rope_tables(n_tok)
    s_zero = jnp.zeros((2 * bsz, S5_LANES), F32)
    tm_c = min(TOKEN_TILE, n_ctx)
    tm_moe = min(MOE_TOKEN_TILE, n_tok)

    xc = ctx
    for l in range(depth):
        with_ctx_out = l < depth - 1
        mod = mod_all[l, :bsz].reshape(bsz, ADA_CHUNKS, d)
        mod_c = jnp.broadcast_to(mod_all[l, bsz].reshape(1, ADA_CHUNKS, d), (bsz, ADA_CHUNKS, d))
        w_in_l = w_in[l].astype(BF16)
        sgu_w_l = sgu_w[l].astype(BF16)
        sgu_bias = jnp.broadcast_to(sgu_b[l].astype(F32)[:, :, None], (SGU_GROUPS, SGU_CHUNK, SGU_CHUNK))
        s5_w_in, s5_lre, s5_lim, s5_w_out = s5_params(
            s5_a_re[l], s5_a_im[l], s5_log_dt[l], s5_b_re[l], s5_b_im[l], s5_c_re[l], s5_c_im[l], bsz)
        table = na_bias_table(na_rpb[l])
        r_w, r_b = router_params(router_group_w[l], router_group_b[l], router_expert_w[l], router_expert_b[l])
        wg, wu, wd = moe_params(exp_w_gate[l], exp_w_up[l], exp_w_down[l])
        merge_w = (s5_d[l].astype(F32), s5_glu_w[l].astype(BF16), w_br_a[l].astype(BF16),
                   w_br_b[l].astype(BF16), w_br_c[l].astype(BF16), w_out[l].astype(BF16),
                   norm_ffn_g[l].astype(F32), r_w, r_b)

        ya_c, xb_c, q_c, k_c, v_c, gate_c = mixer_in(
            xc, mod_c, norm_mix_g[l], w_in_l, sgu_norm_g[l], sgu_w_l, sgu_bias, None, tm_c)
        ya_l, xb_l, q_l, k_l, v_l, gate_l = mixer_in(
            x, mod, norm_mix_g[l], w_in_l, sgu_norm_g[l], sgu_w_l, sgu_bias, rope_tabs, MIXER_TOKEN_TILE)
        ycf, ycr, s_ctx = s5_scan(xb_c, s_zero, s5_w_in, s5_lre, s5_lim, s5_w_out)
        ylf, ylr, _ = s5_scan(xb_l, s_ctx, s5_w_in, s5_lre, s5_lim, s5_w_out)
        yc_l = neighbourhood_attention(q_l, k_l, v_l, k_c, v_c, table)
        xn, h2, comb = merge_and_route(x, ya_l, xb_l, ylf, ylr, yc_l, gate_l, mod, *merge_w, MERGE_TOKEN_TILE)
        x = moe_grouped(xn, h2, comb, mod, wg, wu, wd, final_norm_g, tm_moe, not with_ctx_out)
        if with_ctx_out:
            yc_c = context_attention(q_c, k_c, v_c)
            xcn, hc2, comb_c = merge_and_route(xc, ya_c, xb_c, ycf, ycr, yc_c, gate_c, mod_c, *merge_w, tm_c)

            def flat(t):
                return t.reshape(1, bsz * n_ctx, t.shape[-1])

            xc = moe_grouped(flat(xcn), flat(hc2), flat(comb_c), mod_c[:1], wg, wu, wd, final_norm_g,
                             min(MOE_TOKEN_TILE, bsz * n_ctx), False).reshape(bsz, n_ctx, d)
    return x
```

```python
import functools
import math

import jax
import jax.numpy as jnp
import numpy as np
from jax import lax
from jax.experimental import pallas as pl
from jax.experimental.pallas import tpu as pltpu

F32 = jnp.float32
BF16 = jnp.bfloat16

GRID_W = 64
N_BRANCH = 3
SGU_WIDTH = 512
SGU_GROUPS = 4
SGU_CHUNK = 128
S5_WIDTH = 384
S5_GROUP = 16
S5_GROUPS = S5_WIDTH // S5_GROUP
S5_STATE = 64
NA_HEADS = 8
NA_HEAD_DIM = 64
NA_WIDTH = NA_HEADS * NA_HEAD_DIM
NA_KR = 8
NA_KC = 16
ROPE_BASE = 10000.0
N_GROUPS = 4
EXPERTS_PER_GROUP = 8
N_EXPERTS = N_GROUPS * EXPERTS_PER_GROUP
TOP_K = 2
D_EXPERT = 256
ADA_CHUNKS = 6
EPS = 1e-6
NEG_INF = -1e30

LANES = 128
SUBLANES = 8
VMEM_LIMIT_BYTES = 56 * 1024 * 1024

S5_LANES = 2 * S5_GROUPS * S5_STATE
S5_SLABS = S5_LANES // LANES
S5_BLOCKS = S5_WIDTH // LANES
S5_CHUNK = 128
S5_PITCH = S5_CHUNK + 4


def _cparams(*sem):
    return pltpu.CompilerParams(dimension_semantics=sem, vmem_limit_bytes=VMEM_LIMIT_BYTES)


def _full(shape):
    n = len(shape)
    return pl.BlockSpec(shape, lambda *_: (0,) * n, pipeline_mode=pl.Buffered(1))


def _ada_kernel(c_ref, w_ref, b_ref, o_ref):
    c = c_ref[...]
    s = c * jax.nn.sigmoid(c)
    o_ref[...] = jnp.dot(s, w_ref[...], preferred_element_type=F32) + b_ref[...]


def ada_modulation(cc, ada_w, ada_b):
    n_layers, d, n = ada_w.shape
    tn = 1536
    return pl.pallas_call(
        _ada_kernel,
        grid=(n_layers, n // tn),
        in_specs=[
            pl.BlockSpec((SUBLANES, d), lambda l, j: (0, 0)),
            pl.BlockSpec((None, d, tn), lambda l, j: (l, 0, j)),
            pl.BlockSpec((None, 1, tn), lambda l, j: (l, 0, j)),
        ],
        out_specs=pl.BlockSpec((None, SUBLANES, tn), lambda l, j: (l, 0, j)),
        out_shape=jax.ShapeDtypeStruct((n_layers, SUBLANES, n), F32),
        compiler_params=_cparams("parallel", "parallel"),
        name="ada_modulation",
    )(cc, ada_w, ada_b.reshape(n_layers, 1, n))


def _gelu(x):
    return jax.nn.gelu(x)


def _sigmoid(x):
    return 0.5 * jnp.tanh(0.5 * x) + 0.5


def _mixer_in_kernel(x_ref, mod_ref, g_ref, w_ref, lng_ref, sw_ref, sb_ref, *rest, rope):
    if rope:
        cos_ref, sin_ref, ya_ref, b_ref, q_ref, k_ref, v_ref, gate_ref = rest
    else:
        ya_ref, b_ref, q_ref, k_ref, v_ref, gate_ref = rest
    tm = x_ref.shape[0]
    xf = x_ref[...]
    ms = jnp.mean(xf * xf, axis=-1, keepdims=True)
    y = xf * lax.rsqrt(ms + EPS) * g_ref[...]
    h = y * (1.0 + mod_ref[1:2, :]) + mod_ref[0:1, :]
    hb = h.astype(BF16)

    def proj(lo, hi):
        return jnp.dot(hb, w_ref[:, lo:hi], preferred_element_type=F32)

    o1 = 2 * SGU_WIDTH
    o2 = o1 + S5_WIDTH
    oq, ok, ov = o2, o2 + NA_WIDTH, o2 + 2 * NA_WIDTH
    o3 = o2 + 3 * NA_WIDTH

    u = _gelu(proj(0, SGU_WIDTH))
    v = _gelu(proj(SGU_WIDTH, o1))
    vc = v - jnp.mean(v, axis=-1, keepdims=True)
    vn = vc * lax.rsqrt(jnp.mean(vc * vc, axis=-1, keepdims=True) + EPS) * lng_ref[...]
    vb = vn.astype(BF16)
    cw = SGU_WIDTH // SGU_GROUPS
    for c in range(tm // SGU_CHUNK):
        r0 = c * SGU_CHUNK
        for g in range(SGU_GROUPS):
            sp = jnp.dot(sw_ref[g], vb[r0:r0 + SGU_CHUNK, g * cw:(g + 1) * cw],
                         preferred_element_type=F32) + sb_ref[g]
            ya_ref[r0:r0 + SGU_CHUNK, g * cw:(g + 1) * cw] = (
                u[r0:r0 + SGU_CHUNK, g * cw:(g + 1) * cw] * sp).astype(ya_ref.dtype)

    b_ref[...] = proj(o1, o2).astype(b_ref.dtype)

    q = proj(oq, ok)
    k = proj(ok, ov)
    if rope:
        cos = cos_ref[...]
        sin = sin_ref[...]

        seg_half = NA_HEAD_DIM // 4
        first_half = (lax.broadcasted_iota(jnp.int32, cos.shape, 1) % (2 * seg_half)) < seg_half

        def rotate(t):
            out = []
            for j in range(NA_WIDTH // LANES):
                tj = t[:, j * LANES:(j + 1) * LANES]
                swapped = jnp.where(first_half, pltpu.roll(tj, LANES - seg_half, 1), pltpu.roll(tj, seg_half, 1))
                out.append(tj * cos + swapped * sin)
            return jnp.concatenate(out, axis=-1)

        q = rotate(q)
        k = rotate(k)
    q_scale = NA_HEAD_DIM ** -0.5 * (LOG2E if rope else 1.0)
    q_ref[...] = (q * q_scale).astype(q_ref.dtype)
    k_ref[...] = k.astype(k_ref.dtype)
    v_ref[...] = proj(ov, o3).astype(v_ref.dtype)
    gate_ref[...] = _sigmoid(proj(o3, o3 + N_BRANCH * x_ref.shape[1])).astype(gate_ref.dtype)


def mixer_in(x, mod, norm_g, w_in, sgu_norm_g, sgu_w, sgu_bias, rope_tabs, tm):
    bsz, n_tok, d = x.shape
    rope = rope_tabs is not None
    in_specs = [
        pl.BlockSpec((None, tm, d), lambda b, i: (b, i, 0)),
        pl.BlockSpec((None, ADA_CHUNKS, d), lambda b, i: (b, 0, 0)),
        _full((1, d)),
        _full(w_in.shape),
        _full((1, SGU_WIDTH)),
        _full(sgu_w.shape),
        _full(sgu_bias.shape),
    ]
    args = [x, mod, norm_g.reshape(1, d), w_in, sgu_norm_g.reshape(1, SGU_WIDTH), sgu_w, sgu_bias]
    if rope:
        cos_t, sin_t = rope_tabs
        in_specs += [
            pl.BlockSpec((tm, LANES), lambda b, i: (i, 0)),
            pl.BlockSpec((tm, LANES), lambda b, i: (i, 0)),
        ]
        args += [cos_t, sin_t]

    def tok(width):
        return pl.BlockSpec((None, tm, width), lambda b, i: (b, i, 0))

    out_shapes = [
        jax.ShapeDtypeStruct((bsz, n_tok, SGU_WIDTH), BF16),
        jax.ShapeDtypeStruct((bsz, n_tok, S5_WIDTH), BF16),
        jax.ShapeDtypeStruct((bsz, n_tok, NA_WIDTH), BF16),
        jax.ShapeDtypeStruct((bsz, n_tok, NA_WIDTH), BF16),
        jax.ShapeDtypeStruct((bsz, n_tok, NA_WIDTH), BF16),
        jax.ShapeDtypeStruct((bsz, n_tok, N_BRANCH * d), BF16),
    ]
    out_specs = [tok(SGU_WIDTH), tok(S5_WIDTH), tok(NA_WIDTH), tok(NA_WIDTH), tok(NA_WIDTH), tok(N_BRANCH * d)]
    return pl.pallas_call(
        functools.partial(_mixer_in_kernel, rope=rope),
        grid=(bsz, n_tok // tm),
        in_specs=in_specs,
        out_specs=out_specs,
        out_shape=out_shapes,
        compiler_params=_cparams("parallel", "parallel"),
        name="mixer_in_rope" if rope else "mixer_in",
    )(*args)


def rope_tables(n_tok):
    pos = np.arange(n_tok)
    rows = (pos // GRID_W).astype(np.float32)
    cols = (pos % GRID_W).astype(np.float32)
    seg = NA_HEAD_DIM // 2
    half = seg // 2
    inv_freq = (ROPE_BASE ** (-np.arange(half, dtype=np.float32) / half)).astype(np.float32)
    ang_r = rows[:, None] * inv_freq
    ang_c = cols[:, None] * inv_freq
    cos = np.concatenate([np.cos(ang_r)] * 2 + [np.cos(ang_c)] * 2, axis=-1)
    sin = np.concatenate([-np.sin(ang_r), np.sin(ang_r), -np.sin(ang_c), np.sin(ang_c)], axis=-1)
    reps = LANES // NA_HEAD_DIM
    return jnp.asarray(np.tile(cos, (1, reps)), F32), jnp.asarray(np.tile(sin, (1, reps)), F32)


def _s5_kernel(xf_ref, xfn_ref, xr_ref, xrn_ref, s0_ref, win_ref, lre_ref, lim_ref, wout_ref, rev_ref,
               yf_ref, yr_ref, send_ref, lhs_scr, st_a, st_b, state_scr):
    bsz = xf_ref.shape[0]
    tc = S5_CHUNK
    pitch = S5_PITCH
    rows = bsz * pitch
    half = S5_SLABS // 2
    bw = half // S5_BLOCKS
    j = pl.program_id(0)
    rev = rev_ref[...]
    first, second = slice(0, tc), slice(tc, 2 * tc)

    def project_in(f_ref, f_half, r_ref, r_half, st):
        for b in range(bsz):
            lhs_scr[0, b * pitch:b * pitch + tc, :] = f_ref[b, f_half, :].astype(F32)
            lhs_scr[1, b * pitch:b * pitch + tc, :] = jnp.dot(rev, r_ref[b, r_half, :], preferred_element_type=F32)
        for d in range(2):
            lhs = lhs_scr[d].astype(BF16)
            for cb in range(S5_BLOCKS):
                res = jnp.dot(lhs[:, cb * LANES:(cb + 1) * LANES], win_ref[d, cb], preferred_element_type=F32)
                for k in range(bw):
                    st[cb * bw + k, d * rows:(d + 1) * rows, :] = res[:, k * LANES:(k + 1) * LANES]
                    st[half + cb * bw + k, d * rows:(d + 1) * rows, :] = res[:, (bw + k) * LANES:(bw + k + 1) * LANES]

    def scan(st, state):
        cur = list(state)
        for t in range(tc):
            idx = pl.ds(t, 2 * bsz, stride=pitch)
            for c in range(half):
                sre, sim = cur[c], cur[c + half]
                lr = lre_ref[c]
                li = lim_ref[c]
                nre = lr * sre - li * sim + st[c, idx, :]
                nim = lr * sim + li * sre + st[c + half, idx, :]
                st[c, idx, :] = nre
                st[c + half, idx, :] = nim
                cur[c], cur[c + half] = nre, nim
        return cur

    def project_out(st, f_half, r_half):
        for d in range(2):
            y_blocks = []
            for cb in range(S5_BLOCKS):
                slabs = [cb * bw + k for k in range(bw)] + [half + cb * bw + k for k in range(bw)]
                s_blk = jnp.concatenate([st[c, d * rows:(d + 1) * rows, :] for c in slabs], axis=-1)
                y_blocks.append(jnp.dot(s_blk.astype(BF16), wout_ref[d, cb], preferred_element_type=F32))
            y = jnp.concatenate(y_blocks, axis=-1)
            for b in range(bsz):
                yb = y[b * pitch:b * pitch + tc, :]
                if d == 0:
                    yf_ref[b, f_half, :] = yb
                else:
                    yr_ref[b, r_half, :] = jnp.dot(rev, yb.astype(BF16), preferred_element_type=F32)

    @pl.when(j == 0)
    def _():
        state_scr[...] = s0_ref[...]
        lhs_scr[...] = jnp.zeros_like(lhs_scr)
        project_in(xf_ref, first, xr_ref, second, st_a)

    project_in(xf_ref, second, xr_ref, first, st_b)
    state = scan(st_a, [state_scr[:, c * LANES:(c + 1) * LANES] for c in range(S5_SLABS)])
    project_out(st_a, first, second)
    project_in(xfn_ref, first, xrn_ref, second, st_a)
    state = scan(st_b, state)
    project_out(st_b, second, first)
    for c in range(S5_SLABS):
        state_scr[:, c * LANES:(c + 1) * LANES] = state[c]
    send_ref[...] = state_scr[...]


def s5_scan(xb, s0, w_in, lam_re, lam_im, w_out):
    bsz, n_tok, width = xb.shape
    tc = S5_CHUNK
    assert 2 * bsz == SUBLANES and n_tok % (2 * tc) == 0
    n = n_tok // (2 * tc)
    rev = jnp.asarray(np.eye(tc, dtype=np.float32)[::-1], BF16)
    blk = (bsz, 2 * tc, width)
    return pl.pallas_call(
        _s5_kernel,
        grid=(n,),
        in_specs=[
            pl.BlockSpec(blk, lambda j: (0, j, 0)),
            pl.BlockSpec(blk, lambda j: (0, jnp.minimum(j + 1, n - 1), 0)),
            pl.BlockSpec(blk, lambda j: (0, n - 1 - j, 0)),
            pl.BlockSpec(blk, lambda j: (0, jnp.maximum(n - 2 - j, 0), 0)),
            _full(s0.shape),
            _full(w_in.shape),
            _full(lam_re.shape),
            _full(lam_im.shape),
            _full(w_out.shape),
            _full(rev.shape),
        ],
        out_specs=[
            pl.BlockSpec(blk, lambda j: (0, j, 0)),
            pl.BlockSpec(blk, lambda j: (0, n - 1 - j, 0)),
            pl.BlockSpec(s0.shape, lambda j: (0, 0)),
        ],
        out_shape=[
            jax.ShapeDtypeStruct(xb.shape, F32),
            jax.ShapeDtypeStruct(xb.shape, F32),
            jax.ShapeDtypeStruct(s0.shape, F32),
        ],
        scratch_shapes=[
            pltpu.VMEM((2, bsz * S5_PITCH, width), F32),
            pltpu.VMEM((S5_SLABS, 2 * bsz * S5_PITCH, LANES), F32),
            pltpu.VMEM((S5_SLABS, 2 * bsz * S5_PITCH, LANES), F32),
            pltpu.VMEM(s0.shape, F32),
        ],
        compiler_params=_cparams("arbitrary"),
        name="s5_scan",
    )(xb, xb, xb, xb, s0, w_in, lam_re, lam_im, w_out, rev)


NA_ROWS_PER_STEP = 4
_NT = (((1,), (1,)), ((), ()))


def _na_head_pair(qp, kp, vp, kcp, vcp, bias_fn):
    lane_head = lax.broadcasted_iota(jnp.int32, qp.shape, 1) // NA_HEAD_DIM
    o_pair = None
    for hh in range(2):
        qm = jnp.where(lane_head == hh, qp, jnp.zeros_like(qp))
        s_ctx = lax.dot_general(qm, kcp, _NT, preferred_element_type=F32)
        m = jnp.max(s_ctx, axis=-1, keepdims=True)
        if kp is not None:
            s_win = lax.dot_general(qm, kp, _NT, preferred_element_type=F32) + bias_fn(hh)
            m = jnp.maximum(m, jnp.max(s_win, axis=-1, keepdims=True))
            e_win = jnp.exp(s_win - m)
        e_ctx = jnp.exp(s_ctx - m)
        den = jnp.sum(e_ctx, axis=-1, keepdims=True)
        o = jnp.dot(e_ctx.astype(BF16), vcp, preferred_element_type=F32)
        if kp is not None:
            den = den + jnp.sum(e_win, axis=-1, keepdims=True)
            o = o + jnp.dot(e_win.astype(BF16), vp, preferred_element_type=F32)
        o = o * (1.0 / den)
        o_pair = o if hh == 0 else jnp.where(lane_head == 0, o_pair, o)
    return o_pair


NA_UNION_ROWS = 12
NA_SLAB = 32
LOG2E = math.log2(math.e)
NA_TAB_LEFT_OUT = 2 * NA_KR
NA_TAB_RIGHT_OUT = NA_TAB_LEFT_OUT + 2 * NA_KR - 1
NA_TAB_BOTH_OUT = NA_TAB_RIGHT_OUT + 2 * NA_KR - 1


def _na_union_start(i, n_rows):
    return jnp.clip(i * NA_ROWS_PER_STEP - NA_KR // 2, 0, n_rows - NA_UNION_ROWS)


def _na_kernel(q_ref, k_ref, v_ref, kc_ref, vc_ref, tab_ref, o_ref, s_scr, p_scr, rden_scr, *, n_rows):
    i = pl.program_id(1)
    r0 = i * NA_ROWS_PER_STEP
    ks = _na_union_start(i, n_rows)
    n_keys = NA_UNION_ROWS * GRID_W
    n_q = NA_ROWS_PER_STEP * GRID_W
    n_pairs = NA_UNION_ROWS // 2
    slabs_per_row = GRID_W // NA_SLAB

    def table_entry(rr, jj):
        r = r0 + rr
        kst = jnp.clip(r - NA_KR // 2, 0, n_rows - NA_KR)
        key0 = ks + 2 * jj
        out0 = jnp.logical_or(key0 < kst, key0 >= kst + NA_KR)
        out1 = jnp.logical_or(key0 + 1 < kst, key0 + 1 >= kst + NA_KR)
        e = key0 - r + NA_KR
        both_in = jnp.clip(e, 0, 2 * NA_KR - 1)
        left_out = NA_TAB_LEFT_OUT + jnp.clip(e, 0, 2 * NA_KR - 2)
        right_out = NA_TAB_RIGHT_OUT + jnp.clip(e - 1, 0, 2 * NA_KR - 2)
        return jnp.where(out0, jnp.where(out1, NA_TAB_BOTH_OUT, left_out), jnp.where(out1, right_out, both_in))

    entries = [[table_entry(rr, jj) for jj in range(n_pairs)] for rr in range(NA_ROWS_PER_STEP)]

    lane_head = lax.broadcasted_iota(jnp.int32, (n_q, LANES), 1) // NA_HEAD_DIM

    def scores(p):
        ls = slice(p * LANES, (p + 1) * LANES)
        qp = q_ref[:, ls]
        zero = jnp.zeros_like(qp)
        q_stack = jnp.concatenate([jnp.where(lane_head == 0, qp, zero), jnp.where(lane_head == 1, qp, zero)], axis=0)
        k_all = jnp.concatenate([k_ref[0, :, ls], kc_ref[:, ls]], axis=0)
        s_scr[p] = lax.dot_general(q_stack, k_all, _NT, preferred_element_type=F32)

    scores(0)
    for p in range(NA_HEADS // 2):
        if p + 1 < NA_HEADS // 2:
            scores(p + 1)
        ls = slice(p * LANES, (p + 1) * LANES)
        v_all = jnp.concatenate([v_ref[0, :, ls], vc_ref[:, ls]], axis=0)
        s_p, p_p, rden_p = s_scr.at[p], p_scr.at[p], rden_scr.at[p]
        for sl in range(2 * n_q // NA_SLAB):
            hh = sl // (NA_ROWS_PER_STEP * slabs_per_row)
            rr = (sl // slabs_per_row) % NA_ROWS_PER_STEP
            q0 = (sl % slabs_per_row) * NA_SLAB
            rows = slice(sl * NA_SLAB, (sl + 1) * NA_SLAB)
            bias = jnp.concatenate(
                [tab_ref[2 * p + hh, entries[rr][jj], q0:q0 + NA_SLAB, :] for jj in range(n_pairs)], axis=-1)
            s_win = s_p[rows, :n_keys] + bias
            s_ctx = s_p[rows, n_keys:]
            m = jnp.maximum(jnp.max(s_win, axis=-1, keepdims=True), jnp.max(s_ctx, axis=-1, keepdims=True))
            e_win = jnp.exp2(s_win - m)
            e_ctx = jnp.exp2(s_ctx - m)
            den = jnp.sum(e_win, axis=-1, keepdims=True) + jnp.sum(e_ctx, axis=-1, keepdims=True)
            p_p[rows, :n_keys] = e_win.astype(BF16)
            p_p[rows, n_keys:] = e_ctx.astype(BF16)
            rden_p[rows, :] = jnp.broadcast_to(1.0 / den, (NA_SLAB, LANES))
        o = jnp.dot(p_p[...], v_all, preferred_element_type=F32) * rden_p[...]
        o_ref[:, ls] = jnp.where(lane_head == 0, o[:n_q], o[n_q:]).astype(o_ref.dtype)


def na_bias_table(rpb):
    w = np.arange(GRID_W)
    col_start = np.clip(w - NA_KC // 2, 0, GRID_W - NA_KC)
    col_mask = (w[None, :] >= col_start[:, None]) & (w[None, :] < col_start[:, None] + NA_KC)
    d_col = np.clip(w[None, :] - w[:, None], -(NA_KC - 1), NA_KC - 1) + (NA_KC - 1)
    pick = (d_col[None] == np.arange(2 * NA_KC - 1)[:, None, None]).astype(np.float32)
    full = jnp.einsum('hdj,jwu->hdwu', rpb.astype(F32), jnp.asarray(pick), precision=lax.Precision.HIGHEST)
    full = jnp.where(col_mask[None, None], full * LOG2E, NEG_INF)
    pad = jnp.zeros_like(full[:, :1])
    neg = jnp.full_like(full, NEG_INF)
    both_in = jnp.concatenate([jnp.concatenate([pad, full], axis=1), jnp.concatenate([full, pad], axis=1)], axis=-1)
    left_out = jnp.concatenate([neg, full], axis=-1)
    right_out = jnp.concatenate([full, neg], axis=-1)
    both_out = jnp.concatenate([neg[:, :1], neg[:, :1]], axis=-1)
    return jnp.concatenate([both_in, left_out, right_out, both_out], axis=1)


def neighbourhood_attention(q, k, v, kc, vc, table):
    bsz, n_tok, width = q.shape
    n_ctx = kc.shape[1]
    n_rows = n_tok // GRID_W
    assert n_rows >= NA_UNION_ROWS and n_rows % NA_ROWS_PER_STEP == 0
    tq = NA_ROWS_PER_STEP * GRID_W
    n_keys = NA_UNION_ROWS * GRID_W
    window = pl.BlockSpec((pl.Element(1), pl.Element(n_keys), pl.Element(width)),
                          lambda b, i: (b, _na_union_start(i, n_rows) * GRID_W, 0))
    return pl.pallas_call(
        functools.partial(_na_kernel, n_rows=n_rows),
        grid=(bsz, n_rows // NA_ROWS_PER_STEP),
        in_specs=[
            pl.BlockSpec((None, tq, width), lambda b, i: (b, i, 0)),
            window,
            window,
            pl.BlockSpec((None, n_ctx, width), lambda b, i: (b, 0, 0)),
            pl.BlockSpec((None, n_ctx, width), lambda b, i: (b, 0, 0)),
            _full(table.shape),
        ],
        out_specs=pl.BlockSpec((None, tq, width), lambda b, i: (b, i, 0)),
        out_shape=jax.ShapeDtypeStruct(q.shape, BF16),
        scratch_shapes=[
            pltpu.VMEM((NA_HEADS // 2, 2 * tq, n_keys + n_ctx), F32),
            pltpu.VMEM((NA_HEADS // 2, 2 * tq, n_keys + n_ctx), BF16),
            pltpu.VMEM((NA_HEADS // 2, 2 * tq, LANES), F32),
        ],
        compiler_params=_cparams("parallel", "arbitrary"),
        name="neighbourhood_attention",
    )(q, k, v, kc, vc, table)


def _ctx_attn_kernel(q_ref, k_ref, v_ref, o_ref):
    for p in range(NA_HEADS // 2):
        ls = slice(p * LANES, (p + 1) * LANES)
        o_pair = _na_head_pair(q_ref[:, ls], None, None, k_ref[:, ls], v_ref[:, ls], None)
        o_ref[:, ls] = o_pair.astype(o_ref.dtype)


def context_attention(qc, kc, vc):
    bsz, n_ctx, width = qc.shape
    spec = pl.BlockSpec((None, n_ctx, width), lambda b: (b, 0, 0))
    return pl.pallas_call(
        _ctx_attn_kernel,
        grid=(bsz,),
        in_specs=[spec, spec, spec],
        out_specs=spec,
        out_shape=jax.ShapeDtypeStruct(qc.shape, BF16),
        compiler_params=_cparams("parallel"),
        name="context_attention",
    )(qc, kc, vc)


ROUTER_LANES = LANES
EXPERT_LANE0 = N_GROUPS
GROUP_ID_LANE = 0
RANK_LANE = 1


def _route(logits):
    lane = lax.broadcasted_iota(jnp.int32, logits.shape, 1)
    big = jnp.int32(ROUTER_LANES)
    is_g = lane < N_GROUPS
    lg = jnp.where(is_g, logits, -jnp.inf)
    mg = jnp.max(lg, axis=-1, keepdims=True)
    grp = jnp.min(jnp.where(lg == mg, lane, big), axis=-1, keepdims=True)
    g_weight = 1.0 / jnp.sum(jnp.where(is_g, jnp.exp(logits - mg), 0.0), axis=-1, keepdims=True)
    e_idx = lane - EXPERT_LANE0
    sel = (e_idx >= 0) & (e_idx < N_EXPERTS) & ((e_idx // EXPERTS_PER_GROUP) == grp)
    ls1 = jnp.where(sel, logits, -jnp.inf)
    v1 = jnp.max(ls1, axis=-1, keepdims=True)
    i1 = jnp.min(jnp.where(ls1 == v1, lane, big), axis=-1, keepdims=True)
    ls2 = jnp.where(lane == i1, -jnp.inf, ls1)
    v2 = jnp.max(ls2, axis=-1, keepdims=True)
    i2 = jnp.min(jnp.where(ls2 == v2, lane, big), axis=-1, keepdims=True)
    e2 = jnp.exp(v2 - v1)
    w1 = 1.0 / (1.0 + e2)
    w2 = e2 * w1
    comb = g_weight * (jnp.where(lane == i1, w1, 0.0) + jnp.where(lane == i2, w2, 0.0))
    return jnp.where(lane == GROUP_ID_LANE, grp.astype(F32), comb)


def _merge_kernel(x_ref, ya_ref, xb_ref, yf_ref, yr_ref, yc_ref, gate_ref, mod_ref, d_ref, glu_ref,
                  wa_ref, wb_ref, wc_ref, wo_ref, g_ref, rw_ref, rb_ref, own_ref, xn_ref, h_ref, comb_ref):
    d = x_ref.shape[1]
    yb = d_ref[...] * xb_ref[...].astype(F32) + yf_ref[...] + yr_ref[...]
    yb = _gelu(yb)
    yb = yb * _sigmoid(jnp.dot(yb.astype(BF16), glu_ref[...], preferred_element_type=F32))

    def gate(j):
        return gate_ref[:, j * d:(j + 1) * d].astype(F32)

    m = gate(0) * jnp.dot(ya_ref[...], wa_ref[...], preferred_element_type=F32)
    m = m + gate(1) * jnp.dot(yb.astype(BF16), wb_ref[...], preferred_element_type=F32)
    m = m + gate(2) * jnp.dot(yc_ref[...], wc_ref[...], preferred_element_type=F32)
    xn = x_ref[...] + mod_ref[2:3, :] * jnp.dot(m.astype(BF16), wo_ref[...], preferred_element_type=F32)
    xn_ref[...] = xn
    ms = jnp.mean(xn * xn, axis=-1, keepdims=True)
    h = xn * lax.rsqrt(ms + EPS) * g_ref[...]
    h = h * (1.0 + mod_ref[4:5, :]) + mod_ref[3:4, :]
    h_ref[:, :d] = h.astype(h_ref.dtype)
    logits = jnp.dot(h, rw_ref[...], preferred_element_type=F32) + rb_ref[...]
    comb = _route(logits)
    comb_ref[...] = comb
    own = jnp.dot(comb, own_ref[...], preferred_element_type=F32)
    own_hi, own_lo = _split_bf16(own)
    lane = lax.broadcasted_iota(jnp.int32, own.shape, 1)
    h_ref[:, d:] = jnp.where(lane < EXPERTS_PER_GROUP, own_hi, own_lo)


def merge_and_route(x, ya, xb, yf, yr, yc, gates, mod, s5_d, glu_w, w_br_a, w_br_b, w_br_c, w_out,
                    norm_ffn_g, router_w, router_b, tm):
    bsz, n_tok, d = x.shape

    def tok(width):
        return pl.BlockSpec((None, tm, width), lambda b, i: (b, i, 0))

    own = np.zeros((ROUTER_LANES, ROUTER_LANES), np.float32)
    for e in range(N_EXPERTS):
        own[EXPERT_LANE0 + e, e % EXPERTS_PER_GROUP] = 1.0
        own[EXPERT_LANE0 + e, EXPERTS_PER_GROUP + e % EXPERTS_PER_GROUP] = 1.0
    weights = [s5_d.reshape(1, S5_WIDTH), glu_w, w_br_a, w_br_b, w_br_c, w_out,
               norm_ffn_g.reshape(1, d), router_w, router_b, jnp.asarray(own)]
    return pl.pallas_call(
        _merge_kernel,
        grid=(bsz, n_tok // tm),
        in_specs=[tok(d), tok(SGU_WIDTH), tok(S5_WIDTH), tok(S5_WIDTH), tok(S5_WIDTH), tok(NA_WIDTH),
                  tok(N_BRANCH * d), pl.BlockSpec((None, ADA_CHUNKS, d), lambda b, i: (b, 0, 0))]
        + [_full(w.shape) for w in weights],
        out_specs=[tok(d), tok(d + ROUTER_LANES), tok(ROUTER_LANES)],
        out_shape=[
            jax.ShapeDtypeStruct(x.shape, F32),
            jax.ShapeDtypeStruct((bsz, n_tok, d + ROUTER_LANES), BF16),
            jax.ShapeDtypeStruct((bsz, n_tok, ROUTER_LANES), F32),
        ],
        compiler_params=_cparams("parallel", "parallel"),
        name="merge_and_route",
    )(x, ya, xb, yf, yr, yc, gates, mod, *weights)


def router_params(rg_w, rg_b, re_w, re_b):
    d = rg_w.shape[0]
    pad = ROUTER_LANES - N_GROUPS - N_EXPERTS
    w = jnp.concatenate([rg_w, re_w, jnp.zeros((d, pad), F32)], axis=1).astype(F32)
    b = jnp.concatenate([rg_b, re_b, jnp.zeros((pad,), F32)]).astype(F32).reshape(1, ROUTER_LANES)
    return w, b


MOE_BLOCK = 144


def _split_bf16(x):
    hi = x.astype(BF16)
    return hi, (x - hi.astype(F32)).astype(BF16)


MOE_SUBTILE = 512


def _moe_kernel(xn_ref, hx_ref, comb_ref, mod_ref, wg_ref, wu_ref, wd_ref, fg_ref, o_ref,
                aux_col, aux_row, hid_scr, cnt_ref, *, final_norm):
    g = pl.program_id(2)
    tm, d = o_ref.shape
    st = min(MOE_SUBTILE, tm)
    n_sub = tm // st

    @pl.when(g == 0)
    def _():
        o_ref[...] = xn_ref[...]
        row_i = lax.broadcasted_iota(jnp.int32, (st, st), 0)
        col_i = lax.broadcasted_iota(jnp.int32, (st, st), 1)
        tri = jnp.where(col_i < row_i, 1.0, 0.0).astype(BF16)
        for s in range(n_sub):
            comb = comb_ref[s * st:(s + 1) * st, :]
            lane = lax.broadcasted_iota(jnp.int32, comb.shape, 1)
            grp = comb[:, GROUP_ID_LANE:GROUP_ID_LANE + 1]
            onehot = jnp.where(lane < N_GROUPS, jnp.where(lane.astype(F32) == grp, 1.0, 0.0), 0.0)
            ranks = jnp.dot(tri, onehot.astype(BF16), preferred_element_type=F32)
            own = jnp.sum(onehot * ranks, axis=-1, keepdims=True)
            aux = jnp.where(lane == GROUP_ID_LANE, grp, jnp.where(lane == RANK_LANE, own, 0.0))
            aux_col[s * st:(s + 1) * st, :] = aux
            aux_row[:, s * st:(s + 1) * st] = aux.T
            for gg in range(N_GROUPS):
                cnt_ref[s * N_GROUPS + gg] = jnp.sum(onehot[:, gg:gg + 1]).astype(jnp.int32)

    gf = g.astype(F32)
    slot_r = lax.broadcasted_iota(jnp.int32, (MOE_BLOCK, st), 0).astype(F32)
    slot_c = lax.broadcasted_iota(jnp.int32, (st, MOE_BLOCK), 1).astype(F32)
    scale = mod_ref[5:6, :]

    sub_rows = [slice(s * st, (s + 1) * st) for s in range(n_sub)]
    rank_rows = [jnp.where(aux_row[GROUP_ID_LANE:GROUP_ID_LANE + 1, r] == gf,
                           aux_row[RANK_LANE:RANK_LANE + 1, r], -1.0) for r in sub_rows]
    rank_cols = [jnp.where(aux_col[r, GROUP_ID_LANE:GROUP_ID_LANE + 1] == gf,
                           aux_col[r, RANK_LANE:RANK_LANE + 1], -1.0) for r in sub_rows]
    n_rounds = cnt_ref[g]
    for s in range(1, n_sub):
        n_rounds = jnp.maximum(n_rounds, cnt_ref[s * N_GROUPS + g])
    n_rounds = (n_rounds + MOE_BLOCK - 1) // MOE_BLOCK

    def round_(j, carry):
        base = (j * MOE_BLOCK).astype(F32)
        hcx = jnp.concatenate(
            [jnp.dot(jnp.where(rank_rows[s] - base == slot_r, 1.0, 0.0).astype(BF16), hx_ref[sub_rows[s], :],
                     preferred_element_type=F32) for s in range(n_sub)], axis=0)
        hc = hcx[:, :d].astype(BF16)
        wt = hcx[:, d:]
        for e in range(EXPERTS_PER_GROUP):
            a = jnp.dot(hc, wg_ref[e], preferred_element_type=F32)
            u = jnp.dot(hc, wu_ref[e], preferred_element_type=F32)
            cw = wt[:, e:e + 1] + wt[:, EXPERTS_PER_GROUP + e:EXPERTS_PER_GROUP + e + 1]
            hid_scr[:, e * D_EXPERT:(e + 1) * D_EXPERT] = (a * jax.nn.sigmoid(a) * u * cw).astype(BF16)
        oc = jnp.dot(hid_scr[...], wd_ref[...], preferred_element_type=F32).astype(BF16)
        for s in range(n_sub):
            scatter = jnp.where(rank_cols[s] - base == slot_c, 1.0, 0.0).astype(BF16)
            o_ref[sub_rows[s], :] += scale * jnp.dot(
                scatter, oc[s * MOE_BLOCK:(s + 1) * MOE_BLOCK], preferred_element_type=F32)
        return carry

    lax.fori_loop(0, n_rounds, round_, 0)

    if final_norm:
        @pl.when(g == pl.num_programs(2) - 1)
        def _():
            xo = o_ref[...]
            ms = jnp.mean(xo * xo, axis=-1, keepdims=True)
            o_ref[...] = xo * lax.rsqrt(ms + EPS) * fg_ref[...]


def moe_grouped(xn, hx, comb, mod, wg, wu, wd, final_g, tm, final_norm):
    bsz, n_tok, d = xn.shape
    gw = EXPERTS_PER_GROUP * D_EXPERT
    n_sub = tm // min(MOE_SUBTILE, tm)

    def tok(width):
        return pl.BlockSpec((None, tm, width), lambda b, i, g: (b, i, 0))

    return pl.pallas_call(
        functools.partial(_moe_kernel, final_norm=final_norm),
        grid=(bsz, n_tok // tm, N_GROUPS),
        in_specs=[
            tok(d), tok(d + ROUTER_LANES), tok(ROUTER_LANES),
            pl.BlockSpec((None, ADA_CHUNKS, d), lambda b, i, g: (b, 0, 0)),
            pl.BlockSpec((None, EXPERTS_PER_GROUP, d, D_EXPERT), lambda b, i, g: (g, 0, 0, 0)),
            pl.BlockSpec((None, EXPERTS_PER_GROUP, d, D_EXPERT), lambda b, i, g: (g, 0, 0, 0)),
            pl.BlockSpec((None, gw, d), lambda b, i, g: (g, 0, 0)),
            pl.BlockSpec((1, d), lambda b, i, g: (0, 0)),
        ],
        out_specs=tok(d),
        out_shape=jax.ShapeDtypeStruct(xn.shape, F32),
        scratch_shapes=[
            pltpu.VMEM((tm, ROUTER_LANES), F32),
            pltpu.VMEM((ROUTER_LANES, tm), F32),
            pltpu.VMEM((n_sub * MOE_BLOCK, gw), BF16),
            pltpu.SMEM((n_sub * N_GROUPS,), jnp.int32),
        ],
        compiler_params=_cparams("parallel", "parallel", "arbitrary"),
        name="moe_grouped",
    )(xn, hx, comb, mod, wg, wu, wd, final_g.reshape(1, d))


def moe_params(e_gate, e_up, e_down):
    _, d, f = e_gate.shape
    wg = e_gate.astype(BF16).reshape(N_GROUPS, EXPERTS_PER_GROUP, d, f)
    wu = e_up.astype(BF16).reshape(N_GROUPS, EXPERTS_PER_GROUP, d, f)
    wd = e_down.astype(BF16).reshape(N_GROUPS, EXPERTS_PER_GROUP * f, d)
    return wg, wu, wd


def s5_params(a_re, a_im, log_dt, b_re, b_im, c_re, c_im, bsz):
    lam = lax.complex(a_re.astype(F32), a_im.astype(F32))
    dt = jnp.exp(log_dt.astype(F32))[..., None]
    lam_bar = jnp.exp(lam * dt)
    b_bar = ((lam_bar - 1) / lam)[..., None] * lax.complex(b_re.astype(F32), b_im.astype(F32))
    gpb = S5_GROUPS // S5_BLOCKS
    eye = jnp.eye(gpb, dtype=F32)
    gp = S5_GROUPS * S5_STATE

    def in_mat(m):
        m = m.reshape(2, S5_BLOCKS, gpb, S5_STATE, S5_GROUP)
        return jnp.einsum('dkgpc,gh->dkgchp', m, eye).reshape(2, S5_BLOCKS, gpb * S5_GROUP, gpb * S5_STATE)

    def out_mat(m):
        m = m.reshape(2, S5_BLOCKS, gpb, S5_GROUP, S5_STATE)
        return jnp.einsum('dkgcp,gh->dkgphc', m, eye).reshape(2, S5_BLOCKS, gpb * S5_STATE, gpb * S5_GROUP)

    w_in = jnp.concatenate([in_mat(b_bar.real), in_mat(b_bar.imag)], axis=-1).astype(BF16)
    w_out = jnp.concatenate([out_mat(c_re.astype(F32)), -out_mat(c_im.astype(F32))], axis=2).astype(BF16)

    def tiles(v):
        t = v.reshape(2, gp // LANES, 1, LANES)
        t = jnp.broadcast_to(t, (2, gp // LANES, bsz, LANES))
        return jnp.concatenate([t[0], t[1]], axis=1)

    lam_flat = lam_bar.reshape(2, gp)
    return w_in, tiles(lam_flat.real), tiles(lam_flat.imag), w_out


TOKEN_TILE = 256
MIXER_TOKEN_TILE = 512
MERGE_TOKEN_TILE = 512
MOE_TOKEN_TILE = 1024


def kernel(x, c, ctx, c_ctx, ada_w, ada_b, norm_mix_g, norm_ffn_g, w_in, sgu_norm_g, sgu_w, sgu_b, s5_a_re, s5_a_im, s5_log_dt, s5_b_re, s5_b_im, s5_c_re, s5_c_im, s5_d, s5_glu_w, na_rpb, w_br_a, w_br_b, w_br_c, w_out, router_group_w, router_group_b, router_expert_w, router_expert_b, exp_w_gate, exp_w_up, exp_w_down, final_norm_g):
    bsz, n_tok, d = x.shape
    n_ctx = ctx.shape[1]
    depth = ada_w.shape[0]
    assert bsz + 1 <= SUBLANES

    cc = jnp.concatenate([c, c_ctx[None], jnp.zeros((SUBLANES - bsz - 1, d), F32)], axis=0)
    mod_all = ada_modulation(cc, ada_w, ada_b)
    rope_tabs = rope_tables(n_tok)
    s_zero = jnp.zeros((2 * bsz, S5_LANES), F32)
    tm_c = min(TOKEN_TILE, n_ctx)
    tm_moe = min(MOE_TOKEN_TILE, n_tok)

    xc = ctx
    for l in range(depth):
        with_ctx_out = l < depth - 1
        mod = mod_all[l, :bsz].reshape(bsz, ADA_CHUNKS, d)
        mod_c = jnp.broadcast_to(mod_all[l, bsz].reshape(1, ADA_CHUNKS, d), (bsz, ADA_CHUNKS, d))
        w_in_l = w_in[l].astype(BF16)
        sgu_w_l = sgu_w[l].astype(BF16)
        sgu_bias = jnp.broadcast_to(sgu_b[l].astype(F32)[:, :, None], (SGU_GROUPS, SGU_CHUNK, SGU_CHUNK))
        s5_w_in, s5_lre, s5_lim, s5_w_out = s5_params(
            s5_a_re[l], s5_a_im[l], s5_log_dt[l], s5_b_re[l], s5_b_im[l], s5_c_re[l], s5_c_im[l], bsz)
        table = na_bias_table(na_rpb[l])
        r_w, r_b = router_params(router_group_w[l], router_group_b[l], router_expert_w[l], router_expert_b[l])
        wg, wu, wd = moe_params(exp_w_gate[l], exp_w_up[l], exp_w_down[l])
        merge_w = (s5_d[l].astype(F32), s5_glu_w[l].astype(BF16), w_br_a[l].astype(BF16),
                   w_br_b[l].astype(BF16), w_br_c[l].astype(BF16), w_out[l].astype(BF16),
                   norm_ffn_g[l].astype(F32), r_w, r_b)

        ya_c, xb_c, q_c, k_c, v_c, gate_c = mixer_in(
            xc, mod_c, norm_mix_g[l], w_in_l, sgu_norm_g[l], sgu_w_l, sgu_bias, None, tm_c)
        ya_l, xb_l, q_l, k_l, v_l, gate_l = mixer_in(
            x, mod, norm_mix_g[l], w_in_l, sgu_norm_g[l], sgu_w_l, sgu_bias, rope_tabs, MIXER_TOKEN_TILE)
        ycf, ycr, s_ctx = s5_scan(xb_c, s_zero, s5_w_in, s5_lre, s5_lim, s5_w_out)
        ylf, ylr, _ = s5_scan(xb_l, s_ctx, s5_w_in, s5_lre, s5_lim, s5_w_out)
        yc_l = neighbourhood_attention(q_l, k_l, v_l, k_c, v_c, table)
        xn, h2, comb = merge_and_route(x, ya_l, xb_l, ylf, ylr, yc_l, gate_l, mod, *merge_w, MERGE_TOKEN_TILE)
        x = moe_grouped(xn, h2, comb, mod, wg, wu, wd, final_norm_g, tm_moe, not with_ctx_out)
        if with_ctx_out:
            yc_c = context_attention(q_c, k_c, v_c)
            xcn, hc2, comb_c = merge_and_route(xc, ya_c, xb_c, ycf, ycr, yc_c, gate_c, mod_c, *merge_w, tm_c)

            def flat(t):
                return t.reshape(1, bsz * n_ctx, t.shape[-1])

            xc = moe_grouped(flat(xcn), flat(hc2), flat(comb_c), mod_c[:1], wg, wu, wd, final_norm_g,
                             min(MOE_TOKEN_TILE, bsz * n_ctx), False).reshape(bsz, n_ctx, d)
    return x
```

```python
import functools
import math

import jax
import jax.numpy as jnp
import numpy as np
from jax import lax
from jax.experimental import pallas as pl
from jax.experimental.pallas import tpu as pltpu

F32 = jnp.float32
BF16 = jnp.bfloat16

GRID_W = 64
N_BRANCH = 3
SGU_WIDTH = 512
SGU_GROUPS = 4
SGU_CHUNK = 128
S5_WIDTH = 384
S5_GROUP = 16
S5_GROUPS = S5_WIDTH // S5_GROUP
S5_STATE = 64
NA_HEADS = 8
NA_HEAD_DIM = 64
NA_WIDTH = NA_HEADS * NA_HEAD_DIM
NA_KR = 8
NA_KC = 16
ROPE_BASE = 10000.0
N_GROUPS = 4
EXPERTS_PER_GROUP = 8
N_EXPERTS = N_GROUPS * EXPERTS_PER_GROUP
D_EXPERT = 256
ADA_CHUNKS = 6
EPS = 1e-6
NEG_INF = -1e30

LANES = 128
SUBLANES = 8
VMEM_LIMIT_BYTES = 56 * 1024 * 1024

S5_LANES = 2 * S5_GROUPS * S5_STATE
S5_SLABS = S5_LANES // LANES
S5_BLOCKS = S5_WIDTH // LANES
S5_CHUNK = 128
S5_PITCH = S5_CHUNK + 4


def _cparams(*sem):
    return pltpu.CompilerParams(dimension_semantics=sem, vmem_limit_bytes=VMEM_LIMIT_BYTES)


def _full(shape):
    n = len(shape)
    return pl.BlockSpec(shape, lambda *_: (0,) * n, pipeline_mode=pl.Buffered(1))


def _ada_kernel(c_ref, w_ref, b_ref, o_ref):
    c = c_ref[...]
    s = c * jax.nn.sigmoid(c)
    o_ref[...] = jnp.dot(s, w_ref[...], preferred_element_type=F32) + b_ref[...]


def ada_modulation(cc, ada_w, ada_b):
    n_layers, d, n = ada_w.shape
    tn = n // 4
    return pl.pallas_call(
        _ada_kernel,
        grid=(n_layers, n // tn),
        in_specs=[
            pl.BlockSpec((SUBLANES, d), lambda l, j: (0, 0)),
            pl.BlockSpec((None, d, tn), lambda l, j: (l, 0, j)),
            pl.BlockSpec((None, 1, tn), lambda l, j: (l, 0, j)),
        ],
        out_specs=pl.BlockSpec((None, SUBLANES, tn), lambda l, j: (l, 0, j)),
        out_shape=jax.ShapeDtypeStruct((n_layers, SUBLANES, n), F32),
        compiler_params=_cparams("parallel", "parallel"),
        name="ada_modulation",
    )(cc, ada_w, ada_b.reshape(n_layers, 1, n))


def _gelu(x):
    return jax.nn.gelu(x)


def _sigmoid(x):
    return 0.5 * jnp.tanh(0.5 * x) + 0.5


def _mixer_in_kernel(x_ref, mod_ref, g_ref, w_ref, lng_ref, sw_ref, sb_ref, *rest, rope):
    if rope:
        cos_ref, sin_ref, ya_ref, b_ref, q_ref, k_ref, v_ref, gate_ref = rest
    else:
        ya_ref, b_ref, q_ref, k_ref, v_ref, gate_ref = rest
    tm = x_ref.shape[0]
    xf = x_ref[...]
    ms = jnp.mean(xf * xf, axis=-1, keepdims=True)
    y = xf * lax.rsqrt(ms + EPS) * g_ref[...]
    h = y * (1.0 + mod_ref[1:2, :]) + mod_ref[0:1, :]
    hb = h.astype(BF16)

    def proj(lo, hi):
        return jnp.dot(hb, w_ref[:, lo:hi], preferred_element_type=F32)

    o1 = 2 * SGU_WIDTH
    o2 = o1 + S5_WIDTH
    oq, ok, ov = o2, o2 + NA_WIDTH, o2 + 2 * NA_WIDTH
    o3 = o2 + 3 * NA_WIDTH

    u = _gelu(proj(0, SGU_WIDTH))
    v = _gelu(proj(SGU_WIDTH, o1))
    vc = v - jnp.mean(v, axis=-1, keepdims=True)
    vn = vc * lax.rsqrt(jnp.mean(vc * vc, axis=-1, keepdims=True) + EPS) * lng_ref[...]
    vb = vn.astype(BF16)
    cw = SGU_WIDTH // SGU_GROUPS
    for c in range(tm // SGU_CHUNK):
        r0 = c * SGU_CHUNK
        for g in range(SGU_GROUPS):
            sp = jnp.dot(sw_ref[g], vb[r0:r0 + SGU_CHUNK, g * cw:(g + 1) * cw],
                         preferred_element_type=F32) + sb_ref[g]
            ya_ref[r0:r0 + SGU_CHUNK, g * cw:(g + 1) * cw] = (
                u[r0:r0 + SGU_CHUNK, g * cw:(g + 1) * cw] * sp).astype(ya_ref.dtype)

    b_ref[...] = proj(o1, o2).astype(b_ref.dtype)

    q = proj(oq, ok)
    k = proj(ok, ov)
    if rope:
        cos = cos_ref[...]
        sin = sin_ref[...]

        seg_half = NA_HEAD_DIM // 4
        first_half = (lax.broadcasted_iota(jnp.int32, cos.shape, 1) % (2 * seg_half)) < seg_half

        def rotate(t):
            out = []
            for j in range(NA_WIDTH // LANES):
                tj = t[:, j * LANES:(j + 1) * LANES]
                swapped = jnp.where(first_half, pltpu.roll(tj, LANES - seg_half, 1), pltpu.roll(tj, seg_half, 1))
                out.append(tj * cos + swapped * sin)
            return jnp.concatenate(out, axis=-1)

        q = rotate(q)
        k = rotate(k)
    q_scale = NA_HEAD_DIM ** -0.5 * (LOG2E if rope else 1.0)
    q_ref[...] = (q * q_scale).astype(q_ref.dtype)
    k_ref[...] = k.astype(k_ref.dtype)
    v_ref[...] = proj(ov, o3).astype(v_ref.dtype)
    gate_ref[...] = _sigmoid(proj(o3, o3 + N_BRANCH * x_ref.shape[1])).astype(gate_ref.dtype)


def mixer_in(x, mod, norm_g, w_in, sgu_norm_g, sgu_w, sgu_bias, rope_tabs, tm):
    bsz, n_tok, d = x.shape
    rope = rope_tabs is not None
    in_specs = [
        pl.BlockSpec((None, tm, d), lambda b, i: (b, i, 0)),
        pl.BlockSpec((None, ADA_CHUNKS, d), lambda b, i: (b, 0, 0)),
        _full((1, d)),
        _full(w_in.shape),
        _full((1, SGU_WIDTH)),
        _full(sgu_w.shape),
        _full(sgu_bias.shape),
    ]
    args = [x, mod, norm_g.reshape(1, d), w_in, sgu_norm_g.reshape(1, SGU_WIDTH), sgu_w, sgu_bias]
    if rope:
        cos_t, sin_t = rope_tabs
        in_specs += [
            pl.BlockSpec((tm, LANES), lambda b, i: (i, 0)),
            pl.BlockSpec((tm, LANES), lambda b, i: (i, 0)),
        ]
        args += [cos_t, sin_t]

    def tok(width):
        return pl.BlockSpec((None, tm, width), lambda b, i: (b, i, 0))

    out_shapes = [
        jax.ShapeDtypeStruct((bsz, n_tok, SGU_WIDTH), BF16),
        jax.ShapeDtypeStruct((bsz, n_tok, S5_WIDTH), BF16),
        jax.ShapeDtypeStruct((bsz, n_tok, NA_WIDTH), BF16),
        jax.ShapeDtypeStruct((bsz, n_tok, NA_WIDTH), BF16),
        jax.ShapeDtypeStruct((bsz, n_tok, NA_WIDTH), BF16),
        jax.ShapeDtypeStruct((bsz, n_tok, N_BRANCH * d), BF16),
    ]
    out_specs = [tok(SGU_WIDTH), tok(S5_WIDTH), tok(NA_WIDTH), tok(NA_WIDTH), tok(NA_WIDTH), tok(N_BRANCH * d)]
    return pl.pallas_call(
        functools.partial(_mixer_in_kernel, rope=rope),
        grid=(bsz, n_tok // tm),
        in_specs=in_specs,
        out_specs=out_specs,
        out_shape=out_shapes,
        compiler_params=_cparams("parallel", "parallel"),
        name="mixer_in_rope" if rope else "mixer_in",
    )(*args)


def rope_tables(n_tok):
    pos = np.arange(n_tok)
    rows = (pos // GRID_W).astype(np.float32)
    cols = (pos % GRID_W).astype(np.float32)
    seg = NA_HEAD_DIM // 2
    half = seg // 2
    inv_freq = (ROPE_BASE ** (-np.arange(half, dtype=np.float32) / half)).astype(np.float32)
    ang_r = rows[:, None] * inv_freq
    ang_c = cols[:, None] * inv_freq
    cos = np.concatenate([np.cos(ang_r)] * 2 + [np.cos(ang_c)] * 2, axis=-1)
    sin = np.concatenate([-np.sin(ang_r), np.sin(ang_r), -np.sin(ang_c), np.sin(ang_c)], axis=-1)
    reps = LANES // NA_HEAD_DIM
    return jnp.asarray(np.tile(cos, (1, reps)), F32), jnp.asarray(np.tile(sin, (1, reps)), F32)


def _s5_kernel(xf_ref, xfn_ref, xr_ref, xrn_ref, s0_ref, win_ref, lre_ref, lim_ref, wout_ref, rev_ref,
               yf_ref, yr_ref, send_ref, lhs_scr, st_a, st_b, state_scr):
    bsz = xf_ref.shape[0]
    tc = S5_CHUNK
    pitch = S5_PITCH
    rows = bsz * pitch
    half = S5_SLABS // 2
    bw = half // S5_BLOCKS
    j = pl.program_id(0)
    rev = rev_ref[...]
    first, second = slice(0, tc), slice(tc, 2 * tc)

    def project_in(f_ref, f_half, r_ref, r_half, st):
        for b in range(bsz):
            lhs_scr[0, b * pitch:b * pitch + tc, :] = f_ref[b, f_half, :].astype(F32)
            lhs_scr[1, b * pitch:b * pitch + tc, :] = jnp.dot(rev, r_ref[b, r_half, :], preferred_element_type=F32)
        for d in range(2):
            lhs = lhs_scr[d].astype(BF16)
            for cb in range(S5_BLOCKS):
                res = jnp.dot(lhs[:, cb * LANES:(cb + 1) * LANES], win_ref[d, cb], preferred_element_type=F32)
                for k in range(bw):
                    st[cb * bw + k, d * rows:(d + 1) * rows, :] = res[:, k * LANES:(k + 1) * LANES]
                    st[half + cb * bw + k, d * rows:(d + 1) * rows, :] = res[:, (bw + k) * LANES:(bw + k + 1) * LANES]

    def scan(st, state):
        cur = list(state)
        for t in range(tc):
            idx = pl.ds(t, 2 * bsz, stride=pitch)
            for c in range(half):
                sre, sim = cur[c], cur[c + half]
                lr = lre_ref[c]
                li = lim_ref[c]
                nre = lr * sre - li * sim + st[c, idx, :]
                nim = lr * sim + li * sre + st[c + half, idx, :]
                st[c, idx, :] = nre
                st[c + half, idx, :] = nim
                cur[c], cur[c + half] = nre, nim
        return cur

    def project_out(st, f_half, r_half):
        for d in range(2):
            y_blocks = []
            for cb in range(S5_BLOCKS):
                slabs = [cb * bw + k for k in range(bw)] + [half + cb * bw + k for k in range(bw)]
                s_blk = jnp.concatenate([st[c, d * rows:(d + 1) * rows, :] for c in slabs], axis=-1)
                y_blocks.append(jnp.dot(s_blk.astype(BF16), wout_ref[d, cb], preferred_element_type=F32))
            y = jnp.concatenate(y_blocks, axis=-1)
            for b in range(bsz):
                yb = y[b * pitch:b * pitch + tc, :]
                if d == 0:
                    yf_ref[b, f_half, :] = yb
                else:
                    yr_ref[b, r_half, :] = jnp.dot(rev, yb.astype(BF16), preferred_element_type=F32)

    @pl.when(j == 0)
    def _():
        state_scr[...] = s0_ref[...]
        lhs_scr[...] = jnp.zeros_like(lhs_scr)
        project_in(xf_ref, first, xr_ref, second, st_a)

    project_in(xf_ref, second, xr_ref, first, st_b)
    state = scan(st_a, [state_scr[:, c * LANES:(c + 1) * LANES] for c in range(S5_SLABS)])
    project_out(st_a, first, second)
    project_in(xfn_ref, first, xrn_ref, second, st_a)
    state = scan(st_b, state)
    project_out(st_b, second, first)
    for c in range(S5_SLABS):
        state_scr[:, c * LANES:(c + 1) * LANES] = state[c]
    send_ref[...] = state_scr[...]


def s5_scan(xb, s0, w_in, lam_re, lam_im, w_out):
    bsz, n_tok, width = xb.shape
    tc = S5_CHUNK
    assert 2 * bsz == SUBLANES and n_tok % (2 * tc) == 0
    n = n_tok // (2 * tc)
    rev = jnp.asarray(np.eye(tc, dtype=np.float32)[::-1], BF16)
    blk = (bsz, 2 * tc, width)
    return pl.pallas_call(
        _s5_kernel,
        grid=(n,),
        in_specs=[
            pl.BlockSpec(blk, lambda j: (0, j, 0)),
            pl.BlockSpec(blk, lambda j: (0, jnp.minimum(j + 1, n - 1), 0)),
            pl.BlockSpec(blk, lambda j: (0, n - 1 - j, 0)),
            pl.BlockSpec(blk, lambda j: (0, jnp.maximum(n - 2 - j, 0), 0)),
            _full(s0.shape),
            _full(w_in.shape),
            _full(lam_re.shape),
            _full(lam_im.shape),
            _full(w_out.shape),
            _full(rev.shape),
        ],
        out_specs=[
            pl.BlockSpec(blk, lambda j: (0, j, 0)),
            pl.BlockSpec(blk, lambda j: (0, n - 1 - j, 0)),
            pl.BlockSpec(s0.shape, lambda j: (0, 0)),
        ],
        out_shape=[
            jax.ShapeDtypeStruct(xb.shape, F32),
            jax.ShapeDtypeStruct(xb.shape, F32),
            jax.ShapeDtypeStruct(s0.shape, F32),
        ],
        scratch_shapes=[
            pltpu.VMEM((2, bsz * S5_PITCH, width), F32),
            pltpu.VMEM((S5_SLABS, 2 * bsz * S5_PITCH, LANES), F32),
            pltpu.VMEM((S5_SLABS, 2 * bsz * S5_PITCH, LANES), F32),
            pltpu.VMEM(s0.shape, F32),
        ],
        compiler_params=_cparams("arbitrary"),
        name="s5_scan",
    )(xb, xb, xb, xb, s0, w_in, lam_re, lam_im, w_out, rev)


NA_ROWS_PER_STEP = 4
_NT = (((1,), (1,)), ((), ()))


def _na_head_pair(qp, kp, vp, kcp, vcp, bias_fn):
    lane_head = lax.broadcasted_iota(jnp.int32, qp.shape, 1) // NA_HEAD_DIM
    o_pair = None
    for hh in range(2):
        qm = jnp.where(lane_head == hh, qp, jnp.zeros_like(qp))
        s_ctx = lax.dot_general(qm, kcp, _NT, preferred_element_type=F32)
        m = jnp.max(s_ctx, axis=-1, keepdims=True)
        if kp is not None:
            s_win = lax.dot_general(qm, kp, _NT, preferred_element_type=F32) + bias_fn(hh)
            m = jnp.maximum(m, jnp.max(s_win, axis=-1, keepdims=True))
            e_win = jnp.exp(s_win - m)
        e_ctx = jnp.exp(s_ctx - m)
        den = jnp.sum(e_ctx, axis=-1, keepdims=True)
        o = jnp.dot(e_ctx.astype(BF16), vcp, preferred_element_type=F32)
        if kp is not None:
            den = den + jnp.sum(e_win, axis=-1, keepdims=True)
            o = o + jnp.dot(e_win.astype(BF16), vp, preferred_element_type=F32)
        o = o * (1.0 / den)
        o_pair = o if hh == 0 else jnp.where(lane_head == 0, o_pair, o)
    return o_pair


NA_UNION_ROWS = 12
NA_SLAB = 32
LOG2E = math.log2(math.e)
NA_TAB_LEFT_OUT = 2 * NA_KR
NA_TAB_RIGHT_OUT = NA_TAB_LEFT_OUT + 2 * NA_KR - 1
NA_TAB_BOTH_OUT = NA_TAB_RIGHT_OUT + 2 * NA_KR - 1


def _na_union_start(i, n_rows):
    return jnp.clip(i * NA_ROWS_PER_STEP - NA_KR // 2, 0, n_rows - NA_UNION_ROWS)


def _na_kernel(q_ref, k_ref, v_ref, kc_ref, vc_ref, tab_ref, o_ref, s_scr, p_scr, rden_scr, *, n_rows):
    i = pl.program_id(1)
    r0 = i * NA_ROWS_PER_STEP
    ks = _na_union_start(i, n_rows)
    n_keys = NA_UNION_ROWS * GRID_W
    n_q = NA_ROWS_PER_STEP * GRID_W
    n_pairs = NA_UNION_ROWS // 2
    slabs_per_row = GRID_W // NA_SLAB

    def table_entry(rr, jj):
        r = r0 + rr
        kst = jnp.clip(r - NA_KR // 2, 0, n_rows - NA_KR)
        key0 = ks + 2 * jj
        out0 = jnp.logical_or(key0 < kst, key0 >= kst + NA_KR)
        out1 = jnp.logical_or(key0 + 1 < kst, key0 + 1 >= kst + NA_KR)
        e = key0 - r + NA_KR
        both_in = jnp.clip(e, 0, 2 * NA_KR - 1)
        left_out = NA_TAB_LEFT_OUT + jnp.clip(e, 0, 2 * NA_KR - 2)
        right_out = NA_TAB_RIGHT_OUT + jnp.clip(e - 1, 0, 2 * NA_KR - 2)
        return jnp.where(out0, jnp.where(out1, NA_TAB_BOTH_OUT, left_out), jnp.where(out1, right_out, both_in))

    entries = [[table_entry(rr, jj) for jj in range(n_pairs)] for rr in range(NA_ROWS_PER_STEP)]

    lane_head = lax.broadcasted_iota(jnp.int32, (n_q, LANES), 1) // NA_HEAD_DIM

    def scores(p):
        ls = slice(p * LANES, (p + 1) * LANES)
        qp = q_ref[:, ls]
        zero = jnp.zeros_like(qp)
        q_stack = jnp.concatenate([jnp.where(lane_head == 0, qp, zero), jnp.where(lane_head == 1, qp, zero)], axis=0)
        k_all = jnp.concatenate([k_ref[0, :, ls], kc_ref[:, ls]], axis=0)
        s_scr[p] = lax.dot_general(q_stack, k_all, _NT, preferred_element_type=F32)

    scores(0)
    for p in range(NA_HEADS // 2):
        if p + 1 < NA_HEADS // 2:
            scores(p + 1)
        ls = slice(p * LANES, (p + 1) * LANES)
        v_all = jnp.concatenate([v_ref[0, :, ls], vc_ref[:, ls]], axis=0)
        s_p, p_p, rden_p = s_scr.at[p], p_scr.at[p], rden_scr.at[p]
        for sl in range(2 * n_q // NA_SLAB):
            hh = sl // (NA_ROWS_PER_STEP * slabs_per_row)
            rr = (sl // slabs_per_row) % NA_ROWS_PER_STEP
            q0 = (sl % slabs_per_row) * NA_SLAB
            rows = slice(sl * NA_SLAB, (sl + 1) * NA_SLAB)
            bias = jnp.concatenate(
                [tab_ref[2 * p + hh, entries[rr][jj], q0:q0 + NA_SLAB, :] for jj in range(n_pairs)], axis=-1)
            s_win = s_p[rows, :n_keys] + bias
            s_ctx = s_p[rows, n_keys:]
            m = jnp.maximum(jnp.max(s_win, axis=-1, keepdims=True), jnp.max(s_ctx, axis=-1, keepdims=True))
            e_win = jnp.exp2(s_win - m)
            e_ctx = jnp.exp2(s_ctx - m)
            den = jnp.sum(e_win, axis=-1, keepdims=True) + jnp.sum(e_ctx, axis=-1, keepdims=True)
            p_p[rows, :n_keys] = e_win.astype(BF16)
            p_p[rows, n_keys:] = e_ctx.astype(BF16)
            rden_p[rows, :] = jnp.broadcast_to(1.0 / den, (NA_SLAB, LANES))
        o = jnp.dot(p_p[...], v_all, preferred_element_type=F32) * rden_p[...]
        o_ref[:, ls] = jnp.where(lane_head == 0, o[:n_q], o[n_q:]).astype(o_ref.dtype)


def na_bias_table(rpb):
    w = np.arange(GRID_W)
    col_start = np.clip(w - NA_KC // 2, 0, GRID_W - NA_KC)
    col_mask = (w[None, :] >= col_start[:, None]) & (w[None, :] < col_start[:, None] + NA_KC)
    d_col = np.clip(w[None, :] - w[:, None], -(NA_KC - 1), NA_KC - 1) + (NA_KC - 1)
    pick = (d_col[None] == np.arange(2 * NA_KC - 1)[:, None, None]).astype(np.float32)
    full = jnp.einsum('hdj,jwu->hdwu', rpb.astype(F32), jnp.asarray(pick), precision=lax.Precision.HIGHEST)
    full = jnp.where(col_mask[None, None], full * LOG2E, NEG_INF)
    pad = jnp.zeros_like(full[:, :1])
    neg = jnp.full_like(full, NEG_INF)
    both_in = jnp.concatenate([jnp.concatenate([pad, full], axis=1), jnp.concatenate([full, pad], axis=1)], axis=-1)
    left_out = jnp.concatenate([neg, full], axis=-1)
    right_out = jnp.concatenate([full, neg], axis=-1)
    both_out = jnp.concatenate([neg[:, :1], neg[:, :1]], axis=-1)
    return jnp.concatenate([both_in, left_out, right_out, both_out], axis=1)


def neighbourhood_attention(q, k, v, kc, vc, table):
    bsz, n_tok, width = q.shape
    n_ctx = kc.shape[1]
    n_rows = n_tok // GRID_W
    assert n_rows >= NA_UNION_ROWS and n_rows % NA_ROWS_PER_STEP == 0
    tq = NA_ROWS_PER_STEP * GRID_W
    n_keys = NA_UNION_ROWS * GRID_W
    window = pl.BlockSpec((pl.Element(1), pl.Element(n_keys), pl.Element(width)),
                          lambda b, i: (b, _na_union_start(i, n_rows) * GRID_W, 0))
    return pl.pallas_call(
        functools.partial(_na_kernel, n_rows=n_rows),
        grid=(bsz, n_rows // NA_ROWS_PER_STEP),
        in_specs=[
            pl.BlockSpec((None, tq, width), lambda b, i: (b, i, 0)),
            window,
            window,
            pl.BlockSpec((None, n_ctx, width), lambda b, i: (b, 0, 0)),
            pl.BlockSpec((None, n_ctx, width), lambda b, i: (b, 0, 0)),
            _full(table.shape),
        ],
        out_specs=pl.BlockSpec((None, tq, width), lambda b, i: (b, i, 0)),
        out_shape=jax.ShapeDtypeStruct(q.shape, BF16),
        scratch_shapes=[
            pltpu.VMEM((NA_HEADS // 2, 2 * tq, n_keys + n_ctx), F32),
            pltpu.VMEM((NA_HEADS // 2, 2 * tq, n_keys + n_ctx), BF16),
            pltpu.VMEM((NA_HEADS // 2, 2 * tq, LANES), F32),
        ],
        compiler_params=_cparams("parallel", "arbitrary"),
        name="neighbourhood_attention",
    )(q, k, v, kc, vc, table)


def _ctx_attn_kernel(q_ref, k_ref, v_ref, o_ref):
    for p in range(NA_HEADS // 2):
        ls = slice(p * LANES, (p + 1) * LANES)
        o_pair = _na_head_pair(q_ref[:, ls], None, None, k_ref[:, ls], v_ref[:, ls], None)
        o_ref[:, ls] = o_pair.astype(o_ref.dtype)


def context_attention(qc, kc, vc):
    bsz, n_ctx, width = qc.shape
    spec = pl.BlockSpec((None, n_ctx, width), lambda b: (b, 0, 0))
    return pl.pallas_call(
        _ctx_attn_kernel,
        grid=(bsz,),
        in_specs=[spec, spec, spec],
        out_specs=spec,
        out_shape=jax.ShapeDtypeStruct(qc.shape, BF16),
        compiler_params=_cparams("parallel"),
        name="context_attention",
    )(qc, kc, vc)


ROUTER_LANES = LANES
EXPERT_LANE0 = N_GROUPS
GROUP_ID_LANE = 0
RANK_LANE = 1


def _route(logits):
    lane = lax.broadcasted_iota(jnp.int32, logits.shape, 1)
    big = jnp.int32(ROUTER_LANES)
    is_g = lane < N_GROUPS
    lg = jnp.where(is_g, logits, -jnp.inf)
    mg = jnp.max(lg, axis=-1, keepdims=True)
    grp = jnp.min(jnp.where(lg == mg, lane, big), axis=-1, keepdims=True)
    g_weight = 1.0 / jnp.sum(jnp.where(is_g, jnp.exp(logits - mg), 0.0), axis=-1, keepdims=True)
    e_idx = lane - EXPERT_LANE0
    sel = (e_idx >= 0) & (e_idx < N_EXPERTS) & ((e_idx // EXPERTS_PER_GROUP) == grp)
    ls1 = jnp.where(sel, logits, -jnp.inf)
    v1 = jnp.max(ls1, axis=-1, keepdims=True)
    i1 = jnp.min(jnp.where(ls1 == v1, lane, big), axis=-1, keepdims=True)
    ls2 = jnp.where(lane == i1, -jnp.inf, ls1)
    v2 = jnp.max(ls2, axis=-1, keepdims=True)
    i2 = jnp.min(jnp.where(ls2 == v2, lane, big), axis=-1, keepdims=True)
    e2 = jnp.exp(v2 - v1)
    w1 = 1.0 / (1.0 + e2)
    w2 = e2 * w1
    comb = g_weight * (jnp.where(lane == i1, w1, 0.0) + jnp.where(lane == i2, w2, 0.0))
    return jnp.where(lane == GROUP_ID_LANE, grp.astype(F32), comb)


def _merge_kernel(x_ref, ya_ref, xb_ref, yf_ref, yr_ref, yc_ref, gate_ref, mod_ref, d_ref, glu_ref,
                  wa_ref, wb_ref, wc_ref, wo_ref, g_ref, rw_ref, rb_ref, own_ref, xn_ref, h_ref, comb_ref):
    d = x_ref.shape[1]
    yb = d_ref[...] * xb_ref[...].astype(F32) + yf_ref[...] + yr_ref[...]
    yb = _gelu(yb)
    yb = yb * _sigmoid(jnp.dot(yb.astype(BF16), glu_ref[...], preferred_element_type=F32))

    def gate(j):
        return gate_ref[:, j * d:(j + 1) * d].astype(F32)

    m = gate(0) * jnp.dot(ya_ref[...], wa_ref[...], preferred_element_type=F32)
    m = m + gate(1) * jnp.dot(yb.astype(BF16), wb_ref[...], preferred_element_type=F32)
    m = m + gate(2) * jnp.dot(yc_ref[...], wc_ref[...], preferred_element_type=F32)
    xn = x_ref[...] + mod_ref[2:3, :] * jnp.dot(m.astype(BF16), wo_ref[...], preferred_element_type=F32)
    xn_ref[...] = xn
    ms = jnp.mean(xn * xn, axis=-1, keepdims=True)
    h = xn * lax.rsqrt(ms + EPS) * g_ref[...]
    h = h * (1.0 + mod_ref[4:5, :]) + mod_ref[3:4, :]
    h_ref[:, :d] = h.astype(h_ref.dtype)
    logits = jnp.dot(h, rw_ref[...], preferred_element_type=F32) + rb_ref[...]
    comb = _route(logits)
    comb_ref[...] = comb
    own = jnp.dot(comb, own_ref[...], preferred_element_type=F32)
    own_hi, own_lo = _split_bf16(own)
    lane = lax.broadcasted_iota(jnp.int32, own.shape, 1)
    h_ref[:, d:] = jnp.where(lane < EXPERTS_PER_GROUP, own_hi, own_lo)


def merge_and_route(x, ya, xb, yf, yr, yc, gates, mod, s5_d, glu_w, w_br_a, w_br_b, w_br_c, w_out,
                    norm_ffn_g, router_w, router_b, tm):
    bsz, n_tok, d = x.shape

    def tok(width):
        return pl.BlockSpec((None, tm, width), lambda b, i: (b, i, 0))

    own = np.zeros((ROUTER_LANES, ROUTER_LANES), np.float32)
    for e in range(N_EXPERTS):
        own[EXPERT_LANE0 + e, e % EXPERTS_PER_GROUP] = 1.0
        own[EXPERT_LANE0 + e, EXPERTS_PER_GROUP + e % EXPERTS_PER_GROUP] = 1.0
    weights = [s5_d.reshape(1, S5_WIDTH), glu_w, w_br_a, w_br_b, w_br_c, w_out,
               norm_ffn_g.reshape(1, d), router_w, router_b, jnp.asarray(own)]
    return pl.pallas_call(
        _merge_kernel,
        grid=(bsz, n_tok // tm),
        in_specs=[tok(d), tok(SGU_WIDTH), tok(S5_WIDTH), tok(S5_WIDTH), tok(S5_WIDTH), tok(NA_WIDTH),
                  tok(N_BRANCH * d), pl.BlockSpec((None, ADA_CHUNKS, d), lambda b, i: (b, 0, 0))]
        + [_full(w.shape) for w in weights],
        out_specs=[tok(d), tok(d + ROUTER_LANES), tok(ROUTER_LANES)],
        out_shape=[
            jax.ShapeDtypeStruct(x.shape, F32),
            jax.ShapeDtypeStruct((bsz, n_tok, d + ROUTER_LANES), BF16),
            jax.ShapeDtypeStruct((bsz, n_tok, ROUTER_LANES), F32),
        ],
        compiler_params=_cparams("parallel", "parallel"),
        name="merge_and_route",
    )(x, ya, xb, yf, yr, yc, gates, mod, *weights)


def router_params(rg_w, rg_b, re_w, re_b):
    d = rg_w.shape[0]
    pad = ROUTER_LANES - N_GROUPS - N_EXPERTS
    w = jnp.concatenate([rg_w, re_w, jnp.zeros((d, pad), F32)], axis=1).astype(F32)
    b = jnp.concatenate([rg_b, re_b, jnp.zeros((pad,), F32)]).astype(F32).reshape(1, ROUTER_LANES)
    return w, b


MOE_BLOCK = 144


def _split_bf16(x):
    hi = x.astype(BF16)
    return hi, (x - hi.astype(F32)).astype(BF16)


MOE_SUBTILE = 512


def _moe_kernel(xn_ref, hx_ref, comb_ref, mod_ref, wg_ref, wu_ref, wd_ref, fg_ref, o_ref,
                aux_col, aux_row, hid_scr, cnt_ref, *, final_norm):
    g = pl.program_id(2)
    tm, d = o_ref.shape
    st = min(MOE_SUBTILE, tm)
    n_sub = tm // st

    @pl.when(g == 0)
    def _():
        o_ref[...] = xn_ref[...]
        row_i = lax.broadcasted_iota(jnp.int32, (st, st), 0)
        col_i = lax.broadcasted_iota(jnp.int32, (st, st), 1)
        tri = jnp.where(col_i < row_i, 1.0, 0.0).astype(BF16)
        for s in range(n_sub):
            comb = comb_ref[s * st:(s + 1) * st, :]
            lane = lax.broadcasted_iota(jnp.int32, comb.shape, 1)
            grp = comb[:, GROUP_ID_LANE:GROUP_ID_LANE + 1]
            onehot = jnp.where(lane < N_GROUPS, jnp.where(lane.astype(F32) == grp, 1.0, 0.0), 0.0)
            ranks = jnp.dot(tri, onehot.astype(BF16), preferred_element_type=F32)
            own = jnp.sum(onehot * ranks, axis=-1, keepdims=True)
            aux = jnp.where(lane == GROUP_ID_LANE, grp, jnp.where(lane == RANK_LANE, own, 0.0))
            aux_col[s * st:(s + 1) * st, :] = aux
            aux_row[:, s * st:(s + 1) * st] = aux.T
            for gg in range(N_GROUPS):
                cnt_ref[s * N_GROUPS + gg] = jnp.sum(onehot[:, gg:gg + 1]).astype(jnp.int32)

    gf = g.astype(F32)
    slot_r = lax.broadcasted_iota(jnp.int32, (MOE_BLOCK, st), 0).astype(F32)
    slot_c = lax.broadcasted_iota(jnp.int32, (st, MOE_BLOCK), 1).astype(F32)
    scale = mod_ref[5:6, :]

    sub_rows = [slice(s * st, (s + 1) * st) for s in range(n_sub)]
    rank_rows = [jnp.where(aux_row[GROUP_ID_LANE:GROUP_ID_LANE + 1, r] == gf,
                           aux_row[RANK_LANE:RANK_LANE + 1, r], -1.0) for r in sub_rows]
    rank_cols = [jnp.where(aux_col[r, GROUP_ID_LANE:GROUP_ID_LANE + 1] == gf,
                           aux_col[r, RANK_LANE:RANK_LANE + 1], -1.0) for r in sub_rows]
    n_rounds = cnt_ref[g]
    for s in range(1, n_sub):
        n_rounds = jnp.maximum(n_rounds, cnt_ref[s * N_GROUPS + g])
    n_rounds = (n_rounds + MOE_BLOCK - 1) // MOE_BLOCK

    def round_(j, carry):
        base = (j * MOE_BLOCK).astype(F32)
        hcx = jnp.concatenate(
            [jnp.dot(jnp.where(rank_rows[s] - base == slot_r, 1.0, 0.0).astype(BF16), hx_ref[sub_rows[s], :],
                     preferred_element_type=F32) for s in range(n_sub)], axis=0)
        hc = hcx[:, :d].astype(BF16)
        wt = hcx[:, d:]
        for e in range(EXPERTS_PER_GROUP):
            a = jnp.dot(hc, wg_ref[e], preferred_element_type=F32)
            u = jnp.dot(hc, wu_ref[e], preferred_element_type=F32)
            cw = wt[:, e:e + 1] + wt[:, EXPERTS_PER_GROUP + e:EXPERTS_PER_GROUP + e + 1]
            hid_scr[:, e * D_EXPERT:(e + 1) * D_EXPERT] = (a * jax.nn.sigmoid(a) * u * cw).astype(BF16)
        oc = jnp.dot(hid_scr[...], wd_ref[...], preferred_element_type=F32).astype(BF16)
        for s in range(n_sub):
            scatter = jnp.where(rank_cols[s] - base == slot_c, 1.0, 0.0).astype(BF16)
            o_ref[sub_rows[s], :] += scale * jnp.dot(
                scatter, oc[s * MOE_BLOCK:(s + 1) * MOE_BLOCK], preferred_element_type=F32)
        return carry

    lax.fori_loop(0, n_rounds, round_, 0)

    if final_norm:
        @pl.when(g == pl.num_programs(2) - 1)
        def _():
            xo = o_ref[...]
            ms = jnp.mean(xo * xo, axis=-1, keepdims=True)
            o_ref[...] = xo * lax.rsqrt(ms + EPS) * fg_ref[...]


def moe_grouped(xn, hx, comb, mod, wg, wu, wd, final_g, tm, final_norm):
    bsz, n_tok, d = xn.shape
    gw = EXPERTS_PER_GROUP * D_EXPERT
    n_sub = tm // min(MOE_SUBTILE, tm)

    def tok(width):
        return pl.BlockSpec((None, tm, width), lambda b, i, g: (b, i, 0))

    return pl.pallas_call(
        functools.partial(_moe_kernel, final_norm=final_norm),
        grid=(bsz, n_tok // tm, N_GROUPS),
        in_specs=[
            tok(d), tok(d + ROUTER_LANES), tok(ROUTER_LANES),
            pl.BlockSpec((None, ADA_CHUNKS, d), lambda b, i, g: (b, 0, 0)),
            pl.BlockSpec((None, EXPERTS_PER_GROUP, d, D_EXPERT), lambda b, i, g: (g, 0, 0, 0)),
            pl.BlockSpec((None, EXPERTS_PER_GROUP, d, D_EXPERT), lambda b, i, g: (g, 0, 0, 0)),
            pl.BlockSpec((None, gw, d), lambda b, i, g: (g, 0, 0)),
            pl.BlockSpec((1, d), lambda b, i, g: (0, 0)),
        ],
        out_specs=tok(d),
        out_shape=jax.ShapeDtypeStruct(xn.shape, F32),
        scratch_shapes=[
            pltpu.VMEM((tm, ROUTER_LANES), F32),
            pltpu.VMEM((ROUTER_LANES, tm), F32),
            pltpu.VMEM((n_sub * MOE_BLOCK, gw), BF16),
            pltpu.SMEM((n_sub * N_GROUPS,), jnp.int32),
        ],
        compiler_params=_cparams("parallel", "parallel", "arbitrary"),
        name="moe_grouped",
    )(xn, hx, comb, mod, wg, wu, wd, final_g.reshape(1, d))


def moe_params(e_gate, e_up, e_down):
    _, d, f = e_gate.shape
    wg = e_gate.astype(BF16).reshape(N_GROUPS, EXPERTS_PER_GROUP, d, f)
    wu = e_up.astype(BF16).reshape(N_GROUPS, EXPERTS_PER_GROUP, d, f)
    wd = e_down.astype(BF16).reshape(N_GROUPS, EXPERTS_PER_GROUP * f, d)
    return wg, wu, wd


def s5_params(a_re, a_im, log_dt, b_re, b_im, c_re, c_im, bsz):
    lam = lax.complex(a_re.astype(F32), a_im.astype(F32))
    dt = jnp.exp(log_dt.astype(F32))[..., None]
    lam_bar = jnp.exp(lam * dt)
    b_bar = ((lam_bar - 1) / lam)[..., None] * lax.complex(b_re.astype(F32), b_im.astype(F32))
    gpb = S5_GROUPS // S5_BLOCKS
    eye = jnp.eye(gpb, dtype=F32)
    gp = S5_GROUPS * S5_STATE

    def in_mat(m):
        m = m.reshape(2, S5_BLOCKS, gpb, S5_STATE, S5_GROUP)
        return jnp.einsum('dkgpc,gh->dkgchp', m, eye).reshape(2, S5_BLOCKS, gpb * S5_GROUP, gpb * S5_STATE)

    def out_mat(m):
        m = m.reshape(2, S5_BLOCKS, gpb, S5_GROUP, S5_STATE)
        return jnp.einsum('dkgcp,gh->dkgphc', m, eye).reshape(2, S5_BLOCKS, gpb * S5_STATE, gpb * S5_GROUP)

    w_in = jnp.concatenate([in_mat(b_bar.real), in_mat(b_bar.imag)], axis=-1).astype(BF16)
    w_out = jnp.concatenate([out_mat(c_re.astype(F32)), -out_mat(c_im.astype(F32))], axis=2).astype(BF16)

    def tiles(v):
        t = v.reshape(2, gp // LANES, 1, LANES)
        t = jnp.broadcast_to(t, (2, gp // LANES, bsz, LANES))
        return jnp.concatenate([t[0], t[1]], axis=1)

    lam_flat = lam_bar.reshape(2, gp)
    return w_in, tiles(lam_flat.real), tiles(lam_flat.imag), w_out


CTX_TOKEN_TILE = 256
MIXER_TOKEN_TILE = 512
MERGE_TOKEN_TILE = 512
MOE_TOKEN_TILE = 1024


def kernel(x, c, ctx, c_ctx, ada_w, ada_b, norm_mix_g, norm_ffn_g, w_in, sgu_norm_g, sgu_w, sgu_b, s5_a_re, s5_a_im, s5_log_dt, s5_b_re, s5_b_im, s5_c_re, s5_c_im, s5_d, s5_glu_w, na_rpb, w_br_a, w_br_b, w_br_c, w_out, router_group_w, router_group_b, router_expert_w, router_expert_b, exp_w_gate, exp_w_up, exp_w_down, final_norm_g):
    bsz, n_tok, d = x.shape
    n_ctx = ctx.shape[1]
    depth = ada_w.shape[0]
    assert bsz + 1 <= SUBLANES

    cc = jnp.concatenate([c, c_ctx[None], jnp.zeros((SUBLANES - bsz - 1, d), F32)], axis=0)
    mod_all = ada_modulation(cc, ada_w, ada_b)
    rope_tabs = rope_tables(n_tok)
    s_zero = jnp.zeros((2 * bsz, S5_LANES), F32)
    tm_c = min(CTX_TOKEN_TILE, n_ctx)
    tm_moe = min(MOE_TOKEN_TILE, n_tok)

    xc = ctx
    for l in range(depth):
        with_ctx_out = l < depth - 1
        mod = mod_all[l, :bsz].reshape(bsz, ADA_CHUNKS, d)
        mod_c = jnp.broadcast_to(mod_all[l, bsz].reshape(1, ADA_CHUNKS, d), (bsz, ADA_CHUNKS, d))
        w_in_l = w_in[l].astype(BF16)
        sgu_w_l = sgu_w[l].astype(BF16)
        sgu_bias = jnp.broadcast_to(sgu_b[l].astype(F32)[:, :, None], (SGU_GROUPS, SGU_CHUNK, SGU_CHUNK))
        s5_w_in, s5_lre, s5_lim, s5_w_out = s5_params(
            s5_a_re[l], s5_a_im[l], s5_log_dt[l], s5_b_re[l], s5_b_im[l], s5_c_re[l], s5_c_im[l], bsz)
        table = na_bias_table(na_rpb[l])
        r_w, r_b = router_params(router_group_w[l], router_group_b[l], router_expert_w[l], router_expert_b[l])
        wg, wu, wd = moe_params(exp_w_gate[l], exp_w_up[l], exp_w_down[l])
        merge_w = (s5_d[l].astype(F32), s5_glu_w[l].astype(BF16), w_br_a[l].astype(BF16),
                   w_br_b[l].astype(BF16), w_br_c[l].astype(BF16), w_out[l].astype(BF16),
                   norm_ffn_g[l].astype(F32), r_w, r_b)

        ya_c, xb_c, q_c, k_c, v_c, gate_c = mixer_in(
            xc, mod_c, norm_mix_g[l], w_in_l, sgu_norm_g[l], sgu_w_l, sgu_bias, None, tm_c)
        ya_l, xb_l, q_l, k_l, v_l, gate_l = mixer_in(
            x, mod, norm_mix_g[l], w_in_l, sgu_norm_g[l], sgu_w_l, sgu_bias, rope_tabs, MIXER_TOKEN_TILE)
        ycf, ycr, s_ctx = s5_scan(xb_c, s_zero, s5_w_in, s5_lre, s5_lim, s5_w_out)
        ylf, ylr, _ = s5_scan(xb_l, s_ctx, s5_w_in, s5_lre, s5_lim, s5_w_out)
        yc_l = neighbourhood_attention(q_l, k_l, v_l, k_c, v_c, table)
        xn, h2, comb = merge_and_route(x, ya_l, xb_l, ylf, ylr, yc_l, gate_l, mod, *merge_w, MERGE_TOKEN_TILE)
        x = moe_grouped(xn, h2, comb, mod, wg, wu, wd, final_norm_g, tm_moe, not with_ctx_out)
        if with_ctx_out:
            yc_c = context_attention(q_c, k_c, v_c)
            xcn, hc2, comb_c = merge_and_route(xc, ya_c, xb_c, ycf, ycr, yc_c, gate_c, mod_c, *merge_w, tm_c)

            def flat(t):
                return t.reshape(1, bsz * n_ctx, t.shape[-1])

            xc = moe_grouped(flat(xcn), flat(hc2), flat(comb_c), mod_c[:1], wg, wu, wd, final_norm_g,
                             min(MOE_TOKEN_TILE, bsz * n_ctx), False).reshape(bsz, n_ctx, d)
    return x
```

```python
import functools
import math

import jax
import jax.numpy as jnp
import numpy as np
from jax import lax
from jax.experimental import pallas as pl
from jax.experimental.pallas import tpu as pltpu

F32 = jnp.float32
BF16 = jnp.bfloat16

GRID_W = 64
N_BRANCH = 3
SGU_WIDTH = 512
SGU_GROUPS = 4
SGU_CHUNK = 128
S5_WIDTH = 384
S5_GROUP = 16
S5_GROUPS = S5_WIDTH // S5_GROUP
S5_STATE = 64
NA_HEADS = 8
NA_HEAD_DIM = 64
NA_WIDTH = NA_HEADS * NA_HEAD_DIM
NA_KR = 8
NA_KC = 16
ROPE_BASE = 10000.0
N_GROUPS = 4
EXPERTS_PER_GROUP = 8
N_EXPERTS = N_GROUPS * EXPERTS_PER_GROUP
D_EXPERT = 256
ADA_CHUNKS = 6
EPS = 1e-6
NEG_INF = -1e30

LANES = 128
SUBLANES = 8
VMEM_LIMIT_BYTES = 56 * 1024 * 1024

S5_LANES = 2 * S5_GROUPS * S5_STATE
S5_SLABS = S5_LANES // LANES
S5_BLOCKS = S5_WIDTH // LANES
S5_CHUNK = 128
S5_PITCH = S5_CHUNK + 4


def _cparams(*sem):
    return pltpu.CompilerParams(dimension_semantics=sem, vmem_limit_bytes=VMEM_LIMIT_BYTES)


def _full(shape):
    n = len(shape)
    return pl.BlockSpec(shape, lambda *_: (0,) * n, pipeline_mode=pl.Buffered(1))


def _ada_kernel(c_ref, w_ref, b_ref, o_ref):
    c = c_ref[...]
    s = c * jax.nn.sigmoid(c)
    o_ref[...] = jnp.dot(s, w_ref[...], preferred_element_type=F32) + b_ref[...]


def ada_modulation(cc, ada_w, ada_b):
    n_layers, d, n = ada_w.shape
    tn = n // 4
    return pl.pallas_call(
        _ada_kernel,
        grid=(n_layers, n // tn),
        in_specs=[
            pl.BlockSpec((SUBLANES, d), lambda l, j: (0, 0)),
            pl.BlockSpec((None, d, tn), lambda l, j: (l, 0, j)),
            pl.BlockSpec((None, 1, tn), lambda l, j: (l, 0, j)),
        ],
        out_specs=pl.BlockSpec((None, SUBLANES, tn), lambda l, j: (l, 0, j)),
        out_shape=jax.ShapeDtypeStruct((n_layers, SUBLANES, n), F32),
        compiler_params=_cparams("parallel", "parallel"),
        name="ada_modulation",
    )(cc, ada_w, ada_b.reshape(n_layers, 1, n))


def _gelu(x):
    return jax.nn.gelu(x)


def _sigmoid(x):
    return 0.5 * jnp.tanh(0.5 * x) + 0.5


def _mixer_in_kernel(x_ref, mod_ref, g_ref, w_ref, lng_ref, sw_ref, sb_ref, *rest, rope):
    if rope:
        cos_ref, sin_ref, ya_ref, b_ref, q_ref, k_ref, v_ref, gate_ref = rest
    else:
        ya_ref, b_ref, q_ref, k_ref, v_ref, gate_ref = rest
    tm = x_ref.shape[0]
    xf = x_ref[...]
    ms = jnp.mean(xf * xf, axis=-1, keepdims=True)
    y = xf * lax.rsqrt(ms + EPS) * g_ref[...]
    h = y * (1.0 + mod_ref[1:2, :]) + mod_ref[0:1, :]
    hb = h.astype(BF16)

    def proj(lo, hi):
        return jnp.dot(hb, w_ref[:, lo:hi], preferred_element_type=F32)

    o1 = 2 * SGU_WIDTH
    o2 = o1 + S5_WIDTH
    oq, ok, ov = o2, o2 + NA_WIDTH, o2 + 2 * NA_WIDTH
    o3 = o2 + 3 * NA_WIDTH

    u = _gelu(proj(0, SGU_WIDTH))
    v = _gelu(proj(SGU_WIDTH, o1))
    vc = v - jnp.mean(v, axis=-1, keepdims=True)
    vn = vc * lax.rsqrt(jnp.mean(vc * vc, axis=-1, keepdims=True) + EPS) * lng_ref[...]
    vb = vn.astype(BF16)
    cw = SGU_WIDTH // SGU_GROUPS
    for c in range(tm // SGU_CHUNK):
        r0 = c * SGU_CHUNK
        for g in range(SGU_GROUPS):
            sp = jnp.dot(sw_ref[g], vb[r0:r0 + SGU_CHUNK, g * cw:(g + 1) * cw],
                         preferred_element_type=F32) + sb_ref[g]
            ya_ref[r0:r0 + SGU_CHUNK, g * cw:(g + 1) * cw] = (
                u[r0:r0 + SGU_CHUNK, g * cw:(g + 1) * cw] * sp).astype(ya_ref.dtype)

    b_ref[...] = proj(o1, o2).astype(b_ref.dtype)

    q = proj(oq, ok)
    k = proj(ok, ov)
    if rope:
        cos = cos_ref[...]
        sin = sin_ref[...]

        seg_half = NA_HEAD_DIM // 4
        first_half = (lax.broadcasted_iota(jnp.int32, cos.shape, 1) % (2 * seg_half)) < seg_half

        def rotate(t):
            out = []
            for j in range(NA_WIDTH // LANES):
                tj = t[:, j * LANES:(j + 1) * LANES]
                swapped = jnp.where(first_half, pltpu.roll(tj, LANES - seg_half, 1), pltpu.roll(tj, seg_half, 1))
                out.append(tj * cos + swapped * sin)
            return jnp.concatenate(out, axis=-1)

        q = rotate(q)
        k = rotate(k)
    q_scale = NA_HEAD_DIM ** -0.5 * (LOG2E if rope else 1.0)
    q_ref[...] = (q * q_scale).astype(q_ref.dtype)
    k_ref[...] = k.astype(k_ref.dtype)
    v_ref[...] = proj(ov, o3).astype(v_ref.dtype)
    gate_ref[...] = _sigmoid(proj(o3, o3 + N_BRANCH * x_ref.shape[1])).astype(gate_ref.dtype)


def mixer_in(x, mod, norm_g, w_in, sgu_norm_g, sgu_w, sgu_bias, rope_tabs, tm):
    bsz, n_tok, d = x.shape
    rope = rope_tabs is not None
    in_specs = [
        pl.BlockSpec((None, tm, d), lambda b, i: (b, i, 0)),
        pl.BlockSpec((None, ADA_CHUNKS, d), lambda b, i: (b, 0, 0)),
        _full((1, d)),
        _full(w_in.shape),
        _full((1, SGU_WIDTH)),
        _full(sgu_w.shape),
        _full(sgu_bias.shape),
    ]
    args = [x, mod, norm_g.reshape(1, d), w_in, sgu_norm_g.reshape(1, SGU_WIDTH), sgu_w, sgu_bias]
    if rope:
        cos_t, sin_t = rope_tabs
        in_specs += [
            pl.BlockSpec((tm, LANES), lambda b, i: (i, 0)),
            pl.BlockSpec((tm, LANES), lambda b, i: (i, 0)),
        ]
        args += [cos_t, sin_t]

    def tok(width):
        return pl.BlockSpec((None, tm, width), lambda b, i: (b, i, 0))

    out_shapes = [
        jax.ShapeDtypeStruct((bsz, n_tok, SGU_WIDTH), BF16),
        jax.ShapeDtypeStruct((bsz, n_tok, S5_WIDTH), BF16),
        jax.ShapeDtypeStruct((bsz, n_tok, NA_WIDTH), BF16),
        jax.ShapeDtypeStruct((bsz, n_tok, NA_WIDTH), BF16),
        jax.ShapeDtypeStruct((bsz, n_tok, NA_WIDTH), BF16),
        jax.ShapeDtypeStruct((bsz, n_tok, N_BRANCH * d), BF16),
    ]
    out_specs = [tok(SGU_WIDTH), tok(S5_WIDTH), tok(NA_WIDTH), tok(NA_WIDTH), tok(NA_WIDTH), tok(N_BRANCH * d)]
    return pl.pallas_call(
        functools.partial(_mixer_in_kernel, rope=rope),
        grid=(bsz, n_tok // tm),
        in_specs=in_specs,
        out_specs=out_specs,
        out_shape=out_shapes,
        compiler_params=_cparams("parallel", "parallel"),
        name="mixer_in_rope" if rope else "mixer_in",
    )(*args)


def rope_tables(n_tok):
    pos = np.arange(n_tok)
    rows = (pos // GRID_W).astype(np.float32)
    cols = (pos % GRID_W).astype(np.float32)
    seg = NA_HEAD_DIM // 2
    half = seg // 2
    inv_freq = (ROPE_BASE ** (-np.arange(half, dtype=np.float32) / half)).astype(np.float32)
    ang_r = rows[:, None] * inv_freq
    ang_c = cols[:, None] * inv_freq
    cos = np.concatenate([np.cos(ang_r)] * 2 + [np.cos(ang_c)] * 2, axis=-1)
    sin = np.concatenate([-np.sin(ang_r), np.sin(ang_r), -np.sin(ang_c), np.sin(ang_c)], axis=-1)
    reps = LANES // NA_HEAD_DIM
    return jnp.asarray(np.tile(cos, (1, reps)), F32), jnp.asarray(np.tile(sin, (1, reps)), F32)


def _s5_kernel(xf_ref, xfn_ref, xr_ref, xrn_ref, s0_ref, win_ref, lre_ref, lim_ref, wout_ref, rev_ref,
               yf_ref, yr_ref, send_ref, lhs_scr, st_a, st_b, state_scr):
    bsz = xf_ref.shape[0]
    tc = S5_CHUNK
    pitch = S5_PITCH
    rows = bsz * pitch
    half = S5_SLABS // 2
    bw = half // S5_BLOCKS
    j = pl.program_id(0)
    rev = rev_ref[...]
    first, second = slice(0, tc), slice(tc, 2 * tc)

    def project_in(f_ref, f_half, r_ref, r_half, st):
        for b in range(bsz):
            lhs_scr[0, b * pitch:b * pitch + tc, :] = f_ref[b, f_half, :].astype(F32)
            lhs_scr[1, b * pitch:b * pitch + tc, :] = jnp.dot(rev, r_ref[b, r_half, :], preferred_element_type=F32)
        for d in range(2):
            lhs = lhs_scr[d].astype(BF16)
            for cb in range(S5_BLOCKS):
                res = jnp.dot(lhs[:, cb * LANES:(cb + 1) * LANES], win_ref[d, cb], preferred_element_type=F32)
                for k in range(bw):
                    st[cb * bw + k, d * rows:(d + 1) * rows, :] = res[:, k * LANES:(k + 1) * LANES]
                    st[half + cb * bw + k, d * rows:(d + 1) * rows, :] = res[:, (bw + k) * LANES:(bw + k + 1) * LANES]

    def scan(st, state):
        cur = list(state)
        for t in range(tc):
            idx = pl.ds(t, 2 * bsz, stride=pitch)
            for c in range(half):
                sre, sim = cur[c], cur[c + half]
                lr = lre_ref[c]
                li = lim_ref[c]
                nre = lr * sre - li * sim + st[c, idx, :]
                nim = lr * sim + li * sre + st[c + half, idx, :]
                st[c, idx, :] = nre
                st[c + half, idx, :] = nim
                cur[c], cur[c + half] = nre, nim
        return cur

    def project_out(st, f_half, r_half):
        for d in range(2):
            y_blocks = []
            for cb in range(S5_BLOCKS):
                slabs = [cb * bw + k for k in range(bw)] + [half + cb * bw + k for k in range(bw)]
                s_blk = jnp.concatenate([st[c, d * rows:(d + 1) * rows, :] for c in slabs], axis=-1)
                y_blocks.append(jnp.dot(s_blk.astype(BF16), wout_ref[d, cb], preferred_element_type=F32))
            y = jnp.concatenate(y_blocks, axis=-1)
            for b in range(bsz):
                yb = y[b * pitch:b * pitch + tc, :]
                if d == 0:
                    yf_ref[b, f_half, :] = yb
                else:
                    yr_ref[b, r_half, :] = jnp.dot(rev, yb.astype(BF16), preferred_element_type=F32)

    @pl.when(j == 0)
    def _():
        state_scr[...] = s0_ref[...]
        lhs_scr[...] = jnp.zeros_like(lhs_scr)
        project_in(xf_ref, first, xr_ref, second, st_a)

    project_in(xf_ref, second, xr_ref, first, st_b)
    state = scan(st_a, [state_scr[:, c * LANES:(c + 1) * LANES] for c in range(S5_SLABS)])
    project_out(st_a, first, second)
    project_in(xfn_ref, first, xrn_ref, second, st_a)
    state = scan(st_b, state)
    project_out(st_b, second, first)
    for c in range(S5_SLABS):
        state_scr[:, c * LANES:(c + 1) * LANES] = state[c]
    send_ref[...] = state_scr[...]


def s5_scan(xb, s0, w_in, lam_re, lam_im, w_out):
    bsz, n_tok, width = xb.shape
    tc = S5_CHUNK
    assert 2 * bsz == SUBLANES and n_tok % (2 * tc) == 0
    n = n_tok // (2 * tc)
    rev = jnp.asarray(np.eye(tc, dtype=np.float32)[::-1], BF16)
    blk = (bsz, 2 * tc, width)
    return pl.pallas_call(
        _s5_kernel,
        grid=(n,),
        in_specs=[
            pl.BlockSpec(blk, lambda j: (0, j, 0)),
            pl.BlockSpec(blk, lambda j: (0, jnp.minimum(j + 1, n - 1), 0)),
            pl.BlockSpec(blk, lambda j: (0, n - 1 - j, 0)),
            pl.BlockSpec(blk, lambda j: (0, jnp.maximum(n - 2 - j, 0), 0)),
            _full(s0.shape),
            _full(w_in.shape),
            _full(lam_re.shape),
            _full(lam_im.shape),
            _full(w_out.shape),
            _full(rev.shape),
        ],
        out_specs=[
            pl.BlockSpec(blk, lambda j: (0, j, 0)),
            pl.BlockSpec(blk, lambda j: (0, n - 1 - j, 0)),
            pl.BlockSpec(s0.shape, lambda j: (0, 0)),
        ],
        out_shape=[
            jax.ShapeDtypeStruct(xb.shape, F32),
            jax.ShapeDtypeStruct(xb.shape, F32),
            jax.ShapeDtypeStruct(s0.shape, F32),
        ],
        scratch_shapes=[
            pltpu.VMEM((2, bsz * S5_PITCH, width), F32),
            pltpu.VMEM((S5_SLABS, 2 * bsz * S5_PITCH, LANES), F32),
            pltpu.VMEM((S5_SLABS, 2 * bsz * S5_PITCH, LANES), F32),
            pltpu.VMEM(s0.shape, F32),
        ],
        compiler_params=_cparams("arbitrary"),
        name="s5_scan",
    )(xb, xb, xb, xb, s0, w_in, lam_re, lam_im, w_out, rev)


NA_ROWS_PER_STEP = 4
_NT = (((1,), (1,)), ((), ()))


def _na_head_pair(qp, kp, vp, kcp, vcp, bias_fn):
    lane_head = lax.broadcasted_iota(jnp.int32, qp.shape, 1) // NA_HEAD_DIM
    o_pair = None
    for hh in range(2):
        qm = jnp.where(lane_head == hh, qp, jnp.zeros_like(qp))
        s_ctx = lax.dot_general(qm, kcp, _NT, preferred_element_type=F32)
        m = jnp.max(s_ctx, axis=-1, keepdims=True)
        if kp is not None:
            s_win = lax.dot_general(qm, kp, _NT, preferred_element_type=F32) + bias_fn(hh)
            m = jnp.maximum(m, jnp.max(s_win, axis=-1, keepdims=True))
            e_win = jnp.exp(s_win - m)
        e_ctx = jnp.exp(s_ctx - m)
        den = jnp.sum(e_ctx, axis=-1, keepdims=True)
        o = jnp.dot(e_ctx.astype(BF16), vcp, preferred_element_type=F32)
        if kp is not None:
            den = den + jnp.sum(e_win, axis=-1, keepdims=True)
            o = o + jnp.dot(e_win.astype(BF16), vp, preferred_element_type=F32)
        o = o * (1.0 / den)
        o_pair = o if hh == 0 else jnp.where(lane_head == 0, o_pair, o)
    return o_pair


NA_UNION_ROWS = 12
NA_SLAB = 32
LOG2E = math.log2(math.e)
NA_TAB_LEFT_OUT = 2 * NA_KR
NA_TAB_RIGHT_OUT = NA_TAB_LEFT_OUT + 2 * NA_KR - 1
NA_TAB_BOTH_OUT = NA_TAB_RIGHT_OUT + 2 * NA_KR - 1


def _na_union_start(i, n_rows):
    return jnp.clip(i * NA_ROWS_PER_STEP - NA_KR // 2, 0, n_rows - NA_UNION_ROWS)


def _na_kernel(q_ref, k_ref, v_ref, kc_ref, vc_ref, tab_ref, o_ref, s_scr, p_scr, rden_scr, *, n_rows):
    i = pl.program_id(1)
    r0 = i * NA_ROWS_PER_STEP
    ks = _na_union_start(i, n_rows)
    n_keys = NA_UNION_ROWS * GRID_W
    n_q = NA_ROWS_PER_STEP * GRID_W
    n_pairs = NA_UNION_ROWS // 2
    slabs_per_row = GRID_W // NA_SLAB

    def table_entry(rr, jj):
        r = r0 + rr
        kst = jnp.clip(r - NA_KR // 2, 0, n_rows - NA_KR)
        key0 = ks + 2 * jj
        out0 = jnp.logical_or(key0 < kst, key0 >= kst + NA_KR)
        out1 = jnp.logical_or(key0 + 1 < kst, key0 + 1 >= kst + NA_KR)
        e = key0 - r + NA_KR
        both_in = jnp.clip(e, 0, 2 * NA_KR - 1)
        left_out = NA_TAB_LEFT_OUT + jnp.clip(e, 0, 2 * NA_KR - 2)
        right_out = NA_TAB_RIGHT_OUT + jnp.clip(e - 1, 0, 2 * NA_KR - 2)
        return jnp.where(out0, jnp.where(out1, NA_TAB_BOTH_OUT, left_out), jnp.where(out1, right_out, both_in))

    entries = [[table_entry(rr, jj) for jj in range(n_pairs)] for rr in range(NA_ROWS_PER_STEP)]

    lane_head = lax.broadcasted_iota(jnp.int32, (n_q, LANES), 1) // NA_HEAD_DIM

    def scores(p):
        ls = slice(p * LANES, (p + 1) * LANES)
        qp = q_ref[:, ls]
        zero = jnp.zeros_like(qp)
        q_stack = jnp.concatenate([jnp.where(lane_head == 0, qp, zero), jnp.where(lane_head == 1, qp, zero)], axis=0)
        k_all = jnp.concatenate([k_ref[0, :, ls], kc_ref[:, ls]], axis=0)
        s_scr[p] = lax.dot_general(q_stack, k_all, _NT, preferred_element_type=F32)

    scores(0)
    for p in range(NA_HEADS // 2):
        if p + 1 < NA_HEADS // 2:
            scores(p + 1)
        ls = slice(p * LANES, (p + 1) * LANES)
        v_all = jnp.concatenate([v_ref[0, :, ls], vc_ref[:, ls]], axis=0)
        s_p, p_p, rden_p = s_scr.at[p], p_scr.at[p], rden_scr.at[p]
        for sl in range(2 * n_q // NA_SLAB):
            hh = sl // (NA_ROWS_PER_STEP * slabs_per_row)
            rr = (sl // slabs_per_row) % NA_ROWS_PER_STEP
            q0 = (sl % slabs_per_row) * NA_SLAB
            rows = slice(sl * NA_SLAB, (sl + 1) * NA_SLAB)
            bias = jnp.concatenate(
                [tab_ref[2 * p + hh, entries[rr][jj], q0:q0 + NA_SLAB, :] for jj in range(n_pairs)], axis=-1)
            s_win = s_p[rows, :n_keys] + bias
            s_ctx = s_p[rows, n_keys:]
            m = jnp.maximum(jnp.max(s_win, axis=-1, keepdims=True), jnp.max(s_ctx, axis=-1, keepdims=True))
            e_win = jnp.exp2(s_win - m)
            e_ctx = jnp.exp2(s_ctx - m)
            den = jnp.sum(e_win, axis=-1, keepdims=True) + jnp.sum(e_ctx, axis=-1, keepdims=True)
            p_p[rows, :n_keys] = e_win.astype(BF16)
            p_p[rows, n_keys:] = e_ctx.astype(BF16)
            rden_p[rows, :] = jnp.broadcast_to(1.0 / den, (NA_SLAB, LANES))
        o = jnp.dot(p_p[...], v_all, preferred_element_type=F32) * rden_p[...]
        o_ref[:, ls] = jnp.where(lane_head == 0, o[:n_q], o[n_q:]).astype(o_ref.dtype)


def na_bias_table(rpb):
    w = np.arange(GRID_W)
    col_start = np.clip(w - NA_KC // 2, 0, GRID_W - NA_KC)
    col_mask = (w[None, :] >= col_start[:, None]) & (w[None, :] < col_start[:, None] + NA_KC)
    d_col = np.clip(w[None, :] - w[:, None], -(NA_KC - 1), NA_KC - 1) + (NA_KC - 1)
    pick = (d_col[None] == np.arange(2 * NA_KC - 1)[:, None, None]).astype(np.float32)
    full = jnp.einsum('hdj,jwu->hdwu', rpb.astype(F32), jnp.asarray(pick), precision=lax.Precision.HIGHEST)
    full = jnp.where(col_mask[None, None], full * LOG2E, NEG_INF)
    pad = jnp.zeros_like(full[:, :1])
    neg = jnp.full_like(full, NEG_INF)
    both_in = jnp.concatenate([jnp.concatenate([pad, full], axis=1), jnp.concatenate([full, pad], axis=1)], axis=-1)
    left_out = jnp.concatenate([neg, full], axis=-1)
    right_out = jnp.concatenate([full, neg], axis=-1)
    both_out = jnp.concatenate([neg[:, :1], neg[:, :1]], axis=-1)
    return jnp.concatenate([both_in, left_out, right_out, both_out], axis=1)


def neighbourhood_attention(q, k, v, kc, vc, table):
    bsz, n_tok, width = q.shape
    n_ctx = kc.shape[1]
    n_rows = n_tok // GRID_W
    assert n_rows >= NA_UNION_ROWS and n_rows % NA_ROWS_PER_STEP == 0
    tq = NA_ROWS_PER_STEP * GRID_W
    n_keys = NA_UNION_ROWS * GRID_W
    window = pl.BlockSpec((pl.Element(1), pl.Element(n_keys), pl.Element(width)),
                          lambda b, i: (b, _na_union_start(i, n_rows) * GRID_W, 0))
    return pl.pallas_call(
        functools.partial(_na_kernel, n_rows=n_rows),
        grid=(bsz, n_rows // NA_ROWS_PER_STEP),
        in_specs=[
            pl.BlockSpec((None, tq, width), lambda b, i: (b, i, 0)),
            window,
            window,
            pl.BlockSpec((None, n_ctx, width), lambda b, i: (b, 0, 0)),
            pl.BlockSpec((None, n_ctx, width), lambda b, i: (b, 0, 0)),
            _full(table.shape),
        ],
        out_specs=pl.BlockSpec((None, tq, width), lambda b, i: (b, i, 0)),
        out_shape=jax.ShapeDtypeStruct(q.shape, BF16),
        scratch_shapes=[
            pltpu.VMEM((NA_HEADS // 2, 2 * tq, n_keys + n_ctx), F32),
            pltpu.VMEM((NA_HEADS // 2, 2 * tq, n_keys + n_ctx), BF16),
            pltpu.VMEM((NA_HEADS // 2, 2 * tq, LANES), F32),
        ],
        compiler_params=_cparams("parallel", "arbitrary"),
        name="neighbourhood_attention",
    )(q, k, v, kc, vc, table)


def _ctx_attn_kernel(q_ref, k_ref, v_ref, o_ref):
    for p in range(NA_HEADS // 2):
        ls = slice(p * LANES, (p + 1) * LANES)
        o_pair = _na_head_pair(q_ref[:, ls], None, None, k_ref[:, ls], v_ref[:, ls], None)
        o_ref[:, ls] = o_pair.astype(o_ref.dtype)


def context_attention(qc, kc, vc):
    bsz, n_ctx, width = qc.shape
    spec = pl.BlockSpec((None, n_ctx, width), lambda b: (b, 0, 0))
    return pl.pallas_call(
        _ctx_attn_kernel,
        grid=(bsz,),
        in_specs=[spec, spec, spec],
        out_specs=spec,
        out_shape=jax.ShapeDtypeStruct(qc.shape, BF16),
        compiler_params=_cparams("parallel"),
        name="context_attention",
    )(qc, kc, vc)


ROUTER_LANES = LANES
EXPERT_LANE0 = N_GROUPS
GROUP_ID_LANE = 0
RANK_LANE = 1


def _route(logits):
    lane = lax.broadcasted_iota(jnp.int32, logits.shape, 1)
    big = jnp.int32(ROUTER_LANES)
    is_g = lane < N_GROUPS
    lg = jnp.where(is_g, logits, -jnp.inf)
    mg = jnp.max(lg, axis=-1, keepdims=True)
    grp = jnp.min(jnp.where(lg == mg, lane, big), axis=-1, keepdims=True)
    g_weight = 1.0 / jnp.sum(jnp.where(is_g, jnp.exp(logits - mg), 0.0), axis=-1, keepdims=True)
    e_idx = lane - EXPERT_LANE0
    sel = (e_idx >= 0) & (e_idx < N_EXPERTS) & ((e_idx // EXPERTS_PER_GROUP) == grp)
    ls1 = jnp.where(sel, logits, -jnp.inf)
    v1 = jnp.max(ls1, axis=-1, keepdims=True)
    i1 = jnp.min(jnp.where(ls1 == v1, lane, big), axis=-1, keepdims=True)
    ls2 = jnp.where(lane == i1, -jnp.inf, ls1)
    v2 = jnp.max(ls2, axis=-1, keepdims=True)
    i2 = jnp.min(jnp.where(ls2 == v2, lane, big), axis=-1, keepdims=True)
    e2 = jnp.exp(v2 - v1)
    w1 = 1.0 / (1.0 + e2)
    w2 = e2 * w1
    comb = g_weight * (jnp.where(lane == i1, w1, 0.0) + jnp.where(lane == i2, w2, 0.0))
    return jnp.where(lane == GROUP_ID_LANE, grp.astype(F32), comb)


def _merge_kernel(x_ref, ya_ref, xb_ref, yf_ref, yr_ref, yc_ref, gate_ref, mod_ref, d_ref, glu_ref,
                  wa_ref, wb_ref, wc_ref, wo_ref, g_ref, rw_ref, rb_ref, own_ref, xn_ref, h_ref, comb_ref):
    d = x_ref.shape[1]
    yb = d_ref[...] * xb_ref[...].astype(F32) + yf_ref[...] + yr_ref[...]
    yb = _gelu(yb)
    yb = yb * _sigmoid(jnp.dot(yb.astype(BF16), glu_ref[...], preferred_element_type=F32))

    def gate(j):
        return gate_ref[:, j * d:(j + 1) * d].astype(F32)

    m = gate(0) * jnp.dot(ya_ref[...], wa_ref[...], preferred_element_type=F32)
    m = m + gate(1) * jnp.dot(yb.astype(BF16), wb_ref[...], preferred_element_type=F32)
    m = m + gate(2) * jnp.dot(yc_ref[...], wc_ref[...], preferred_element_type=F32)
    xn = x_ref[...] + mod_ref[2:3, :] * jnp.dot(m.astype(BF16), wo_ref[...], preferred_element_type=F32)
    xn_ref[...] = xn
    ms = jnp.mean(xn * xn, axis=-1, keepdims=True)
    h = xn * lax.rsqrt(ms + EPS) * g_ref[...]
    h = h * (1.0 + mod_ref[4:5, :]) + mod_ref[3:4, :]
    h_ref[:, :d] = h.astype(h_ref.dtype)
    logits = jnp.dot(h, rw_ref[...], preferred_element_type=F32) + rb_ref[...]
    comb = _route(logits)
    comb_ref[...] = comb
    own = jnp.dot(comb, own_ref[...], preferred_element_type=F32)
    own_hi, own_lo = _split_bf16(own)
    lane = lax.broadcasted_iota(jnp.int32, own.shape, 1)
    h_ref[:, d:] = jnp.where(lane < EXPERTS_PER_GROUP, own_hi, own_lo)


def merge_and_route(x, ya, xb, yf, yr, yc, gates, mod, s5_d, glu_w, w_br_a, w_br_b, w_br_c, w_out,
                    norm_ffn_g, router_w, router_b, tm):
    bsz, n_tok, d = x.shape

    def tok(width):
        return pl.BlockSpec((None, tm, width), lambda b, i: (b, i, 0))

    own = np.zeros((ROUTER_LANES, ROUTER_LANES), np.float32)
    for e in range(N_EXPERTS):
        own[EXPERT_LANE0 + e, e % EXPERTS_PER_GROUP] = 1.0
        own[EXPERT_LANE0 + e, EXPERTS_PER_GROUP + e % EXPERTS_PER_GROUP] = 1.0
    weights = [s5_d.reshape(1, S5_WIDTH), glu_w, w_br_a, w_br_b, w_br_c, w_out,
               norm_ffn_g.reshape(1, d), router_w, router_b, jnp.asarray(own)]
    return pl.pallas_call(
        _merge_kernel,
        grid=(bsz, n_tok // tm),
        in_specs=[tok(d), tok(SGU_WIDTH), tok(S5_WIDTH), tok(S5_WIDTH), tok(S5_WIDTH), tok(NA_WIDTH),
                  tok(N_BRANCH * d), pl.BlockSpec((None, ADA_CHUNKS, d), lambda b, i: (b, 0, 0))]
        + [_full(w.shape) for w in weights],
        out_specs=[tok(d), tok(d + ROUTER_LANES), tok(ROUTER_LANES)],
        out_shape=[
            jax.ShapeDtypeStruct(x.shape, F32),
            jax.ShapeDtypeStruct((bsz, n_tok, d + ROUTER_LANES), BF16),
            jax.ShapeDtypeStruct((bsz, n_tok, ROUTER_LANES), F32),
        ],
        compiler_params=_cparams("parallel", "parallel"),
        name="merge_and_route",
    )(x, ya, xb, yf, yr, yc, gates, mod, *weights)


def router_params(rg_w, rg_b, re_w, re_b):
    d = rg_w.shape[0]
    pad = ROUTER_LANES - N_GROUPS - N_EXPERTS
    w = jnp.concatenate([rg_w, re_w, jnp.zeros((d, pad), F32)], axis=1).astype(F32)
    b = jnp.concatenate([rg_b, re_b, jnp.zeros((pad,), F32)]).astype(F32).reshape(1, ROUTER_LANES)
    return w, b


MOE_BLOCK = 144


def _split_bf16(x):
    hi = x.astype(BF16)
    return hi, (x - hi.astype(F32)).astype(BF16)


MOE_SUBTILE = 512


def _moe_kernel(xn_ref, hx_ref, comb_ref, mod_ref, wg_ref, wu_ref, wd_ref, fg_ref, o_ref,
                aux_col, aux_row, hid_scr, cnt_ref, *, final_norm):
    g = pl.program_id(2)
    tm, d = o_ref.shape
    st = min(MOE_SUBTILE, tm)
    n_sub = tm // st

    @pl.when(g == 0)
    def _():
        o_ref[...] = xn_ref[...]
        row_i = lax.broadcasted_iota(jnp.int32, (st, st), 0)
        col_i = lax.broadcasted_iota(jnp.int32, (st, st), 1)
        tri = jnp.where(col_i < row_i, 1.0, 0.0).astype(BF16)
        for s in range(n_sub):
            comb = comb_ref[s * st:(s + 1) * st, :]
            lane = lax.broadcasted_iota(jnp.int32, comb.shape, 1)
            grp = comb[:, GROUP_ID_LANE:GROUP_ID_LANE + 1]
            onehot = jnp.where(lane < N_GROUPS, jnp.where(lane.astype(F32) == grp, 1.0, 0.0), 0.0)
            ranks = jnp.dot(tri, onehot.astype(BF16), preferred_element_type=F32)
            own = jnp.sum(onehot * ranks, axis=-1, keepdims=True)
            aux = jnp.where(lane == GROUP_ID_LANE, grp, jnp.where(lane == RANK_LANE, own, 0.0))
            aux_col[s * st:(s + 1) * st, :] = aux
            aux_row[:, s * st:(s + 1) * st] = aux.T
            for gg in range(N_GROUPS):
                cnt_ref[s * N_GROUPS + gg] = jnp.sum(onehot[:, gg:gg + 1]).astype(jnp.int32)

    gf = g.astype(F32)
    slot_r = lax.broadcasted_iota(jnp.int32, (MOE_BLOCK, st), 0).astype(F32)
    slot_c = lax.broadcasted_iota(jnp.int32, (st, MOE_BLOCK), 1).astype(F32)
    scale = mod_ref[5:6, :]

    sub_rows = [slice(s * st, (s + 1) * st) for s in range(n_sub)]
    rank_rows = [jnp.where(aux_row[GROUP_ID_LANE:GROUP_ID_LANE + 1, r] == gf,
                           aux_row[RANK_LANE:RANK_LANE + 1, r], -1.0) for r in sub_rows]
    rank_cols = [jnp.where(aux_col[r, GROUP_ID_LANE:GROUP_ID_LANE + 1] == gf,
                           aux_col[r, RANK_LANE:RANK_LANE + 1], -1.0) for r in sub_rows]
    n_rounds = cnt_ref[g]
    for s in range(1, n_sub):
        n_rounds = jnp.maximum(n_rounds, cnt_ref[s * N_GROUPS + g])
    n_rounds = (n_rounds + MOE_BLOCK - 1) // MOE_BLOCK

    def round_(j, carry):
        base = (j * MOE_BLOCK).astype(F32)
        hcx = jnp.concatenate(
            [jnp.dot(jnp.where(rank_rows[s] - base == slot_r, 1.0, 0.0).astype(BF16), hx_ref[sub_rows[s], :],
                     preferred_element_type=F32) for s in range(n_sub)], axis=0)
        hc = hcx[:, :d].astype(BF16)
        wt = hcx[:, d:]
        for e in range(EXPERTS_PER_GROUP):
            a = jnp.dot(hc, wg_ref[e], preferred_element_type=F32)
            u = jnp.dot(hc, wu_ref[e], preferred_element_type=F32)
            cw = wt[:, e:e + 1] + wt[:, EXPERTS_PER_GROUP + e:EXPERTS_PER_GROUP + e + 1]
            hid_scr[:, e * D_EXPERT:(e + 1) * D_EXPERT] = (a * jax.nn.sigmoid(a) * u * cw).astype(BF16)
        oc = jnp.dot(hid_scr[...], wd_ref[...], preferred_element_type=F32).astype(BF16)
        for s in range(n_sub):
            scatter = jnp.where(rank_cols[s] - base == slot_c, 1.0, 0.0).astype(BF16)
            o_ref[sub_rows[s], :] += scale * jnp.dot(
                scatter, oc[s * MOE_BLOCK:(s + 1) * MOE_BLOCK], preferred_element_type=F32)
        return carry

    lax.fori_loop(0, n_rounds, round_, 0)

    if final_norm:
        @pl.when(g == pl.num_programs(2) - 1)
        def _():
            xo = o_ref[...]
            ms = jnp.mean(xo * xo, axis=-1, keepdims=True)
            o_ref[...] = xo * lax.rsqrt(ms + EPS) * fg_ref[...]


def moe_grouped(xn, hx, comb, mod, wg, wu, wd, layer, final_g, tm, final_norm):
    bsz, n_tok, d = xn.shape
    gw = EXPERTS_PER_GROUP * D_EXPERT
    n_sub = tm // min(MOE_SUBTILE, tm)

    def tok(width):
        return pl.BlockSpec((None, tm, width), lambda b, i, g: (b, i, 0))

    return pl.pallas_call(
        functools.partial(_moe_kernel, final_norm=final_norm),
        grid=(bsz, n_tok // tm, N_GROUPS),
        in_specs=[
            tok(d), tok(d + ROUTER_LANES), tok(ROUTER_LANES),
            pl.BlockSpec((None, ADA_CHUNKS, d), lambda b, i, g: (b, 0, 0)),
            pl.BlockSpec((None, None, EXPERTS_PER_GROUP, d, D_EXPERT), lambda b, i, g: (layer, g, 0, 0, 0)),
            pl.BlockSpec((None, None, EXPERTS_PER_GROUP, d, D_EXPERT), lambda b, i, g: (layer, g, 0, 0, 0)),
            pl.BlockSpec((None, None, gw, d), lambda b, i, g: (layer, g, 0, 0)),
            pl.BlockSpec((1, d), lambda b, i, g: (0, 0)),
        ],
        out_specs=tok(d),
        out_shape=jax.ShapeDtypeStruct(xn.shape, F32),
        scratch_shapes=[
            pltpu.VMEM((tm, ROUTER_LANES), F32),
            pltpu.VMEM((ROUTER_LANES, tm), F32),
            pltpu.VMEM((n_sub * MOE_BLOCK, gw), BF16),
            pltpu.SMEM((n_sub * N_GROUPS,), jnp.int32),
        ],
        compiler_params=_cparams("parallel", "parallel", "arbitrary"),
        name="moe_grouped",
    )(xn, hx, comb, mod, wg, wu, wd, final_g.reshape(1, d))


EXPERTS_PER_CAST_STEP = 4


def _cast_kernel(a_ref, b_ref, c_ref, oa_ref, ob_ref, oc_ref):
    oa_ref[...] = a_ref[...].astype(oa_ref.dtype)
    ob_ref[...] = b_ref[...].astype(ob_ref.dtype)
    oc_ref[...] = c_ref[...].astype(oc_ref.dtype)


def moe_params(e_gate, e_up, e_down):
    n_layers, n_exp, d, f = e_gate.shape
    flat = [w.reshape(n_layers * n_exp, *w.shape[2:]) for w in (e_gate, e_up, e_down)]

    def spec(w):
        return pl.BlockSpec((EXPERTS_PER_CAST_STEP,) + w.shape[1:], lambda i: (i, 0, 0))

    wg, wu, wd = pl.pallas_call(
        _cast_kernel,
        grid=(n_layers * n_exp // EXPERTS_PER_CAST_STEP,),
        in_specs=[spec(w) for w in flat],
        out_specs=[spec(w) for w in flat],
        out_shape=[jax.ShapeDtypeStruct(w.shape, BF16) for w in flat],
        compiler_params=_cparams("parallel"),
        name="expert_weights_bf16",
    )(*flat)
    return (wg.reshape(n_layers, N_GROUPS, EXPERTS_PER_GROUP, d, f),
            wu.reshape(n_layers, N_GROUPS, EXPERTS_PER_GROUP, d, f),
            wd.reshape(n_layers, N_GROUPS, EXPERTS_PER_GROUP * f, d))


def s5_params(a_re, a_im, log_dt, b_re, b_im, c_re, c_im, bsz):
    lam = lax.complex(a_re.astype(F32), a_im.astype(F32))
    dt = jnp.exp(log_dt.astype(F32))[..., None]
    lam_bar = jnp.exp(lam * dt)
    b_bar = ((lam_bar - 1) / lam)[..., None] * lax.complex(b_re.astype(F32), b_im.astype(F32))
    gpb = S5_GROUPS // S5_BLOCKS
    eye = jnp.eye(gpb, dtype=F32)
    gp = S5_GROUPS * S5_STATE

    def in_mat(m):
        m = m.reshape(2, S5_BLOCKS, gpb, S5_STATE, S5_GROUP)
        return jnp.einsum('dkgpc,gh->dkgchp', m, eye).reshape(2, S5_BLOCKS, gpb * S5_GROUP, gpb * S5_STATE)

    def out_mat(m):
        m = m.reshape(2, S5_BLOCKS, gpb, S5_GROUP, S5_STATE)
        return jnp.einsum('dkgcp,gh->dkgphc', m, eye).reshape(2, S5_BLOCKS, gpb * S5_STATE, gpb * S5_GROUP)

    w_in = jnp.concatenate([in_mat(b_bar.real), in_mat(b_bar.imag)], axis=-1).astype(BF16)
    w_out = jnp.concatenate([out_mat(c_re.astype(F32)), -out_mat(c_im.astype(F32))], axis=2).astype(BF16)

    def tiles(v):
        t = v.reshape(2, gp // LANES, 1, LANES)
        t = jnp.broadcast_to(t, (2, gp // LANES, bsz, LANES))
        return jnp.concatenate([t[0], t[1]], axis=1)

    lam_flat = lam_bar.reshape(2, gp)
    return w_in, tiles(lam_flat.real), tiles(lam_flat.imag), w_out


CTX_TOKEN_TILE = 256
MIXER_TOKEN_TILE = 512
MERGE_TOKEN_TILE = 512
MOE_TOKEN_TILE = 1024


def kernel(x, c, ctx, c_ctx, ada_w, ada_b, norm_mix_g, norm_ffn_g, w_in, sgu_norm_g, sgu_w, sgu_b, s5_a_re, s5_a_im, s5_log_dt, s5_b_re, s5_b_im, s5_c_re, s5_c_im, s5_d, s5_glu_w, na_rpb, w_br_a, w_br_b, w_br_c, w_out, router_group_w, router_group_b, router_expert_w, router_expert_b, exp_w_gate, exp_w_up, exp_w_down, final_norm_g):
    bsz, n_tok, d = x.shape
    n_ctx = ctx.shape[1]
    depth = ada_w.shape[0]
    assert bsz + 1 <= SUBLANES

    cc = jnp.concatenate([c, c_ctx[None], jnp.zeros((SUBLANES - bsz - 1, d), F32)], axis=0)
    mod_all = ada_modulation(cc, ada_w, ada_b)
    rope_tabs = rope_tables(n_tok)
    s_zero = jnp.zeros((2 * bsz, S5_LANES), F32)
    tm_c = min(CTX_TOKEN_TILE, n_ctx)
    tm_moe = min(MOE_TOKEN_TILE, n_tok)

    wg_all, wu_all, wd_all = moe_params(exp_w_gate, exp_w_up, exp_w_down)

    xc = ctx
    for l in range(depth):
        with_ctx_out = l < depth - 1
        mod = mod_all[l, :bsz].reshape(bsz, ADA_CHUNKS, d)
        mod_c = jnp.broadcast_to(mod_all[l, bsz].reshape(1, ADA_CHUNKS, d), (bsz, ADA_CHUNKS, d))
        w_in_l = w_in[l].astype(BF16)
        sgu_w_l = sgu_w[l].astype(BF16)
        sgu_bias = jnp.broadcast_to(sgu_b[l].astype(F32)[:, :, None], (SGU_GROUPS, SGU_CHUNK, SGU_CHUNK))
        s5_w_in, s5_lre, s5_lim, s5_w_out = s5_params(
            s5_a_re[l], s5_a_im[l], s5_log_dt[l], s5_b_re[l], s5_b_im[l], s5_c_re[l], s5_c_im[l], bsz)
        table = na_bias_table(na_rpb[l])
        r_w, r_b = router_params(router_group_w[l], router_group_b[l], router_expert_w[l], router_expert_b[l])
        merge_w = (s5_d[l].astype(F32), s5_glu_w[l].astype(BF16), w_br_a[l].astype(BF16),
                   w_br_b[l].astype(BF16), w_br_c[l].astype(BF16), w_out[l].astype(BF16),
                   norm_ffn_g[l].astype(F32), r_w, r_b)

        ya_c, xb_c, q_c, k_c, v_c, gate_c = mixer_in(
            xc, mod_c, norm_mix_g[l], w_in_l, sgu_norm_g[l], sgu_w_l, sgu_bias, None, tm_c)
        ya_l, xb_l, q_l, k_l, v_l, gate_l = mixer_in(
            x, mod, norm_mix_g[l], w_in_l, sgu_norm_g[l], sgu_w_l, sgu_bias, rope_tabs, MIXER_TOKEN_TILE)
        ycf, ycr, s_ctx = s5_scan(xb_c, s_zero, s5_w_in, s5_lre, s5_lim, s5_w_out)
        ylf, ylr, _ = s5_scan(xb_l, s_ctx, s5_w_in, s5_lre, s5_lim, s5_w_out)
        yc_l = neighbourhood_attention(q_l, k_l, v_l, k_c, v_c, table)
        xn, h2, comb = merge_and_route(x, ya_l, xb_l, ylf, ylr, yc_l, gate_l, mod, *merge_w, MERGE_TOKEN_TILE)
        x = moe_grouped(xn, h2, comb, mod, wg_all, wu_all, wd_all, l, final_norm_g, tm_moe, not with_ctx_out)
        if with_ctx_out:
            yc_c = context_attention(q_c, k_c, v_c)
            xcn, hc2, comb_c = merge_and_route(xc, ya_c, xb_c, ycf, ycr, yc_c, gate_c, mod_c, *merge_w, tm_c)

            def flat(t):
                return t.reshape(1, bsz * n_ctx, t.shape[-1])

            xc = moe_grouped(flat(xcn), flat(hc2), flat(comb_c), mod_c[:1], wg_all, wu_all, wd_all, l, final_norm_g,
                             min(MOE_TOKEN_TILE, bsz * n_ctx), False).reshape(bsz, n_ctx, d)
    return x
```

```python
import functools
import math

import jax
import jax.numpy as jnp
import numpy as np
from jax import lax
from jax.experimental import pallas as pl
from jax.experimental.pallas import tpu as pltpu

F32 = jnp.float32
BF16 = jnp.bfloat16

GRID_W = 64
N_BRANCH = 3
SGU_WIDTH = 512
SGU_GROUPS = 4
SGU_CHUNK = 128
S5_WIDTH = 384
S5_GROUP = 16
S5_GROUPS = S5_WIDTH // S5_GROUP
S5_STATE = 64
NA_HEADS = 8
NA_HEAD_DIM = 64
NA_WIDTH = NA_HEADS * NA_HEAD_DIM
NA_KR = 8
NA_KC = 16
ROPE_BASE = 10000.0
N_GROUPS = 4
EXPERTS_PER_GROUP = 8
N_EXPERTS = N_GROUPS * EXPERTS_PER_GROUP
D_EXPERT = 256
ADA_CHUNKS = 6
EPS = 1e-6
NEG_INF = -1e30

LANES = 128
SUBLANES = 8
VMEM_LIMIT_BYTES = 56 * 1024 * 1024

S5_LANES = 2 * S5_GROUPS * S5_STATE
S5_SLABS = S5_LANES // LANES
S5_BLOCKS = S5_WIDTH // LANES
S5_CHUNK = 128
S5_PITCH = S5_CHUNK + 4


def _cparams(*sem):
    return pltpu.CompilerParams(dimension_semantics=sem, vmem_limit_bytes=VMEM_LIMIT_BYTES)


def _full(shape):
    n = len(shape)
    return pl.BlockSpec(shape, lambda *_: (0,) * n, pipeline_mode=pl.Buffered(1))


def _ada_kernel(c_ref, w_ref, b_ref, o_ref):
    c = c_ref[...]
    s = c * jax.nn.sigmoid(c)
    o_ref[...] = jnp.dot(s, w_ref[...], preferred_element_type=F32) + b_ref[...]


def ada_modulation(cc, ada_w, ada_b):
    n_layers, d, n = ada_w.shape
    tn = n // 4
    return pl.pallas_call(
        _ada_kernel,
        grid=(n_layers, n // tn),
        in_specs=[
            pl.BlockSpec((SUBLANES, d), lambda l, j: (0, 0)),
            pl.BlockSpec((None, d, tn), lambda l, j: (l, 0, j)),
            pl.BlockSpec((None, 1, tn), lambda l, j: (l, 0, j)),
        ],
        out_specs=pl.BlockSpec((None, SUBLANES, tn), lambda l, j: (l, 0, j)),
        out_shape=jax.ShapeDtypeStruct((n_layers, SUBLANES, n), F32),
        compiler_params=_cparams("parallel", "parallel"),
        name="ada_modulation",
    )(cc, ada_w, ada_b.reshape(n_layers, 1, n))


def _gelu(x):
    return jax.nn.gelu(x)


def _sigmoid(x):
    return 0.5 * jnp.tanh(0.5 * x) + 0.5


def _mixer_in_kernel(x_ref, mod_ref, g_ref, w_ref, lng_ref, sw_ref, sb_ref, *rest, rope):
    if rope:
        cos_ref, sin_ref, ya_ref, b_ref, q_ref, k_ref, v_ref, gate_ref = rest
    else:
        ya_ref, b_ref, q_ref, k_ref, v_ref, gate_ref = rest
    tm = x_ref.shape[0]
    xf = x_ref[...]
    ms = jnp.mean(xf * xf, axis=-1, keepdims=True)
    y = xf * lax.rsqrt(ms + EPS) * g_ref[...]
    h = y * (1.0 + mod_ref[1:2, :]) + mod_ref[0:1, :]
    hb = h.astype(BF16)

    def proj(lo, hi):
        return jnp.dot(hb, w_ref[:, lo:hi], preferred_element_type=F32)

    o1 = 2 * SGU_WIDTH
    o2 = o1 + S5_WIDTH
    oq, ok, ov = o2, o2 + NA_WIDTH, o2 + 2 * NA_WIDTH
    o3 = o2 + 3 * NA_WIDTH

    d_model = x_ref.shape[1]

    def gate_chunk(j):
        cols = slice(j * d_model, (j + 1) * d_model)
        gate_ref[:, cols] = _sigmoid(proj(o3 + j * d_model, o3 + (j + 1) * d_model)).astype(gate_ref.dtype)

    u = _gelu(proj(0, SGU_WIDTH))
    v = _gelu(proj(SGU_WIDTH, o1))
    gate_chunk(0)
    vc = v - jnp.mean(v, axis=-1, keepdims=True)
    vn = vc * lax.rsqrt(jnp.mean(vc * vc, axis=-1, keepdims=True) + EPS) * lng_ref[...]
    vb = vn.astype(BF16)
    cw = SGU_WIDTH // SGU_GROUPS
    for c in range(tm // SGU_CHUNK):
        r0 = c * SGU_CHUNK
        for g in range(SGU_GROUPS):
            sp = jnp.dot(sw_ref[g], vb[r0:r0 + SGU_CHUNK, g * cw:(g + 1) * cw],
                         preferred_element_type=F32) + sb_ref[g]
            ya_ref[r0:r0 + SGU_CHUNK, g * cw:(g + 1) * cw] = (
                u[r0:r0 + SGU_CHUNK, g * cw:(g + 1) * cw] * sp).astype(ya_ref.dtype)

    b_ref[...] = proj(o1, o2).astype(b_ref.dtype)
    gate_chunk(1)

    q = proj(oq, ok)
    k = proj(ok, ov)
    if rope:
        cos = cos_ref[...]
        sin = sin_ref[...]

        seg_half = NA_HEAD_DIM // 4
        first_half = (lax.broadcasted_iota(jnp.int32, cos.shape, 1) % (2 * seg_half)) < seg_half

        def rotate(t):
            out = []
            for j in range(NA_WIDTH // LANES):
                tj = t[:, j * LANES:(j + 1) * LANES]
                swapped = jnp.where(first_half, pltpu.roll(tj, LANES - seg_half, 1), pltpu.roll(tj, seg_half, 1))
                out.append(tj * cos + swapped * sin)
            return jnp.concatenate(out, axis=-1)

        q = rotate(q)
        k = rotate(k)
    q_scale = NA_HEAD_DIM ** -0.5 * (LOG2E if rope else 1.0)
    q_ref[...] = (q * q_scale).astype(q_ref.dtype)
    k_ref[...] = k.astype(k_ref.dtype)
    v_ref[...] = proj(ov, o3).astype(v_ref.dtype)
    gate_chunk(2)


def mixer_in(x, mod, norm_g, w_in, sgu_norm_g, sgu_w, sgu_bias, rope_tabs, tm):
    bsz, n_tok, d = x.shape
    rope = rope_tabs is not None
    in_specs = [
        pl.BlockSpec((None, tm, d), lambda b, i: (b, i, 0)),
        pl.BlockSpec((None, ADA_CHUNKS, d), lambda b, i: (b, 0, 0)),
        _full((1, d)),
        _full(w_in.shape),
        _full((1, SGU_WIDTH)),
        _full(sgu_w.shape),
        _full(sgu_bias.shape),
    ]
    args = [x, mod, norm_g.reshape(1, d), w_in, sgu_norm_g.reshape(1, SGU_WIDTH), sgu_w, sgu_bias]
    if rope:
        cos_t, sin_t = rope_tabs
        in_specs += [
            pl.BlockSpec((tm, LANES), lambda b, i: (i, 0)),
            pl.BlockSpec((tm, LANES), lambda b, i: (i, 0)),
        ]
        args += [cos_t, sin_t]

    def tok(width):
        return pl.BlockSpec((None, tm, width), lambda b, i: (b, i, 0))

    out_shapes = [
        jax.ShapeDtypeStruct((bsz, n_tok, SGU_WIDTH), BF16),
        jax.ShapeDtypeStruct((bsz, n_tok, S5_WIDTH), BF16),
        jax.ShapeDtypeStruct((bsz, n_tok, NA_WIDTH), BF16),
        jax.ShapeDtypeStruct((bsz, n_tok, NA_WIDTH), BF16),
        jax.ShapeDtypeStruct((bsz, n_tok, NA_WIDTH), BF16),
        jax.ShapeDtypeStruct((bsz, n_tok, N_BRANCH * d), BF16),
    ]
    out_specs = [tok(SGU_WIDTH), tok(S5_WIDTH), tok(NA_WIDTH), tok(NA_WIDTH), tok(NA_WIDTH), tok(N_BRANCH * d)]
    return pl.pallas_call(
        functools.partial(_mixer_in_kernel, rope=rope),
        grid=(bsz, n_tok // tm),
        in_specs=in_specs,
        out_specs=out_specs,
        out_shape=out_shapes,
        compiler_params=_cparams("parallel", "parallel"),
        name="mixer_in_rope" if rope else "mixer_in",
    )(*args)


def rope_tables(n_tok):
    pos = np.arange(n_tok)
    rows = (pos // GRID_W).astype(np.float32)
    cols = (pos % GRID_W).astype(np.float32)
    seg = NA_HEAD_DIM // 2
    half = seg // 2
    inv_freq = (ROPE_BASE ** (-np.arange(half, dtype=np.float32) / half)).astype(np.float32)
    ang_r = rows[:, None] * inv_freq
    ang_c = cols[:, None] * inv_freq
    cos = np.concatenate([np.cos(ang_r)] * 2 + [np.cos(ang_c)] * 2, axis=-1)
    sin = np.concatenate([-np.sin(ang_r), np.sin(ang_r), -np.sin(ang_c), np.sin(ang_c)], axis=-1)
    reps = LANES // NA_HEAD_DIM
    return jnp.asarray(np.tile(cos, (1, reps)), F32), jnp.asarray(np.tile(sin, (1, reps)), F32)


def _s5_kernel(xf_ref, xfn_ref, xr_ref, xrn_ref, s0_ref, win_ref, lre_ref, lim_ref, wout_ref, rev_ref,
               yf_ref, yr_ref, send_ref, lhs_scr, st_a, st_b, state_scr):
    bsz = xf_ref.shape[0]
    tc = S5_CHUNK
    pitch = S5_PITCH
    rows = bsz * pitch
    half = S5_SLABS // 2
    bw = half // S5_BLOCKS
    j = pl.program_id(0)
    rev = rev_ref[...]
    first, second = slice(0, tc), slice(tc, 2 * tc)

    def project_in(f_ref, f_half, r_ref, r_half, st):
        for b in range(bsz):
            lhs_scr[0, b * pitch:b * pitch + tc, :] = f_ref[b, f_half, :].astype(F32)
            lhs_scr[1, b * pitch:b * pitch + tc, :] = jnp.dot(rev, r_ref[b, r_half, :], preferred_element_type=F32)
        for d in range(2):
            lhs = lhs_scr[d].astype(BF16)
            for cb in range(S5_BLOCKS):
                res = jnp.dot(lhs[:, cb * LANES:(cb + 1) * LANES], win_ref[d, cb], preferred_element_type=F32)
                for k in range(bw):
                    st[cb * bw + k, d * rows:(d + 1) * rows, :] = res[:, k * LANES:(k + 1) * LANES]
                    st[half + cb * bw + k, d * rows:(d + 1) * rows, :] = res[:, (bw + k) * LANES:(bw + k + 1) * LANES]

    def scan(st, state):
        cur = list(state)
        for t in range(tc):
            idx = pl.ds(t, 2 * bsz, stride=pitch)
            for c in range(half):
                sre, sim = cur[c], cur[c + half]
                lr = lre_ref[c]
                li = lim_ref[c]
                nre = lr * sre - li * sim + st[c, idx, :]
                nim = lr * sim + li * sre + st[c + half, idx, :]
                st[c, idx, :] = nre
                st[c + half, idx, :] = nim
                cur[c], cur[c + half] = nre, nim
        return cur

    def project_out(st, f_half, r_half):
        for d in range(2):
            y_blocks = []
            for cb in range(S5_BLOCKS):
                slabs = [cb * bw + k for k in range(bw)] + [half + cb * bw + k for k in range(bw)]
                s_blk = jnp.concatenate([st[c, d * rows:(d + 1) * rows, :] for c in slabs], axis=-1)
                y_blocks.append(jnp.dot(s_blk.astype(BF16), wout_ref[d, cb], preferred_element_type=F32))
            y = jnp.concatenate(y_blocks, axis=-1)
            for b in range(bsz):
                yb = y[b * pitch:b * pitch + tc, :]
                if d == 0:
                    yf_ref[b, f_half, :] = yb
                else:
                    yr_ref[b, r_half, :] = jnp.dot(rev, yb.astype(BF16), preferred_element_type=F32)

    @pl.when(j == 0)
    def _():
        state_scr[...] = s0_ref[...]
        lhs_scr[...] = jnp.zeros_like(lhs_scr)
        project_in(xf_ref, first, xr_ref, second, st_a)

    project_in(xf_ref, second, xr_ref, first, st_b)
    state = scan(st_a, [state_scr[:, c * LANES:(c + 1) * LANES] for c in range(S5_SLABS)])
    project_out(st_a, first, second)
    project_in(xfn_ref, first, xrn_ref, second, st_a)
    state = scan(st_b, state)
    project_out(st_b, second, first)
    for c in range(S5_SLABS):
        state_scr[:, c * LANES:(c + 1) * LANES] = state[c]
    send_ref[...] = state_scr[...]


def s5_scan(xb, s0, w_in, lam_re, lam_im, w_out):
    bsz, n_tok, width = xb.shape
    tc = S5_CHUNK
    assert 2 * bsz == SUBLANES and n_tok % (2 * tc) == 0
    n = n_tok // (2 * tc)
    rev = jnp.asarray(np.eye(tc, dtype=np.float32)[::-1], BF16)
    blk = (bsz, 2 * tc, width)
    return pl.pallas_call(
        _s5_kernel,
        grid=(n,),
        in_specs=[
            pl.BlockSpec(blk, lambda j: (0, j, 0)),
            pl.BlockSpec(blk, lambda j: (0, jnp.minimum(j + 1, n - 1), 0)),
            pl.BlockSpec(blk, lambda j: (0, n - 1 - j, 0)),
            pl.BlockSpec(blk, lambda j: (0, jnp.maximum(n - 2 - j, 0), 0)),
            _full(s0.shape),
            _full(w_in.shape),
            _full(lam_re.shape),
            _full(lam_im.shape),
            _full(w_out.shape),
            _full(rev.shape),
        ],
        out_specs=[
            pl.BlockSpec(blk, lambda j: (0, j, 0)),
            pl.BlockSpec(blk, lambda j: (0, n - 1 - j, 0)),
            pl.BlockSpec(s0.shape, lambda j: (0, 0)),
        ],
        out_shape=[
            jax.ShapeDtypeStruct(xb.shape, F32),
            jax.ShapeDtypeStruct(xb.shape, F32),
            jax.ShapeDtypeStruct(s0.shape, F32),
        ],
        scratch_shapes=[
            pltpu.VMEM((2, bsz * S5_PITCH, width), F32),
            pltpu.VMEM((S5_SLABS, 2 * bsz * S5_PITCH, LANES), F32),
            pltpu.VMEM((S5_SLABS, 2 * bsz * S5_PITCH, LANES), F32),
            pltpu.VMEM(s0.shape, F32),
        ],
        compiler_params=_cparams("arbitrary"),
        name="s5_scan",
    )(xb, xb, xb, xb, s0, w_in, lam_re, lam_im, w_out, rev)


NA_ROWS_PER_STEP = 4
_NT = (((1,), (1,)), ((), ()))


def _na_head_pair(qp, kp, vp, kcp, vcp, bias_fn):
    lane_head = lax.broadcasted_iota(jnp.int32, qp.shape, 1) // NA_HEAD_DIM
    o_pair = None
    for hh in range(2):
        qm = jnp.where(lane_head == hh, qp, jnp.zeros_like(qp))
        s_ctx = lax.dot_general(qm, kcp, _NT, preferred_element_type=F32)
        m = jnp.max(s_ctx, axis=-1, keepdims=True)
        if kp is not None:
            s_win = lax.dot_general(qm, kp, _NT, preferred_element_type=F32) + bias_fn(hh)
            m = jnp.maximum(m, jnp.max(s_win, axis=-1, keepdims=True))
            e_win = jnp.exp(s_win - m)
        e_ctx = jnp.exp(s_ctx - m)
        den = jnp.sum(e_ctx, axis=-1, keepdims=True)
        o = jnp.dot(e_ctx.astype(BF16), vcp, preferred_element_type=F32)
        if kp is not None:
            den = den + jnp.sum(e_win, axis=-1, keepdims=True)
            o = o + jnp.dot(e_win.astype(BF16), vp, preferred_element_type=F32)
        o = o * (1.0 / den)
        o_pair = o if hh == 0 else jnp.where(lane_head == 0, o_pair, o)
    return o_pair


NA_UNION_ROWS = 12
NA_SLAB = 32
LOG2E = math.log2(math.e)
NA_TAB_LEFT_OUT = 2 * NA_KR
NA_TAB_RIGHT_OUT = NA_TAB_LEFT_OUT + 2 * NA_KR - 1
NA_TAB_BOTH_OUT = NA_TAB_RIGHT_OUT + 2 * NA_KR - 1


def _na_union_start(i, n_rows):
    return jnp.clip(i * NA_ROWS_PER_STEP - NA_KR // 2, 0, n_rows - NA_UNION_ROWS)


def _na_kernel(q_ref, k_ref, v_ref, kc_ref, vc_ref, tab_ref, o_ref, s_scr, p_scr, rden_scr, *, n_rows):
    i = pl.program_id(1)
    r0 = i * NA_ROWS_PER_STEP
    ks = _na_union_start(i, n_rows)
    n_keys = NA_UNION_ROWS * GRID_W
    n_q = NA_ROWS_PER_STEP * GRID_W
    n_pairs = NA_UNION_ROWS // 2
    slabs_per_row = GRID_W // NA_SLAB

    def table_entry(rr, jj):
        r = r0 + rr
        kst = jnp.clip(r - NA_KR // 2, 0, n_rows - NA_KR)
        key0 = ks + 2 * jj
        out0 = jnp.logical_or(key0 < kst, key0 >= kst + NA_KR)
        out1 = jnp.logical_or(key0 + 1 < kst, key0 + 1 >= kst + NA_KR)
        e = key0 - r + NA_KR
        both_in = jnp.clip(e, 0, 2 * NA_KR - 1)
        left_out = NA_TAB_LEFT_OUT + jnp.clip(e, 0, 2 * NA_KR - 2)
        right_out = NA_TAB_RIGHT_OUT + jnp.clip(e - 1, 0, 2 * NA_KR - 2)
        return jnp.where(out0, jnp.where(out1, NA_TAB_BOTH_OUT, left_out), jnp.where(out1, right_out, both_in))

    entries = [[table_entry(rr, jj) for jj in range(n_pairs)] for rr in range(NA_ROWS_PER_STEP)]

    lane_head = lax.broadcasted_iota(jnp.int32, (n_q, LANES), 1) // NA_HEAD_DIM

    def scores(p):
        ls = slice(p * LANES, (p + 1) * LANES)
        qp = q_ref[:, ls]
        zero = jnp.zeros_like(qp)
        q_stack = jnp.concatenate([jnp.where(lane_head == 0, qp, zero), jnp.where(lane_head == 1, qp, zero)], axis=0)
        k_all = jnp.concatenate([k_ref[0, :, ls], kc_ref[:, ls]], axis=0)
        s_scr[p] = lax.dot_general(q_stack, k_all, _NT, preferred_element_type=F32)

    scores(0)
    for p in range(NA_HEADS // 2):
        if p + 1 < NA_HEADS // 2:
            scores(p + 1)
        ls = slice(p * LANES, (p + 1) * LANES)
        v_all = jnp.concatenate([v_ref[0, :, ls], vc_ref[:, ls]], axis=0)
        s_p, p_p, rden_p = s_scr.at[p], p_scr.at[p], rden_scr.at[p]
        for sl in range(2 * n_q // NA_SLAB):
            hh = sl // (NA_ROWS_PER_STEP * slabs_per_row)
            rr = (sl // slabs_per_row) % NA_ROWS_PER_STEP
            q0 = (sl % slabs_per_row) * NA_SLAB
            rows = slice(sl * NA_SLAB, (sl + 1) * NA_SLAB)
            bias = jnp.concatenate(
                [tab_ref[2 * p + hh, entries[rr][jj], q0:q0 + NA_SLAB, :] for jj in range(n_pairs)], axis=-1)
            s_win = s_p[rows, :n_keys] + bias
            s_ctx = s_p[rows, n_keys:]
            m = jnp.maximum(jnp.max(s_win, axis=-1, keepdims=True), jnp.max(s_ctx, axis=-1, keepdims=True))
            e_win = jnp.exp2(s_win - m)
            e_ctx = jnp.exp2(s_ctx - m)
            den = jnp.sum(e_win, axis=-1, keepdims=True) + jnp.sum(e_ctx, axis=-1, keepdims=True)
            p_p[rows, :n_keys] = e_win.astype(BF16)
            p_p[rows, n_keys:] = e_ctx.astype(BF16)
            rden_p[rows, :] = jnp.broadcast_to(1.0 / den, (NA_SLAB, LANES))
        o = jnp.dot(p_p[...], v_all, preferred_element_type=F32) * rden_p[...]
        o_ref[:, ls] = jnp.where(lane_head == 0, o[:n_q], o[n_q:]).astype(o_ref.dtype)


def na_bias_table(rpb):
    w = np.arange(GRID_W)
    col_start = np.clip(w - NA_KC // 2, 0, GRID_W - NA_KC)
    col_mask = (w[None, :] >= col_start[:, None]) & (w[None, :] < col_start[:, None] + NA_KC)
    d_col = np.clip(w[None, :] - w[:, None], -(NA_KC - 1), NA_KC - 1) + (NA_KC - 1)
    pick = (d_col[None] == np.arange(2 * NA_KC - 1)[:, None, None]).astype(np.float32)
    full = jnp.einsum('hdj,jwu->hdwu', rpb.astype(F32), jnp.asarray(pick), precision=lax.Precision.HIGHEST)
    full = jnp.where(col_mask[None, None], full * LOG2E, NEG_INF)
    pad = jnp.zeros_like(full[:, :1])
    neg = jnp.full_like(full, NEG_INF)
    both_in = jnp.concatenate([jnp.concatenate([pad, full], axis=1), jnp.concatenate([full, pad], axis=1)], axis=-1)
    left_out = jnp.concatenate([neg, full], axis=-1)
    right_out = jnp.concatenate([full, neg], axis=-1)
    both_out = jnp.concatenate([neg[:, :1], neg[:, :1]], axis=-1)
    return jnp.concatenate([both_in, left_out, right_out, both_out], axis=1)


def neighbourhood_attention(q, k, v, kc, vc, table):
    bsz, n_tok, width = q.shape
    n_ctx = kc.shape[1]
    n_rows = n_tok // GRID_W
    assert n_rows >= NA_UNION_ROWS and n_rows % NA_ROWS_PER_STEP == 0
    tq = NA_ROWS_PER_STEP * GRID_W
    n_keys = NA_UNION_ROWS * GRID_W
    window = pl.BlockSpec((pl.Element(1), pl.Element(n_keys), pl.Element(width)),
                          lambda b, i: (b, _na_union_start(i, n_rows) * GRID_W, 0))
    return pl.pallas_call(
        functools.partial(_na_kernel, n_rows=n_rows),
        grid=(bsz, n_rows // NA_ROWS_PER_STEP),
        in_specs=[
            pl.BlockSpec((None, tq, width), lambda b, i: (b, i, 0)),
            window,
            window,
            pl.BlockSpec((None, n_ctx, width), lambda b, i: (b, 0, 0)),
            pl.BlockSpec((None, n_ctx, width), lambda b, i: (b, 0, 0)),
            _full(table.shape),
        ],
        out_specs=pl.BlockSpec((None, tq, width), lambda b, i: (b, i, 0)),
        out_shape=jax.ShapeDtypeStruct(q.shape, BF16),
        scratch_shapes=[
            pltpu.VMEM((NA_HEADS // 2, 2 * tq, n_keys + n_ctx), F32),
            pltpu.VMEM((NA_HEADS // 2, 2 * tq, n_keys + n_ctx), BF16),
            pltpu.VMEM((NA_HEADS // 2, 2 * tq, LANES), F32),
        ],
        compiler_params=_cparams("parallel", "arbitrary"),
        name="neighbourhood_attention",
    )(q, k, v, kc, vc, table)


def _ctx_attn_kernel(q_ref, k_ref, v_ref, o_ref):
    for p in range(NA_HEADS // 2):
        ls = slice(p * LANES, (p + 1) * LANES)
        o_pair = _na_head_pair(q_ref[:, ls], None, None, k_ref[:, ls], v_ref[:, ls], None)
        o_ref[:, ls] = o_pair.astype(o_ref.dtype)


def context_attention(qc, kc, vc):
    bsz, n_ctx, width = qc.shape
    spec = pl.BlockSpec((None, n_ctx, width), lambda b: (b, 0, 0))
    return pl.pallas_call(
        _ctx_attn_kernel,
        grid=(bsz,),
        in_specs=[spec, spec, spec],
        out_specs=spec,
        out_shape=jax.ShapeDtypeStruct(qc.shape, BF16),
        compiler_params=_cparams("parallel"),
        name="context_attention",
    )(qc, kc, vc)


ROUTER_LANES = LANES
EXPERT_LANE0 = N_GROUPS
GROUP_ID_LANE = 0
RANK_LANE = 1


def _route(logits):
    lane = lax.broadcasted_iota(jnp.int32, logits.shape, 1)
    big = jnp.int32(ROUTER_LANES)
    is_g = lane < N_GROUPS
    lg = jnp.where(is_g, logits, -jnp.inf)
    mg = jnp.max(lg, axis=-1, keepdims=True)
    grp = jnp.min(jnp.where(lg == mg, lane, big), axis=-1, keepdims=True)
    g_weight = 1.0 / jnp.sum(jnp.where(is_g, jnp.exp(logits - mg), 0.0), axis=-1, keepdims=True)
    e_idx = lane - EXPERT_LANE0
    sel = (e_idx >= 0) & (e_idx < N_EXPERTS) & ((e_idx // EXPERTS_PER_GROUP) == grp)
    ls1 = jnp.where(sel, logits, -jnp.inf)
    v1 = jnp.max(ls1, axis=-1, keepdims=True)
    i1 = jnp.min(jnp.where(ls1 == v1, lane, big), axis=-1, keepdims=True)
    ls2 = jnp.where(lane == i1, -jnp.inf, ls1)
    v2 = jnp.max(ls2, axis=-1, keepdims=True)
    i2 = jnp.min(jnp.where(ls2 == v2, lane, big), axis=-1, keepdims=True)
    e2 = jnp.exp(v2 - v1)
    w1 = 1.0 / (1.0 + e2)
    w2 = e2 * w1
    comb = g_weight * (jnp.where(lane == i1, w1, 0.0) + jnp.where(lane == i2, w2, 0.0))
    return jnp.where(lane == GROUP_ID_LANE, grp.astype(F32), comb)


def _merge_kernel(x_ref, ya_ref, xb_ref, yf_ref, yr_ref, yc_ref, gate_ref, mod_ref, d_ref, glu_ref,
                  wa_ref, wb_ref, wc_ref, wo_ref, g_ref, rw_ref, rb_ref, own_ref, xn_ref, h_ref, comb_ref):
    d = x_ref.shape[1]
    yb = d_ref[...] * xb_ref[...].astype(F32) + yf_ref[...] + yr_ref[...]
    yb = _gelu(yb)
    yb = yb * _sigmoid(jnp.dot(yb.astype(BF16), glu_ref[...], preferred_element_type=F32))

    def gate(j):
        return gate_ref[:, j * d:(j + 1) * d].astype(F32)

    m = gate(0) * jnp.dot(ya_ref[...], wa_ref[...], preferred_element_type=F32)
    m = m + gate(1) * jnp.dot(yb.astype(BF16), wb_ref[...], preferred_element_type=F32)
    m = m + gate(2) * jnp.dot(yc_ref[...], wc_ref[...], preferred_element_type=F32)
    xn = x_ref[...] + mod_ref[2:3, :] * jnp.dot(m.astype(BF16), wo_ref[...], preferred_element_type=F32)
    xn_ref[...] = xn
    ms = jnp.mean(xn * xn, axis=-1, keepdims=True)
    h = xn * lax.rsqrt(ms + EPS) * g_ref[...]
    h = h * (1.0 + mod_ref[4:5, :]) + mod_ref[3:4, :]
    h_ref[:, :d] = h.astype(h_ref.dtype)
    logits = jnp.dot(h, rw_ref[...], preferred_element_type=F32) + rb_ref[...]
    comb = _route(logits)
    comb_ref[...] = comb
    own = jnp.dot(comb, own_ref[...], preferred_element_type=F32)
    own_hi, own_lo = _split_bf16(own)
    lane = lax.broadcasted_iota(jnp.int32, own.shape, 1)
    h_ref[:, d:] = jnp.where(lane < EXPERTS_PER_GROUP, own_hi, own_lo)


def merge_and_route(x, ya, xb, yf, yr, yc, gates, mod, s5_d, glu_w, w_br_a, w_br_b, w_br_c, w_out,
                    norm_ffn_g, router_w, router_b, tm):
    bsz, n_tok, d = x.shape

    def tok(width):
        return pl.BlockSpec((None, tm, width), lambda b, i: (b, i, 0))

    own = np.zeros((ROUTER_LANES, ROUTER_LANES), np.float32)
    for e in range(N_EXPERTS):
        own[EXPERT_LANE0 + e, e % EXPERTS_PER_GROUP] = 1.0
        own[EXPERT_LANE0 + e, EXPERTS_PER_GROUP + e % EXPERTS_PER_GROUP] = 1.0
    weights = [s5_d.reshape(1, S5_WIDTH), glu_w, w_br_a, w_br_b, w_br_c, w_out,
               norm_ffn_g.reshape(1, d), router_w, router_b, jnp.asarray(own)]
    return pl.pallas_call(
        _merge_kernel,
        grid=(bsz, n_tok // tm),
        in_specs=[tok(d), tok(SGU_WIDTH), tok(S5_WIDTH), tok(S5_WIDTH), tok(S5_WIDTH), tok(NA_WIDTH),
                  tok(N_BRANCH * d), pl.BlockSpec((None, ADA_CHUNKS, d), lambda b, i: (b, 0, 0))]
        + [_full(w.shape) for w in weights],
        out_specs=[tok(d), tok(d + ROUTER_LANES), tok(ROUTER_LANES)],
        out_shape=[
            jax.ShapeDtypeStruct(x.shape, F32),
            jax.ShapeDtypeStruct((bsz, n_tok, d + ROUTER_LANES), BF16),
            jax.ShapeDtypeStruct((bsz, n_tok, ROUTER_LANES), F32),
        ],
        compiler_params=_cparams("parallel", "parallel"),
        name="merge_and_route",
    )(x, ya, xb, yf, yr, yc, gates, mod, *weights)


def router_params(rg_w, rg_b, re_w, re_b):
    d = rg_w.shape[0]
    pad = ROUTER_LANES - N_GROUPS - N_EXPERTS
    w = jnp.concatenate([rg_w, re_w, jnp.zeros((d, pad), F32)], axis=1).astype(F32)
    b = jnp.concatenate([rg_b, re_b, jnp.zeros((pad,), F32)]).astype(F32).reshape(1, ROUTER_LANES)
    return w, b


MOE_BLOCK = 144


def _split_bf16(x):
    hi = x.astype(BF16)
    return hi, (x - hi.astype(F32)).astype(BF16)


MOE_SUBTILE = 512


def _moe_kernel(xn_ref, hx_ref, comb_ref, mod_ref, wg_ref, wu_ref, wd_ref, fg_ref, o_ref,
                aux_col, aux_row, hid_scr, cnt_ref, *, final_norm):
    g = pl.program_id(2)
    tm, d = o_ref.shape
    st = min(MOE_SUBTILE, tm)
    n_sub = tm // st

    @pl.when(g == 0)
    def _():
        o_ref[...] = xn_ref[...]
        row_i = lax.broadcasted_iota(jnp.int32, (st, st), 0)
        col_i = lax.broadcasted_iota(jnp.int32, (st, st), 1)
        tri = jnp.where(col_i < row_i, 1.0, 0.0).astype(BF16)
        for s in range(n_sub):
            comb = comb_ref[s * st:(s + 1) * st, :]
            lane = lax.broadcasted_iota(jnp.int32, comb.shape, 1)
            grp = comb[:, GROUP_ID_LANE:GROUP_ID_LANE + 1]
            onehot = jnp.where(lane < N_GROUPS, jnp.where(lane.astype(F32) == grp, 1.0, 0.0), 0.0)
            ranks = jnp.dot(tri, onehot.astype(BF16), preferred_element_type=F32)
            own = jnp.sum(onehot * ranks, axis=-1, keepdims=True)
            aux = jnp.where(lane == GROUP_ID_LANE, grp, jnp.where(lane == RANK_LANE, own, 0.0))
            aux_col[s * st:(s + 1) * st, :] = aux
            aux_row[:, s * st:(s + 1) * st] = aux.T
            for gg in range(N_GROUPS):
                cnt_ref[s * N_GROUPS + gg] = jnp.sum(onehot[:, gg:gg + 1]).astype(jnp.int32)

    gf = g.astype(F32)
    slot_r = lax.broadcasted_iota(jnp.int32, (MOE_BLOCK, st), 0).astype(F32)
    slot_c = lax.broadcasted_iota(jnp.int32, (st, MOE_BLOCK), 1).astype(F32)
    scale = mod_ref[5:6, :]

    sub_rows = [slice(s * st, (s + 1) * st) for s in range(n_sub)]
    rank_rows = [jnp.where(aux_row[GROUP_ID_LANE:GROUP_ID_LANE + 1, r] == gf,
                           aux_row[RANK_LANE:RANK_LANE + 1, r], -1.0) for r in sub_rows]
    rank_cols = [jnp.where(aux_col[r, GROUP_ID_LANE:GROUP_ID_LANE + 1] == gf,
                           aux_col[r, RANK_LANE:RANK_LANE + 1], -1.0) for r in sub_rows]
    n_rounds = cnt_ref[g]
    for s in range(1, n_sub):
        n_rounds = jnp.maximum(n_rounds, cnt_ref[s * N_GROUPS + g])
    n_rounds = (n_rounds + MOE_BLOCK - 1) // MOE_BLOCK

    def round_(j, carry):
        base = (j * MOE_BLOCK).astype(F32)
        hcx = jnp.concatenate(
            [jnp.dot(jnp.where(rank_rows[s] - base == slot_r, 1.0, 0.0).astype(BF16), hx_ref[sub_rows[s], :],
                     preferred_element_type=F32) for s in range(n_sub)], axis=0)
        hc = hcx[:, :d].astype(BF16)
        wt = hcx[:, d:]
        for e in range(EXPERTS_PER_GROUP):
            a = jnp.dot(hc, wg_ref[e], preferred_element_type=F32)
            u = jnp.dot(hc, wu_ref[e], preferred_element_type=F32)
            cw = wt[:, e:e + 1] + wt[:, EXPERTS_PER_GROUP + e:EXPERTS_PER_GROUP + e + 1]
            hid_scr[:, e * D_EXPERT:(e + 1) * D_EXPERT] = (a * jax.nn.sigmoid(a) * u * cw).astype(BF16)
        oc = jnp.dot(hid_scr[...], wd_ref[...], preferred_element_type=F32).astype(BF16)
        for s in range(n_sub):
            scatter = jnp.where(rank_cols[s] - base == slot_c, 1.0, 0.0).astype(BF16)
            o_ref[sub_rows[s], :] += scale * jnp.dot(
                scatter, oc[s * MOE_BLOCK:(s + 1) * MOE_BLOCK], preferred_element_type=F32)
        return carry

    lax.fori_loop(0, n_rounds, round_, 0)

    if final_norm:
        @pl.when(g == pl.num_programs(2) - 1)
        def _():
            xo = o_ref[...]
            ms = jnp.mean(xo * xo, axis=-1, keepdims=True)
            o_ref[...] = xo * lax.rsqrt(ms + EPS) * fg_ref[...]


def moe_grouped(xn, hx, comb, mod, wg, wu, wd, layer, final_g, tm, final_norm):
    bsz, n_tok, d = xn.shape
    gw = EXPERTS_PER_GROUP * D_EXPERT
    n_sub = tm // min(MOE_SUBTILE, tm)

    def tok(width):
        return pl.BlockSpec((None, tm, width), lambda b, i, g: (b, i, 0))

    return pl.pallas_call(
        functools.partial(_moe_kernel, final_norm=final_norm),
        grid=(bsz, n_tok // tm, N_GROUPS),
        in_specs=[
            tok(d), tok(d + ROUTER_LANES), tok(ROUTER_LANES),
            pl.BlockSpec((None, ADA_CHUNKS, d), lambda b, i, g: (b, 0, 0)),
            pl.BlockSpec((None, None, EXPERTS_PER_GROUP, d, D_EXPERT), lambda b, i, g: (layer, g, 0, 0, 0)),
            pl.BlockSpec((None, None, EXPERTS_PER_GROUP, d, D_EXPERT), lambda b, i, g: (layer, g, 0, 0, 0)),
            pl.BlockSpec((None, None, gw, d), lambda b, i, g: (layer, g, 0, 0)),
            pl.BlockSpec((1, d), lambda b, i, g: (0, 0)),
        ],
        out_specs=tok(d),
        out_shape=jax.ShapeDtypeStruct(xn.shape, F32),
        scratch_shapes=[
            pltpu.VMEM((tm, ROUTER_LANES), F32),
            pltpu.VMEM((ROUTER_LANES, tm), F32),
            pltpu.VMEM((n_sub * MOE_BLOCK, gw), BF16),
            pltpu.SMEM((n_sub * N_GROUPS,), jnp.int32),
        ],
        compiler_params=_cparams("parallel", "parallel", "arbitrary"),
        name="moe_grouped",
    )(xn, hx, comb, mod, wg, wu, wd, final_g.reshape(1, d))


EXPERTS_PER_CAST_STEP = 4


def _cast_kernel(a_ref, b_ref, c_ref, oa_ref, ob_ref, oc_ref):
    oa_ref[...] = a_ref[...].astype(oa_ref.dtype)
    ob_ref[...] = b_ref[...].astype(ob_ref.dtype)
    oc_ref[...] = c_ref[...].astype(oc_ref.dtype)


def moe_params(e_gate, e_up, e_down):
    n_layers, n_exp, d, f = e_gate.shape
    flat = [w.reshape(n_layers * n_exp, *w.shape[2:]) for w in (e_gate, e_up, e_down)]

    def spec(w):
        return pl.BlockSpec((EXPERTS_PER_CAST_STEP,) + w.shape[1:], lambda i: (i, 0, 0))

    wg, wu, wd = pl.pallas_call(
        _cast_kernel,
        grid=(n_layers * n_exp // EXPERTS_PER_CAST_STEP,),
        in_specs=[spec(w) for w in flat],
        out_specs=[spec(w) for w in flat],
        out_shape=[jax.ShapeDtypeStruct(w.shape, BF16) for w in flat],
        compiler_params=_cparams("parallel"),
        name="expert_weights_bf16",
    )(*flat)
    return (wg.reshape(n_layers, N_GROUPS, EXPERTS_PER_GROUP, d, f),
            wu.reshape(n_layers, N_GROUPS, EXPERTS_PER_GROUP, d, f),
            wd.reshape(n_layers, N_GROUPS, EXPERTS_PER_GROUP * f, d))


def s5_params(a_re, a_im, log_dt, b_re, b_im, c_re, c_im, bsz):
    lam = lax.complex(a_re.astype(F32), a_im.astype(F32))
    dt = jnp.exp(log_dt.astype(F32))[..., None]
    lam_bar = jnp.exp(lam * dt)
    b_bar = ((lam_bar - 1) / lam)[..., None] * lax.complex(b_re.astype(F32), b_im.astype(F32))
    gpb = S5_GROUPS // S5_BLOCKS
    eye = jnp.eye(gpb, dtype=F32)
    gp = S5_GROUPS * S5_STATE

    def in_mat(m):
        m = m.reshape(2, S5_BLOCKS, gpb, S5_STATE, S5_GROUP)
        return jnp.einsum('dkgpc,gh->dkgchp', m, eye).reshape(2, S5_BLOCKS, gpb * S5_GROUP, gpb * S5_STATE)

    def out_mat(m):
        m = m.reshape(2, S5_BLOCKS, gpb, S5_GROUP, S5_STATE)
        return jnp.einsum('dkgcp,gh->dkgphc', m, eye).reshape(2, S5_BLOCKS, gpb * S5_STATE, gpb * S5_GROUP)

    w_in = jnp.concatenate([in_mat(b_bar.real), in_mat(b_bar.imag)], axis=-1).astype(BF16)
    w_out = jnp.concatenate([out_mat(c_re.astype(F32)), -out_mat(c_im.astype(F32))], axis=2).astype(BF16)

    def tiles(v):
        t = v.reshape(2, gp // LANES, 1, LANES)
        t = jnp.broadcast_to(t, (2, gp // LANES, bsz, LANES))
        return jnp.concatenate([t[0], t[1]], axis=1)

    lam_flat = lam_bar.reshape(2, gp)
    return w_in, tiles(lam_flat.real), tiles(lam_flat.imag), w_out


CTX_TOKEN_TILE = 256
MIXER_TOKEN_TILE = 512
MERGE_TOKEN_TILE = 512
MOE_TOKEN_TILE = 1024


def kernel(x, c, ctx, c_ctx, ada_w, ada_b, norm_mix_g, norm_ffn_g, w_in, sgu_norm_g, sgu_w, sgu_b, s5_a_re, s5_a_im, s5_log_dt, s5_b_re, s5_b_im, s5_c_re, s5_c_im, s5_d, s5_glu_w, na_rpb, w_br_a, w_br_b, w_br_c, w_out, router_group_w, router_group_b, router_expert_w, router_expert_b, exp_w_gate, exp_w_up, exp_w_down, final_norm_g):
    bsz, n_tok, d = x.shape
    n_ctx = ctx.shape[1]
    depth = ada_w.shape[0]
    assert bsz + 1 <= SUBLANES

    cc = jnp.concatenate([c, c_ctx[None], jnp.zeros((SUBLANES - bsz - 1, d), F32)], axis=0)
    mod_all = ada_modulation(cc, ada_w, ada_b)
    rope_tabs = rope_tables(n_tok)
    s_zero = jnp.zeros((2 * bsz, S5_LANES), F32)
    tm_c = min(CTX_TOKEN_TILE, n_ctx)
    tm_moe = min(MOE_TOKEN_TILE, n_tok)

    wg_all, wu_all, wd_all = moe_params(exp_w_gate, exp_w_up, exp_w_down)

    xc = ctx
    for l in range(depth):
        with_ctx_out = l < depth - 1
        mod = mod_all[l, :bsz].reshape(bsz, ADA_CHUNKS, d)
        mod_c = jnp.broadcast_to(mod_all[l, bsz].reshape(1, ADA_CHUNKS, d), (bsz, ADA_CHUNKS, d))
        w_in_l = w_in[l].astype(BF16)
        sgu_w_l = sgu_w[l].astype(BF16)
        sgu_bias = jnp.broadcast_to(sgu_b[l].astype(F32)[:, :, None], (SGU_GROUPS, SGU_CHUNK, SGU_CHUNK))
        s5_w_in, s5_lre, s5_lim, s5_w_out = s5_params(
            s5_a_re[l], s5_a_im[l], s5_log_dt[l], s5_b_re[l], s5_b_im[l], s5_c_re[l], s5_c_im[l], bsz)
        table = na_bias_table(na_rpb[l])
        r_w, r_b = router_params(router_group_w[l], router_group_b[l], router_expert_w[l], router_expert_b[l])
        merge_w = (s5_d[l].astype(F32), s5_glu_w[l].astype(BF16), w_br_a[l].astype(BF16),
                   w_br_b[l].astype(BF16), w_br_c[l].astype(BF16), w_out[l].astype(BF16),
                   norm_ffn_g[l].astype(F32), r_w, r_b)

        ya_c, xb_c, q_c, k_c, v_c, gate_c = mixer_in(
            xc, mod_c, norm_mix_g[l], w_in_l, sgu_norm_g[l], sgu_w_l, sgu_bias, None, tm_c)
        ya_l, xb_l, q_l, k_l, v_l, gate_l = mixer_in(
            x, mod, norm_mix_g[l], w_in_l, sgu_norm_g[l], sgu_w_l, sgu_bias, rope_tabs, MIXER_TOKEN_TILE)
        ycf, ycr, s_ctx = s5_scan(xb_c, s_zero, s5_w_in, s5_lre, s5_lim, s5_w_out)
        ylf, ylr, _ = s5_scan(xb_l, s_ctx, s5_w_in, s5_lre, s5_lim, s5_w_out)
        yc_l = neighbourhood_attention(q_l, k_l, v_l, k_c, v_c, table)
        xn, h2, comb = merge_and_route(x, ya_l, xb_l, ylf, ylr, yc_l, gate_l, mod, *merge_w, MERGE_TOKEN_TILE)
        x = moe_grouped(xn, h2, comb, mod, wg_all, wu_all, wd_all, l, final_norm_g, tm_moe, not with_ctx_out)
        if with_ctx_out:
            yc_c = context_attention(q_c, k_c, v_c)
            xcn, hc2, comb_c = merge_and_route(xc, ya_c, xb_c, ycf, ycr, yc_c, gate_c, mod_c, *merge_w, tm_c)

            def flat(t):
                return t.reshape(1, bsz * n_ctx, t.shape[-1])

            xc = moe_grouped(flat(xcn), flat(hc2), flat(comb_c), mod_c[:1], wg_all, wu_all, wd_all, l, final_norm_g,
                             min(MOE_TOKEN_TILE, bsz * n_ctx), False).reshape(bsz, n_ctx, d)
    return x
```

```python
import functools
import math

import jax
import jax.numpy as jnp
import numpy as np
from jax import lax
from jax.experimental import pallas as pl
from jax.experimental.pallas import tpu as pltpu

F32 = jnp.float32
BF16 = jnp.bfloat16

GRID_W = 64
N_BRANCH = 3
SGU_WIDTH = 512
SGU_GROUPS = 4
SGU_CHUNK = 128
S5_WIDTH = 384
S5_GROUP = 16
S5_GROUPS = S5_WIDTH // S5_GROUP
S5_STATE = 64
NA_HEADS = 8
NA_HEAD_DIM = 64
NA_WIDTH = NA_HEADS * NA_HEAD_DIM
NA_KR = 8
NA_KC = 16
ROPE_BASE = 10000.0
N_GROUPS = 4
EXPERTS_PER_GROUP = 8
N_EXPERTS = N_GROUPS * EXPERTS_PER_GROUP
D_EXPERT = 256
ADA_CHUNKS = 6
EPS = 1e-6
NEG_INF = -1e30

LANES = 128
SUBLANES = 8
VMEM_LIMIT_BYTES = 56 * 1024 * 1024

S5_LANES = 2 * S5_GROUPS * S5_STATE
S5_SLABS = S5_LANES // LANES
S5_BLOCKS = S5_WIDTH // LANES
S5_CHUNK = 128
S5_PITCH = S5_CHUNK + 4


def _cparams(*sem):
    return pltpu.CompilerParams(dimension_semantics=sem, vmem_limit_bytes=VMEM_LIMIT_BYTES)


def _full(shape):
    n = len(shape)
    return pl.BlockSpec(shape, lambda *_: (0,) * n, pipeline_mode=pl.Buffered(1))


def _ada_kernel(c_ref, w_ref, b_ref, o_ref):
    c = c_ref[...]
    s = c * jax.nn.sigmoid(c)
    o_ref[...] = jnp.dot(s, w_ref[...], preferred_element_type=F32) + b_ref[...]


def ada_modulation(cc, ada_w, ada_b):
    n_layers, d, n = ada_w.shape
    tn = n // 4
    return pl.pallas_call(
        _ada_kernel,
        grid=(n_layers, n // tn),
        in_specs=[
            pl.BlockSpec((SUBLANES, d), lambda l, j: (0, 0)),
            pl.BlockSpec((None, d, tn), lambda l, j: (l, 0, j)),
            pl.BlockSpec((None, 1, tn), lambda l, j: (l, 0, j)),
        ],
        out_specs=pl.BlockSpec((None, SUBLANES, tn), lambda l, j: (l, 0, j)),
        out_shape=jax.ShapeDtypeStruct((n_layers, SUBLANES, n), F32),
        compiler_params=_cparams("parallel", "parallel"),
        name="ada_modulation",
    )(cc, ada_w, ada_b.reshape(n_layers, 1, n))


def _gelu(x):
    return jax.nn.gelu(x)


def _sigmoid(x):
    return 0.5 * jnp.tanh(0.5 * x) + 0.5


def _mixer_in_kernel(x_ref, mod_ref, g_ref, w_ref, lng_ref, sw_ref, sb_ref, *rest, rope):
    if rope:
        cos_ref, sin_ref, ya_ref, b_ref, q_ref, k_ref, v_ref, gate_ref = rest
    else:
        ya_ref, b_ref, q_ref, k_ref, v_ref, gate_ref = rest
    tm = x_ref.shape[0]
    xf = x_ref[...]
    ms = jnp.mean(xf * xf, axis=-1, keepdims=True)
    y = xf * lax.rsqrt(ms + EPS) * g_ref[...]
    h = y * (1.0 + mod_ref[1:2, :]) + mod_ref[0:1, :]
    hb = h.astype(BF16)

    def proj(lo, hi):
        return jnp.dot(hb, w_ref[:, lo:hi], preferred_element_type=F32)

    o1 = 2 * SGU_WIDTH
    o2 = o1 + S5_WIDTH
    oq, ok, ov = o2, o2 + NA_WIDTH, o2 + 2 * NA_WIDTH
    o3 = o2 + 3 * NA_WIDTH

    d_model = x_ref.shape[1]

    def gate_chunk(j):
        cols = slice(j * d_model, (j + 1) * d_model)
        gate_ref[:, cols] = _sigmoid(proj(o3 + j * d_model, o3 + (j + 1) * d_model)).astype(gate_ref.dtype)

    u = _gelu(proj(0, SGU_WIDTH))
    v = _gelu(proj(SGU_WIDTH, o1))
    gate_chunk(0)
    vc = v - jnp.mean(v, axis=-1, keepdims=True)
    vn = vc * lax.rsqrt(jnp.mean(vc * vc, axis=-1, keepdims=True) + EPS) * lng_ref[...]
    vb = vn.astype(BF16)
    cw = SGU_WIDTH // SGU_GROUPS
    for c in range(tm // SGU_CHUNK):
        r0 = c * SGU_CHUNK
        for g in range(SGU_GROUPS):
            sp = jnp.dot(sw_ref[g], vb[r0:r0 + SGU_CHUNK, g * cw:(g + 1) * cw],
                         preferred_element_type=F32) + sb_ref[g]
            ya_ref[r0:r0 + SGU_CHUNK, g * cw:(g + 1) * cw] = (
                u[r0:r0 + SGU_CHUNK, g * cw:(g + 1) * cw] * sp).astype(ya_ref.dtype)

    b_ref[...] = proj(o1, o2).astype(b_ref.dtype)
    gate_chunk(1)

    q = proj(oq, ok)
    k = proj(ok, ov)
    if rope:
        cos = cos_ref[...]
        sin = sin_ref[...]

        seg_half = NA_HEAD_DIM // 4
        first_half = (lax.broadcasted_iota(jnp.int32, cos.shape, 1) % (2 * seg_half)) < seg_half

        def rotate(t):
            out = []
            for j in range(NA_WIDTH // LANES):
                tj = t[:, j * LANES:(j + 1) * LANES]
                swapped = jnp.where(first_half, pltpu.roll(tj, LANES - seg_half, 1), pltpu.roll(tj, seg_half, 1))
                out.append(tj * cos + swapped * sin)
            return jnp.concatenate(out, axis=-1)

        q = rotate(q)
        k = rotate(k)
    q_scale = NA_HEAD_DIM ** -0.5 * (LOG2E if rope else 1.0)
    q_ref[...] = (q * q_scale).astype(q_ref.dtype)
    k_ref[...] = k.astype(k_ref.dtype)
    v_ref[...] = proj(ov, o3).astype(v_ref.dtype)
    gate_chunk(2)


def mixer_in(x, mod, norm_g, w_in, sgu_norm_g, sgu_w, sgu_bias, rope_tabs, tm):
    bsz, n_tok, d = x.shape
    rope = rope_tabs is not None
    in_specs = [
        pl.BlockSpec((None, tm, d), lambda b, i: (b, i, 0)),
        pl.BlockSpec((None, ADA_CHUNKS, d), lambda b, i: (b, 0, 0)),
        _full((1, d)),
        _full(w_in.shape),
        _full((1, SGU_WIDTH)),
        _full(sgu_w.shape),
        _full(sgu_bias.shape),
    ]
    args = [x, mod, norm_g.reshape(1, d), w_in, sgu_norm_g.reshape(1, SGU_WIDTH), sgu_w, sgu_bias]
    if rope:
        cos_t, sin_t = rope_tabs
        in_specs += [
            pl.BlockSpec((tm, LANES), lambda b, i: (i, 0)),
            pl.BlockSpec((tm, LANES), lambda b, i: (i, 0)),
        ]
        args += [cos_t, sin_t]

    def tok(width):
        return pl.BlockSpec((None, tm, width), lambda b, i: (b, i, 0))

    out_shapes = [
        jax.ShapeDtypeStruct((bsz, n_tok, SGU_WIDTH), BF16),
        jax.ShapeDtypeStruct((bsz, n_tok, S5_WIDTH), BF16),
        jax.ShapeDtypeStruct((bsz, n_tok, NA_WIDTH), BF16),
        jax.ShapeDtypeStruct((bsz, n_tok, NA_WIDTH), BF16),
        jax.ShapeDtypeStruct((bsz, n_tok, NA_WIDTH), BF16),
        jax.ShapeDtypeStruct((bsz, n_tok, N_BRANCH * d), BF16),
    ]
    out_specs = [tok(SGU_WIDTH), tok(S5_WIDTH), tok(NA_WIDTH), tok(NA_WIDTH), tok(NA_WIDTH), tok(N_BRANCH * d)]
    return pl.pallas_call(
        functools.partial(_mixer_in_kernel, rope=rope),
        grid=(bsz, n_tok // tm),
        in_specs=in_specs,
        out_specs=out_specs,
        out_shape=out_shapes,
        compiler_params=_cparams("parallel", "parallel"),
        name="mixer_in_rope" if rope else "mixer_in",
    )(*args)


def rope_tables(n_tok):
    pos = np.arange(n_tok)
    rows = (pos // GRID_W).astype(np.float32)
    cols = (pos % GRID_W).astype(np.float32)
    seg = NA_HEAD_DIM // 2
    half = seg // 2
    inv_freq = (ROPE_BASE ** (-np.arange(half, dtype=np.float32) / half)).astype(np.float32)
    ang_r = rows[:, None] * inv_freq
    ang_c = cols[:, None] * inv_freq
    cos = np.concatenate([np.cos(ang_r)] * 2 + [np.cos(ang_c)] * 2, axis=-1)
    sin = np.concatenate([-np.sin(ang_r), np.sin(ang_r), -np.sin(ang_c), np.sin(ang_c)], axis=-1)
    reps = LANES // NA_HEAD_DIM
    return jnp.asarray(np.tile(cos, (1, reps)), F32), jnp.asarray(np.tile(sin, (1, reps)), F32)


def _s5_kernel(xf_ref, xfn_ref, xr_ref, xrn_ref, s0_ref, win_ref, lre_ref, lim_ref, wout_ref, rev_ref,
               yf_ref, yr_ref, send_ref, lhs_scr, st_a, st_b, state_scr):
    bsz = xf_ref.shape[0]
    tc = S5_CHUNK
    pitch = S5_PITCH
    rows = bsz * pitch
    half = S5_SLABS // 2
    bw = half // S5_BLOCKS
    j = pl.program_id(0)
    rev = rev_ref[...]
    first, second = slice(0, tc), slice(tc, 2 * tc)

    def project_in(f_ref, f_half, r_ref, r_half, st):
        for b in range(bsz):
            lhs_scr[0, b * pitch:b * pitch + tc, :] = f_ref[b, f_half, :].astype(F32)
            lhs_scr[1, b * pitch:b * pitch + tc, :] = jnp.dot(rev, r_ref[b, r_half, :], preferred_element_type=F32)
        for d in range(2):
            lhs = lhs_scr[d].astype(BF16)
            for cb in range(S5_BLOCKS):
                res = jnp.dot(lhs[:, cb * LANES:(cb + 1) * LANES], win_ref[d, cb], preferred_element_type=F32)
                for k in range(bw):
                    st[cb * bw + k, d * rows:(d + 1) * rows, :] = res[:, k * LANES:(k + 1) * LANES]
                    st[half + cb * bw + k, d * rows:(d + 1) * rows, :] = res[:, (bw + k) * LANES:(bw + k + 1) * LANES]

    def scan(st, state):
        cur = list(state)
        for t in range(tc):
            idx = pl.ds(t, 2 * bsz, stride=pitch)
            for c in range(half):
                sre, sim = cur[c], cur[c + half]
                lr = lre_ref[c]
                li = lim_ref[c]
                nre = lr * sre - li * sim + st[c, idx, :]
                nim = lr * sim + li * sre + st[c + half, idx, :]
                st[c, idx, :] = nre
                st[c + half, idx, :] = nim
                cur[c], cur[c + half] = nre, nim
        return cur

    def project_out(st, f_half, r_half):
        for d in range(2):
            y_blocks = []
            for cb in range(S5_BLOCKS):
                slabs = [cb * bw + k for k in range(bw)] + [half + cb * bw + k for k in range(bw)]
                s_blk = jnp.concatenate([st[c, d * rows:(d + 1) * rows, :] for c in slabs], axis=-1)
                y_blocks.append(jnp.dot(s_blk.astype(BF16), wout_ref[d, cb], preferred_element_type=F32))
            y = jnp.concatenate(y_blocks, axis=-1)
            for b in range(bsz):
                yb = y[b * pitch:b * pitch + tc, :]
                if d == 0:
                    yf_ref[b, f_half, :] = yb
                else:
                    yr_ref[b, r_half, :] = jnp.dot(rev, yb.astype(BF16), preferred_element_type=F32)

    @pl.when(j == 0)
    def _():
        state_scr[...] = s0_ref[...]
        lhs_scr[...] = jnp.zeros_like(lhs_scr)
        project_in(xf_ref, first, xr_ref, second, st_a)

    project_in(xf_ref, second, xr_ref, first, st_b)
    state = scan(st_a, [state_scr[:, c * LANES:(c + 1) * LANES] for c in range(S5_SLABS)])
    project_out(st_a, first, second)
    project_in(xfn_ref, first, xrn_ref, second, st_a)
    state = scan(st_b, state)
    project_out(st_b, second, first)
    for c in range(S5_SLABS):
        state_scr[:, c * LANES:(c + 1) * LANES] = state[c]
    send_ref[...] = state_scr[...]


def s5_scan(xb, s0, w_in, lam_re, lam_im, w_out):
    bsz, n_tok, width = xb.shape
    tc = S5_CHUNK
    assert 2 * bsz == SUBLANES and n_tok % (2 * tc) == 0
    n = n_tok // (2 * tc)
    rev = jnp.asarray(np.eye(tc, dtype=np.float32)[::-1], BF16)
    blk = (bsz, 2 * tc, width)
    return pl.pallas_call(
        _s5_kernel,
        grid=(n,),
        in_specs=[
            pl.BlockSpec(blk, lambda j: (0, j, 0)),
            pl.BlockSpec(blk, lambda j: (0, jnp.minimum(j + 1, n - 1), 0)),
            pl.BlockSpec(blk, lambda j: (0, n - 1 - j, 0)),
            pl.BlockSpec(blk, lambda j: (0, jnp.maximum(n - 2 - j, 0), 0)),
            _full(s0.shape),
            _full(w_in.shape),
            _full(lam_re.shape),
            _full(lam_im.shape),
            _full(w_out.shape),
            _full(rev.shape),
        ],
        out_specs=[
            pl.BlockSpec(blk, lambda j: (0, j, 0)),
            pl.BlockSpec(blk, lambda j: (0, n - 1 - j, 0)),
            pl.BlockSpec(s0.shape, lambda j: (0, 0)),
        ],
        out_shape=[
            jax.ShapeDtypeStruct(xb.shape, F32),
            jax.ShapeDtypeStruct(xb.shape, F32),
            jax.ShapeDtypeStruct(s0.shape, F32),
        ],
        scratch_shapes=[
            pltpu.VMEM((2, bsz * S5_PITCH, width), F32),
            pltpu.VMEM((S5_SLABS, 2 * bsz * S5_PITCH, LANES), F32),
            pltpu.VMEM((S5_SLABS, 2 * bsz * S5_PITCH, LANES), F32),
            pltpu.VMEM(s0.shape, F32),
        ],
        compiler_params=_cparams("arbitrary"),
        name="s5_scan",
    )(xb, xb, xb, xb, s0, w_in, lam_re, lam_im, w_out, rev)


NA_ROWS_PER_STEP = 4
_NT = (((1,), (1,)), ((), ()))


def _na_head_pair(qp, kp, vp, kcp, vcp, bias_fn):
    lane_head = lax.broadcasted_iota(jnp.int32, qp.shape, 1) // NA_HEAD_DIM
    o_pair = None
    for hh in range(2):
        qm = jnp.where(lane_head == hh, qp, jnp.zeros_like(qp))
        s_ctx = lax.dot_general(qm, kcp, _NT, preferred_element_type=F32)
        m = jnp.max(s_ctx, axis=-1, keepdims=True)
        if kp is not None:
            s_win = lax.dot_general(qm, kp, _NT, preferred_element_type=F32) + bias_fn(hh)
            m = jnp.maximum(m, jnp.max(s_win, axis=-1, keepdims=True))
            e_win = jnp.exp(s_win - m)
        e_ctx = jnp.exp(s_ctx - m)
        den = jnp.sum(e_ctx, axis=-1, keepdims=True)
        o = jnp.dot(e_ctx.astype(BF16), vcp, preferred_element_type=F32)
        if kp is not None:
            den = den + jnp.sum(e_win, axis=-1, keepdims=True)
            o = o + jnp.dot(e_win.astype(BF16), vp, preferred_element_type=F32)
        o = o * (1.0 / den)
        o_pair = o if hh == 0 else jnp.where(lane_head == 0, o_pair, o)
    return o_pair


NA_UNION_ROWS = 12
NA_SLAB = 32
LOG2E = math.log2(math.e)
NA_TAB_LEFT_OUT = 2 * NA_KR
NA_TAB_RIGHT_OUT = NA_TAB_LEFT_OUT + 2 * NA_KR - 1
NA_TAB_BOTH_OUT = NA_TAB_RIGHT_OUT + 2 * NA_KR - 1


def _na_union_start(i, n_rows):
    return jnp.clip(i * NA_ROWS_PER_STEP - NA_KR // 2, 0, n_rows - NA_UNION_ROWS)


def _na_kernel(q_ref, k_ref, v_ref, kc_ref, vc_ref, tab_ref, o_ref, s_scr, p_scr, rden_scr, *, n_rows):
    i = pl.program_id(1)
    r0 = i * NA_ROWS_PER_STEP
    ks = _na_union_start(i, n_rows)
    n_keys = NA_UNION_ROWS * GRID_W
    n_q = NA_ROWS_PER_STEP * GRID_W
    n_pairs = NA_UNION_ROWS // 2
    slabs_per_row = GRID_W // NA_SLAB

    def table_entry(rr, jj):
        r = r0 + rr
        kst = jnp.clip(r - NA_KR // 2, 0, n_rows - NA_KR)
        key0 = ks + 2 * jj
        out0 = jnp.logical_or(key0 < kst, key0 >= kst + NA_KR)
        out1 = jnp.logical_or(key0 + 1 < kst, key0 + 1 >= kst + NA_KR)
        e = key0 - r + NA_KR
        both_in = jnp.clip(e, 0, 2 * NA_KR - 1)
        left_out = NA_TAB_LEFT_OUT + jnp.clip(e, 0, 2 * NA_KR - 2)
        right_out = NA_TAB_RIGHT_OUT + jnp.clip(e - 1, 0, 2 * NA_KR - 2)
        return jnp.where(out0, jnp.where(out1, NA_TAB_BOTH_OUT, left_out), jnp.where(out1, right_out, both_in))

    entries = [[table_entry(rr, jj) for jj in range(n_pairs)] for rr in range(NA_ROWS_PER_STEP)]

    lane_head = lax.broadcasted_iota(jnp.int32, (n_q, LANES), 1) // NA_HEAD_DIM

    def scores(p):
        ls = slice(p * LANES, (p + 1) * LANES)
        qp = q_ref[:, ls]
        zero = jnp.zeros_like(qp)
        q_stack = jnp.concatenate([jnp.where(lane_head == 0, qp, zero), jnp.where(lane_head == 1, qp, zero)], axis=0)
        k_all = jnp.concatenate([k_ref[0, :, ls], kc_ref[:, ls]], axis=0)
        s_scr[p] = lax.dot_general(q_stack, k_all, _NT, preferred_element_type=F32)

    scores(0)
    for p in range(NA_HEADS // 2):
        if p + 1 < NA_HEADS // 2:
            scores(p + 1)
        ls = slice(p * LANES, (p + 1) * LANES)
        v_all = jnp.concatenate([v_ref[0, :, ls], vc_ref[:, ls]], axis=0)
        s_p, p_p, rden_p = s_scr.at[p], p_scr.at[p], rden_scr.at[p]
        for sl in range(2 * n_q // NA_SLAB):
            hh = sl // (NA_ROWS_PER_STEP * slabs_per_row)
            rr = (sl // slabs_per_row) % NA_ROWS_PER_STEP
            q0 = (sl % slabs_per_row) * NA_SLAB
            rows = slice(sl * NA_SLAB, (sl + 1) * NA_SLAB)
            bias = jnp.concatenate(
                [tab_ref[2 * p + hh, entries[rr][jj], q0:q0 + NA_SLAB, :] for jj in range(n_pairs)], axis=-1)
            s_win = s_p[rows, :n_keys] + bias
            s_ctx = s_p[rows, n_keys:]
            m = jnp.maximum(jnp.max(s_win, axis=-1, keepdims=True), jnp.max(s_ctx, axis=-1, keepdims=True))
            e_win = jnp.exp2(s_win - m)
            e_ctx = jnp.exp2(s_ctx - m)
            den = jnp.sum(e_win, axis=-1, keepdims=True) + jnp.sum(e_ctx, axis=-1, keepdims=True)
            p_p[rows, :n_keys] = e_win.astype(BF16)
            p_p[rows, n_keys:] = e_ctx.astype(BF16)
            rden_p[rows, :] = jnp.broadcast_to(1.0 / den, (NA_SLAB, LANES))
        o = jnp.dot(p_p[...], v_all, preferred_element_type=F32) * rden_p[...]
        o_ref[:, ls] = jnp.where(lane_head == 0, o[:n_q], o[n_q:]).astype(o_ref.dtype)


def na_bias_table(rpb):
    w = np.arange(GRID_W)
    col_start = np.clip(w - NA_KC // 2, 0, GRID_W - NA_KC)
    col_mask = (w[None, :] >= col_start[:, None]) & (w[None, :] < col_start[:, None] + NA_KC)
    d_col = np.clip(w[None, :] - w[:, None], -(NA_KC - 1), NA_KC - 1) + (NA_KC - 1)
    pick = (d_col[None] == np.arange(2 * NA_KC - 1)[:, None, None]).astype(np.float32)
    full = jnp.einsum('hdj,jwu->hdwu', rpb.astype(F32), jnp.asarray(pick), precision=lax.Precision.HIGHEST)
    full = jnp.where(col_mask[None, None], full * LOG2E, NEG_INF)
    pad = jnp.zeros_like(full[:, :1])
    neg = jnp.full_like(full, NEG_INF)
    both_in = jnp.concatenate([jnp.concatenate([pad, full], axis=1), jnp.concatenate([full, pad], axis=1)], axis=-1)
    left_out = jnp.concatenate([neg, full], axis=-1)
    right_out = jnp.concatenate([full, neg], axis=-1)
    both_out = jnp.concatenate([neg[:, :1], neg[:, :1]], axis=-1)
    return jnp.concatenate([both_in, left_out, right_out, both_out], axis=1)


def neighbourhood_attention(q, k, v, kc, vc, table):
    bsz, n_tok, width = q.shape
    n_ctx = kc.shape[1]
    n_rows = n_tok // GRID_W
    assert n_rows >= NA_UNION_ROWS and n_rows % NA_ROWS_PER_STEP == 0
    tq = NA_ROWS_PER_STEP * GRID_W
    n_keys = NA_UNION_ROWS * GRID_W
    window = pl.BlockSpec((pl.Element(1), pl.Element(n_keys), pl.Element(width)),
                          lambda b, i: (b, _na_union_start(i, n_rows) * GRID_W, 0))
    return pl.pallas_call(
        functools.partial(_na_kernel, n_rows=n_rows),
        grid=(bsz, n_rows // NA_ROWS_PER_STEP),
        in_specs=[
            pl.BlockSpec((None, tq, width), lambda b, i: (b, i, 0)),
            window,
            window,
            pl.BlockSpec((None, n_ctx, width), lambda b, i: (b, 0, 0)),
            pl.BlockSpec((None, n_ctx, width), lambda b, i: (b, 0, 0)),
            _full(table.shape),
        ],
        out_specs=pl.BlockSpec((None, tq, width), lambda b, i: (b, i, 0)),
        out_shape=jax.ShapeDtypeStruct(q.shape, BF16),
        scratch_shapes=[
            pltpu.VMEM((NA_HEADS // 2, 2 * tq, n_keys + n_ctx), F32),
            pltpu.VMEM((NA_HEADS // 2, 2 * tq, n_keys + n_ctx), BF16),
            pltpu.VMEM((NA_HEADS // 2, 2 * tq, LANES), F32),
        ],
        compiler_params=_cparams("parallel", "arbitrary"),
        name="neighbourhood_attention",
    )(q, k, v, kc, vc, table)


def _ctx_attn_kernel(q_ref, k_ref, v_ref, o_ref):
    for p in range(NA_HEADS // 2):
        ls = slice(p * LANES, (p + 1) * LANES)
        o_pair = _na_head_pair(q_ref[:, ls], None, None, k_ref[:, ls], v_ref[:, ls], None)
        o_ref[:, ls] = o_pair.astype(o_ref.dtype)


def context_attention(qc, kc, vc):
    bsz, n_ctx, width = qc.shape
    spec = pl.BlockSpec((None, n_ctx, width), lambda b: (b, 0, 0))
    return pl.pallas_call(
        _ctx_attn_kernel,
        grid=(bsz,),
        in_specs=[spec, spec, spec],
        out_specs=spec,
        out_shape=jax.ShapeDtypeStruct(qc.shape, BF16),
        compiler_params=_cparams("parallel"),
        name="context_attention",
    )(qc, kc, vc)


ROUTER_LANES = LANES
EXPERT_LANE0 = N_GROUPS
GROUP_ID_LANE = 0
RANK_LANE = 1


def _route(logits):
    lane = lax.broadcasted_iota(jnp.int32, logits.shape, 1)
    big = jnp.int32(ROUTER_LANES)
    is_g = lane < N_GROUPS
    lg = jnp.where(is_g, logits, -jnp.inf)
    mg = jnp.max(lg, axis=-1, keepdims=True)
    grp = jnp.min(jnp.where(lg == mg, lane, big), axis=-1, keepdims=True)
    g_weight = 1.0 / jnp.sum(jnp.where(is_g, jnp.exp(logits - mg), 0.0), axis=-1, keepdims=True)
    e_idx = lane - EXPERT_LANE0
    sel = (e_idx >= 0) & (e_idx < N_EXPERTS) & ((e_idx // EXPERTS_PER_GROUP) == grp)
    ls1 = jnp.where(sel, logits, -jnp.inf)
    v1 = jnp.max(ls1, axis=-1, keepdims=True)
    i1 = jnp.min(jnp.where(ls1 == v1, lane, big), axis=-1, keepdims=True)
    ls2 = jnp.where(lane == i1, -jnp.inf, ls1)
    v2 = jnp.max(ls2, axis=-1, keepdims=True)
    i2 = jnp.min(jnp.where(ls2 == v2, lane, big), axis=-1, keepdims=True)
    e2 = jnp.exp(v2 - v1)
    w1 = 1.0 / (1.0 + e2)
    w2 = e2 * w1
    comb = g_weight * (jnp.where(lane == i1, w1, 0.0) + jnp.where(lane == i2, w2, 0.0))
    return jnp.where(lane == GROUP_ID_LANE, grp.astype(F32), comb)


def _merge_kernel(x_ref, ya_ref, xb_ref, yf_ref, yr_ref, yc_ref, gate_ref, mod_ref, d_ref, glu_ref,
                  wa_ref, wb_ref, wc_ref, wo_ref, g_ref, rw_ref, rb_ref, own_ref, xn_ref, h_ref, comb_ref):
    d = x_ref.shape[1]
    yb = d_ref[...] * xb_ref[...].astype(F32) + yf_ref[...] + yr_ref[...]
    yb = _gelu(yb)
    yb = yb * _sigmoid(jnp.dot(yb.astype(BF16), glu_ref[...], preferred_element_type=F32))

    def gate(j):
        return gate_ref[:, j * d:(j + 1) * d].astype(F32)

    m = gate(0) * jnp.dot(ya_ref[...], wa_ref[...], preferred_element_type=F32)
    m = m + gate(1) * jnp.dot(yb.astype(BF16), wb_ref[...], preferred_element_type=F32)
    m = m + gate(2) * jnp.dot(yc_ref[...], wc_ref[...], preferred_element_type=F32)
    xn = x_ref[...] + mod_ref[2:3, :] * jnp.dot(m.astype(BF16), wo_ref[...], preferred_element_type=F32)
    xn_ref[...] = xn
    ms = jnp.mean(xn * xn, axis=-1, keepdims=True)
    h = xn * lax.rsqrt(ms + EPS) * g_ref[...]
    h = h * (1.0 + mod_ref[4:5, :]) + mod_ref[3:4, :]
    hb = h.astype(BF16)
    h_ref[:, :d] = hb
    logits = jnp.dot(hb, rw_ref[...], preferred_element_type=F32) + rb_ref[...]
    comb = _route(logits)
    comb_ref[...] = comb
    own = jnp.dot(comb, own_ref[...], preferred_element_type=F32)
    own_hi, own_lo = _split_bf16(own)
    lane = lax.broadcasted_iota(jnp.int32, own.shape, 1)
    h_ref[:, d:] = jnp.where(lane < EXPERTS_PER_GROUP, own_hi, own_lo)


def merge_and_route(x, ya, xb, yf, yr, yc, gates, mod, s5_d, glu_w, w_br_a, w_br_b, w_br_c, w_out,
                    norm_ffn_g, router_w, router_b, tm):
    bsz, n_tok, d = x.shape

    def tok(width):
        return pl.BlockSpec((None, tm, width), lambda b, i: (b, i, 0))

    own = np.zeros((ROUTER_LANES, ROUTER_LANES), np.float32)
    for e in range(N_EXPERTS):
        own[EXPERT_LANE0 + e, e % EXPERTS_PER_GROUP] = 1.0
        own[EXPERT_LANE0 + e, EXPERTS_PER_GROUP + e % EXPERTS_PER_GROUP] = 1.0
    weights = [s5_d.reshape(1, S5_WIDTH), glu_w, w_br_a, w_br_b, w_br_c, w_out,
               norm_ffn_g.reshape(1, d), router_w, router_b, jnp.asarray(own)]
    return pl.pallas_call(
        _merge_kernel,
        grid=(bsz, n_tok // tm),
        in_specs=[tok(d), tok(SGU_WIDTH), tok(S5_WIDTH), tok(S5_WIDTH), tok(S5_WIDTH), tok(NA_WIDTH),
                  tok(N_BRANCH * d), pl.BlockSpec((None, ADA_CHUNKS, d), lambda b, i: (b, 0, 0))]
        + [_full(w.shape) for w in weights],
        out_specs=[tok(d), tok(d + ROUTER_LANES), tok(ROUTER_LANES)],
        out_shape=[
            jax.ShapeDtypeStruct(x.shape, F32),
            jax.ShapeDtypeStruct((bsz, n_tok, d + ROUTER_LANES), BF16),
            jax.ShapeDtypeStruct((bsz, n_tok, ROUTER_LANES), F32),
        ],
        compiler_params=_cparams("parallel", "parallel"),
        name="merge_and_route",
    )(x, ya, xb, yf, yr, yc, gates, mod, *weights)


def router_params(rg_w, rg_b, re_w, re_b):
    d = rg_w.shape[0]
    pad = ROUTER_LANES - N_GROUPS - N_EXPERTS
    w = jnp.concatenate([rg_w, re_w, jnp.zeros((d, pad), F32)], axis=1).astype(BF16)
    b = jnp.concatenate([rg_b, re_b, jnp.zeros((pad,), F32)]).astype(F32).reshape(1, ROUTER_LANES)
    return w, b


MOE_BLOCK = 144


def _split_bf16(x):
    hi = x.astype(BF16)
    return hi, (x - hi.astype(F32)).astype(BF16)


MOE_SUBTILE = 512


def _moe_kernel(xn_ref, hx_ref, comb_ref, mod_ref, wg_ref, wu_ref, wd_ref, fg_ref, o_ref,
                aux_col, aux_row, hid_scr, cnt_ref, *, final_norm):
    g = pl.program_id(2)
    tm, d = o_ref.shape
    st = min(MOE_SUBTILE, tm)
    n_sub = tm // st

    @pl.when(g == 0)
    def _():
        o_ref[...] = xn_ref[...]
        row_i = lax.broadcasted_iota(jnp.int32, (st, st), 0)
        col_i = lax.broadcasted_iota(jnp.int32, (st, st), 1)
        tri = jnp.where(col_i < row_i, 1.0, 0.0).astype(BF16)
        for s in range(n_sub):
            comb = comb_ref[s * st:(s + 1) * st, :]
            lane = lax.broadcasted_iota(jnp.int32, comb.shape, 1)
            grp = comb[:, GROUP_ID_LANE:GROUP_ID_LANE + 1]
            onehot = jnp.where(lane < N_GROUPS, jnp.where(lane.astype(F32) == grp, 1.0, 0.0), 0.0)
            ranks = jnp.dot(tri, onehot.astype(BF16), preferred_element_type=F32)
            own = jnp.sum(onehot * ranks, axis=-1, keepdims=True)
            aux = jnp.where(lane == GROUP_ID_LANE, grp, jnp.where(lane == RANK_LANE, own, 0.0))
            aux_col[s * st:(s + 1) * st, :] = aux
            aux_row[:, s * st:(s + 1) * st] = aux.T
            for gg in range(N_GROUPS):
                cnt_ref[s * N_GROUPS + gg] = jnp.sum(onehot[:, gg:gg + 1]).astype(jnp.int32)

    gf = g.astype(F32)
    slot_r = lax.broadcasted_iota(jnp.int32, (MOE_BLOCK, st), 0).astype(F32)
    slot_c = lax.broadcasted_iota(jnp.int32, (st, MOE_BLOCK), 1).astype(F32)
    scale = mod_ref[5:6, :]

    sub_rows = [slice(s * st, (s + 1) * st) for s in range(n_sub)]
    rank_rows = [jnp.where(aux_row[GROUP_ID_LANE:GROUP_ID_LANE + 1, r] == gf,
                           aux_row[RANK_LANE:RANK_LANE + 1, r], -1.0) for r in sub_rows]
    rank_cols = [jnp.where(aux_col[r, GROUP_ID_LANE:GROUP_ID_LANE + 1] == gf,
                           aux_col[r, RANK_LANE:RANK_LANE + 1], -1.0) for r in sub_rows]
    n_rounds = cnt_ref[g]
    for s in range(1, n_sub):
        n_rounds = jnp.maximum(n_rounds, cnt_ref[s * N_GROUPS + g])
    n_rounds = (n_rounds + MOE_BLOCK - 1) // MOE_BLOCK

    def round_(j, carry):
        base = (j * MOE_BLOCK).astype(F32)
        hcx = jnp.concatenate(
            [jnp.dot(jnp.where(rank_rows[s] - base == slot_r, 1.0, 0.0).astype(BF16), hx_ref[sub_rows[s], :],
                     preferred_element_type=F32) for s in range(n_sub)], axis=0)
        hc = hcx[:, :d].astype(BF16)
        wt = hcx[:, d:]
        for e in range(EXPERTS_PER_GROUP):
            a = jnp.dot(hc, wg_ref[e], preferred_element_type=F32)
            u = jnp.dot(hc, wu_ref[e], preferred_element_type=F32)
            cw = wt[:, e:e + 1] + wt[:, EXPERTS_PER_GROUP + e:EXPERTS_PER_GROUP + e + 1]
            hid_scr[:, e * D_EXPERT:(e + 1) * D_EXPERT] = (a * jax.nn.sigmoid(a) * u * cw).astype(BF16)
        oc = jnp.dot(hid_scr[...], wd_ref[...], preferred_element_type=F32).astype(BF16)
        for s in range(n_sub):
            scatter = jnp.where(rank_cols[s] - base == slot_c, 1.0, 0.0).astype(BF16)
            o_ref[sub_rows[s], :] += scale * jnp.dot(
                scatter, oc[s * MOE_BLOCK:(s + 1) * MOE_BLOCK], preferred_element_type=F32)
        return carry

    lax.fori_loop(0, n_rounds, round_, 0)

    if final_norm:
        @pl.when(g == pl.num_programs(2) - 1)
        def _():
            xo = o_ref[...]
            ms = jnp.mean(xo * xo, axis=-1, keepdims=True)
            o_ref[...] = xo * lax.rsqrt(ms + EPS) * fg_ref[...]


def moe_grouped(xn, hx, comb, mod, wg, wu, wd, layer, final_g, tm, final_norm):
    bsz, n_tok, d = xn.shape
    gw = EXPERTS_PER_GROUP * D_EXPERT
    n_sub = tm // min(MOE_SUBTILE, tm)

    def tok(width):
        return pl.BlockSpec((None, tm, width), lambda b, i, g: (b, i, 0))

    return pl.pallas_call(
        functools.partial(_moe_kernel, final_norm=final_norm),
        grid=(bsz, n_tok // tm, N_GROUPS),
        in_specs=[
            tok(d), tok(d + ROUTER_LANES), tok(ROUTER_LANES),
            pl.BlockSpec((None, ADA_CHUNKS, d), lambda b, i, g: (b, 0, 0)),
            pl.BlockSpec((None, None, EXPERTS_PER_GROUP, d, D_EXPERT), lambda b, i, g: (layer, g, 0, 0, 0)),
            pl.BlockSpec((None, None, EXPERTS_PER_GROUP, d, D_EXPERT), lambda b, i, g: (layer, g, 0, 0, 0)),
            pl.BlockSpec((None, None, gw, d), lambda b, i, g: (layer, g, 0, 0)),
            pl.BlockSpec((1, d), lambda b, i, g: (0, 0)),
        ],
        out_specs=tok(d),
        out_shape=jax.ShapeDtypeStruct(xn.shape, F32),
        scratch_shapes=[
            pltpu.VMEM((tm, ROUTER_LANES), F32),
            pltpu.VMEM((ROUTER_LANES, tm), F32),
            pltpu.VMEM((n_sub * MOE_BLOCK, gw), BF16),
            pltpu.SMEM((n_sub * N_GROUPS,), jnp.int32),
        ],
        compiler_params=_cparams("parallel", "parallel", "arbitrary"),
        name="moe_grouped",
    )(xn, hx, comb, mod, wg, wu, wd, final_g.reshape(1, d))


EXPERTS_PER_CAST_STEP = 4


def _cast_kernel(a_ref, b_ref, c_ref, oa_ref, ob_ref, oc_ref):
    oa_ref[...] = a_ref[...].astype(oa_ref.dtype)
    ob_ref[...] = b_ref[...].astype(ob_ref.dtype)
    oc_ref[...] = c_ref[...].astype(oc_ref.dtype)


def moe_params(e_gate, e_up, e_down):
    n_layers, n_exp, d, f = e_gate.shape
    flat = [w.reshape(n_layers * n_exp, *w.shape[2:]) for w in (e_gate, e_up, e_down)]

    def spec(w):
        return pl.BlockSpec((EXPERTS_PER_CAST_STEP,) + w.shape[1:], lambda i: (i, 0, 0))

    wg, wu, wd = pl.pallas_call(
        _cast_kernel,
        grid=(n_layers * n_exp // EXPERTS_PER_CAST_STEP,),
        in_specs=[spec(w) for w in flat],
        out_specs=[spec(w) for w in flat],
        out_shape=[jax.ShapeDtypeStruct(w.shape, BF16) for w in flat],
        compiler_params=_cparams("parallel"),
        name="expert_weights_bf16",
    )(*flat)
    return (wg.reshape(n_layers, N_GROUPS, EXPERTS_PER_GROUP, d, f),
            wu.reshape(n_layers, N_GROUPS, EXPERTS_PER_GROUP, d, f),
            wd.reshape(n_layers, N_GROUPS, EXPERTS_PER_GROUP * f, d))


def s5_params(a_re, a_im, log_dt, b_re, b_im, c_re, c_im, bsz):
    lam = lax.complex(a_re.astype(F32), a_im.astype(F32))
    dt = jnp.exp(log_dt.astype(F32))[..., None]
    lam_bar = jnp.exp(lam * dt)
    b_bar = ((lam_bar - 1) / lam)[..., None] * lax.complex(b_re.astype(F32), b_im.astype(F32))
    gpb = S5_GROUPS // S5_BLOCKS
    eye = jnp.eye(gpb, dtype=F32)
    gp = S5_GROUPS * S5_STATE

    def in_mat(m):
        m = m.reshape(2, S5_BLOCKS, gpb, S5_STATE, S5_GROUP)
        return jnp.einsum('dkgpc,gh->dkgchp', m, eye).reshape(2, S5_BLOCKS, gpb * S5_GROUP, gpb * S5_STATE)

    def out_mat(m):
        m = m.reshape(2, S5_BLOCKS, gpb, S5_GROUP, S5_STATE)
        return jnp.einsum('dkgcp,gh->dkgphc', m, eye).reshape(2, S5_BLOCKS, gpb * S5_STATE, gpb * S5_GROUP)

    w_in = jnp.concatenate([in_mat(b_bar.real), in_mat(b_bar.imag)], axis=-1).astype(BF16)
    w_out = jnp.concatenate([out_mat(c_re.astype(F32)), -out_mat(c_im.astype(F32))], axis=2).astype(BF16)

    def tiles(v):
        t = v.reshape(2, gp // LANES, 1, LANES)
        t = jnp.broadcast_to(t, (2, gp // LANES, bsz, LANES))
        return jnp.concatenate([t[0], t[1]], axis=1)

    lam_flat = lam_bar.reshape(2, gp)
    return w_in, tiles(lam_flat.real), tiles(lam_flat.imag), w_out


CTX_TOKEN_TILE = 256
MIXER_TOKEN_TILE = 512
MERGE_TOKEN_TILE = 512
MOE_TOKEN_TILE = 1024


def kernel(x, c, ctx, c_ctx, ada_w, ada_b, norm_mix_g, norm_ffn_g, w_in, sgu_norm_g, sgu_w, sgu_b, s5_a_re, s5_a_im, s5_log_dt, s5_b_re, s5_b_im, s5_c_re, s5_c_im, s5_d, s5_glu_w, na_rpb, w_br_a, w_br_b, w_br_c, w_out, router_group_w, router_group_b, router_expert_w, router_expert_b, exp_w_gate, exp_w_up, exp_w_down, final_norm_g):
    bsz, n_tok, d = x.shape
    n_ctx = ctx.shape[1]
    depth = ada_w.shape[0]
    assert bsz + 1 <= SUBLANES

    cc = jnp.concatenate([c, c_ctx[None], jnp.zeros((SUBLANES - bsz - 1, d), F32)], axis=0)
    mod_all = ada_modulation(cc, ada_w, ada_b)
    rope_tabs = rope_tables(n_tok)
    s_zero = jnp.zeros((2 * bsz, S5_LANES), F32)
    tm_c = min(CTX_TOKEN_TILE, n_ctx)
    tm_moe = min(MOE_TOKEN_TILE, n_tok)

    wg_all, wu_all, wd_all = moe_params(exp_w_gate, exp_w_up, exp_w_down)

    xc = ctx
    for l in range(depth):
        with_ctx_out = l < depth - 1
        mod = mod_all[l, :bsz].reshape(bsz, ADA_CHUNKS, d)
        mod_c = jnp.broadcast_to(mod_all[l, bsz].reshape(1, ADA_CHUNKS, d), (bsz, ADA_CHUNKS, d))
        w_in_l = w_in[l].astype(BF16)
        sgu_w_l = sgu_w[l].astype(BF16)
        sgu_bias = jnp.broadcast_to(sgu_b[l].astype(F32)[:, :, None], (SGU_GROUPS, SGU_CHUNK, SGU_CHUNK))
        s5_w_in, s5_lre, s5_lim, s5_w_out = s5_params(
            s5_a_re[l], s5_a_im[l], s5_log_dt[l], s5_b_re[l], s5_b_im[l], s5_c_re[l], s5_c_im[l], bsz)
        table = na_bias_table(na_rpb[l])
        r_w, r_b = router_params(router_group_w[l], router_group_b[l], router_expert_w[l], router_expert_b[l])
        merge_w = (s5_d[l].astype(F32), s5_glu_w[l].astype(BF16), w_br_a[l].astype(BF16),
                   w_br_b[l].astype(BF16), w_br_c[l].astype(BF16), w_out[l].astype(BF16),
                   norm_ffn_g[l].astype(F32), r_w, r_b)

        ya_c, xb_c, q_c, k_c, v_c, gate_c = mixer_in(
            xc, mod_c, norm_mix_g[l], w_in_l, sgu_norm_g[l], sgu_w_l, sgu_bias, None, tm_c)
        ya_l, xb_l, q_l, k_l, v_l, gate_l = mixer_in(
            x, mod, norm_mix_g[l], w_in_l, sgu_norm_g[l], sgu_w_l, sgu_bias, rope_tabs, MIXER_TOKEN_TILE)
        ycf, ycr, s_ctx = s5_scan(xb_c, s_zero, s5_w_in, s5_lre, s5_lim, s5_w_out)
        ylf, ylr, _ = s5_scan(xb_l, s_ctx, s5_w_in, s5_lre, s5_lim, s5_w_out)
        yc_l = neighbourhood_attention(q_l, k_l, v_l, k_c, v_c, table)
        xn, h2, comb = merge_and_route(x, ya_l, xb_l, ylf, ylr, yc_l, gate_l, mod, *merge_w, MERGE_TOKEN_TILE)
        x = moe_grouped(xn, h2, comb, mod, wg_all, wu_all, wd_all, l, final_norm_g, tm_moe, not with_ctx_out)
        if with_ctx_out:
            yc_c = context_attention(q_c, k_c, v_c)
            xcn, hc2, comb_c = merge_and_route(xc, ya_c, xb_c, ycf, ycr, yc_c, gate_c, mod_c, *merge_w, tm_c)

            def flat(t):
                return t.reshape(1, bsz * n_ctx, t.shape[-1])

            xc = moe_grouped(flat(xcn), flat(hc2), flat(comb_c), mod_c[:1], wg_all, wu_all, wd_all, l, final_norm_g,
                             min(MOE_TOKEN_TILE, bsz * n_ctx), False).reshape(bsz, n_ctx, d)
    return x
```
